```python
import math
import jax
import jax.numpy as jnp
from jax import lax
import numpy as np

D_MODEL = 1024
BATCH = 16
SEQ = 256
DEPTH = 2
DEC_BATCH = 8
DEC_SEQ = 2048
PAST_LEN = 512

GRID_W = 64
ROPE_THETA = 10000.0
Q_BLOCK = 128
EPS = 1e-6
MLA_HEADS = 8
MLA_NOPE = 64
MLA_ROPE = 32
MLA_V = 64
MLA_Q_LORA = 384
MLA_KV_LORA = 256
MLA_WIDTH = MLA_HEADS * MLA_V
MLA_SCALE = (MLA_NOPE + MLA_ROPE) ** -0.5
SWA_HEADS = 8
SWA_KV_HEADS = 2
SWA_DH = 64
SWA_WIDTH = SWA_HEADS * SWA_DH
WINDOW = 128
SWA_SCALE = SWA_DH ** -0.5
GDN_HEADS = 4
GDN_DK = 128
GDN_DV = 128
GDN_WIDTH = GDN_HEADS * GDN_DV
CONV_K = 3
GDN_CHUNK = 64
ATT_HEADS = 4
ATT_KV_HEADS = 2
ATT_DH = 128
ATT_WIDTH = ATT_HEADS * ATT_DH
ATT_SCALE = ATT_DH ** -0.5
L0_SPLIT = (MLA_Q_LORA, MLA_KV_LORA, MLA_ROPE, MLA_WIDTH, SWA_WIDTH, SWA_KV_HEADS * SWA_DH, SWA_KV_HEADS * SWA_DH, SWA_WIDTH)
L1_SPLIT = (GDN_HEADS * (2 * GDN_DK + GDN_DV), 2 * GDN_HEADS, 2 * GDN_HEADS, GDN_WIDTH, ATT_WIDTH, ATT_KV_HEADS * ATT_DH, ATT_KV_HEADS * ATT_DH, ATT_WIDTH)
L0_COLS = sum(L0_SPLIT)
L1_COLS = sum(L1_SPLIT)
NEG = -1e30

kernel_name = 'hybrid_mla_swa_gdn_axial_step'


def _rms(x, g):
    xf = x.astype(jnp.float32)
    y = xf * lax.rsqrt(jnp.mean(xf * xf, axis=-1, keepdims=True) + EPS)
    return (y * g.astype(jnp.float32)).astype(x.dtype)


def _l2(x):
    xf = x.astype(jnp.float32)
    return (xf * lax.rsqrt(jnp.sum(xf * xf, axis=-1, keepdims=True) + EPS)).astype(x.dtype)


def _split(u, sizes):
    cuts, acc = [], 0
    for s in sizes[:-1]:
        acc += s
        cuts.append(acc)
    return jnp.split(u, cuts, axis=-1)


def _adaln(x, gain, mod):
    shift, scale, gate = jnp.split(mod, 3, axis=-1)
    return _rms(x, gain) * (1 + scale) + shift, gate


def _group(q, n_kv):
    B, T, H, d = q.shape
    return q.reshape(B, T, n_kv, H // n_kv, d)


def _rope_tables(n_tok, rot_dim):
    rows = n_tok // GRID_W
    quarter = rot_dim // 4
    inv = ROPE_THETA ** (-jnp.arange(quarter, dtype=jnp.float32) / quarter)
    t = jnp.arange(rows * GRID_W)
    pos = jnp.stack([t // GRID_W, t % GRID_W], axis=-1).astype(jnp.float32)
    ang = pos[:, :, None] * inv
    return jnp.cos(ang), jnp.sin(ang)


def _apply_rope(x, cos, sin):
    B, T, H, R = x.shape
    xs = x.astype(jnp.float32).reshape(B, T, H, 2, 2, R // 4)
    x1, x2 = xs[..., 0, :], xs[..., 1, :]
    c, s = cos[None, :, None], sin[None, :, None]
    out = jnp.stack([x1 * c - x2 * s, x2 * c + x1 * s], axis=-2)
    return out.reshape(B, T, H, R).astype(x.dtype)


def _attend(q, k, v, mask, sink, scale):
    s = jnp.einsum('bqhgd,bkhd->bhgqk', q, k).astype(jnp.float32) * scale
    if mask is not None:
        s = jnp.where(mask, s, NEG)
    if sink is not None:
        sk = jnp.broadcast_to(sink.astype(jnp.float32)[None, :, :, None, None], s.shape[:-1] + (1,))
        p = jax.nn.softmax(jnp.concatenate([s, sk], axis=-1), axis=-1)[..., :-1]
    else:
        p = jax.nn.softmax(s, axis=-1)
    return jnp.einsum('bhgqk,bkhd->bqhgd', p.astype(v.dtype), v)


def _sweep_queries(q, fn):
    B, Q = q.shape[0], q.shape[1]
    nb = Q // Q_BLOCK
    qb = jnp.moveaxis(q.reshape((B, nb, Q_BLOCK) + q.shape[2:]), 1, 0)
    out = lax.map(lambda a: fn(a[0], a[1]), (qb, jnp.arange(nb)))
    return jnp.moveaxis(out, 0, 1).reshape((B, Q) + out.shape[3:])


def _window_ctx_attend(q, k, v, k_ctx, v_ctx, sink):
    S, L = k.shape[1], k_ctx.shape[1]
    span = Q_BLOCK + 2 * WINDOW
    pad = ((0, 0), (WINDOW, WINDOW), (0, 0), (0, 0))
    kp, vp = jnp.pad(k, pad), jnp.pad(v, pad)
    a = jnp.arange(Q_BLOCK)[:, None]
    b = jnp.arange(span)[None, :]
    band = (b - a >= 0) & (b - a <= 2 * WINDOW)
    ctx_mask = jnp.ones((Q_BLOCK, L), dtype=bool)

    def blk(qb, n):
        start = n * Q_BLOCK
        ks = lax.dynamic_slice_in_dim(kp, start, span, axis=1)
        vs = lax.dynamic_slice_in_dim(vp, start, span, axis=1)
        j = start - WINDOW + b
        mask = jnp.concatenate([band & (j >= 0) & (j < S), ctx_mask], axis=1)
        return _attend(qb, jnp.concatenate([ks, k_ctx], axis=1), jnp.concatenate([vs, v_ctx], axis=1), mask, sink, SWA_SCALE)

    return _sweep_queries(q, blk)


def _short_conv(x, w):
    y = lax.conv_general_dilated(x, w[:, None, :].astype(x.dtype), window_strides=(1,),
                                 padding=((CONV_K // 2, CONV_K // 2),),
                                 dimension_numbers=('NWC', 'WIO', 'NWC'), feature_group_count=x.shape[-1])
    return jax.nn.silu(y)


def _gdn_chunked(q, k, v, g, beta, s0):
    out_dtype = v.dtype
    f32 = jnp.float32
    B, T, H, dk = k.shape
    dv = v.shape[-1]
    n = T // GDN_CHUNK

    def chunks(t):
        t = t.astype(f32).reshape((B, n, GDN_CHUNK, H) + t.shape[3:])
        return jnp.moveaxis(t, 3, 1)

    q = chunks(q) * dk ** -0.5
    k, v, beta, g = chunks(k), chunks(v), chunks(beta), chunks(g)
    gc = jnp.cumsum(g, axis=-1)
    causal = jnp.tril(jnp.ones((GDN_CHUNK, GDN_CHUNK), dtype=bool))
    strict = jnp.tril(jnp.ones((GDN_CHUNK, GDN_CHUNK), dtype=bool), -1)
    decay = jnp.exp(jnp.where(causal, gc[..., :, None] - gc[..., None, :], -jnp.inf))
    kb = k * beta[..., None]
    lmat = jnp.where(strict, jnp.einsum('bhncd,bhnsd->bhncs', kb, k) * decay, 0.0)
    amat = lmat + jnp.eye(GDN_CHUNK, dtype=f32)
    rhs = jnp.concatenate([v * beta[..., None], kb * jnp.exp(gc)[..., None]], axis=-1)
    sol = lax.linalg.triangular_solve(amat, rhs, left_side=True, lower=True)
    u, w = sol[..., :dv], sol[..., dv:]
    intra = jnp.einsum('bhncd,bhnsd->bhncs', q, k) * decay

    def step(S, xs):
        q_i, k_i, u_i, w_i, g_i, a_i = xs
        v_new = u_i - jnp.einsum('bhcd,bhde->bhce', w_i, S)
        o = jnp.einsum('bhcd,bhde->bhce', q_i * jnp.exp(g_i)[..., None], S) + jnp.einsum('bhcs,bhse->bhce', a_i, v_new)
        g_last = g_i[..., -1:]
        S = S * jnp.exp(g_last)[..., None] + jnp.einsum('bhcd,bhce->bhde', k_i * jnp.exp(g_last - g_i)[..., None], v_new)
        return S, o

    xs = tuple(jnp.moveaxis(t, 2, 0) for t in (q, k, u, w, gc, intra))
    S, o = lax.scan(step, s0.astype(f32), xs)
    o = jnp.moveaxis(jnp.moveaxis(o, 0, 2), 1, 3).reshape(B, T, H, dv)
    return o.astype(out_dtype), S.astype(out_dtype)


def _bidir_gdn(q, k, v, g, beta, s0):
    o_f, s_f = _gdn_chunked(q, k, v, g[:, :, 0], beta[:, :, 0], s0[:, 0])
    rev = lambda t: jnp.flip(t, axis=1)
    o_b, s_b = _gdn_chunked(rev(q), rev(k), rev(v), rev(g[:, :, 1]), rev(beta[:, :, 1]), s0[:, 1])
    return o_f + rev(o_b), jnp.stack([s_f, s_b], axis=1)


def _mla_expand(ckv, krope, w_ukv):
    B, T, _ = ckv.shape
    kv = (ckv @ w_ukv).reshape(B, T, MLA_HEADS, MLA_NOPE + MLA_V)
    k = jnp.concatenate([kv[..., :MLA_NOPE], jnp.broadcast_to(krope[:, :, None, :], (B, T, MLA_HEADS, MLA_ROPE))], axis=-1)
    return k, kv[..., MLA_NOPE:]


def _layer0(x, mod, ctx, ln, w_in, mla_q_norm, w_uq, mla_kv_norm, w_ukv, swa_sink, w_out):
    B, T, _ = x.shape
    h, gate = _adaln(x, ln, mod)
    cq, ckv, krope, z_a, q_b, k_b, v_b, z_b = _split(h @ w_in, L0_SPLIT)
    q_a = (_rms(cq, mla_q_norm) @ w_uq).reshape(B, T, MLA_HEADS, MLA_NOPE + MLA_ROPE)
    ckv = _rms(ckv, mla_kv_norm)
    q_b = q_b.reshape(B, T, SWA_HEADS, SWA_DH)
    k_b = k_b.reshape(B, T, SWA_KV_HEADS, SWA_DH)
    v_b = v_b.reshape(B, T, SWA_KV_HEADS, SWA_DH)
    sink = swa_sink.reshape(SWA_KV_HEADS, SWA_HEADS // SWA_KV_HEADS)
    if ctx is None:
        k_a, v_a = _mla_expand(ckv, krope, w_ukv)
        o_b = _sweep_queries(_group(q_b, SWA_KV_HEADS), lambda qb, n: _attend(qb, k_b, v_b, None, sink, SWA_SCALE))
        new_ctx = (ckv, krope, k_b, v_b)
    else:
        ckv_c, krope_c, k_c, v_c = ctx
        cos, sin = _rope_tables(T, MLA_ROPE)
        q_a = jnp.concatenate([q_a[..., :MLA_NOPE], _apply_rope(q_a[..., MLA_NOPE:], cos, sin)], axis=-1)
        krope_r = _apply_rope(krope[:, :, None, :], cos, sin)[:, :, 0]
        k_a, v_a = _mla_expand(jnp.concatenate([ckv, ckv_c], axis=1), jnp.concatenate([krope_r, krope_c], axis=1), w_ukv)
        cos, sin = _rope_tables(T, SWA_DH)
        o_b = _window_ctx_attend(_group(_apply_rope(q_b, cos, sin), SWA_KV_HEADS), _apply_rope(k_b, cos, sin), v_b, k_c, v_c, sink)
        new_ctx = ()
    o_a = _sweep_queries(q_a[:, :, :, None, :], lambda qb, n: _attend(qb, k_a, v_a, None, None, MLA_SCALE))
    y = jnp.concatenate([o_a.reshape(B, T, MLA_WIDTH) * jax.nn.silu(z_a),
                         o_b.reshape(B, T, SWA_WIDTH) * jax.nn.silu(z_b)], axis=-1) @ w_out
    return x + gate * y, new_ctx


def _layer1(x, mod, ctx, ln, w_in, gdn_conv, gdn_a_log, gdn_dt_bias, gdn_norm, att_q_norm, att_k_norm, w_out):
    B, T, _ = x.shape
    f32 = jnp.float32
    h, gate = _adaln(x, ln, mod)
    qkv, a, b, z_c, q_d, k_d, v_d, z_d = _split(h @ w_in, L1_SPLIT)
    q_c, k_c, v_c = _split(_short_conv(qkv, gdn_conv), (GDN_HEADS * GDN_DK, GDN_HEADS * GDN_DK, GDN_HEADS * GDN_DV))
    q_c = _l2(q_c.reshape(B, T, GDN_HEADS, GDN_DK))
    k_c = _l2(k_c.reshape(B, T, GDN_HEADS, GDN_DK))
    v_c = v_c.reshape(B, T, GDN_HEADS, GDN_DV)
    a = a.astype(f32).reshape(B, T, 2, GDN_HEADS)
    g = -jnp.exp(gdn_a_log.astype(f32)) * jax.nn.softplus(a + gdn_dt_bias.astype(f32))
    beta = jax.nn.sigmoid(b.astype(f32).reshape(B, T, 2, GDN_HEADS))
    s0 = jnp.zeros((B, 2, GDN_HEADS, GDN_DK, GDN_DV), x.dtype) if ctx is None else ctx[0]
    o_c, s_fin = _bidir_gdn(q_c, k_c, v_c, g, beta, s0)
    o_c = _rms(o_c, gdn_norm) * jax.nn.silu(z_c.reshape(B, T, GDN_HEADS, GDN_DV))
    q_d = _rms(q_d.reshape(B, T, ATT_HEADS, ATT_DH), att_q_norm)
    k_d = _rms(k_d.reshape(B, T, ATT_KV_HEADS, ATT_DH), att_k_norm)
    v_d = v_d.reshape(B, T, ATT_KV_HEADS, ATT_DH)
    if ctx is None:
        q_att, k_all, v_all = q_d, k_d, v_d
        new_ctx = (s_fin, k_d, v_d)
    else:
        cos, sin = _rope_tables(T, ATT_DH)
        q_att = _apply_rope(q_d, cos, sin)
        k_all = jnp.concatenate([_apply_rope(k_d, cos, sin), ctx[1]], axis=1)
        v_all = jnp.concatenate([v_d, ctx[2]], axis=1)
        new_ctx = ()
    o_d = _sweep_queries(_group(q_att, ATT_KV_HEADS), lambda qb, n: _attend(qb, k_all, v_all, None, None, ATT_SCALE))
    y = jnp.concatenate([o_c.reshape(B, T, GDN_WIDTH),
                         o_d.reshape(B, T, ATT_WIDTH) * jax.nn.silu(z_d)], axis=-1) @ w_out
    return x + gate * y, new_ctx


def setup_inputs(seed: int = 0) -> dict:
    key = jax.random.key(seed)
    ks = iter(jax.random.split(key, 48))
    f32 = jnp.float32

    def nrm(shape, scale=1.0):
        return jax.random.normal(next(ks), shape, f32) * scale

    D = D_MODEL
    inp = {}
    inp['x_prompt'] = nrm((BATCH, SEQ, D))
    inp['x_sample'] = nrm((DEC_BATCH, DEC_SEQ, D))
    inp['cache_l0_mla_ckv'] = nrm((DEC_BATCH, PAST_LEN, MLA_KV_LORA))
    inp['cache_l0_mla_krope'] = nrm((DEC_BATCH, PAST_LEN, MLA_ROPE))
    inp['cache_l0_swa_k'] = nrm((DEC_BATCH, PAST_LEN, SWA_KV_HEADS, SWA_DH))
    inp['cache_l0_swa_v'] = nrm((DEC_BATCH, PAST_LEN, SWA_KV_HEADS, SWA_DH))
    inp['state_l1_gdn'] = nrm((DEC_BATCH, 2, GDN_HEADS, GDN_DK, GDN_DV), 0.1)
    inp['cache_l1_attn_k'] = nrm((DEC_BATCH, PAST_LEN, ATT_KV_HEADS, ATT_DH))
    inp['cache_l1_attn_v'] = nrm((DEC_BATCH, PAST_LEN, ATT_KV_HEADS, ATT_DH))
    inp['c'] = nrm((DEC_BATCH, D))
    inp['c_ctx'] = nrm((D,))
    inp['w_mod0'] = nrm((D, 3 * D), 0.3 * D ** -0.5)
    inp['b_mod0'] = nrm((3 * D,), 0.02)
    inp['ln0'] = 1.0 + nrm((D,), 0.1)
    inp['w_in0'] = nrm((D, L0_COLS), D ** -0.5)
    inp['mla_q_norm'] = 1.0 + nrm((MLA_Q_LORA,), 0.1)
    inp['w_uq'] = nrm((MLA_Q_LORA, MLA_HEADS * (MLA_NOPE + MLA_ROPE)), MLA_Q_LORA ** -0.5)
    inp['mla_kv_norm'] = 1.0 + nrm((MLA_KV_LORA,), 0.1)
    inp['w_ukv'] = nrm((MLA_KV_LORA, MLA_HEADS * (MLA_NOPE + MLA_V)), MLA_KV_LORA ** -0.5)
    inp['swa_sink'] = nrm((SWA_HEADS,))
    inp['w_out0'] = nrm((MLA_WIDTH + SWA_WIDTH, D), (MLA_WIDTH + SWA_WIDTH) ** -0.5)
    inp['w_mod1'] = nrm((D, 3 * D), 0.3 * D ** -0.5)
    inp['b_mod1'] = nrm((3 * D,), 0.02)
    inp['ln1'] = 1.0 + nrm((D,), 0.1)
    inp['w_in1'] = nrm((D, L1_COLS), D ** -0.5)
    inp['gdn_conv'] = nrm((CONV_K, GDN_HEADS * (2 * GDN_DK + GDN_DV)), CONV_K ** -0.5)
    inp['gdn_a_log'] = jnp.log(jax.random.uniform(next(ks), (2, GDN_HEADS), f32, 1.0, 16.0))
    dt = jnp.exp(jax.random.uniform(next(ks), (2, GDN_HEADS), f32, math.log(1e-3), math.log(1e-1)))
    inp['gdn_dt_bias'] = dt + jnp.log(-jnp.expm1(-dt))
    inp['gdn_norm'] = 1.0 + nrm((GDN_DV,), 0.1)
    inp['att_q_norm'] = 1.0 + nrm((ATT_DH,), 0.1)
    inp['att_k_norm'] = 1.0 + nrm((ATT_DH,), 0.1)
    inp['w_out1'] = nrm((GDN_WIDTH + ATT_WIDTH, D), (GDN_WIDTH + ATT_WIDTH) ** -0.5)
    inp['ln_f'] = 1.0 + nrm((D,), 0.1)
    return inp


def reference(x_prompt, x_sample, cache_l0_mla_ckv, cache_l0_mla_krope, cache_l0_swa_k, cache_l0_swa_v,
              state_l1_gdn, cache_l1_attn_k, cache_l1_attn_v, c, c_ctx,
              w_mod0, b_mod0, ln0, w_in0, mla_q_norm, w_uq, mla_kv_norm, w_ukv, swa_sink, w_out0,
              w_mod1, b_mod1, ln1, w_in1, gdn_conv, gdn_a_log, gdn_dt_bias, gdn_norm, att_q_norm, att_k_norm, w_out1,
              ln_f):
    layer_fns = (_layer0, _layer1)
    layer_params = ((w_mod0, b_mod0, (ln0, w_in0, mla_q_norm, w_uq, mla_kv_norm, w_ukv, swa_sink, w_out0)),
                    (w_mod1, b_mod1, (ln1, w_in1, gdn_conv, gdn_a_log, gdn_dt_bias, gdn_norm, att_q_norm, att_k_norm, w_out1)))
    caches = ((cache_l0_mla_ckv, cache_l0_mla_krope, cache_l0_swa_k, cache_l0_swa_v),
              (state_l1_gdn, cache_l1_attn_k, cache_l1_attn_v))

    x = x_prompt
    ctx_out = []
    for l in range(DEPTH):
        w_mod, b_mod, params = layer_params[l]
        mod = (jax.nn.silu(c_ctx) @ w_mod + b_mod)[None, None, :]
        x, new_ctx = layer_fns[l](x, mod, None, *params)
        ctx_out.extend(new_ctx)
    y_prompt = _rms(x, ln_f)

    x = x_sample
    for l in range(DEPTH):
        w_mod, b_mod, params = layer_params[l]
        mod = (jax.nn.silu(c) @ w_mod + b_mod)[:, None, :]
        x, _ = layer_fns[l](x, mod, caches[l], *params)
    y_sample = _rms(x, ln_f)

    l0_mla_ckv, l0_mla_krope, l0_swa_k, l0_swa_v, l1_gdn_state, l1_attn_k, l1_attn_v = ctx_out
    return (y_prompt, y_sample, l0_mla_ckv, l0_mla_krope, l0_swa_k, l0_swa_v, l1_gdn_state, l1_attn_k, l1_attn_v)
```

```python
import functools
import math

import jax
import jax.numpy as jnp
from jax import lax
from jax.experimental import pallas as pl
from jax.experimental.pallas import tpu as pltpu

F32 = jnp.float32
BF16 = jnp.bfloat16

GRID_W = 64
ROPE_THETA = 10000.0
EPS = 1e-6
WINDOW = 128
MLA_HEADS, MLA_NOPE, MLA_ROPE, MLA_V = 8, 64, 32, 64
MLA_Q_LORA, MLA_KV_LORA = 384, 256
SWA_HEADS, SWA_KV_HEADS, SWA_DH = 8, 2, 64
GDN_HEADS, GDN_DK, GDN_DV, CONV_K, GDN_CHUNK = 4, 128, 128, 3, 64
ATT_HEADS, ATT_KV_HEADS, ATT_DH = 4, 2, 128
LANE = 128
LOG2E = math.log2(math.e)
NEG = -1e30
VMEM_LIMIT = 56 * 1024 * 1024


def _cparams(n_axes):
    return pltpu.CompilerParams(dimension_semantics=("arbitrary",) * n_axes, vmem_limit_bytes=VMEM_LIMIT)


def _dot(a, b):
    return jnp.dot(a, b, preferred_element_type=F32)


def _dot_nt(a, b):
    return lax.dot_general(a, b, (((1,), (1,)), ((), ())), preferred_element_type=F32)


def _dot_tn(a, b):
    return lax.dot_general(a, b, (((0,), (0,)), ((), ())), preferred_element_type=F32)


def _dot_exact(a, b):
    return jnp.dot(a, b, preferred_element_type=F32, precision=lax.Precision.HIGHEST)


def _split(a):
    hi = a.astype(BF16)
    return hi, (a - hi.astype(F32)).astype(BF16)


SOLVE_PASSES = 3


def _dot3(a, b):
    ah, al = _split(a)
    bh, bl = _split(b)
    if SOLVE_PASSES == 1:
        return _dot(ah, bh)
    return _dot(ah, bh) + (_dot(ah, bl) + _dot(al, bh))


def _silu(x):
    return x * jax.nn.sigmoid(x)


def _rms_rows(x, g):
    return x * lax.rsqrt(jnp.mean(x * x, axis=-1, keepdims=True) + EPS) * g


def _rope_block(x, cos, sin, half):
    lane = lax.broadcasted_iota(jnp.int32, x.shape, 1)
    first = (lane // half) % 2 == 0
    partner = jnp.where(first, pltpu.roll(x, LANE - half, 1), pltpu.roll(x, half, 1))
    return x * cos + partner * sin


def _mod_kernel(c_ref, w_ref, b_ref, o_ref):
    a = _silu(c_ref[...]).astype(BF16)
    o_ref[...] = _dot(a, w_ref[...].astype(BF16)) + b_ref[...]


def _mod(c_rows, w_mod, b_mod):
    r, d = c_rows.shape
    n = w_mod.shape[1]
    tn = 1024
    return pl.pallas_call(
        _mod_kernel,
        grid=(n // tn,),
        in_specs=[pl.BlockSpec((r, d), lambda j: (0, 0)),
                  pl.BlockSpec((d, tn), lambda j: (0, j)),
                  pl.BlockSpec((1, tn), lambda j: (0, j))],
        out_specs=pl.BlockSpec((r, tn), lambda j: (0, j)),
        out_shape=jax.ShapeDtypeStruct((r, n), F32),
        compiler_params=_cparams(1),
        name="mod",
    )(c_rows, w_mod, b_mod.reshape(1, n))


def _adaln(x, mod_ref, ln_ref):
    h = _rms_rows(x, ln_ref[...])
    return h * (1.0 + mod_ref[0, 1:2, :]) + mod_ref[0, 0:1, :]


L0_OFF = dict(cq=0, ckv=384, kr=640, za=768, qb=1280, kb=1792, vb=1920, zb=2048)
L0_W = 2560


def _inproj0_kernel(*refs, rope):
    if rope:
        (x_ref, mod_ref, ln_ref, w_ref, wvbt_ref, qn_ref, wuq_ref, kvn_ref, cm_ref, sm_ref, cs_ref, ss_ref,
         qa_ref, ckv_ref, kr_ref, za_ref, qb_ref, kb_ref, vbt_ref, zb_ref) = refs
    else:
        (x_ref, mod_ref, ln_ref, w_ref, wvbt_ref, qn_ref, wuq_ref, kvn_ref,
         qa_ref, ckv_ref, kr_ref, za_ref, qb_ref, kb_ref, vbt_ref, zb_ref,
         ckv32_ref, kr32_ref, kb32_ref, vb32_ref) = refs
    h = _adaln(x_ref[0], mod_ref, ln_ref).astype(BF16)
    u = _dot(h, w_ref[...])
    o = L0_OFF
    cq = _rms_rows(u[:, o["cq"]:o["cq"] + 384], qn_ref[...]).astype(BF16)
    qa = _dot(cq, wuq_ref[...])
    ckv = _rms_rows(u[:, o["ckv"]:o["ckv"] + 256], kvn_ref[...])
    kr = u[:, o["kr"]:o["kr"] + 128]
    qb = u[:, o["qb"]:o["qb"] + 512]
    kb = u[:, o["kb"]:o["kb"] + 128]
    vb = u[:, o["vb"]:o["vb"] + 128]
    if not rope:
        ckv32_ref[0] = ckv
        kr32_ref[0] = kr[:, 64:96]
        kb32_ref[0] = kb
        vb32_ref[0] = vb
    qa_scale = (MLA_NOPE + MLA_ROPE) ** -0.5 * LOG2E
    qb_scale = SWA_DH ** -0.5 * LOG2E
    for j in range(MLA_HEADS):
        blk = qa[:, j * LANE:(j + 1) * LANE]
        if rope:
            blk = _rope_block(blk, cm_ref[...], sm_ref[...], MLA_ROPE // 4)
        qa_ref[0, :, j * LANE:(j + 1) * LANE] = (blk * qa_scale).astype(BF16)
    for j in range(SWA_HEADS * SWA_DH // LANE):
        blk = qb[:, j * LANE:(j + 1) * LANE]
        if rope:
            blk = _rope_block(blk, cs_ref[...], ss_ref[...], SWA_DH // 4)
        qb_ref[0, :, j * LANE:(j + 1) * LANE] = (blk * qb_scale).astype(BF16)
    if rope:
        kr = _rope_block(kr, cm_ref[...], sm_ref[...], MLA_ROPE // 4)
        kb = _rope_block(kb, cs_ref[...], ss_ref[...], SWA_DH // 4)
    ckv_ref[0] = ckv.astype(BF16)
    kr_ref[0] = kr.astype(BF16)
    kb_ref[0] = kb.astype(BF16)
    za_ref[0] = u[:, o["za"]:o["za"] + 512].astype(BF16)
    zb_ref[0] = u[:, o["zb"]:o["zb"] + 512].astype(BF16)
    vbt_ref[0] = _dot_nt(wvbt_ref[...], h).astype(BF16)


def _row_tile(t):
    return 512 if t % 512 == 0 else 256


def _inproj0(x, mod, ln, w, wvbt, qn, wuq, kvn, tables):
    b, t, d = x.shape
    tr = _row_tile(t)
    rope = tables is not None
    bm = mod.shape[0]
    full = lambda a: pl.BlockSpec(a.shape, lambda i, j: (0,) * a.ndim)
    rows = lambda c: pl.BlockSpec((1, tr, c), lambda i, j: (i, j, 0))
    in_specs = [rows(d), pl.BlockSpec((1, 3, d), (lambda i, j: (i, 0, 0)) if bm > 1 else (lambda i, j: (0, 0, 0))),
                full(ln), full(w), full(wvbt), full(qn), full(wuq), full(kvn)]
    args = [x, mod, ln, w, wvbt, qn, wuq, kvn]
    if rope:
        for tab in tables:
            in_specs.append(pl.BlockSpec((tr, LANE), lambda i, j: (j, 0)))
            args.append(tab)
    out_shape = [jax.ShapeDtypeStruct((b, t, 1024), BF16), jax.ShapeDtypeStruct((b, t, 256), BF16),
                 jax.ShapeDtypeStruct((b, t, 128), BF16), jax.ShapeDtypeStruct((b, t, 512), BF16),
                 jax.ShapeDtypeStruct((b, t, 512), BF16), jax.ShapeDtypeStruct((b, t, 128), BF16),
                 jax.ShapeDtypeStruct((b, 128, t), BF16), jax.ShapeDtypeStruct((b, t, 512), BF16)]
    out_specs = [rows(1024), rows(256), rows(128), rows(512), rows(512), rows(128),
                 pl.BlockSpec((1, 128, tr), lambda i, j: (i, 0, j)), rows(512)]
    if not rope:
        out_shape += [jax.ShapeDtypeStruct((b, t, 256), F32), jax.ShapeDtypeStruct((b, t, 32), F32),
                      jax.ShapeDtypeStruct((b, t, 128), F32), jax.ShapeDtypeStruct((b, t, 128), F32)]
        out_specs += [rows(256), rows(32), rows(128), rows(128)]
    return pl.pallas_call(
        functools.partial(_inproj0_kernel, rope=rope),
        grid=(b, t // tr), in_specs=in_specs, out_specs=out_specs, out_shape=out_shape,
        compiler_params=_cparams(2), name="inproj0_dec" if rope else "inproj0_ctx",
    )(*args)


def _attend_t(q, k_segs, vt_segs, masks=None, sink=None):
    s = [_dot_nt(k, q) for k in k_segs]
    if masks is not None:
        s = [si if mi is None else jnp.where(mi, si, NEG) for si, mi in zip(s, masks)]
    m = s[0].max(axis=0, keepdims=True)
    for si in s[1:]:
        m = jnp.maximum(m, si.max(axis=0, keepdims=True))
    if sink is not None:
        m = jnp.maximum(m, sink)
    l = None
    ot = None
    for si, vt in zip(s, vt_segs):
        p = jnp.exp2(si - m)
        ls = p.sum(axis=0, keepdims=True)
        l = ls if l is None else l + ls
        pv = _dot(vt, p.astype(BF16))
        ot = pv if ot is None else ot + pv
    if sink is not None:
        l = l + jnp.exp2(sink - m)
    return ot * (1.0 / l)


def _mla_kernel(*refs, n_new, n_ctx, hp):
    if n_ctx:
        q_ref, ckv_ref, kr_ref, ckvc_ref, krc_ref, wuk_ref, wuvt_ref, o_ref, k_s, vt_s = refs
    else:
        q_ref, ckv_ref, kr_ref, wuk_ref, wuvt_ref, o_ref, k_s, vt_s = refs
    qi, gi = pl.program_id(1), pl.program_id(2)

    @pl.when((qi == 0) & (gi == 0))
    def _():
        def expand(ckv, kr, r0, n):
            kn = _dot(ckv, wuk_ref[...])
            for j in range(MLA_HEADS):
                k_s[j, r0:r0 + n, :] = (kn[:, j * LANE:(j + 1) * LANE] + kr).astype(BF16)
            vt_s[:, r0:r0 + n] = _dot_nt(wuvt_ref[...], ckv).astype(BF16)

        blk = 512 if n_new % 512 == 0 else 256
        for r0 in range(0, n_new, blk):
            expand(ckv_ref[0, r0:r0 + blk, :], kr_ref[0, r0:r0 + blk, :].astype(F32), r0, blk)
        if n_ctx:
            expand(ckvc_ref[0].astype(BF16), krc_ref[0], n_new, n_ctx)

    outs = []
    for j in range(hp):
        h = gi * hp + j
        q = q_ref[0, :, j * LANE:(j + 1) * LANE]
        vt = vt_s[pl.ds(pl.multiple_of(h * MLA_V, MLA_V), MLA_V), :]
        outs.append(_attend_t(q, [k_s[h]], [vt]))
    ot = jnp.concatenate(outs, axis=0)
    o_ref[0] = ot.T.astype(BF16)


def _mla(q, ckv, kr, ckv_c, kr_c, wuk, wuvt, tq):
    b, t, _ = q.shape
    n_ctx = 0 if ckv_c is None else ckv_c.shape[1]
    hp = 2
    tk = t + n_ctx
    rows_q = pl.BlockSpec((1, tq, hp * LANE), lambda i, j, g: (i, j, g))
    per_b = lambda a: pl.BlockSpec((1,) + a.shape[1:], lambda i, j, g: (i, 0, 0))
    full = lambda a: pl.BlockSpec(a.shape, lambda i, j, g: (0, 0))
    in_specs = [rows_q, per_b(ckv), per_b(kr)]
    args = [q, ckv, kr]
    if n_ctx:
        in_specs += [per_b(ckv_c), per_b(kr_c)]
        args += [ckv_c, kr_c]
    in_specs += [full(wuk), full(wuvt)]
    args += [wuk, wuvt]
    return pl.pallas_call(
        functools.partial(_mla_kernel, n_new=t, n_ctx=n_ctx, hp=hp),
        grid=(b, t // tq, MLA_HEADS // hp), in_specs=in_specs,
        out_specs=pl.BlockSpec((1, tq, hp * MLA_V), lambda i, j, g: (i, j, g)),
        out_shape=jax.ShapeDtypeStruct((b, t, MLA_HEADS * MLA_V), BF16),
        scratch_shapes=[pltpu.VMEM((MLA_HEADS, tk, LANE), BF16), pltpu.VMEM((MLA_HEADS * MLA_V, tk), BF16)],
        compiler_params=_cparams(3), name="mla_dec" if n_ctx else "mla_ctx",
    )(*args)


def _swa_kernel(*refs, n_new, n_ctx, tq):
    if n_ctx:
        sink_ref, q_ref, k_ref, vt_ref, kc_ref, vct_ref, o_ref = refs
    else:
        sink_ref, q_ref, k_ref, vt_ref, o_ref = refs
    qi = pl.program_id(1)
    grp = SWA_HEADS // SWA_KV_HEADS
    if n_ctx:
        span = tq + 2 * WINDOW
        q0 = qi * tq
        start = pl.multiple_of(jnp.clip(q0 - WINDOW, 0, n_new - span), LANE)
        kpos = start + lax.broadcasted_iota(jnp.int32, (span, tq), 0)
        qpos = q0 + lax.broadcasted_iota(jnp.int32, (span, tq), 1)
        band = jnp.abs(kpos - qpos) <= WINDOW
    outs = []
    for g in range(SWA_KV_HEADS):
        if n_ctx:
            k_segs = [k_ref[0, pl.ds(start, span), g * SWA_DH:(g + 1) * SWA_DH],
                      kc_ref[0, :, g * SWA_DH:(g + 1) * SWA_DH].astype(BF16)]
            vt_segs = [vt_ref[0, g * SWA_DH:(g + 1) * SWA_DH, pl.ds(start, span)],
                       vct_ref[0, g * SWA_DH:(g + 1) * SWA_DH, :].astype(BF16)]
            masks = [band, None]
        else:
            k_segs = [k_ref[0, :, g * SWA_DH:(g + 1) * SWA_DH]]
            vt_segs = [vt_ref[0, g * SWA_DH:(g + 1) * SWA_DH, :]]
            masks = None
        for j in range(grp):
            h = g * grp + j
            q = q_ref[0, :, h * SWA_DH:(h + 1) * SWA_DH]
            outs.append(_attend_t(q, k_segs, vt_segs, masks, sink_ref[h] * LOG2E))
    o_ref[0] = jnp.concatenate(outs, axis=0).T.astype(BF16)


def _swa(sink, q, k, vt, k_c, v_ct, tq):
    b, t, _ = q.shape
    n_ctx = 0 if k_c is None else k_c.shape[1]
    per_b = lambda a: pl.BlockSpec((1,) + a.shape[1:], lambda i, j: (i, 0, 0))
    in_specs = [pl.BlockSpec(memory_space=pltpu.SMEM), pl.BlockSpec((1, tq, 512), lambda i, j: (i, j, 0)),
                per_b(k), per_b(vt)]
    args = [sink, q, k, vt]
    if n_ctx:
        in_specs += [per_b(k_c), per_b(v_ct)]
        args += [k_c, v_ct]
    return pl.pallas_call(
        functools.partial(_swa_kernel, n_new=t, n_ctx=n_ctx, tq=tq),
        grid=(b, t // tq), in_specs=in_specs,
        out_specs=pl.BlockSpec((1, tq, 512), lambda i, j: (i, j, 0)),
        out_shape=jax.ShapeDtypeStruct((b, t, 512), BF16),
        compiler_params=_cparams(2), name="swa_dec" if n_ctx else "swa_ctx",
    )(*args)


def _outproj0_kernel(oa_ref, za_ref, ob_ref, zb_ref, x_ref, mod_ref, w_ref, y_ref):
    ga = (oa_ref[0].astype(F32) * _silu(za_ref[0].astype(F32))).astype(BF16)
    gb = (ob_ref[0].astype(F32) * _silu(zb_ref[0].astype(F32))).astype(BF16)
    y = _dot(ga, w_ref[0:512, :]) + _dot(gb, w_ref[512:1024, :])
    y_ref[0] = x_ref[0] + mod_ref[0, 2:3, :] * y


def _outproj0(oa, za, ob, zb, x, mod, w):
    b, t, d = x.shape
    tr = _row_tile(t)
    bm = mod.shape[0]
    rows = lambda c: pl.BlockSpec((1, tr, c), lambda i, j: (i, j, 0))
    return pl.pallas_call(
        _outproj0_kernel, grid=(b, t // tr),
        in_specs=[rows(512), rows(512), rows(512), rows(512), rows(d),
                  pl.BlockSpec((1, 3, d), (lambda i, j: (i, 0, 0)) if bm > 1 else (lambda i, j: (0, 0, 0))),
                  pl.BlockSpec(w.shape, lambda i, j: (0, 0))],
        out_specs=rows(d), out_shape=jax.ShapeDtypeStruct((b, t, d), F32),
        compiler_params=_cparams(2), name="outproj0",
    )(oa, za, ob, zb, x, mod, w)


L1_OFF = dict(qkv=0, zc=1536, qd=2048, kd=2560, vd=2816, zd=3072, ab=3584)
L1_W = 3712


def _inproj1_kernel(*refs, rope):
    if rope:
        (x_ref, mod_ref, ln_ref, w_ref, wvdt_ref, qn_ref, kn_ref, c_ref, s_ref,
         qkv_ref, zc_ref, qd_ref, kd_ref, vdt_ref, zd_ref, ab_ref) = refs
    else:
        (x_ref, mod_ref, ln_ref, w_ref, wvdt_ref, qn_ref, kn_ref,
         qkv_ref, zc_ref, qd_ref, kd_ref, vdt_ref, zd_ref, ab_ref, kd32_ref, vd32_ref) = refs
    h = _adaln(x_ref[0], mod_ref, ln_ref).astype(BF16)
    u = _dot(h, w_ref[...])
    o = L1_OFF
    qkv_ref[0] = u[:, o["qkv"]:o["qkv"] + 1536].astype(BF16)
    zc_ref[0] = u[:, o["zc"]:o["zc"] + 512].astype(BF16)
    zd_ref[0] = u[:, o["zd"]:o["zd"] + 512].astype(BF16)
    ab_ref[0] = u[:, o["ab"]:o["ab"] + 16]
    qd_scale = ATT_DH ** -0.5 * LOG2E
    for j in range(ATT_HEADS):
        blk = _rms_rows(u[:, o["qd"] + j * LANE:o["qd"] + (j + 1) * LANE], qn_ref[...])
        if rope:
            blk = _rope_block(blk, c_ref[...], s_ref[...], ATT_DH // 4)
        qd_ref[0, :, j * LANE:(j + 1) * LANE] = (blk * qd_scale).astype(BF16)
    for j in range(ATT_KV_HEADS):
        blk = _rms_rows(u[:, o["kd"] + j * LANE:o["kd"] + (j + 1) * LANE], kn_ref[...])
        if rope:
            blk = _rope_block(blk, c_ref[...], s_ref[...], ATT_DH // 4)
        else:
            kd32_ref[0, :, j * LANE:(j + 1) * LANE] = blk
        kd_ref[0, :, j * LANE:(j + 1) * LANE] = blk.astype(BF16)
    if not rope:
        vd32_ref[0] = u[:, o["vd"]:o["vd"] + 256]
    vdt_ref[0] = _dot_nt(wvdt_ref[...], h).astype(BF16)


def _inproj1(x, mod, ln, w, wvdt, qn, kn, tables):
    b, t, d = x.shape
    tr = _row_tile(t)
    rope = tables is not None
    bm = mod.shape[0]
    full = lambda a: pl.BlockSpec(a.shape, lambda i, j: (0,) * a.ndim)
    rows = lambda c: pl.BlockSpec((1, tr, c), lambda i, j: (i, j, 0))
    in_specs = [rows(d), pl.BlockSpec((1, 3, d), (lambda i, j: (i, 0, 0)) if bm > 1 else (lambda i, j: (0, 0, 0))),
                full(ln), full(w), full(wvdt), full(qn), full(kn)]
    args = [x, mod, ln, w, wvdt, qn, kn]
    if rope:
        for tab in tables:
            in_specs.append(pl.BlockSpec((tr, LANE), lambda i, j: (j, 0)))
            args.append(tab)
    out_shape = [jax.ShapeDtypeStruct((b, t, 1536), BF16), jax.ShapeDtypeStruct((b, t, 512), BF16),
                 jax.ShapeDtypeStruct((b, t, 512), BF16), jax.ShapeDtypeStruct((b, t, 256), BF16),
                 jax.ShapeDtypeStruct((b, 256, t), BF16), jax.ShapeDtypeStruct((b, t, 512), BF16),
                 jax.ShapeDtypeStruct((b, t, 16), F32)]
    out_specs = [rows(1536), rows(512), rows(512), rows(256),
                 pl.BlockSpec((1, 256, tr), lambda i, j: (i, 0, j)), rows(512), rows(16)]
    if not rope:
        out_shape += [jax.ShapeDtypeStruct((b, t, 256), F32), jax.ShapeDtypeStruct((b, t, 256), F32)]
        out_specs += [rows(256), rows(256)]
    return pl.pallas_call(
        functools.partial(_inproj1_kernel, rope=rope),
        grid=(b, t // tr), in_specs=in_specs, out_specs=out_specs, out_shape=out_shape,
        compiler_params=_cparams(2), name="inproj1_dec" if rope else "inproj1_ctx",
    )(*args)


def _unit_tri_solve(x, rhs):
    r = rhs
    p = x
    for step in range(6):
        r = r + _dot3(p, r)
        if step < 5:
            p = _dot3(p, p)
    return r


def _gdn_chunk(q, k, v, gc_col, gc_row, beta_col, glast, s, upper):
    c = GDN_CHUNK
    row = lax.broadcasted_iota(jnp.int32, (c, c), 0)
    col = lax.broadcasted_iota(jnp.int32, (c, c), 1)
    causal = (row <= col) if upper else (row >= col)
    strict = (row < col) if upper else (row > col)
    decay = jnp.exp(jnp.where(causal, gc_col - gc_row, -jnp.inf))
    kb = k * beta_col
    kbf = k.astype(BF16)
    lmat = jnp.where(strict, _dot_nt(kb.astype(BF16), kbf) * decay, 0.0)
    intra = _dot_nt(q.astype(BF16), kbf) * decay
    egc = jnp.exp(gc_col)
    rhs = jnp.concatenate([v * beta_col, kb * egc], axis=1)
    sol = _unit_tri_solve(-lmat, rhs)
    u, w = sol[:, :GDN_DV], sol[:, GDN_DV:]
    sb = s.astype(BF16)
    v_new = u - _dot(w.astype(BF16), sb)
    vnb = v_new.astype(BF16)
    o = _dot((q * egc).astype(BF16), sb) + _dot(intra.astype(BF16), vnb)
    kd = (k * jnp.exp(glast - gc_col)).astype(BF16)
    s_new = s * jnp.exp(glast) + _dot_tn(kd, vnb)
    return o, s_new


def _gdn_kernel(qkv_ref, ab_ref, abt_ref, cw_ref, al_ref, dt_ref, alt_ref, dtt_ref, s0_ref, o_ref, sf_ref,
                x_s, gcol_s, grow_s, beta_s, st_s, *, t):
    c = GDN_CHUNK
    n = t // c
    nh = GDN_HEADS
    ab = ab_ref[0]
    gact = -jnp.exp(al_ref[...]) * jax.nn.softplus(ab + dt_ref[...])
    lane16 = lax.broadcasted_iota(jnp.int32, ab.shape, 1)
    beta_s[...] = jnp.where(lane16 < 2 * nh, gact, jax.nn.sigmoid(ab))
    rowt = lax.broadcasted_iota(jnp.int32, (t, LANE), 0)
    for j in range(3 * nh):
        x = qkv_ref[0, :, j * LANE:(j + 1) * LANE].astype(F32)
        xp = jnp.where(rowt == 0, 0.0, pltpu.roll(x, 1, 0))
        xn = jnp.where(rowt == t - 1, 0.0, pltpu.roll(x, t - 1, 0))
        cw = cw_ref[:, j * LANE:(j + 1) * LANE]
        y = _silu(xp * cw[0:1, :] + x * cw[1:2, :] + xn * cw[2:3, :])
        if j < 2 * nh:
            y = y * lax.rsqrt(jnp.sum(y * y, axis=-1, keepdims=True) + EPS)
        if j < nh:
            y = y * GDN_DK ** -0.5
        x_s[:, j * LANE:(j + 1) * LANE] = y
    r64 = lax.broadcasted_iota(jnp.int32, (c, c), 0)
    c64 = lax.broadcasted_iota(jnp.int32, (c, c), 1)
    tril = (r64 >= c64).astype(F32)
    triu = (r64 <= c64).astype(F32)
    lane_c = lax.broadcasted_iota(jnp.int32, (c, 16), 1)
    sub_c = lax.broadcasted_iota(jnp.int32, (16, c), 0)

    def cum_body(i, carry):
        r0 = pl.multiple_of(i * c, c)
        g = beta_s[pl.ds(r0, c), :]
        gcol_s[pl.ds(r0, c), :] = jnp.where(lane_c < nh, _dot_exact(tril, g), _dot_exact(triu, g))
        gt = -jnp.exp(alt_ref[...]) * jax.nn.softplus(abt_ref[0, i] + dtt_ref[...])
        grow_s[i] = jnp.where(sub_c < nh, _dot_exact(gt, triu), _dot_exact(gt, tril))
        return carry

    lax.fori_loop(0, n, cum_body, 0)
    for d in range(2):
        for hh in range(nh):
            st_s[d * nh + hh] = s0_ref[0, d, hh]
    o_ref[...] = jnp.zeros_like(o_ref)

    def body(i, carry):
        for d in range(2):
            ci = i if d == 0 else n - 1 - i
            r0 = pl.multiple_of(ci * c, c)
            gcol = gcol_s[pl.ds(r0, c), :]
            bet = beta_s[pl.ds(r0, c), :]
            grow = grow_s[ci]
            last = c - 1 if d == 0 else 0
            for hh in range(nh):
                ch = d * nh + hh
                q = x_s[pl.ds(r0, c), hh * LANE:(hh + 1) * LANE]
                k = x_s[pl.ds(r0, c), (nh + hh) * LANE:(nh + hh + 1) * LANE]
                v = x_s[pl.ds(r0, c), (2 * nh + hh) * LANE:(2 * nh + hh + 1) * LANE]
                o, s_new = _gdn_chunk(q, k, v, gcol[:, ch:ch + 1], grow[ch:ch + 1, :],
                                      bet[:, 2 * nh + ch:2 * nh + ch + 1], gcol[last:last + 1, ch:ch + 1],
                                      st_s[ch], upper=(d == 1))
                st_s[ch] = s_new
                o_ref[0, pl.ds(r0, c), hh * LANE:(hh + 1) * LANE] += o
        return carry

    lax.fori_loop(0, n, body, 0)
    for d in range(2):
        for hh in range(nh):
            sf_ref[0, d, hh] = st_s[d * nh + hh]


def _gdn(qkv, ab, abt, cw, al, dt, alt, dtt, s0):
    b, t, _ = qkv.shape
    n = t // GDN_CHUNK
    per_b = lambda a: pl.BlockSpec((1,) + a.shape[1:], lambda i: (i,) + (0,) * (a.ndim - 1))
    full = lambda a: pl.BlockSpec(a.shape, lambda i: (0,) * a.ndim)
    return pl.pallas_call(
        functools.partial(_gdn_kernel, t=t), grid=(b,),
        in_specs=[per_b(qkv), per_b(ab), per_b(abt), full(cw), full(al), full(dt), full(alt), full(dtt), per_b(s0)],
        out_specs=[pl.BlockSpec((1, t, GDN_HEADS * GDN_DV), lambda i: (i, 0, 0)), per_b(s0)],
        out_shape=[jax.ShapeDtypeStruct((b, t, GDN_HEADS * GDN_DV), F32), jax.ShapeDtypeStruct(s0.shape, F32)],
        scratch_shapes=[pltpu.VMEM((t, 3 * GDN_HEADS * LANE), F32), pltpu.VMEM((t, 16), F32),
                        pltpu.VMEM((n, 16, GDN_CHUNK), F32), pltpu.VMEM((t, 16), F32),
                        pltpu.VMEM((2 * GDN_HEADS, GDN_DK, GDN_DV), F32)],
        compiler_params=_cparams(1), name="gdn",
    )(qkv, ab, abt, cw, al, dt, alt, dtt, s0)


def _attd_kernel(*refs, n_ctx):
    if n_ctx:
        q_ref, k_ref, vt_ref, kc_ref, vct_ref, o_ref = refs
    else:
        q_ref, k_ref, vt_ref, o_ref = refs
    grp = ATT_HEADS // ATT_KV_HEADS
    outs = []
    for g in range(ATT_KV_HEADS):
        k_segs = [k_ref[0, :, g * LANE:(g + 1) * LANE]]
        vt_segs = [vt_ref[0, g * LANE:(g + 1) * LANE, :]]
        if n_ctx:
            k_segs.append(kc_ref[0, :, g * LANE:(g + 1) * LANE].astype(BF16))
            vt_segs.append(vct_ref[0, g * LANE:(g + 1) * LANE, :].astype(BF16))
        for j in range(grp):
            h = g * grp + j
            outs.append(_attend_t(q_ref[0, :, h * LANE:(h + 1) * LANE], k_segs, vt_segs))
    o_ref[0] = jnp.concatenate(outs, axis=0).T.astype(BF16)


def _attd(q, k, vt, k_c, v_ct, tq):
    b, t, _ = q.shape
    n_ctx = 0 if k_c is None else k_c.shape[1]
    per_b = lambda a: pl.BlockSpec((1,) + a.shape[1:], lambda i, j: (i, 0, 0))
    in_specs = [pl.BlockSpec((1, tq, 512), lambda i, j: (i, j, 0)), per_b(k), per_b(vt)]
    args = [q, k, vt]
    if n_ctx:
        in_specs += [per_b(k_c), per_b(v_ct)]
        args += [k_c, v_ct]
    return pl.pallas_call(
        functools.partial(_attd_kernel, n_ctx=n_ctx),
        grid=(b, t // tq), in_specs=in_specs,
        out_specs=pl.BlockSpec((1, tq, 512), lambda i, j: (i, j, 0)),
        out_shape=jax.ShapeDtypeStruct((b, t, 512), BF16),
        compiler_params=_cparams(2), name="attd_dec" if n_ctx else "attd_ctx",
    )(*args)


def _outproj1_kernel(oc_ref, zc_ref, od_ref, zd_ref, x_ref, mod_ref, gn_ref, w_ref, lnf_ref, y_ref):
    zc = zc_ref[0].astype(F32)
    parts = []
    for j in range(GDN_HEADS):
        oc = _rms_rows(oc_ref[0, :, j * LANE:(j + 1) * LANE], gn_ref[...])
        parts.append((oc * _silu(zc[:, j * LANE:(j + 1) * LANE])).astype(BF16))
    gc = jnp.concatenate(parts, axis=1)
    gd = (od_ref[0].astype(F32) * _silu(zd_ref[0].astype(F32))).astype(BF16)
    y = _dot(gc, w_ref[0:512, :]) + _dot(gd, w_ref[512:1024, :])
    x2 = x_ref[0] + mod_ref[0, 2:3, :] * y
    y_ref[0] = _rms_rows(x2, lnf_ref[...])


def _outproj1(oc, zc, od, zd, x, mod, gn, w, lnf):
    b, t, d = x.shape
    tr = _row_tile(t)
    bm = mod.shape[0]
    rows = lambda c: pl.BlockSpec((1, tr, c), lambda i, j: (i, j, 0))
    full = lambda a: pl.BlockSpec(a.shape, lambda i, j: (0,) * a.ndim)
    return pl.pallas_call(
        _outproj1_kernel, grid=(b, t // tr),
        in_specs=[rows(512), rows(512), rows(512), rows(512), rows(d),
                  pl.BlockSpec((1, 3, d), (lambda i, j: (i, 0, 0)) if bm > 1 else (lambda i, j: (0, 0, 0))),
                  full(gn), full(w), full(lnf)],
        out_specs=rows(d), out_shape=jax.ShapeDtypeStruct((b, t, d), F32),
        compiler_params=_cparams(2), name="outproj1",
    )(oc, zc, od, zd, x, mod, gn, w, lnf)


def _rope_table(n_tok, rot_dim, used, block=LANE):
    quarter = rot_dim // 4
    inv = ROPE_THETA ** (-jnp.arange(quarter, dtype=F32) / quarter)
    tt = jnp.arange(n_tok)
    pos = jnp.stack([tt // GRID_W, tt % GRID_W], axis=-1).astype(F32)
    ang = pos[:, :, None] * inv
    cos, sin = jnp.cos(ang), jnp.sin(ang)
    c = jnp.concatenate([cos, cos], axis=-1).reshape(n_tok, rot_dim)
    s = jnp.concatenate([-sin, sin], axis=-1).reshape(n_tok, rot_dim)
    return c, s


def _place(tab, fill, off, width):
    n = tab.shape[0]
    left = jnp.full((n, off), fill, F32)
    right = jnp.full((n, width - off - tab.shape[1]), fill, F32)
    return jnp.concatenate([left, tab, right], axis=1)


def _prep_l0(w_in0, w_uq, w_ukv):
    d = w_in0.shape[0]
    cq, ckv, kr, za, qb, kb, vb, zb = jnp.split(w_in0, [384, 640, 672, 1184, 1696, 1824, 1952], axis=1)
    kr_pad = jnp.concatenate([jnp.zeros((d, 64), F32), kr, jnp.zeros((d, 32), F32)], axis=1)
    w = jnp.concatenate([cq, ckv, kr_pad, za, qb, kb, vb, zb], axis=1).astype(BF16)
    wvbt = vb.T.astype(BF16)
    uq = w_uq.reshape(MLA_Q_LORA, MLA_HEADS, MLA_NOPE + MLA_ROPE)
    wuq = jnp.pad(uq, ((0, 0), (0, 0), (0, LANE - MLA_NOPE - MLA_ROPE))).reshape(MLA_Q_LORA, MLA_HEADS * LANE)
    ukv = w_ukv.reshape(MLA_KV_LORA, MLA_HEADS, MLA_NOPE + MLA_V)
    wuk = jnp.pad(ukv[:, :, :MLA_NOPE], ((0, 0), (0, 0), (0, LANE - MLA_NOPE))).reshape(MLA_KV_LORA, MLA_HEADS * LANE)
    wuvt = ukv[:, :, MLA_NOPE:].reshape(MLA_KV_LORA, MLA_HEADS * MLA_V).T
    return w, wvbt, wuq.astype(BF16), wuk.astype(BF16), wuvt.astype(BF16)


def _prep_l1(w_in1):
    d = w_in1.shape[0]
    qkv, a, bb, zc, qd, kd, vd, zd = jnp.split(w_in1, [1536, 1544, 1552, 2064, 2576, 2832, 3088], axis=1)
    w = jnp.concatenate([qkv, zc, qd, kd, vd, zd, a, bb, jnp.zeros((d, 112), F32)], axis=1).astype(BF16)
    return w, vd.T.astype(BF16)


def _chunk_rows(ab):
    b, t, c = ab.shape
    return jnp.swapaxes(ab.reshape(b, t // GDN_CHUNK, GDN_CHUNK, c), 2, 3)


def _trunk(x, mod, caches, p, tables, tq):
    dec = caches is not None
    t0m, t0s, t1 = tables if dec else (None, None, None)
    (qa, ckv, kr, za, qb, kb, vbt, zb, *ctx0) = _inproj0(
        x, mod[0], p["ln0"], p["w0"], p["wvbt"], p["qn"], p["wuq"], p["kvn"], (t0m + t0s) if dec else None)
    if dec:
        ckv_c, kr_c, kb_c, vb_ct, s0, kd_c, vd_ct = caches
    else:
        ckv_c = kr_c = kb_c = vb_ct = kd_c = vd_ct = None
        s0 = jnp.zeros((x.shape[0], 2, GDN_HEADS, GDN_DK, GDN_DV), F32)
    oa = _mla(qa, ckv, kr, ckv_c, kr_c, p["wuk"], p["wuvt"], tq)
    ob = _swa(p["sink"], qb, kb, vbt, kb_c, vb_ct, tq)
    x1 = _outproj0(oa, za, ob, zb, x, mod[0], p["wout0"])
    (qkv, zc, qd, kd, vdt, zd, ab, *ctx1) = _inproj1(
        x1, mod[1], p["ln1"], p["w1"], p["wvdt"], p["aqn"], p["akn"], t1 if dec else None)
    oc, sfin = _gdn(qkv, ab, _chunk_rows(ab), p["cw"], p["al"], p["dt"], p["alt"], p["dtt"], s0)
    od = _attd(qd, kd, vdt, kd_c, vd_ct, tq)
    y = _outproj1(oc, zc, od, zd, x1, mod[1], p["gn"], p["wout1"], p["lnf"])
    return y, ctx0, sfin, ctx1


def kernel(x_prompt, x_sample, cache_l0_mla_ckv, cache_l0_mla_krope, cache_l0_swa_k, cache_l0_swa_v,
           state_l1_gdn, cache_l1_attn_k, cache_l1_attn_v, c, c_ctx,
           w_mod0, b_mod0, ln0, w_in0, mla_q_norm, w_uq, mla_kv_norm, w_ukv, swa_sink, w_out0,
           w_mod1, b_mod1, ln1, w_in1, gdn_conv, gdn_a_log, gdn_dt_bias, gdn_norm, att_q_norm, att_k_norm, w_out1,
           ln_f):
    d = x_prompt.shape[-1]
    bd, td = x_sample.shape[:2]
    bc, tc = x_prompt.shape[:2]
    past = cache_l0_mla_ckv.shape[1]
    row = lambda v: v.reshape(1, -1)
    w0, wvbt, wuq, wuk, wuvt = _prep_l0(w_in0, w_uq, w_ukv)
    w1, wvdt = _prep_l1(w_in1)
    al8 = gdn_a_log.reshape(1, 2 * GDN_HEADS)
    dt8 = gdn_dt_bias.reshape(1, 2 * GDN_HEADS)
    al16 = jnp.pad(al8, ((0, 0), (0, 8)))
    dt16 = jnp.pad(dt8, ((0, 0), (0, 8)))
    p = dict(ln0=row(ln0), w0=w0, wvbt=wvbt, qn=row(mla_q_norm), wuq=wuq, kvn=row(mla_kv_norm), wuk=wuk, wuvt=wuvt,
             sink=swa_sink, wout0=w_out0.astype(BF16), ln1=row(ln1), w1=w1, wvdt=wvdt, aqn=row(att_q_norm),
             akn=row(att_k_norm), cw=gdn_conv, al=al16, dt=dt16, alt=al16.T, dtt=dt16.T, gn=row(gdn_norm),
             wout1=w_out1.astype(BF16), lnf=row(ln_f))
    n_rows = -(-(bd + 1) // 8) * 8
    c_rows = jnp.concatenate([c, c_ctx[None, :], jnp.zeros((n_rows - bd - 1, d), F32)], axis=0)
    mods = [_mod(c_rows, w_mod0, b_mod0), _mod(c_rows, w_mod1, b_mod1)]
    mod_dec = [m[:bd].reshape(bd, 3, d) for m in mods]
    mod_ctx = [m[bd:bd + 1].reshape(1, 3, d) for m in mods]
    cm, sm = _rope_table(td, MLA_ROPE, LANE)
    t0m = (_place(cm, 1.0, MLA_NOPE, LANE), _place(sm, 0.0, MLA_NOPE, LANE))
    cs, ss = _rope_table(td, SWA_DH, SWA_DH)
    t0s = (jnp.tile(cs, (1, LANE // SWA_DH)), jnp.tile(ss, (1, LANE // SWA_DH)))
    t1 = _rope_table(td, ATT_DH, LANE)
    caches = (cache_l0_mla_ckv,
              jnp.pad(cache_l0_mla_krope, ((0, 0), (0, 0), (MLA_NOPE, LANE - MLA_NOPE - MLA_ROPE))),
              cache_l0_swa_k.reshape(bd, past, SWA_KV_HEADS * SWA_DH),
              jnp.swapaxes(cache_l0_swa_v.reshape(bd, past, SWA_KV_HEADS * SWA_DH), 1, 2),
              state_l1_gdn,
              cache_l1_attn_k.reshape(bd, past, ATT_KV_HEADS * ATT_DH),
              jnp.swapaxes(cache_l1_attn_v.reshape(bd, past, ATT_KV_HEADS * ATT_DH), 1, 2))
    y_prompt, ctx0, sfin, ctx1 = _trunk(x_prompt, mod_ctx, None, p, None, tq=tc)
    y_sample, _, _, _ = _trunk(x_sample, mod_dec, caches, p, (t0m, t0s, t1), tq=256)
    ckv32, kr32, kb32, vb32 = ctx0
    kd32, vd32 = ctx1
    return (y_prompt, y_sample, ckv32, kr32,
            kb32.reshape(bc, tc, SWA_KV_HEADS, SWA_DH), vb32.reshape(bc, tc, SWA_KV_HEADS, SWA_DH),
            sfin, kd32.reshape(bc, tc, ATT_KV_HEADS, ATT_DH), vd32.reshape(bc, tc, ATT_KV_HEADS, ATT_DH))
```

```python
import functools
import math

import jax
import jax.numpy as jnp
from jax import lax
from jax.experimental import pallas as pl
from jax.experimental.pallas import tpu as pltpu

F32 = jnp.float32
BF16 = jnp.bfloat16

GRID_W = 64
ROPE_THETA = 10000.0
EPS = 1e-6
WINDOW = 128
MLA_HEADS, MLA_NOPE, MLA_ROPE, MLA_V = 8, 64, 32, 64
MLA_Q_LORA, MLA_KV_LORA = 384, 256
SWA_HEADS, SWA_KV_HEADS, SWA_DH = 8, 2, 64
GDN_HEADS, GDN_DK, GDN_DV, CONV_K, GDN_CHUNK = 4, 128, 128, 3, 64
ATT_HEADS, ATT_KV_HEADS, ATT_DH = 4, 2, 128
LANE = 128
LOG2E = math.log2(math.e)
NEG = -1e30
VMEM_LIMIT = 56 * 1024 * 1024


def _cparams(n_axes):
    return pltpu.CompilerParams(dimension_semantics=("arbitrary",) * n_axes, vmem_limit_bytes=VMEM_LIMIT)


def _dot(a, b):
    return jnp.dot(a, b, preferred_element_type=F32)


def _dot_nt(a, b):
    return lax.dot_general(a, b, (((1,), (1,)), ((), ())), preferred_element_type=F32)


def _dot_tn(a, b):
    return lax.dot_general(a, b, (((0,), (0,)), ((), ())), preferred_element_type=F32)


def _dot_exact(a, b):
    return jnp.dot(a, b, preferred_element_type=F32, precision=lax.Precision.HIGHEST)


def _split(a):
    hi = a.astype(BF16)
    return hi, (a - hi.astype(F32)).astype(BF16)


def _silu(x):
    return x * jax.nn.sigmoid(x)


def _rms_rows(x, g):
    return x * lax.rsqrt(jnp.mean(x * x, axis=-1, keepdims=True) + EPS) * g


def _rope_block(x, cos, sin, half):
    lane = lax.broadcasted_iota(jnp.int32, x.shape, 1)
    first = (lane // half) % 2 == 0
    partner = jnp.where(first, pltpu.roll(x, LANE - half, 1), pltpu.roll(x, half, 1))
    return x * cos + partner * sin


def _mod_kernel(c_ref, w_ref, b_ref, o_ref):
    a = _silu(c_ref[...]).astype(BF16)
    o_ref[...] = _dot(a, w_ref[...].astype(BF16)) + b_ref[...]


def _mod(c_rows, w_mod, b_mod):
    r, d = c_rows.shape
    n = w_mod.shape[1]
    tn = 1024
    return pl.pallas_call(
        _mod_kernel,
        grid=(n // tn,),
        in_specs=[pl.BlockSpec((r, d), lambda j: (0, 0)),
                  pl.BlockSpec((d, tn), lambda j: (0, j)),
                  pl.BlockSpec((1, tn), lambda j: (0, j))],
        out_specs=pl.BlockSpec((r, tn), lambda j: (0, j)),
        out_shape=jax.ShapeDtypeStruct((r, n), F32),
        compiler_params=_cparams(1),
        name="mod",
    )(c_rows, w_mod, b_mod.reshape(1, n))


def _adaln(x, mod_ref, ln_ref):
    h = _rms_rows(x, ln_ref[...])
    return h * (1.0 + mod_ref[0, 1:2, :]) + mod_ref[0, 0:1, :]


L0_OFF = dict(cq=0, ckv=384, kr=640, za=768, qb=1280, kb=1792, vb=1920, zb=2048)
L0_W = 2560


def _inproj0_kernel(*refs, rope):
    if rope:
        (x_ref, mod_ref, ln_ref, w_ref, wvbt_ref, qn_ref, wuq_ref, kvn_ref, cm_ref, sm_ref, cs_ref, ss_ref,
         qa_ref, ckv_ref, kr_ref, za_ref, qb_ref, kb_ref, vbt_ref, zb_ref) = refs
    else:
        (x_ref, mod_ref, ln_ref, w_ref, wvbt_ref, qn_ref, wuq_ref, kvn_ref,
         qa_ref, ckv_ref, kr_ref, za_ref, qb_ref, kb_ref, vbt_ref, zb_ref,
         ckv32_ref, kr32_ref, kb32_ref, vb32_ref) = refs
    h = _adaln(x_ref[0], mod_ref, ln_ref).astype(BF16)
    u = _dot(h, w_ref[...])
    o = L0_OFF
    cq = _rms_rows(u[:, o["cq"]:o["cq"] + 384], qn_ref[...]).astype(BF16)
    qa = _dot(cq, wuq_ref[...])
    ckv = _rms_rows(u[:, o["ckv"]:o["ckv"] + 256], kvn_ref[...])
    kr = u[:, o["kr"]:o["kr"] + 128]
    qb = u[:, o["qb"]:o["qb"] + 512]
    kb = u[:, o["kb"]:o["kb"] + 128]
    vb = u[:, o["vb"]:o["vb"] + 128]
    if not rope:
        ckv32_ref[0] = ckv
        kr32_ref[0] = kr[:, 64:96]
        kb32_ref[0] = kb
        vb32_ref[0] = vb
    qa_scale = (MLA_NOPE + MLA_ROPE) ** -0.5 * LOG2E
    qb_scale = SWA_DH ** -0.5 * LOG2E
    for j in range(MLA_HEADS):
        blk = qa[:, j * LANE:(j + 1) * LANE]
        if rope:
            blk = _rope_block(blk, cm_ref[...], sm_ref[...], MLA_ROPE // 4)
        qa_ref[0, :, j * LANE:(j + 1) * LANE] = (blk * qa_scale).astype(BF16)
    for j in range(SWA_HEADS * SWA_DH // LANE):
        blk = qb[:, j * LANE:(j + 1) * LANE]
        if rope:
            blk = _rope_block(blk, cs_ref[...], ss_ref[...], SWA_DH // 4)
        qb_ref[0, :, j * LANE:(j + 1) * LANE] = (blk * qb_scale).astype(BF16)
    if rope:
        kr = _rope_block(kr, cm_ref[...], sm_ref[...], MLA_ROPE // 4)
        kb = _rope_block(kb, cs_ref[...], ss_ref[...], SWA_DH // 4)
    ckv_ref[0] = ckv.astype(BF16)
    kr_ref[0] = kr.astype(BF16)
    kb_ref[0] = kb.astype(BF16)
    za_ref[0] = u[:, o["za"]:o["za"] + 512].astype(BF16)
    zb_ref[0] = u[:, o["zb"]:o["zb"] + 512].astype(BF16)
    vbt_ref[0] = _dot_nt(wvbt_ref[...], h).astype(BF16)


def _row_tile(t):
    return 512 if t % 512 == 0 else 256


def _inproj0(x, mod, ln, w, wvbt, qn, wuq, kvn, tables):
    b, t, d = x.shape
    tr = _row_tile(t)
    rope = tables is not None
    bm = mod.shape[0]
    full = lambda a: pl.BlockSpec(a.shape, lambda i, j: (0,) * a.ndim)
    rows = lambda c: pl.BlockSpec((1, tr, c), lambda i, j: (i, j, 0))
    in_specs = [rows(d), pl.BlockSpec((1, 3, d), (lambda i, j: (i, 0, 0)) if bm > 1 else (lambda i, j: (0, 0, 0))),
                full(ln), full(w), full(wvbt), full(qn), full(wuq), full(kvn)]
    args = [x, mod, ln, w, wvbt, qn, wuq, kvn]
    if rope:
        for tab in tables:
            in_specs.append(pl.BlockSpec((tr, LANE), lambda i, j: (j, 0)))
            args.append(tab)
    out_shape = [jax.ShapeDtypeStruct((b, t, 1024), BF16), jax.ShapeDtypeStruct((b, t, 256), BF16),
                 jax.ShapeDtypeStruct((b, t, 128), BF16), jax.ShapeDtypeStruct((b, t, 512), BF16),
                 jax.ShapeDtypeStruct((b, t, 512), BF16), jax.ShapeDtypeStruct((b, t, 128), BF16),
                 jax.ShapeDtypeStruct((b, 128, t), BF16), jax.ShapeDtypeStruct((b, t, 512), BF16)]
    out_specs = [rows(1024), rows(256), rows(128), rows(512), rows(512), rows(128),
                 pl.BlockSpec((1, 128, tr), lambda i, j: (i, 0, j)), rows(512)]
    if not rope:
        out_shape += [jax.ShapeDtypeStruct((b, t, 256), F32), jax.ShapeDtypeStruct((b, t, 32), F32),
                      jax.ShapeDtypeStruct((b, t, 128), F32), jax.ShapeDtypeStruct((b, t, 128), F32)]
        out_specs += [rows(256), rows(32), rows(128), rows(128)]
    return pl.pallas_call(
        functools.partial(_inproj0_kernel, rope=rope),
        grid=(b, t // tr), in_specs=in_specs, out_specs=out_specs, out_shape=out_shape,
        compiler_params=_cparams(2), name="inproj0_dec" if rope else "inproj0_ctx",
    )(*args)


def _attend_t(q, k_segs, vt_segs, masks=None, sink=None):
    s = [_dot_nt(k, q) for k in k_segs]
    if masks is not None:
        s = [si if mi is None else jnp.where(mi, si, NEG) for si, mi in zip(s, masks)]
    m = s[0].max(axis=0, keepdims=True)
    for si in s[1:]:
        m = jnp.maximum(m, si.max(axis=0, keepdims=True))
    if sink is not None:
        m = jnp.maximum(m, sink)
    l = None
    ot = None
    for si, vt in zip(s, vt_segs):
        p = jnp.exp2(si - m)
        ls = p.sum(axis=0, keepdims=True)
        l = ls if l is None else l + ls
        pv = _dot(vt, p.astype(BF16))
        ot = pv if ot is None else ot + pv
    if sink is not None:
        l = l + jnp.exp2(sink - m)
    return ot * (1.0 / l)


def _mla_kernel(*refs, n_new, n_ctx, hp):
    if n_ctx:
        q_ref, ckv_ref, kr_ref, ckvc_ref, krc_ref, wuk_ref, wuvt_ref, o_ref, k_s, vt_s = refs
    else:
        q_ref, ckv_ref, kr_ref, wuk_ref, wuvt_ref, o_ref, k_s, vt_s = refs
    qi, gi = pl.program_id(1), pl.program_id(2)

    @pl.when((qi == 0) & (gi == 0))
    def _():
        def expand(ckv, kr, r0, n):
            kn = _dot(ckv, wuk_ref[...])
            for j in range(MLA_HEADS):
                k_s[j, r0:r0 + n, :] = (kn[:, j * LANE:(j + 1) * LANE] + kr).astype(BF16)
            vt_s[:, r0:r0 + n] = _dot_nt(wuvt_ref[...], ckv).astype(BF16)

        blk = 512 if n_new % 512 == 0 else 256
        for r0 in range(0, n_new, blk):
            expand(ckv_ref[0, r0:r0 + blk, :], kr_ref[0, r0:r0 + blk, :].astype(F32), r0, blk)
        if n_ctx:
            expand(ckvc_ref[0].astype(BF16), krc_ref[0], n_new, n_ctx)

    outs = []
    for j in range(hp):
        h = gi * hp + j
        q = q_ref[0, :, j * LANE:(j + 1) * LANE]
        vt = vt_s[pl.ds(pl.multiple_of(h * MLA_V, MLA_V), MLA_V), :]
        outs.append(_attend_t(q, [k_s[h]], [vt]))
    ot = jnp.concatenate(outs, axis=0)
    o_ref[0] = ot.T.astype(BF16)


def _mla(q, ckv, kr, ckv_c, kr_c, wuk, wuvt, tq):
    b, t, _ = q.shape
    n_ctx = 0 if ckv_c is None else ckv_c.shape[1]
    hp = 2
    tk = t + n_ctx
    rows_q = pl.BlockSpec((1, tq, hp * LANE), lambda i, j, g: (i, j, g))
    per_b = lambda a: pl.BlockSpec((1,) + a.shape[1:], lambda i, j, g: (i, 0, 0))
    full = lambda a: pl.BlockSpec(a.shape, lambda i, j, g: (0, 0))
    in_specs = [rows_q, per_b(ckv), per_b(kr)]
    args = [q, ckv, kr]
    if n_ctx:
        in_specs += [per_b(ckv_c), per_b(kr_c)]
        args += [ckv_c, kr_c]
    in_specs += [full(wuk), full(wuvt)]
    args += [wuk, wuvt]
    return pl.pallas_call(
        functools.partial(_mla_kernel, n_new=t, n_ctx=n_ctx, hp=hp),
        grid=(b, t // tq, MLA_HEADS // hp), in_specs=in_specs,
        out_specs=pl.BlockSpec((1, tq, hp * MLA_V), lambda i, j, g: (i, j, g)),
        out_shape=jax.ShapeDtypeStruct((b, t, MLA_HEADS * MLA_V), BF16),
        scratch_shapes=[pltpu.VMEM((MLA_HEADS, tk, LANE), BF16), pltpu.VMEM((MLA_HEADS * MLA_V, tk), BF16)],
        compiler_params=_cparams(3), name="mla_dec" if n_ctx else "mla_ctx",
    )(*args)


def _swa_kernel(*refs, n_new, n_ctx, tq):
    if n_ctx:
        sink_ref, q_ref, k_ref, vt_ref, kc_ref, vct_ref, o_ref = refs
    else:
        sink_ref, q_ref, k_ref, vt_ref, o_ref = refs
    qi = pl.program_id(1)
    grp = SWA_HEADS // SWA_KV_HEADS
    if n_ctx:
        span = tq + 2 * WINDOW
        q0 = qi * tq
        start = pl.multiple_of(jnp.clip(q0 - WINDOW, 0, n_new - span), LANE)
        kpos = start + lax.broadcasted_iota(jnp.int32, (span, tq), 0)
        qpos = q0 + lax.broadcasted_iota(jnp.int32, (span, tq), 1)
        band = jnp.abs(kpos - qpos) <= WINDOW
    outs = []
    for g in range(SWA_KV_HEADS):
        if n_ctx:
            k_segs = [k_ref[0, pl.ds(start, span), g * SWA_DH:(g + 1) * SWA_DH],
                      kc_ref[0, :, g * SWA_DH:(g + 1) * SWA_DH].astype(BF16)]
            vt_segs = [vt_ref[0, g * SWA_DH:(g + 1) * SWA_DH, pl.ds(start, span)],
                       vct_ref[0, g * SWA_DH:(g + 1) * SWA_DH, :].astype(BF16)]
            masks = [band, None]
        else:
            k_segs = [k_ref[0, :, g * SWA_DH:(g + 1) * SWA_DH]]
            vt_segs = [vt_ref[0, g * SWA_DH:(g + 1) * SWA_DH, :]]
            masks = None
        for j in range(grp):
            h = g * grp + j
            q = q_ref[0, :, h * SWA_DH:(h + 1) * SWA_DH]
            outs.append(_attend_t(q, k_segs, vt_segs, masks, sink_ref[h] * LOG2E))
    o_ref[0] = jnp.concatenate(outs, axis=0).T.astype(BF16)


def _swa(sink, q, k, vt, k_c, v_ct, tq):
    b, t, _ = q.shape
    n_ctx = 0 if k_c is None else k_c.shape[1]
    per_b = lambda a: pl.BlockSpec((1,) + a.shape[1:], lambda i, j: (i, 0, 0))
    in_specs = [pl.BlockSpec(memory_space=pltpu.SMEM), pl.BlockSpec((1, tq, 512), lambda i, j: (i, j, 0)),
                per_b(k), per_b(vt)]
    args = [sink, q, k, vt]
    if n_ctx:
        in_specs += [per_b(k_c), per_b(v_ct)]
        args += [k_c, v_ct]
    return pl.pallas_call(
        functools.partial(_swa_kernel, n_new=t, n_ctx=n_ctx, tq=tq),
        grid=(b, t // tq), in_specs=in_specs,
        out_specs=pl.BlockSpec((1, tq, 512), lambda i, j: (i, j, 0)),
        out_shape=jax.ShapeDtypeStruct((b, t, 512), BF16),
        compiler_params=_cparams(2), name="swa_dec" if n_ctx else "swa_ctx",
    )(*args)


def _outproj0_kernel(oa_ref, za_ref, ob_ref, zb_ref, x_ref, mod_ref, w_ref, y_ref):
    ga = (oa_ref[0].astype(F32) * _silu(za_ref[0].astype(F32))).astype(BF16)
    gb = (ob_ref[0].astype(F32) * _silu(zb_ref[0].astype(F32))).astype(BF16)
    y = _dot(ga, w_ref[0:512, :]) + _dot(gb, w_ref[512:1024, :])
    y_ref[0] = x_ref[0] + mod_ref[0, 2:3, :] * y


def _outproj0(oa, za, ob, zb, x, mod, w):
    b, t, d = x.shape
    tr = _row_tile(t)
    bm = mod.shape[0]
    rows = lambda c: pl.BlockSpec((1, tr, c), lambda i, j: (i, j, 0))
    return pl.pallas_call(
        _outproj0_kernel, grid=(b, t // tr),
        in_specs=[rows(512), rows(512), rows(512), rows(512), rows(d),
                  pl.BlockSpec((1, 3, d), (lambda i, j: (i, 0, 0)) if bm > 1 else (lambda i, j: (0, 0, 0))),
                  pl.BlockSpec(w.shape, lambda i, j: (0, 0))],
        out_specs=rows(d), out_shape=jax.ShapeDtypeStruct((b, t, d), F32),
        compiler_params=_cparams(2), name="outproj0",
    )(oa, za, ob, zb, x, mod, w)


L1_OFF = dict(qkv=0, zc=1536, qd=2048, kd=2560, vd=2816, zd=3072, ab=3584)
L1_W = 3712


def _inproj1_kernel(*refs, rope):
    if rope:
        (x_ref, mod_ref, ln_ref, w_ref, wvdt_ref, qn_ref, kn_ref, c_ref, s_ref,
         qkv_ref, zc_ref, qd_ref, kd_ref, vdt_ref, zd_ref, ab_ref) = refs
    else:
        (x_ref, mod_ref, ln_ref, w_ref, wvdt_ref, qn_ref, kn_ref,
         qkv_ref, zc_ref, qd_ref, kd_ref, vdt_ref, zd_ref, ab_ref, kd32_ref, vd32_ref) = refs
    h = _adaln(x_ref[0], mod_ref, ln_ref).astype(BF16)
    u = _dot(h, w_ref[...])
    o = L1_OFF
    qkv_ref[0] = u[:, o["qkv"]:o["qkv"] + 1536].astype(BF16)
    zc_ref[0] = u[:, o["zc"]:o["zc"] + 512].astype(BF16)
    zd_ref[0] = u[:, o["zd"]:o["zd"] + 512].astype(BF16)
    ab_ref[0] = u[:, o["ab"]:o["ab"] + 16]
    qd_scale = ATT_DH ** -0.5 * LOG2E
    for j in range(ATT_HEADS):
        blk = _rms_rows(u[:, o["qd"] + j * LANE:o["qd"] + (j + 1) * LANE], qn_ref[...])
        if rope:
            blk = _rope_block(blk, c_ref[...], s_ref[...], ATT_DH // 4)
        qd_ref[0, :, j * LANE:(j + 1) * LANE] = (blk * qd_scale).astype(BF16)
    for j in range(ATT_KV_HEADS):
        blk = _rms_rows(u[:, o["kd"] + j * LANE:o["kd"] + (j + 1) * LANE], kn_ref[...])
        if rope:
            blk = _rope_block(blk, c_ref[...], s_ref[...], ATT_DH // 4)
        else:
            kd32_ref[0, :, j * LANE:(j + 1) * LANE] = blk
        kd_ref[0, :, j * LANE:(j + 1) * LANE] = blk.astype(BF16)
    if not rope:
        vd32_ref[0] = u[:, o["vd"]:o["vd"] + 256]
    vdt_ref[0] = _dot_nt(wvdt_ref[...], h).astype(BF16)


def _inproj1(x, mod, ln, w, wvdt, qn, kn, tables):
    b, t, d = x.shape
    tr = _row_tile(t)
    rope = tables is not None
    bm = mod.shape[0]
    full = lambda a: pl.BlockSpec(a.shape, lambda i, j: (0,) * a.ndim)
    rows = lambda c: pl.BlockSpec((1, tr, c), lambda i, j: (i, j, 0))
    in_specs = [rows(d), pl.BlockSpec((1, 3, d), (lambda i, j: (i, 0, 0)) if bm > 1 else (lambda i, j: (0, 0, 0))),
                full(ln), full(w), full(wvdt), full(qn), full(kn)]
    args = [x, mod, ln, w, wvdt, qn, kn]
    if rope:
        for tab in tables:
            in_specs.append(pl.BlockSpec((tr, LANE), lambda i, j: (j, 0)))
            args.append(tab)
    out_shape = [jax.ShapeDtypeStruct((b, t, 1536), BF16), jax.ShapeDtypeStruct((b, t, 512), BF16),
                 jax.ShapeDtypeStruct((b, t, 512), BF16), jax.ShapeDtypeStruct((b, t, 256), BF16),
                 jax.ShapeDtypeStruct((b, 256, t), BF16), jax.ShapeDtypeStruct((b, t, 512), BF16),
                 jax.ShapeDtypeStruct((b, t, 16), F32)]
    out_specs = [rows(1536), rows(512), rows(512), rows(256),
                 pl.BlockSpec((1, 256, tr), lambda i, j: (i, 0, j)), rows(512), rows(16)]
    if not rope:
        out_shape += [jax.ShapeDtypeStruct((b, t, 256), F32), jax.ShapeDtypeStruct((b, t, 256), F32)]
        out_specs += [rows(256), rows(256)]
    return pl.pallas_call(
        functools.partial(_inproj1_kernel, rope=rope),
        grid=(b, t // tr), in_specs=in_specs, out_specs=out_specs, out_shape=out_shape,
        compiler_params=_cparams(2), name="inproj1_dec" if rope else "inproj1_ctx",
    )(*args)


def _gdn_chunks(chains):
    c = GDN_CHUNK
    row = lax.broadcasted_iota(jnp.int32, (c, c), 0)
    col = lax.broadcasted_iota(jnp.int32, (c, c), 1)
    lane2 = lax.broadcasted_iota(jnp.int32, (c, 2 * c), 1)
    eye = (row == col).astype(F32)
    zeros_w = jnp.zeros((c, 2 * c), BF16)
    zeros_w2 = jnp.zeros((c, 4 * c), BF16)
    for ch in chains:
        causal = (row <= col) if ch["upper"] else (row >= col)
        ch["strict"] = (row < col) if ch["upper"] else (row > col)
        ch["decay"] = jnp.exp(jnp.where(causal, ch["gc_col"] - ch["gc_row"], -jnp.inf))
        ch["kb"] = ch["k"] * ch["beta_col"]
        ch["egc"] = jnp.exp(ch["gc_col"])
        ch["sb"] = ch["s"].astype(BF16)
    for ch in chains:
        lhs = jnp.concatenate([ch["kb"], ch["q"]], axis=0).astype(BF16)
        a = _dot_nt(lhs, ch["k"].astype(BF16))
        ch["qs"] = _dot((ch["q"] * ch["egc"]).astype(BF16), ch["sb"])
        x = jnp.where(ch["strict"], -(a[:c] * ch["decay"]), 0.0)
        ch["intra"] = (a[c:] * ch["decay"]).astype(BF16)
        ch["w"] = jnp.concatenate([eye, x], axis=1)
    for _ in range(6):
        for ch in chains:
            wh, wl = _split(ch["w"])
            rhs_h = jnp.concatenate([zeros_w2, jnp.concatenate([wh, wl], axis=1)], axis=0)
            rhs_l = jnp.concatenate([zeros_w, wh], axis=0)
            pw = _dot(wh, rhs_h)
            pw = pw[:, :2 * c] + pw[:, 2 * c:] + _dot(wl, rhs_l)
            ch["w"] = jnp.where(lane2 < c, ch["w"], 0.0) + pw
    for ch in chains:
        rhs = jnp.concatenate([ch["v"] * ch["beta_col"], ch["kb"] * ch["egc"]], axis=1).astype(BF16)
        sol = _dot(ch["w"][:, :c].astype(BF16), rhs)
        ch["u"], ch["wv"] = sol[:, :GDN_DV], sol[:, GDN_DV:].astype(BF16)
    for ch in chains:
        ch["vn"] = (ch["u"] - _dot(ch["wv"], ch["sb"])).astype(BF16)
    outs = []
    for ch in chains:
        o = ch["qs"] + _dot(ch["intra"], ch["vn"])
        kd = (ch["k"] * jnp.exp(ch["glast"] - ch["gc_col"])).astype(BF16)
        s_new = ch["s"] * jnp.exp(ch["glast"]) + _dot_tn(kd, ch["vn"])
        outs.append((o, s_new))
    return outs


def _gdn_kernel(qkv_ref, ab_ref, abt_ref, cw_ref, al_ref, dt_ref, alt_ref, dtt_ref, s0_ref, o_ref, sf_ref,
                x_s, gcol_s, grow_s, beta_s, st_s, *, t):
    c = GDN_CHUNK
    n = t // c
    nh = GDN_HEADS
    ab = ab_ref[0]
    gact = -jnp.exp(al_ref[...]) * jax.nn.softplus(ab + dt_ref[...])
    lane16 = lax.broadcasted_iota(jnp.int32, ab.shape, 1)
    beta_s[...] = jnp.where(lane16 < 2 * nh, gact, jax.nn.sigmoid(ab))
    rowt = lax.broadcasted_iota(jnp.int32, (t, LANE), 0)
    for j in range(3 * nh):
        x = qkv_ref[0, :, j * LANE:(j + 1) * LANE].astype(F32)
        xp = jnp.where(rowt == 0, 0.0, pltpu.roll(x, 1, 0))
        xn = jnp.where(rowt == t - 1, 0.0, pltpu.roll(x, t - 1, 0))
        cw = cw_ref[:, j * LANE:(j + 1) * LANE]
        y = _silu(xp * cw[0:1, :] + x * cw[1:2, :] + xn * cw[2:3, :])
        if j < 2 * nh:
            y = y * lax.rsqrt(jnp.sum(y * y, axis=-1, keepdims=True) + EPS)
        if j < nh:
            y = y * GDN_DK ** -0.5
        x_s[:, j * LANE:(j + 1) * LANE] = y
    r64 = lax.broadcasted_iota(jnp.int32, (c, c), 0)
    c64 = lax.broadcasted_iota(jnp.int32, (c, c), 1)
    tril = (r64 >= c64).astype(F32)
    triu = (r64 <= c64).astype(F32)
    lane_c = lax.broadcasted_iota(jnp.int32, (c, 16), 1)
    sub_c = lax.broadcasted_iota(jnp.int32, (16, c), 0)

    def cum_body(i, carry):
        r0 = pl.multiple_of(i * c, c)
        g = beta_s[pl.ds(r0, c), :]
        gcol_s[pl.ds(r0, c), :] = jnp.where(lane_c < nh, _dot_exact(tril, g), _dot_exact(triu, g))
        gt = -jnp.exp(alt_ref[...]) * jax.nn.softplus(abt_ref[0, i] + dtt_ref[...])
        grow_s[i] = jnp.where(sub_c < nh, _dot_exact(gt, triu), _dot_exact(gt, tril))
        return carry

    lax.fori_loop(0, n, cum_body, 0)
    for d in range(2):
        for hh in range(nh):
            st_s[d * nh + hh] = s0_ref[0, d, hh]
    o_ref[...] = jnp.zeros_like(o_ref)

    def body(i, carry):
        chains = []
        for d in range(2):
            ci = i if d == 0 else n - 1 - i
            r0 = pl.multiple_of(ci * c, c)
            gcol = gcol_s[pl.ds(r0, c), :]
            bet = beta_s[pl.ds(r0, c), :]
            grow = grow_s[ci]
            last = c - 1 if d == 0 else 0
            for hh in range(nh):
                ch = d * nh + hh
                chains.append(dict(
                    q=x_s[pl.ds(r0, c), hh * LANE:(hh + 1) * LANE],
                    k=x_s[pl.ds(r0, c), (nh + hh) * LANE:(nh + hh + 1) * LANE],
                    v=x_s[pl.ds(r0, c), (2 * nh + hh) * LANE:(2 * nh + hh + 1) * LANE],
                    gc_col=gcol[:, ch:ch + 1], gc_row=grow[ch:ch + 1, :],
                    beta_col=bet[:, 2 * nh + ch:2 * nh + ch + 1], glast=gcol[last:last + 1, ch:ch + 1],
                    s=st_s[ch], upper=(d == 1), r0=r0, hh=hh, ch=ch))
        for chn, (o, s_new) in zip(chains, _gdn_chunks(chains)):
            st_s[chn["ch"]] = s_new
            o_ref[0, pl.ds(chn["r0"], c), chn["hh"] * LANE:(chn["hh"] + 1) * LANE] += o
        return carry

    lax.fori_loop(0, n, body, 0)
    for d in range(2):
        for hh in range(nh):
            sf_ref[0, d, hh] = st_s[d * nh + hh]


def _gdn(qkv, ab, abt, cw, al, dt, alt, dtt, s0):
    b, t, _ = qkv.shape
    n = t // GDN_CHUNK
    per_b = lambda a: pl.BlockSpec((1,) + a.shape[1:], lambda i: (i,) + (0,) * (a.ndim - 1))
    full = lambda a: pl.BlockSpec(a.shape, lambda i: (0,) * a.ndim)
    return pl.pallas_call(
        functools.partial(_gdn_kernel, t=t), grid=(b,),
        in_specs=[per_b(qkv), per_b(ab), per_b(abt), full(cw), full(al), full(dt), full(alt), full(dtt), per_b(s0)],
        out_specs=[pl.BlockSpec((1, t, GDN_HEADS * GDN_DV), lambda i: (i, 0, 0)), per_b(s0)],
        out_shape=[jax.ShapeDtypeStruct((b, t, GDN_HEADS * GDN_DV), F32), jax.ShapeDtypeStruct(s0.shape, F32)],
        scratch_shapes=[pltpu.VMEM((t, 3 * GDN_HEADS * LANE), F32), pltpu.VMEM((t, 16), F32),
                        pltpu.VMEM((n, 16, GDN_CHUNK), F32), pltpu.VMEM((t, 16), F32),
                        pltpu.VMEM((2 * GDN_HEADS, GDN_DK, GDN_DV), F32)],
        compiler_params=_cparams(1), name="gdn",
    )(qkv, ab, abt, cw, al, dt, alt, dtt, s0)


def _attd_kernel(*refs, n_ctx):
    if n_ctx:
        q_ref, k_ref, vt_ref, kc_ref, vct_ref, o_ref = refs
    else:
        q_ref, k_ref, vt_ref, o_ref = refs
    grp = ATT_HEADS // ATT_KV_HEADS
    outs = []
    for g in range(ATT_KV_HEADS):
        k_segs = [k_ref[0, :, g * LANE:(g + 1) * LANE]]
        vt_segs = [vt_ref[0, g * LANE:(g + 1) * LANE, :]]
        if n_ctx:
            k_segs.append(kc_ref[0, :, g * LANE:(g + 1) * LANE].astype(BF16))
            vt_segs.append(vct_ref[0, g * LANE:(g + 1) * LANE, :].astype(BF16))
        for j in range(grp):
            h = g * grp + j
            outs.append(_attend_t(q_ref[0, :, h * LANE:(h + 1) * LANE], k_segs, vt_segs))
    o_ref[0] = jnp.concatenate(outs, axis=0).T.astype(BF16)


def _attd(q, k, vt, k_c, v_ct, tq):
    b, t, _ = q.shape
    n_ctx = 0 if k_c is None else k_c.shape[1]
    per_b = lambda a: pl.BlockSpec((1,) + a.shape[1:], lambda i, j: (i, 0, 0))
    in_specs = [pl.BlockSpec((1, tq, 512), lambda i, j: (i, j, 0)), per_b(k), per_b(vt)]
    args = [q, k, vt]
    if n_ctx:
        in_specs += [per_b(k_c), per_b(v_ct)]
        args += [k_c, v_ct]
    return pl.pallas_call(
        functools.partial(_attd_kernel, n_ctx=n_ctx),
        grid=(b, t // tq), in_specs=in_specs,
        out_specs=pl.BlockSpec((1, tq, 512), lambda i, j: (i, j, 0)),
        out_shape=jax.ShapeDtypeStruct((b, t, 512), BF16),
        compiler_params=_cparams(2), name="attd_dec" if n_ctx else "attd_ctx",
    )(*args)


def _outproj1_kernel(oc_ref, zc_ref, od_ref, zd_ref, x_ref, mod_ref, gn_ref, w_ref, lnf_ref, y_ref):
    zc = zc_ref[0].astype(F32)
    parts = []
    for j in range(GDN_HEADS):
        oc = _rms_rows(oc_ref[0, :, j * LANE:(j + 1) * LANE], gn_ref[...])
        parts.append((oc * _silu(zc[:, j * LANE:(j + 1) * LANE])).astype(BF16))
    gc = jnp.concatenate(parts, axis=1)
    gd = (od_ref[0].astype(F32) * _silu(zd_ref[0].astype(F32))).astype(BF16)
    y = _dot(gc, w_ref[0:512, :]) + _dot(gd, w_ref[512:1024, :])
    x2 = x_ref[0] + mod_ref[0, 2:3, :] * y
    y_ref[0] = _rms_rows(x2, lnf_ref[...])


def _outproj1(oc, zc, od, zd, x, mod, gn, w, lnf):
    b, t, d = x.shape
    tr = _row_tile(t)
    bm = mod.shape[0]
    rows = lambda c: pl.BlockSpec((1, tr, c), lambda i, j: (i, j, 0))
    full = lambda a: pl.BlockSpec(a.shape, lambda i, j: (0,) * a.ndim)
    return pl.pallas_call(
        _outproj1_kernel, grid=(b, t // tr),
        in_specs=[rows(512), rows(512), rows(512), rows(512), rows(d),
                  pl.BlockSpec((1, 3, d), (lambda i, j: (i, 0, 0)) if bm > 1 else (lambda i, j: (0, 0, 0))),
                  full(gn), full(w), full(lnf)],
        out_specs=rows(d), out_shape=jax.ShapeDtypeStruct((b, t, d), F32),
        compiler_params=_cparams(2), name="outproj1",
    )(oc, zc, od, zd, x, mod, gn, w, lnf)


def _rope_table(n_tok, rot_dim, used, block=LANE):
    quarter = rot_dim // 4
    inv = ROPE_THETA ** (-jnp.arange(quarter, dtype=F32) / quarter)
    tt = jnp.arange(n_tok)
    pos = jnp.stack([tt // GRID_W, tt % GRID_W], axis=-1).astype(F32)
    ang = pos[:, :, None] * inv
    cos, sin = jnp.cos(ang), jnp.sin(ang)
    c = jnp.concatenate([cos, cos], axis=-1).reshape(n_tok, rot_dim)
    s = jnp.concatenate([-sin, sin], axis=-1).reshape(n_tok, rot_dim)
    return c, s


def _place(tab, fill, off, width):
    n = tab.shape[0]
    left = jnp.full((n, off), fill, F32)
    right = jnp.full((n, width - off - tab.shape[1]), fill, F32)
    return jnp.concatenate([left, tab, right], axis=1)


def _prep_l0(w_in0, w_uq, w_ukv):
    d = w_in0.shape[0]
    cq, ckv, kr, za, qb, kb, vb, zb = jnp.split(w_in0, [384, 640, 672, 1184, 1696, 1824, 1952], axis=1)
    kr_pad = jnp.concatenate([jnp.zeros((d, 64), F32), kr, jnp.zeros((d, 32), F32)], axis=1)
    w = jnp.concatenate([cq, ckv, kr_pad, za, qb, kb, vb, zb], axis=1).astype(BF16)
    wvbt = vb.T.astype(BF16)
    uq = w_uq.reshape(MLA_Q_LORA, MLA_HEADS, MLA_NOPE + MLA_ROPE)
    wuq = jnp.pad(uq, ((0, 0), (0, 0), (0, LANE - MLA_NOPE - MLA_ROPE))).reshape(MLA_Q_LORA, MLA_HEADS * LANE)
    ukv = w_ukv.reshape(MLA_KV_LORA, MLA_HEADS, MLA_NOPE + MLA_V)
    wuk = jnp.pad(ukv[:, :, :MLA_NOPE], ((0, 0), (0, 0), (0, LANE - MLA_NOPE))).reshape(MLA_KV_LORA, MLA_HEADS * LANE)
    wuvt = ukv[:, :, MLA_NOPE:].reshape(MLA_KV_LORA, MLA_HEADS * MLA_V).T
    return w, wvbt, wuq.astype(BF16), wuk.astype(BF16), wuvt.astype(BF16)


def _prep_l1(w_in1):
    d = w_in1.shape[0]
    qkv, a, bb, zc, qd, kd, vd, zd = jnp.split(w_in1, [1536, 1544, 1552, 2064, 2576, 2832, 3088], axis=1)
    w = jnp.concatenate([qkv, zc, qd, kd, vd, zd, a, bb, jnp.zeros((d, 112), F32)], axis=1).astype(BF16)
    return w, vd.T.astype(BF16)


def _chunk_rows(ab):
    b, t, c = ab.shape
    return jnp.swapaxes(ab.reshape(b, t // GDN_CHUNK, GDN_CHUNK, c), 2, 3)


def _trunk(x, mod, caches, p, tables, tq):
    dec = caches is not None
    t0m, t0s, t1 = tables if dec else (None, None, None)
    (qa, ckv, kr, za, qb, kb, vbt, zb, *ctx0) = _inproj0(
        x, mod[0], p["ln0"], p["w0"], p["wvbt"], p["qn"], p["wuq"], p["kvn"], (t0m + t0s) if dec else None)
    if dec:
        ckv_c, kr_c, kb_c, vb_ct, s0, kd_c, vd_ct = caches
    else:
        ckv_c = kr_c = kb_c = vb_ct = kd_c = vd_ct = None
        s0 = jnp.zeros((x.shape[0], 2, GDN_HEADS, GDN_DK, GDN_DV), F32)
    oa = _mla(qa, ckv, kr, ckv_c, kr_c, p["wuk"], p["wuvt"], tq)
    ob = _swa(p["sink"], qb, kb, vbt, kb_c, vb_ct, tq)
    x1 = _outproj0(oa, za, ob, zb, x, mod[0], p["wout0"])
    (qkv, zc, qd, kd, vdt, zd, ab, *ctx1) = _inproj1(
        x1, mod[1], p["ln1"], p["w1"], p["wvdt"], p["aqn"], p["akn"], t1 if dec else None)
    oc, sfin = _gdn(qkv, ab, _chunk_rows(ab), p["cw"], p["al"], p["dt"], p["alt"], p["dtt"], s0)
    od = _attd(qd, kd, vdt, kd_c, vd_ct, tq)
    y = _outproj1(oc, zc, od, zd, x1, mod[1], p["gn"], p["wout1"], p["lnf"])
    return y, ctx0, sfin, ctx1


def kernel(x_prompt, x_sample, cache_l0_mla_ckv, cache_l0_mla_krope, cache_l0_swa_k, cache_l0_swa_v,
           state_l1_gdn, cache_l1_attn_k, cache_l1_attn_v, c, c_ctx,
           w_mod0, b_mod0, ln0, w_in0, mla_q_norm, w_uq, mla_kv_norm, w_ukv, swa_sink, w_out0,
           w_mod1, b_mod1, ln1, w_in1, gdn_conv, gdn_a_log, gdn_dt_bias, gdn_norm, att_q_norm, att_k_norm, w_out1,
           ln_f):
    d = x_prompt.shape[-1]
    bd, td = x_sample.shape[:2]
    bc, tc = x_prompt.shape[:2]
    past = cache_l0_mla_ckv.shape[1]
    row = lambda v: v.reshape(1, -1)
    w0, wvbt, wuq, wuk, wuvt = _prep_l0(w_in0, w_uq, w_ukv)
    w1, wvdt = _prep_l1(w_in1)
    al8 = gdn_a_log.reshape(1, 2 * GDN_HEADS)
    dt8 = gdn_dt_bias.reshape(1, 2 * GDN_HEADS)
    al16 = jnp.pad(al8, ((0, 0), (0, 8)))
    dt16 = jnp.pad(dt8, ((0, 0), (0, 8)))
    p = dict(ln0=row(ln0), w0=w0, wvbt=wvbt, qn=row(mla_q_norm), wuq=wuq, kvn=row(mla_kv_norm), wuk=wuk, wuvt=wuvt,
             sink=swa_sink, wout0=w_out0.astype(BF16), ln1=row(ln1), w1=w1, wvdt=wvdt, aqn=row(att_q_norm),
             akn=row(att_k_norm), cw=gdn_conv, al=al16, dt=dt16, alt=al16.T, dtt=dt16.T, gn=row(gdn_norm),
             wout1=w_out1.astype(BF16), lnf=row(ln_f))
    n_rows = -(-(bd + 1) // 8) * 8
    c_rows = jnp.concatenate([c, c_ctx[None, :], jnp.zeros((n_rows - bd - 1, d), F32)], axis=0)
    mods = [_mod(c_rows, w_mod0, b_mod0), _mod(c_rows, w_mod1, b_mod1)]
    mod_dec = [m[:bd].reshape(bd, 3, d) for m in mods]
    mod_ctx = [m[bd:bd + 1].reshape(1, 3, d) for m in mods]
    cm, sm = _rope_table(td, MLA_ROPE, LANE)
    t0m = (_place(cm, 1.0, MLA_NOPE, LANE), _place(sm, 0.0, MLA_NOPE, LANE))
    cs, ss = _rope_table(td, SWA_DH, SWA_DH)
    t0s = (jnp.tile(cs, (1, LANE // SWA_DH)), jnp.tile(ss, (1, LANE // SWA_DH)))
    t1 = _rope_table(td, ATT_DH, LANE)
    caches = (cache_l0_mla_ckv,
              jnp.pad(cache_l0_mla_krope, ((0, 0), (0, 0), (MLA_NOPE, LANE - MLA_NOPE - MLA_ROPE))),
              cache_l0_swa_k.reshape(bd, past, SWA_KV_HEADS * SWA_DH),
              jnp.swapaxes(cache_l0_swa_v.reshape(bd, past, SWA_KV_HEADS * SWA_DH), 1, 2),
              state_l1_gdn,
              cache_l1_attn_k.reshape(bd, past, ATT_KV_HEADS * ATT_DH),
              jnp.swapaxes(cache_l1_attn_v.reshape(bd, past, ATT_KV_HEADS * ATT_DH), 1, 2))
    y_prompt, ctx0, sfin, ctx1 = _trunk(x_prompt, mod_ctx, None, p, None, tq=tc)
    y_sample, _, _, _ = _trunk(x_sample, mod_dec, caches, p, (t0m, t0s, t1), tq=256)
    ckv32, kr32, kb32, vb32 = ctx0
    kd32, vd32 = ctx1
    return (y_prompt, y_sample, ckv32, kr32,
            kb32.reshape(bc, tc, SWA_KV_HEADS, SWA_DH), vb32.reshape(bc, tc, SWA_KV_HEADS, SWA_DH),
            sfin, kd32.reshape(bc, tc, ATT_KV_HEADS, ATT_DH), vd32.reshape(bc, tc, ATT_KV_HEADS, ATT_DH))
```

```python
import functools
import math

import jax
import jax.numpy as jnp
from jax import lax
from jax.experimental import pallas as pl
from jax.experimental.pallas import tpu as pltpu

F32 = jnp.float32
BF16 = jnp.bfloat16

GRID_W = 64
ROPE_THETA = 10000.0
EPS = 1e-6
WINDOW = 128
MLA_HEADS, MLA_NOPE, MLA_ROPE, MLA_V = 8, 64, 32, 64
MLA_Q_LORA, MLA_KV_LORA = 384, 256
SWA_HEADS, SWA_KV_HEADS, SWA_DH = 8, 2, 64
GDN_HEADS, GDN_DK, GDN_DV, CONV_K, GDN_CHUNK = 4, 128, 128, 3, 64
ATT_HEADS, ATT_KV_HEADS, ATT_DH = 4, 2, 128
LANE = 128
LOG2E = math.log2(math.e)
NEG = -1e30
VMEM_LIMIT = 56 * 1024 * 1024


def _cparams(n_axes):
    return pltpu.CompilerParams(dimension_semantics=("arbitrary",) * n_axes, vmem_limit_bytes=VMEM_LIMIT)


def _dot(a, b):
    return jnp.dot(a, b, preferred_element_type=F32)


def _dot_nt(a, b):
    return lax.dot_general(a, b, (((1,), (1,)), ((), ())), preferred_element_type=F32)


def _dot_tn(a, b):
    return lax.dot_general(a, b, (((0,), (0,)), ((), ())), preferred_element_type=F32)


def _dot_exact(a, b):
    return jnp.dot(a, b, preferred_element_type=F32, precision=lax.Precision.HIGHEST)


def _split(a):
    hi = a.astype(BF16)
    return hi, (a - hi.astype(F32)).astype(BF16)


def _silu(x):
    return x * jax.nn.sigmoid(x)


def _rms_rows(x, g):
    return x * lax.rsqrt(jnp.mean(x * x, axis=-1, keepdims=True) + EPS) * g


def _rope_block(x, cos, sin, half):
    lane = lax.broadcasted_iota(jnp.int32, x.shape, 1)
    first = (lane // half) % 2 == 0
    partner = jnp.where(first, pltpu.roll(x, LANE - half, 1), pltpu.roll(x, half, 1))
    return x * cos + partner * sin


def _mod_kernel(c_ref, w_ref, b_ref, o_ref):
    a = _silu(c_ref[...]).astype(BF16)
    o_ref[...] = _dot(a, w_ref[...].astype(BF16)) + b_ref[...]


def _mod(c_rows, w_mod, b_mod):
    r, d = c_rows.shape
    n = w_mod.shape[1]
    tn = 1024
    return pl.pallas_call(
        _mod_kernel,
        grid=(n // tn,),
        in_specs=[pl.BlockSpec((r, d), lambda j: (0, 0)),
                  pl.BlockSpec((d, tn), lambda j: (0, j)),
                  pl.BlockSpec((1, tn), lambda j: (0, j))],
        out_specs=pl.BlockSpec((r, tn), lambda j: (0, j)),
        out_shape=jax.ShapeDtypeStruct((r, n), F32),
        compiler_params=_cparams(1),
        name="mod",
    )(c_rows, w_mod, b_mod.reshape(1, n))


def _adaln(x, mod_ref, ln_ref):
    h = _rms_rows(x, ln_ref[...])
    return h * (1.0 + mod_ref[0, 1:2, :]) + mod_ref[0, 0:1, :]


L0_OFF = dict(cq=0, ckv=384, kr=640, za=768, qb=1280, kb=1792, vb=1920, zb=2048)
L0_W = 2560


def _inproj0_kernel(*refs, rope):
    if rope:
        (x_ref, mod_ref, ln_ref, w_ref, wvbt_ref, qn_ref, wuq_ref, kvn_ref, cm_ref, sm_ref, cs_ref, ss_ref,
         qa_ref, ckv_ref, kr_ref, za_ref, qb_ref, kb_ref, vbt_ref, zb_ref) = refs
    else:
        (x_ref, mod_ref, ln_ref, w_ref, wvbt_ref, qn_ref, wuq_ref, kvn_ref,
         qa_ref, ckv_ref, kr_ref, za_ref, qb_ref, kb_ref, vbt_ref, zb_ref,
         ckv32_ref, kr32_ref, kb32_ref, vb32_ref) = refs
    h = _adaln(x_ref[0], mod_ref, ln_ref).astype(BF16)
    u = _dot(h, w_ref[...])
    o = L0_OFF
    cq = _rms_rows(u[:, o["cq"]:o["cq"] + 384], qn_ref[...]).astype(BF16)
    qa = _dot(cq, wuq_ref[...])
    ckv = _rms_rows(u[:, o["ckv"]:o["ckv"] + 256], kvn_ref[...])
    kr = u[:, o["kr"]:o["kr"] + 128]
    qb = u[:, o["qb"]:o["qb"] + 512]
    kb = u[:, o["kb"]:o["kb"] + 128]
    vb = u[:, o["vb"]:o["vb"] + 128]
    if not rope:
        ckv32_ref[0] = ckv
        kr32_ref[0] = kr[:, 64:96]
        kb32_ref[0] = kb
        vb32_ref[0] = vb
    qa_scale = (MLA_NOPE + MLA_ROPE) ** -0.5 * LOG2E
    qb_scale = SWA_DH ** -0.5 * LOG2E
    for j in range(MLA_HEADS):
        blk = qa[:, j * LANE:(j + 1) * LANE]
        if rope:
            blk = _rope_block(blk, cm_ref[...], sm_ref[...], MLA_ROPE // 4)
        qa_ref[0, :, j * LANE:(j + 1) * LANE] = (blk * qa_scale).astype(BF16)
    for j in range(SWA_HEADS * SWA_DH // LANE):
        blk = qb[:, j * LANE:(j + 1) * LANE]
        if rope:
            blk = _rope_block(blk, cs_ref[...], ss_ref[...], SWA_DH // 4)
        qb_ref[0, :, j * LANE:(j + 1) * LANE] = (blk * qb_scale).astype(BF16)
    if rope:
        kr = _rope_block(kr, cm_ref[...], sm_ref[...], MLA_ROPE // 4)
        kb = _rope_block(kb, cs_ref[...], ss_ref[...], SWA_DH // 4)
    ckv_ref[0] = ckv.astype(BF16)
    kr_ref[0] = kr.astype(BF16)
    kb_ref[0] = kb.astype(BF16)
    za_ref[0] = u[:, o["za"]:o["za"] + 512].astype(BF16)
    zb_ref[0] = u[:, o["zb"]:o["zb"] + 512].astype(BF16)
    vbt_ref[0] = _dot_nt(wvbt_ref[...], h).astype(BF16)


def _row_tile(t):
    return 512 if t % 512 == 0 else 256


def _inproj0(x, mod, ln, w, wvbt, qn, wuq, kvn, tables):
    b, t, d = x.shape
    tr = _row_tile(t)
    rope = tables is not None
    bm = mod.shape[0]
    full = lambda a: pl.BlockSpec(a.shape, lambda i, j: (0,) * a.ndim)
    rows = lambda c: pl.BlockSpec((1, tr, c), lambda i, j: (i, j, 0))
    in_specs = [rows(d), pl.BlockSpec((1, 3, d), (lambda i, j: (i, 0, 0)) if bm > 1 else (lambda i, j: (0, 0, 0))),
                full(ln), full(w), full(wvbt), full(qn), full(wuq), full(kvn)]
    args = [x, mod, ln, w, wvbt, qn, wuq, kvn]
    if rope:
        for tab in tables:
            in_specs.append(pl.BlockSpec((tr, LANE), lambda i, j: (j, 0)))
            args.append(tab)
    out_shape = [jax.ShapeDtypeStruct((b, t, 1024), BF16), jax.ShapeDtypeStruct((b, t, 256), BF16),
                 jax.ShapeDtypeStruct((b, t, 128), BF16), jax.ShapeDtypeStruct((b, t, 512), BF16),
                 jax.ShapeDtypeStruct((b, t, 512), BF16), jax.ShapeDtypeStruct((b, t, 128), BF16),
                 jax.ShapeDtypeStruct((b, 128, t), BF16), jax.ShapeDtypeStruct((b, t, 512), BF16)]
    out_specs = [rows(1024), rows(256), rows(128), rows(512), rows(512), rows(128),
                 pl.BlockSpec((1, 128, tr), lambda i, j: (i, 0, j)), rows(512)]
    if not rope:
        out_shape += [jax.ShapeDtypeStruct((b, t, 256), F32), jax.ShapeDtypeStruct((b, t, 32), F32),
                      jax.ShapeDtypeStruct((b, t, 128), F32), jax.ShapeDtypeStruct((b, t, 128), F32)]
        out_specs += [rows(256), rows(32), rows(128), rows(128)]
    return pl.pallas_call(
        functools.partial(_inproj0_kernel, rope=rope),
        grid=(b, t // tr), in_specs=in_specs, out_specs=out_specs, out_shape=out_shape,
        compiler_params=_cparams(2), name="inproj0_dec" if rope else "inproj0_ctx",
    )(*args)


KEY_CHUNK = 512
ATT_LOOKAHEAD = 4


def _attend_heads(qs, kv_of, chunks, sinks=None):
    nh = len(qs)
    tq = qs[0].shape[0]
    m = [None] * nh
    l = [None] * nh
    acc = [None] * nh
    if sinks is not None:
        m = [jnp.full((1, tq), sk, F32) for sk in sinks]
        l = [jnp.ones((1, tq), F32) for _ in sinks]
    items = [(ci, i) for ci in range(len(chunks)) for i in range(nh)]
    loaded = {}

    def kv(ci, src):
        if (ci, src) not in loaded:
            loaded[(ci, src)] = (chunks[ci][0](src), chunks[ci][1](src))
        return loaded[(ci, src)]

    scores = {}
    for t in range(len(items) + ATT_LOOKAHEAD):
        if t < len(items):
            ci, i = items[t]
            scores[t] = _dot_nt(kv(ci, kv_of[i])[0], qs[i])
        t0 = t - ATT_LOOKAHEAD
        if t0 < 0:
            continue
        ci, i = items[t0]
        mask = chunks[ci][2]
        si = scores.pop(t0)
        if mask is not None:
            si = jnp.where(mask, si, NEG)
        cm = si.max(axis=0, keepdims=True)
        alpha = None
        if m[i] is None:
            m_new = cm
        else:
            m_new = jnp.maximum(m[i], cm)
            alpha = jnp.exp2(m[i] - m_new)
        p = jnp.exp2(si - m_new)
        ls = p.sum(axis=0, keepdims=True)
        l[i] = ls if l[i] is None else l[i] * alpha + ls
        m[i] = m_new
        pv = _dot(kv(ci, kv_of[i])[1], p.astype(BF16))
        acc[i] = pv if acc[i] is None else acc[i] * alpha + pv
    return [acc[i] * (1.0 / l[i]) for i in range(nh)]


def _key_chunks(n):
    step = KEY_CHUNK if n % KEY_CHUNK == 0 else n
    return [(c0, step) for c0 in range(0, n, step)]


def _mla_kernel(*refs, n_new, n_ctx, hp):
    if n_ctx:
        q_ref, ckv_ref, kr_ref, ckvc_ref, krc_ref, wuk_ref, wuvt_ref, o_ref, k_s, vt_s = refs
    else:
        q_ref, ckv_ref, kr_ref, wuk_ref, wuvt_ref, o_ref, k_s, vt_s = refs
    qi, gi = pl.program_id(1), pl.program_id(2)

    @pl.when((qi == 0) & (gi == 0))
    def _():
        def expand(ckv, kr, r0, n):
            kn = _dot(ckv, wuk_ref[...])
            for j in range(MLA_HEADS):
                k_s[j, r0:r0 + n, :] = (kn[:, j * LANE:(j + 1) * LANE] + kr).astype(BF16)
            vt_s[:, r0:r0 + n] = _dot_nt(wuvt_ref[...], ckv).astype(BF16)

        blk = 512 if n_new % 512 == 0 else 256
        for r0 in range(0, n_new, blk):
            expand(ckv_ref[0, r0:r0 + blk, :], kr_ref[0, r0:r0 + blk, :].astype(F32), r0, blk)
        if n_ctx:
            expand(ckvc_ref[0].astype(BF16), krc_ref[0], n_new, n_ctx)

    qs = [q_ref[0, :, j * LANE:(j + 1) * LANE] for j in range(hp)]
    chunks = []
    for c0, cn in _key_chunks(n_new + n_ctx):
        chunks.append((
            lambda j, c0=c0, cn=cn: k_s[gi * hp + j, c0:c0 + cn, :],
            lambda j, c0=c0, cn=cn: vt_s[pl.ds(pl.multiple_of((gi * hp + j) * MLA_V, MLA_V), MLA_V), c0:c0 + cn],
            None))
    outs = _attend_heads(qs, list(range(hp)), chunks)
    o_ref[0] = jnp.concatenate(outs, axis=0).T.astype(BF16)


def _mla(q, ckv, kr, ckv_c, kr_c, wuk, wuvt, tq):
    b, t, _ = q.shape
    n_ctx = 0 if ckv_c is None else ckv_c.shape[1]
    hp = 4
    tk = t + n_ctx
    rows_q = pl.BlockSpec((1, tq, hp * LANE), lambda i, j, g: (i, j, g))
    per_b = lambda a: pl.BlockSpec((1,) + a.shape[1:], lambda i, j, g: (i, 0, 0))
    full = lambda a: pl.BlockSpec(a.shape, lambda i, j, g: (0, 0))
    in_specs = [rows_q, per_b(ckv), per_b(kr)]
    args = [q, ckv, kr]
    if n_ctx:
        in_specs += [per_b(ckv_c), per_b(kr_c)]
        args += [ckv_c, kr_c]
    in_specs += [full(wuk), full(wuvt)]
    args += [wuk, wuvt]
    return pl.pallas_call(
        functools.partial(_mla_kernel, n_new=t, n_ctx=n_ctx, hp=hp),
        grid=(b, t // tq, MLA_HEADS // hp), in_specs=in_specs,
        out_specs=pl.BlockSpec((1, tq, hp * MLA_V), lambda i, j, g: (i, j, g)),
        out_shape=jax.ShapeDtypeStruct((b, t, MLA_HEADS * MLA_V), BF16),
        scratch_shapes=[pltpu.VMEM((MLA_HEADS, tk, LANE), BF16), pltpu.VMEM((MLA_HEADS * MLA_V, tk), BF16)],
        compiler_params=_cparams(3), name="mla_dec" if n_ctx else "mla_ctx",
    )(*args)


def _swa_kernel(*refs, n_new, n_ctx, tq):
    if n_ctx:
        sink_ref, q_ref, k_ref, vt_ref, kc_ref, vct_ref, o_ref = refs
    else:
        sink_ref, q_ref, k_ref, vt_ref, o_ref = refs
    qi = pl.program_id(1)
    grp = SWA_HEADS // SWA_KV_HEADS
    if n_ctx:
        span = tq + 2 * WINDOW
        q0 = qi * tq
        start = pl.multiple_of(jnp.clip(q0 - WINDOW, 0, n_new - span), LANE)
        kpos = start + lax.broadcasted_iota(jnp.int32, (span, tq), 0)
        qpos = q0 + lax.broadcasted_iota(jnp.int32, (span, tq), 1)
        band = jnp.abs(kpos - qpos) <= WINDOW
    dh = SWA_DH
    chunks = []
    if n_ctx:
        for c0, cn in _key_chunks(span):
            chunks.append((
                lambda g, c0=c0, cn=cn: k_ref[0, pl.ds(start + c0, cn), g * dh:(g + 1) * dh],
                lambda g, c0=c0, cn=cn: vt_ref[0, g * dh:(g + 1) * dh, pl.ds(start + c0, cn)],
                band[c0:c0 + cn]))
        for c0, cn in _key_chunks(n_ctx):
            chunks.append((
                lambda g, c0=c0, cn=cn: kc_ref[0, c0:c0 + cn, g * dh:(g + 1) * dh].astype(BF16),
                lambda g, c0=c0, cn=cn: vct_ref[0, g * dh:(g + 1) * dh, c0:c0 + cn].astype(BF16),
                None))
    else:
        for c0, cn in _key_chunks(n_new):
            chunks.append((
                lambda g, c0=c0, cn=cn: k_ref[0, c0:c0 + cn, g * dh:(g + 1) * dh],
                lambda g, c0=c0, cn=cn: vt_ref[0, g * dh:(g + 1) * dh, c0:c0 + cn],
                None))
    qs = [q_ref[0, :, h * dh:(h + 1) * dh] for h in range(SWA_HEADS)]
    sinks = [sink_ref[h] * LOG2E for h in range(SWA_HEADS)]
    outs = _attend_heads(qs, [h // grp for h in range(SWA_HEADS)], chunks, sinks)
    o_ref[0] = jnp.concatenate(outs, axis=0).T.astype(BF16)


def _swa(sink, q, k, vt, k_c, v_ct, tq):
    b, t, _ = q.shape
    n_ctx = 0 if k_c is None else k_c.shape[1]
    per_b = lambda a: pl.BlockSpec((1,) + a.shape[1:], lambda i, j: (i, 0, 0))
    in_specs = [pl.BlockSpec(memory_space=pltpu.SMEM), pl.BlockSpec((1, tq, 512), lambda i, j: (i, j, 0)),
                per_b(k), per_b(vt)]
    args = [sink, q, k, vt]
    if n_ctx:
        in_specs += [per_b(k_c), per_b(v_ct)]
        args += [k_c, v_ct]
    return pl.pallas_call(
        functools.partial(_swa_kernel, n_new=t, n_ctx=n_ctx, tq=tq),
        grid=(b, t // tq), in_specs=in_specs,
        out_specs=pl.BlockSpec((1, tq, 512), lambda i, j: (i, j, 0)),
        out_shape=jax.ShapeDtypeStruct((b, t, 512), BF16),
        compiler_params=_cparams(2), name="swa_dec" if n_ctx else "swa_ctx",
    )(*args)


def _outproj0_kernel(oa_ref, za_ref, ob_ref, zb_ref, x_ref, mod_ref, w_ref, y_ref):
    ga = (oa_ref[0].astype(F32) * _silu(za_ref[0].astype(F32))).astype(BF16)
    gb = (ob_ref[0].astype(F32) * _silu(zb_ref[0].astype(F32))).astype(BF16)
    y = _dot(ga, w_ref[0:512, :]) + _dot(gb, w_ref[512:1024, :])
    y_ref[0] = x_ref[0] + mod_ref[0, 2:3, :] * y


def _outproj0(oa, za, ob, zb, x, mod, w):
    b, t, d = x.shape
    tr = _row_tile(t)
    bm = mod.shape[0]
    rows = lambda c: pl.BlockSpec((1, tr, c), lambda i, j: (i, j, 0))
    return pl.pallas_call(
        _outproj0_kernel, grid=(b, t // tr),
        in_specs=[rows(512), rows(512), rows(512), rows(512), rows(d),
                  pl.BlockSpec((1, 3, d), (lambda i, j: (i, 0, 0)) if bm > 1 else (lambda i, j: (0, 0, 0))),
                  pl.BlockSpec(w.shape, lambda i, j: (0, 0))],
        out_specs=rows(d), out_shape=jax.ShapeDtypeStruct((b, t, d), F32),
        compiler_params=_cparams(2), name="outproj0",
    )(oa, za, ob, zb, x, mod, w)


L1_OFF = dict(qkv=0, zc=1536, qd=2048, kd=2560, vd=2816, zd=3072, ab=3584)
L1_W = 3712


def _inproj1_kernel(*refs, rope):
    if rope:
        (x_ref, mod_ref, ln_ref, w_ref, wvdt_ref, qn_ref, kn_ref, c_ref, s_ref,
         qkv_ref, zc_ref, qd_ref, kd_ref, vdt_ref, zd_ref, ab_ref) = refs
    else:
        (x_ref, mod_ref, ln_ref, w_ref, wvdt_ref, qn_ref, kn_ref,
         qkv_ref, zc_ref, qd_ref, kd_ref, vdt_ref, zd_ref, ab_ref, kd32_ref, vd32_ref) = refs
    h = _adaln(x_ref[0], mod_ref, ln_ref).astype(BF16)
    u = _dot(h, w_ref[...])
    o = L1_OFF
    qkv_ref[0] = u[:, o["qkv"]:o["qkv"] + 1536].astype(BF16)
    zc_ref[0] = u[:, o["zc"]:o["zc"] + 512].astype(BF16)
    zd_ref[0] = u[:, o["zd"]:o["zd"] + 512].astype(BF16)
    ab_ref[0] = u[:, o["ab"]:o["ab"] + 16]
    qd_scale = ATT_DH ** -0.5 * LOG2E
    for j in range(ATT_HEADS):
        blk = _rms_rows(u[:, o["qd"] + j * LANE:o["qd"] + (j + 1) * LANE], qn_ref[...])
        if rope:
            blk = _rope_block(blk, c_ref[...], s_ref[...], ATT_DH // 4)
        qd_ref[0, :, j * LANE:(j + 1) * LANE] = (blk * qd_scale).astype(BF16)
    for j in range(ATT_KV_HEADS):
        blk = _rms_rows(u[:, o["kd"] + j * LANE:o["kd"] + (j + 1) * LANE], kn_ref[...])
        if rope:
            blk = _rope_block(blk, c_ref[...], s_ref[...], ATT_DH // 4)
        else:
            kd32_ref[0, :, j * LANE:(j + 1) * LANE] = blk
        kd_ref[0, :, j * LANE:(j + 1) * LANE] = blk.astype(BF16)
    if not rope:
        vd32_ref[0] = u[:, o["vd"]:o["vd"] + 256]
    vdt_ref[0] = _dot_nt(wvdt_ref[...], h).astype(BF16)


def _inproj1(x, mod, ln, w, wvdt, qn, kn, tables):
    b, t, d = x.shape
    tr = _row_tile(t)
    rope = tables is not None
    bm = mod.shape[0]
    full = lambda a: pl.BlockSpec(a.shape, lambda i, j: (0,) * a.ndim)
    rows = lambda c: pl.BlockSpec((1, tr, c), lambda i, j: (i, j, 0))
    in_specs = [rows(d), pl.BlockSpec((1, 3, d), (lambda i, j: (i, 0, 0)) if bm > 1 else (lambda i, j: (0, 0, 0))),
                full(ln), full(w), full(wvdt), full(qn), full(kn)]
    args = [x, mod, ln, w, wvdt, qn, kn]
    if rope:
        for tab in tables:
            in_specs.append(pl.BlockSpec((tr, LANE), lambda i, j: (j, 0)))
            args.append(tab)
    out_shape = [jax.ShapeDtypeStruct((b, t, 1536), BF16), jax.ShapeDtypeStruct((b, t, 512), BF16),
                 jax.ShapeDtypeStruct((b, t, 512), BF16), jax.ShapeDtypeStruct((b, t, 256), BF16),
                 jax.ShapeDtypeStruct((b, 256, t), BF16), jax.ShapeDtypeStruct((b, t, 512), BF16),
                 jax.ShapeDtypeStruct((b, t, 16), F32)]
    out_specs = [rows(1536), rows(512), rows(512), rows(256),
                 pl.BlockSpec((1, 256, tr), lambda i, j: (i, 0, j)), rows(512), rows(16)]
    if not rope:
        out_shape += [jax.ShapeDtypeStruct((b, t, 256), F32), jax.ShapeDtypeStruct((b, t, 256), F32)]
        out_specs += [rows(256), rows(256)]
    return pl.pallas_call(
        functools.partial(_inproj1_kernel, rope=rope),
        grid=(b, t // tr), in_specs=in_specs, out_specs=out_specs, out_shape=out_shape,
        compiler_params=_cparams(2), name="inproj1_dec" if rope else "inproj1_ctx",
    )(*args)


def _gdn_chunks(chains):
    c = GDN_CHUNK
    row = lax.broadcasted_iota(jnp.int32, (c, c), 0)
    col = lax.broadcasted_iota(jnp.int32, (c, c), 1)
    lane2 = lax.broadcasted_iota(jnp.int32, (c, 2 * c), 1)
    eye = (row == col).astype(F32)
    zeros_w = jnp.zeros((c, 2 * c), BF16)
    zeros_w2 = jnp.zeros((c, 4 * c), BF16)
    for ch in chains:
        causal = (row <= col) if ch["upper"] else (row >= col)
        ch["strict"] = (row < col) if ch["upper"] else (row > col)
        ch["decay"] = jnp.exp(jnp.where(causal, ch["gc_col"] - ch["gc_row"], -jnp.inf))
        ch["kb"] = ch["k"] * ch["beta_col"]
        ch["egc"] = jnp.exp(ch["gc_col"])
        ch["sb"] = ch["s"].astype(BF16)
    for ch in chains:
        lhs = jnp.concatenate([ch["kb"], ch["q"]], axis=0).astype(BF16)
        a = _dot_nt(lhs, ch["k"].astype(BF16))
        ch["qs"] = _dot((ch["q"] * ch["egc"]).astype(BF16), ch["sb"])
        x = jnp.where(ch["strict"], -(a[:c] * ch["decay"]), 0.0)
        ch["intra"] = (a[c:] * ch["decay"]).astype(BF16)
        ch["w"] = jnp.concatenate([eye, x], axis=1)
    for _ in range(6):
        for ch in chains:
            wh, wl = _split(ch["w"])
            rhs_h = jnp.concatenate([zeros_w2, jnp.concatenate([wh, wl], axis=1)], axis=0)
            rhs_l = jnp.concatenate([zeros_w, wh], axis=0)
            pw = _dot(wh, rhs_h)
            pw = pw[:, :2 * c] + pw[:, 2 * c:] + _dot(wl, rhs_l)
            ch["w"] = jnp.where(lane2 < c, ch["w"], 0.0) + pw
    for ch in chains:
        rhs = jnp.concatenate([ch["v"] * ch["beta_col"], ch["kb"] * ch["egc"]], axis=1).astype(BF16)
        sol = _dot(ch["w"][:, :c].astype(BF16), rhs)
        ch["u"], ch["wv"] = sol[:, :GDN_DV], sol[:, GDN_DV:].astype(BF16)
    for ch in chains:
        ch["vn"] = (ch["u"] - _dot(ch["wv"], ch["sb"])).astype(BF16)
    outs = []
    for ch in chains:
        o = ch["qs"] + _dot(ch["intra"], ch["vn"])
        kd = (ch["k"] * jnp.exp(ch["glast"] - ch["gc_col"])).astype(BF16)
        s_new = ch["s"] * jnp.exp(ch["glast"]) + _dot_tn(kd, ch["vn"])
        outs.append((o, s_new))
    return outs


def _gdn_kernel(qkv_ref, ab_ref, abt_ref, cw_ref, al_ref, dt_ref, alt_ref, dtt_ref, s0_ref, o_ref, sf_ref,
                x_s, gcol_s, grow_s, beta_s, st_s, *, t):
    c = GDN_CHUNK
    n = t // c
    nh = GDN_HEADS
    ab = ab_ref[0]
    gact = -jnp.exp(al_ref[...]) * jax.nn.softplus(ab + dt_ref[...])
    lane16 = lax.broadcasted_iota(jnp.int32, ab.shape, 1)
    beta_s[...] = jnp.where(lane16 < 2 * nh, gact, jax.nn.sigmoid(ab))
    rowt = lax.broadcasted_iota(jnp.int32, (t, LANE), 0)
    for j in range(3 * nh):
        x = qkv_ref[0, :, j * LANE:(j + 1) * LANE].astype(F32)
        xp = jnp.where(rowt == 0, 0.0, pltpu.roll(x, 1, 0))
        xn = jnp.where(rowt == t - 1, 0.0, pltpu.roll(x, t - 1, 0))
        cw = cw_ref[:, j * LANE:(j + 1) * LANE]
        y = _silu(xp * cw[0:1, :] + x * cw[1:2, :] + xn * cw[2:3, :])
        if j < 2 * nh:
            y = y * lax.rsqrt(jnp.sum(y * y, axis=-1, keepdims=True) + EPS)
        if j < nh:
            y = y * GDN_DK ** -0.5
        x_s[:, j * LANE:(j + 1) * LANE] = y
    r64 = lax.broadcasted_iota(jnp.int32, (c, c), 0)
    c64 = lax.broadcasted_iota(jnp.int32, (c, c), 1)
    tril = (r64 >= c64).astype(F32)
    triu = (r64 <= c64).astype(F32)
    lane_c = lax.broadcasted_iota(jnp.int32, (c, 16), 1)
    sub_c = lax.broadcasted_iota(jnp.int32, (16, c), 0)

    def cum_body(i, carry):
        r0 = pl.multiple_of(i * c, c)
        g = beta_s[pl.ds(r0, c), :]
        gcol_s[pl.ds(r0, c), :] = jnp.where(lane_c < nh, _dot_exact(tril, g), _dot_exact(triu, g))
        gt = -jnp.exp(alt_ref[...]) * jax.nn.softplus(abt_ref[0, i] + dtt_ref[...])
        grow_s[i] = jnp.where(sub_c < nh, _dot_exact(gt, triu), _dot_exact(gt, tril))
        return carry

    lax.fori_loop(0, n, cum_body, 0)
    for d in range(2):
        for hh in range(nh):
            st_s[d * nh + hh] = s0_ref[0, d, hh]
    o_ref[...] = jnp.zeros_like(o_ref)

    def body(i, carry):
        chains = []
        for d in range(2):
            ci = i if d == 0 else n - 1 - i
            r0 = pl.multiple_of(ci * c, c)
            gcol = gcol_s[pl.ds(r0, c), :]
            bet = beta_s[pl.ds(r0, c), :]
            grow = grow_s[ci]
            last = c - 1 if d == 0 else 0
            for hh in range(nh):
                ch = d * nh + hh
                chains.append(dict(
                    q=x_s[pl.ds(r0, c), hh * LANE:(hh + 1) * LANE],
                    k=x_s[pl.ds(r0, c), (nh + hh) * LANE:(nh + hh + 1) * LANE],
                    v=x_s[pl.ds(r0, c), (2 * nh + hh) * LANE:(2 * nh + hh + 1) * LANE],
                    gc_col=gcol[:, ch:ch + 1], gc_row=grow[ch:ch + 1, :],
                    beta_col=bet[:, 2 * nh + ch:2 * nh + ch + 1], glast=gcol[last:last + 1, ch:ch + 1],
                    s=st_s[ch], upper=(d == 1), r0=r0, hh=hh, ch=ch))
        for chn, (o, s_new) in zip(chains, _gdn_chunks(chains)):
            st_s[chn["ch"]] = s_new
            o_ref[0, pl.ds(chn["r0"], c), chn["hh"] * LANE:(chn["hh"] + 1) * LANE] += o
        return carry

    lax.fori_loop(0, n, body, 0)
    for d in range(2):
        for hh in range(nh):
            sf_ref[0, d, hh] = st_s[d * nh + hh]


def _gdn(qkv, ab, abt, cw, al, dt, alt, dtt, s0):
    b, t, _ = qkv.shape
    n = t // GDN_CHUNK
    per_b = lambda a: pl.BlockSpec((1,) + a.shape[1:], lambda i: (i,) + (0,) * (a.ndim - 1))
    full = lambda a: pl.BlockSpec(a.shape, lambda i: (0,) * a.ndim)
    return pl.pallas_call(
        functools.partial(_gdn_kernel, t=t), grid=(b,),
        in_specs=[per_b(qkv), per_b(ab), per_b(abt), full(cw), full(al), full(dt), full(alt), full(dtt), per_b(s0)],
        out_specs=[pl.BlockSpec((1, t, GDN_HEADS * GDN_DV), lambda i: (i, 0, 0)), per_b(s0)],
        out_shape=[jax.ShapeDtypeStruct((b, t, GDN_HEADS * GDN_DV), F32), jax.ShapeDtypeStruct(s0.shape, F32)],
        scratch_shapes=[pltpu.VMEM((t, 3 * GDN_HEADS * LANE), F32), pltpu.VMEM((t, 16), F32),
                        pltpu.VMEM((n, 16, GDN_CHUNK), F32), pltpu.VMEM((t, 16), F32),
                        pltpu.VMEM((2 * GDN_HEADS, GDN_DK, GDN_DV), F32)],
        compiler_params=_cparams(1), name="gdn",
    )(qkv, ab, abt, cw, al, dt, alt, dtt, s0)


def _attd_kernel(*refs, n_ctx):
    if n_ctx:
        q_ref, k_ref, vt_ref, kc_ref, vct_ref, o_ref = refs
    else:
        q_ref, k_ref, vt_ref, o_ref = refs
    n_new = k_ref.shape[1]
    grp = ATT_HEADS // ATT_KV_HEADS
    chunks = []
    for c0, cn in _key_chunks(n_new):
        chunks.append((
            lambda g, c0=c0, cn=cn: k_ref[0, c0:c0 + cn, g * LANE:(g + 1) * LANE],
            lambda g, c0=c0, cn=cn: vt_ref[0, g * LANE:(g + 1) * LANE, c0:c0 + cn],
            None))
    if n_ctx:
        for c0, cn in _key_chunks(n_ctx):
            chunks.append((
                lambda g, c0=c0, cn=cn: kc_ref[0, c0:c0 + cn, g * LANE:(g + 1) * LANE].astype(BF16),
                lambda g, c0=c0, cn=cn: vct_ref[0, g * LANE:(g + 1) * LANE, c0:c0 + cn].astype(BF16),
                None))
    qs = [q_ref[0, :, h * LANE:(h + 1) * LANE] for h in range(ATT_HEADS)]
    outs = _attend_heads(qs, [h // grp for h in range(ATT_HEADS)], chunks)
    o_ref[0] = jnp.concatenate(outs, axis=0).T.astype(BF16)


def _attd(q, k, vt, k_c, v_ct, tq):
    b, t, _ = q.shape
    n_ctx = 0 if k_c is None else k_c.shape[1]
    per_b = lambda a: pl.BlockSpec((1,) + a.shape[1:], lambda i, j: (i, 0, 0))
    in_specs = [pl.BlockSpec((1, tq, 512), lambda i, j: (i, j, 0)), per_b(k), per_b(vt)]
    args = [q, k, vt]
    if n_ctx:
        in_specs += [per_b(k_c), per_b(v_ct)]
        args += [k_c, v_ct]
    return pl.pallas_call(
        functools.partial(_attd_kernel, n_ctx=n_ctx),
        grid=(b, t // tq), in_specs=in_specs,
        out_specs=pl.BlockSpec((1, tq, 512), lambda i, j: (i, j, 0)),
        out_shape=jax.ShapeDtypeStruct((b, t, 512), BF16),
        compiler_params=_cparams(2), name="attd_dec" if n_ctx else "attd_ctx",
    )(*args)


def _outproj1_kernel(oc_ref, zc_ref, od_ref, zd_ref, x_ref, mod_ref, gn_ref, w_ref, lnf_ref, y_ref):
    zc = zc_ref[0].astype(F32)
    parts = []
    for j in range(GDN_HEADS):
        oc = _rms_rows(oc_ref[0, :, j * LANE:(j + 1) * LANE], gn_ref[...])
        parts.append((oc * _silu(zc[:, j * LANE:(j + 1) * LANE])).astype(BF16))
    gc = jnp.concatenate(parts, axis=1)
    gd = (od_ref[0].astype(F32) * _silu(zd_ref[0].astype(F32))).astype(BF16)
    y = _dot(gc, w_ref[0:512, :]) + _dot(gd, w_ref[512:1024, :])
    x2 = x_ref[0] + mod_ref[0, 2:3, :] * y
    y_ref[0] = _rms_rows(x2, lnf_ref[...])


def _outproj1(oc, zc, od, zd, x, mod, gn, w, lnf):
    b, t, d = x.shape
    tr = _row_tile(t)
    bm = mod.shape[0]
    rows = lambda c: pl.BlockSpec((1, tr, c), lambda i, j: (i, j, 0))
    full = lambda a: pl.BlockSpec(a.shape, lambda i, j: (0,) * a.ndim)
    return pl.pallas_call(
        _outproj1_kernel, grid=(b, t // tr),
        in_specs=[rows(512), rows(512), rows(512), rows(512), rows(d),
                  pl.BlockSpec((1, 3, d), (lambda i, j: (i, 0, 0)) if bm > 1 else (lambda i, j: (0, 0, 0))),
                  full(gn), full(w), full(lnf)],
        out_specs=rows(d), out_shape=jax.ShapeDtypeStruct((b, t, d), F32),
        compiler_params=_cparams(2), name="outproj1",
    )(oc, zc, od, zd, x, mod, gn, w, lnf)


def _rope_table(n_tok, rot_dim, used, block=LANE):
    quarter = rot_dim // 4
    inv = ROPE_THETA ** (-jnp.arange(quarter, dtype=F32) / quarter)
    tt = jnp.arange(n_tok)
    pos = jnp.stack([tt // GRID_W, tt % GRID_W], axis=-1).astype(F32)
    ang = pos[:, :, None] * inv
    cos, sin = jnp.cos(ang), jnp.sin(ang)
    c = jnp.concatenate([cos, cos], axis=-1).reshape(n_tok, rot_dim)
    s = jnp.concatenate([-sin, sin], axis=-1).reshape(n_tok, rot_dim)
    return c, s


def _place(tab, fill, off, width):
    n = tab.shape[0]
    left = jnp.full((n, off), fill, F32)
    right = jnp.full((n, width - off - tab.shape[1]), fill, F32)
    return jnp.concatenate([left, tab, right], axis=1)


def _prep_l0(w_in0, w_uq, w_ukv):
    d = w_in0.shape[0]
    cq, ckv, kr, za, qb, kb, vb, zb = jnp.split(w_in0, [384, 640, 672, 1184, 1696, 1824, 1952], axis=1)
    kr_pad = jnp.concatenate([jnp.zeros((d, 64), F32), kr, jnp.zeros((d, 32), F32)], axis=1)
    w = jnp.concatenate([cq, ckv, kr_pad, za, qb, kb, vb, zb], axis=1).astype(BF16)
    wvbt = vb.T.astype(BF16)
    uq = w_uq.reshape(MLA_Q_LORA, MLA_HEADS, MLA_NOPE + MLA_ROPE)
    wuq = jnp.pad(uq, ((0, 0), (0, 0), (0, LANE - MLA_NOPE - MLA_ROPE))).reshape(MLA_Q_LORA, MLA_HEADS * LANE)
    ukv = w_ukv.reshape(MLA_KV_LORA, MLA_HEADS, MLA_NOPE + MLA_V)
    wuk = jnp.pad(ukv[:, :, :MLA_NOPE], ((0, 0), (0, 0), (0, LANE - MLA_NOPE))).reshape(MLA_KV_LORA, MLA_HEADS * LANE)
    wuvt = ukv[:, :, MLA_NOPE:].reshape(MLA_KV_LORA, MLA_HEADS * MLA_V).T
    return w, wvbt, wuq.astype(BF16), wuk.astype(BF16), wuvt.astype(BF16)


def _prep_l1(w_in1):
    d = w_in1.shape[0]
    qkv, a, bb, zc, qd, kd, vd, zd = jnp.split(w_in1, [1536, 1544, 1552, 2064, 2576, 2832, 3088], axis=1)
    w = jnp.concatenate([qkv, zc, qd, kd, vd, zd, a, bb, jnp.zeros((d, 112), F32)], axis=1).astype(BF16)
    return w, vd.T.astype(BF16)


def _chunk_rows(ab):
    b, t, c = ab.shape
    return jnp.swapaxes(ab.reshape(b, t // GDN_CHUNK, GDN_CHUNK, c), 2, 3)


def _trunk(x, mod, caches, p, tables, tq):
    dec = caches is not None
    t0m, t0s, t1 = tables if dec else (None, None, None)
    (qa, ckv, kr, za, qb, kb, vbt, zb, *ctx0) = _inproj0(
        x, mod[0], p["ln0"], p["w0"], p["wvbt"], p["qn"], p["wuq"], p["kvn"], (t0m + t0s) if dec else None)
    if dec:
        ckv_c, kr_c, kb_c, vb_ct, s0, kd_c, vd_ct = caches
    else:
        ckv_c = kr_c = kb_c = vb_ct = kd_c = vd_ct = None
        s0 = jnp.zeros((x.shape[0], 2, GDN_HEADS, GDN_DK, GDN_DV), F32)
    oa = _mla(qa, ckv, kr, ckv_c, kr_c, p["wuk"], p["wuvt"], tq)
    ob = _swa(p["sink"], qb, kb, vbt, kb_c, vb_ct, tq)
    x1 = _outproj0(oa, za, ob, zb, x, mod[0], p["wout0"])
    (qkv, zc, qd, kd, vdt, zd, ab, *ctx1) = _inproj1(
        x1, mod[1], p["ln1"], p["w1"], p["wvdt"], p["aqn"], p["akn"], t1 if dec else None)
    oc, sfin = _gdn(qkv, ab, _chunk_rows(ab), p["cw"], p["al"], p["dt"], p["alt"], p["dtt"], s0)
    od = _attd(qd, kd, vdt, kd_c, vd_ct, tq)
    y = _outproj1(oc, zc, od, zd, x1, mod[1], p["gn"], p["wout1"], p["lnf"])
    return y, ctx0, sfin, ctx1


def kernel(x_prompt, x_sample, cache_l0_mla_ckv, cache_l0_mla_krope, cache_l0_swa_k, cache_l0_swa_v,
           state_l1_gdn, cache_l1_attn_k, cache_l1_attn_v, c, c_ctx,
           w_mod0, b_mod0, ln0, w_in0, mla_q_norm, w_uq, mla_kv_norm, w_ukv, swa_sink, w_out0,
           w_mod1, b_mod1, ln1, w_in1, gdn_conv, gdn_a_log, gdn_dt_bias, gdn_norm, att_q_norm, att_k_norm, w_out1,
           ln_f):
    d = x_prompt.shape[-1]
    bd, td = x_sample.shape[:2]
    bc, tc = x_prompt.shape[:2]
    past = cache_l0_mla_ckv.shape[1]
    row = lambda v: v.reshape(1, -1)
    w0, wvbt, wuq, wuk, wuvt = _prep_l0(w_in0, w_uq, w_ukv)
    w1, wvdt = _prep_l1(w_in1)
    al8 = gdn_a_log.reshape(1, 2 * GDN_HEADS)
    dt8 = gdn_dt_bias.reshape(1, 2 * GDN_HEADS)
    al16 = jnp.pad(al8, ((0, 0), (0, 8)))
    dt16 = jnp.pad(dt8, ((0, 0), (0, 8)))
    p = dict(ln0=row(ln0), w0=w0, wvbt=wvbt, qn=row(mla_q_norm), wuq=wuq, kvn=row(mla_kv_norm), wuk=wuk, wuvt=wuvt,
             sink=swa_sink, wout0=w_out0.astype(BF16), ln1=row(ln1), w1=w1, wvdt=wvdt, aqn=row(att_q_norm),
             akn=row(att_k_norm), cw=gdn_conv, al=al16, dt=dt16, alt=al16.T, dtt=dt16.T, gn=row(gdn_norm),
             wout1=w_out1.astype(BF16), lnf=row(ln_f))
    n_rows = -(-(bd + 1) // 8) * 8
    c_rows = jnp.concatenate([c, c_ctx[None, :], jnp.zeros((n_rows - bd - 1, d), F32)], axis=0)
    mods = [_mod(c_rows, w_mod0, b_mod0), _mod(c_rows, w_mod1, b_mod1)]
    mod_dec = [m[:bd].reshape(bd, 3, d) for m in mods]
    mod_ctx = [m[bd:bd + 1].reshape(1, 3, d) for m in mods]
    cm, sm = _rope_table(td, MLA_ROPE, LANE)
    t0m = (_place(cm, 1.0, MLA_NOPE, LANE), _place(sm, 0.0, MLA_NOPE, LANE))
    cs, ss = _rope_table(td, SWA_DH, SWA_DH)
    t0s = (jnp.tile(cs, (1, LANE // SWA_DH)), jnp.tile(ss, (1, LANE // SWA_DH)))
    t1 = _rope_table(td, ATT_DH, LANE)
    caches = (cache_l0_mla_ckv,
              jnp.pad(cache_l0_mla_krope, ((0, 0), (0, 0), (MLA_NOPE, LANE - MLA_NOPE - MLA_ROPE))),
              cache_l0_swa_k.reshape(bd, past, SWA_KV_HEADS * SWA_DH),
              jnp.swapaxes(cache_l0_swa_v.reshape(bd, past, SWA_KV_HEADS * SWA_DH), 1, 2),
              state_l1_gdn,
              cache_l1_attn_k.reshape(bd, past, ATT_KV_HEADS * ATT_DH),
              jnp.swapaxes(cache_l1_attn_v.reshape(bd, past, ATT_KV_HEADS * ATT_DH), 1, 2))
    y_prompt, ctx0, sfin, ctx1 = _trunk(x_prompt, mod_ctx, None, p, None, tq=tc)
    y_sample, _, _, _ = _trunk(x_sample, mod_dec, caches, p, (t0m, t0s, t1), tq=256)
    ckv32, kr32, kb32, vb32 = ctx0
    kd32, vd32 = ctx1
    return (y_prompt, y_sample, ckv32, kr32,
            kb32.reshape(bc, tc, SWA_KV_HEADS, SWA_DH), vb32.reshape(bc, tc, SWA_KV_HEADS, SWA_DH),
            sfin, kd32.reshape(bc, tc, ATT_KV_HEADS, ATT_DH), vd32.reshape(bc, tc, ATT_KV_HEADS, ATT_DH))
```

```python
import functools
import math

import jax
import jax.numpy as jnp
from jax import lax
from jax.experimental import pallas as pl
from jax.experimental.pallas import tpu as pltpu

F32 = jnp.float32
BF16 = jnp.bfloat16

GRID_W = 64
ROPE_THETA = 10000.0
EPS = 1e-6
WINDOW = 128
MLA_HEADS, MLA_NOPE, MLA_ROPE, MLA_V = 8, 64, 32, 64
MLA_Q_LORA, MLA_KV_LORA = 384, 256
SWA_HEADS, SWA_KV_HEADS, SWA_DH = 8, 2, 64
GDN_HEADS, GDN_DK, GDN_DV, CONV_K, GDN_CHUNK = 4, 128, 128, 3, 64
ATT_HEADS, ATT_KV_HEADS, ATT_DH = 4, 2, 128
LANE = 128
LOG2E = math.log2(math.e)
NEG = -1e30
VMEM_LIMIT = 56 * 1024 * 1024


def _cparams(n_axes):
    return pltpu.CompilerParams(dimension_semantics=("arbitrary",) * n_axes, vmem_limit_bytes=VMEM_LIMIT)


def _dot(a, b):
    return jnp.dot(a, b, preferred_element_type=F32)


def _dot_nt(a, b):
    return lax.dot_general(a, b, (((1,), (1,)), ((), ())), preferred_element_type=F32)


def _dot_tn(a, b):
    return lax.dot_general(a, b, (((0,), (0,)), ((), ())), preferred_element_type=F32)


def _dot_exact(a, b):
    return jnp.dot(a, b, preferred_element_type=F32, precision=lax.Precision.HIGHEST)


def _split(a):
    hi = a.astype(BF16)
    return hi, (a - hi.astype(F32)).astype(BF16)


def _silu(x):
    return x * jax.nn.sigmoid(x)


def _rms_rows(x, g):
    return x * lax.rsqrt(jnp.mean(x * x, axis=-1, keepdims=True) + EPS) * g


def _rope_block(x, cos, sin, half):
    lane = lax.broadcasted_iota(jnp.int32, x.shape, 1)
    first = (lane // half) % 2 == 0
    partner = jnp.where(first, pltpu.roll(x, LANE - half, 1), pltpu.roll(x, half, 1))
    return x * cos + partner * sin


def _mod_kernel(c_ref, w_ref, b_ref, o_ref):
    a = _silu(c_ref[...]).astype(BF16)
    o_ref[...] = _dot(a, w_ref[...].astype(BF16)) + b_ref[...]


def _mod(c_rows, w_mod, b_mod):
    r, d = c_rows.shape
    n = w_mod.shape[1]
    tn = 1024
    return pl.pallas_call(
        _mod_kernel,
        grid=(n // tn,),
        in_specs=[pl.BlockSpec((r, d), lambda j: (0, 0)),
                  pl.BlockSpec((d, tn), lambda j: (0, j)),
                  pl.BlockSpec((1, tn), lambda j: (0, j))],
        out_specs=pl.BlockSpec((r, tn), lambda j: (0, j)),
        out_shape=jax.ShapeDtypeStruct((r, n), F32),
        compiler_params=_cparams(1),
        name="mod",
    )(c_rows, w_mod, b_mod.reshape(1, n))


def _adaln(x, mod_ref, ln_ref):
    h = _rms_rows(x, ln_ref[...])
    return h * (1.0 + mod_ref[0, 1:2, :]) + mod_ref[0, 0:1, :]


L0_OFF = dict(cq=0, ckv=384, kr=640, za=768, qb=1280, kb=1792, vb=1920, zb=2048)
L0_W = 2560


def _inproj0_kernel(*refs, rope):
    if rope:
        (x_ref, mod_ref, ln_ref, w_ref, wvbt_ref, qn_ref, wuq_ref, kvn_ref, cm_ref, sm_ref, cs_ref, ss_ref,
         qa_ref, ckv_ref, kr_ref, za_ref, qb_ref, kb_ref, vbt_ref, zb_ref) = refs
    else:
        (x_ref, mod_ref, ln_ref, w_ref, wvbt_ref, qn_ref, wuq_ref, kvn_ref,
         qa_ref, ckv_ref, kr_ref, za_ref, qb_ref, kb_ref, vbt_ref, zb_ref,
         ckv32_ref, kr32_ref, kb32_ref, vb32_ref) = refs
    h = _adaln(x_ref[0], mod_ref, ln_ref).astype(BF16)
    u = _dot(h, w_ref[...])
    o = L0_OFF
    cq = _rms_rows(u[:, o["cq"]:o["cq"] + 384], qn_ref[...]).astype(BF16)
    qa = _dot(cq, wuq_ref[...])
    ckv = _rms_rows(u[:, o["ckv"]:o["ckv"] + 256], kvn_ref[...])
    kr = u[:, o["kr"]:o["kr"] + 128]
    qb = u[:, o["qb"]:o["qb"] + 512]
    kb = u[:, o["kb"]:o["kb"] + 128]
    vb = u[:, o["vb"]:o["vb"] + 128]
    if not rope:
        ckv32_ref[0] = ckv
        kr32_ref[0] = kr[:, 64:96]
        kb32_ref[0] = kb
        vb32_ref[0] = vb
    qa_scale = (MLA_NOPE + MLA_ROPE) ** -0.5 * LOG2E
    qb_scale = SWA_DH ** -0.5 * LOG2E
    for j in range(MLA_HEADS):
        blk = qa[:, j * LANE:(j + 1) * LANE]
        if rope:
            blk = _rope_block(blk, cm_ref[...], sm_ref[...], MLA_ROPE // 4)
        qa_ref[0, :, j * LANE:(j + 1) * LANE] = (blk * qa_scale).astype(BF16)
    for j in range(SWA_HEADS * SWA_DH // LANE):
        blk = qb[:, j * LANE:(j + 1) * LANE]
        if rope:
            blk = _rope_block(blk, cs_ref[...], ss_ref[...], SWA_DH // 4)
        qb_ref[0, :, j * LANE:(j + 1) * LANE] = (blk * qb_scale).astype(BF16)
    if rope:
        kr = _rope_block(kr, cm_ref[...], sm_ref[...], MLA_ROPE // 4)
        kb = _rope_block(kb, cs_ref[...], ss_ref[...], SWA_DH // 4)
    ckv_ref[0] = ckv.astype(BF16)
    kr_ref[0] = kr.astype(BF16)
    kb_ref[0] = kb.astype(BF16)
    za_ref[0] = u[:, o["za"]:o["za"] + 512].astype(BF16)
    zb_ref[0] = u[:, o["zb"]:o["zb"] + 512].astype(BF16)
    vbt_ref[0] = _dot_nt(wvbt_ref[...], h).astype(BF16)


def _row_tile(t):
    return 512 if t % 512 == 0 else 256


def _inproj0(x, mod, ln, w, wvbt, qn, wuq, kvn, tables):
    b, t, d = x.shape
    tr = _row_tile(t)
    rope = tables is not None
    bm = mod.shape[0]
    full = lambda a: pl.BlockSpec(a.shape, lambda i, j: (0,) * a.ndim)
    rows = lambda c: pl.BlockSpec((1, tr, c), lambda i, j: (i, j, 0))
    in_specs = [rows(d), pl.BlockSpec((1, 3, d), (lambda i, j: (i, 0, 0)) if bm > 1 else (lambda i, j: (0, 0, 0))),
                full(ln), full(w), full(wvbt), full(qn), full(wuq), full(kvn)]
    args = [x, mod, ln, w, wvbt, qn, wuq, kvn]
    if rope:
        for tab in tables:
            in_specs.append(pl.BlockSpec((tr, LANE), lambda i, j: (j, 0)))
            args.append(tab)
    out_shape = [jax.ShapeDtypeStruct((b, t, 1024), BF16), jax.ShapeDtypeStruct((b, t, 256), BF16),
                 jax.ShapeDtypeStruct((b, t, 128), BF16), jax.ShapeDtypeStruct((b, t, 512), BF16),
                 jax.ShapeDtypeStruct((b, t, 512), BF16), jax.ShapeDtypeStruct((b, t, 128), BF16),
                 jax.ShapeDtypeStruct((b, 128, t), BF16), jax.ShapeDtypeStruct((b, t, 512), BF16)]
    out_specs = [rows(1024), rows(256), rows(128), rows(512), rows(512), rows(128),
                 pl.BlockSpec((1, 128, tr), lambda i, j: (i, 0, j)), rows(512)]
    if not rope:
        out_shape += [jax.ShapeDtypeStruct((b, t, 256), F32), jax.ShapeDtypeStruct((b, t, 32), F32),
                      jax.ShapeDtypeStruct((b, t, 128), F32), jax.ShapeDtypeStruct((b, t, 128), F32)]
        out_specs += [rows(256), rows(32), rows(128), rows(128)]
    return pl.pallas_call(
        functools.partial(_inproj0_kernel, rope=rope),
        grid=(b, t // tr), in_specs=in_specs, out_specs=out_specs, out_shape=out_shape,
        compiler_params=_cparams(2), name="inproj0_dec" if rope else "inproj0_ctx",
    )(*args)


KEY_CHUNK = 512
ATT_LOOKAHEAD = 4


def _attend_heads(qs, kv_of, chunks, sinks=None):
    nh = len(qs)
    tq = qs[0].shape[0]
    m = [None] * nh
    l = [None] * nh
    acc = [None] * nh
    if sinks is not None:
        m = [jnp.full((1, tq), sk, F32) for sk in sinks]
        l = [jnp.ones((1, tq), F32) for _ in sinks]
    items = [(ci, i) for ci in range(len(chunks)) for i in range(nh)]
    loaded = {}

    def kv(ci, src):
        if (ci, src) not in loaded:
            loaded[(ci, src)] = (chunks[ci][0](src), chunks[ci][1](src))
        return loaded[(ci, src)]

    scores = {}
    for t in range(len(items) + ATT_LOOKAHEAD):
        if t < len(items):
            ci, i = items[t]
            scores[t] = _dot_nt(kv(ci, kv_of[i])[0], qs[i])
        t0 = t - ATT_LOOKAHEAD
        if t0 < 0:
            continue
        ci, i = items[t0]
        mask = chunks[ci][2]
        si = scores.pop(t0)
        if mask is not None:
            si = jnp.where(mask, si, NEG)
        cm = si.max(axis=0, keepdims=True)
        alpha = None
        if m[i] is None:
            m_new = cm
        else:
            m_new = jnp.maximum(m[i], cm)
            alpha = jnp.exp2(m[i] - m_new)
        p = jnp.exp2(si - m_new)
        ls = p.sum(axis=0, keepdims=True)
        l[i] = ls if l[i] is None else l[i] * alpha + ls
        m[i] = m_new
        pv = _dot(kv(ci, kv_of[i])[1], p.astype(BF16))
        acc[i] = pv if acc[i] is None else acc[i] * alpha + pv
    return [acc[i] * (1.0 / l[i]) for i in range(nh)]


def _key_chunks(n):
    step = KEY_CHUNK if n % KEY_CHUNK == 0 else n
    return [(c0, step) for c0 in range(0, n, step)]


def _mla_kernel(*refs, n_new, n_ctx, hp):
    if n_ctx:
        q_ref, ckv_ref, kr_ref, ckvc_ref, krc_ref, wuk_ref, wuvt_ref, o_ref, k_s, vt_s = refs
    else:
        q_ref, ckv_ref, kr_ref, wuk_ref, wuvt_ref, o_ref, k_s, vt_s = refs
    qi, gi = pl.program_id(1), pl.program_id(2)

    @pl.when((qi == 0) & (gi == 0))
    def _():
        def expand(ckv, kr, r0, n):
            kn = _dot(ckv, wuk_ref[...])
            for j in range(MLA_HEADS):
                k_s[j, r0:r0 + n, :] = (kn[:, j * LANE:(j + 1) * LANE] + kr).astype(BF16)
            vt_s[:, r0:r0 + n] = _dot_nt(wuvt_ref[...], ckv).astype(BF16)

        blk = 512 if n_new % 512 == 0 else 256
        for r0 in range(0, n_new, blk):
            expand(ckv_ref[0, r0:r0 + blk, :], kr_ref[0, r0:r0 + blk, :].astype(F32), r0, blk)
        if n_ctx:
            expand(ckvc_ref[0].astype(BF16), krc_ref[0], n_new, n_ctx)

    qs = [q_ref[0, :, j * LANE:(j + 1) * LANE] for j in range(hp)]
    chunks = []
    for c0, cn in _key_chunks(n_new + n_ctx):
        chunks.append((
            lambda j, c0=c0, cn=cn: k_s[gi * hp + j, c0:c0 + cn, :],
            lambda j, c0=c0, cn=cn: vt_s[pl.ds(pl.multiple_of((gi * hp + j) * MLA_V, MLA_V), MLA_V), c0:c0 + cn],
            None))
    outs = _attend_heads(qs, list(range(hp)), chunks)
    o_ref[0] = jnp.concatenate(outs, axis=0).T.astype(BF16)


def _mla(q, ckv, kr, ckv_c, kr_c, wuk, wuvt, tq):
    b, t, _ = q.shape
    n_ctx = 0 if ckv_c is None else ckv_c.shape[1]
    hp = 4
    tk = t + n_ctx
    rows_q = pl.BlockSpec((1, tq, hp * LANE), lambda i, j, g: (i, j, g))
    per_b = lambda a: pl.BlockSpec((1,) + a.shape[1:], lambda i, j, g: (i, 0, 0))
    full = lambda a: pl.BlockSpec(a.shape, lambda i, j, g: (0, 0))
    in_specs = [rows_q, per_b(ckv), per_b(kr)]
    args = [q, ckv, kr]
    if n_ctx:
        in_specs += [per_b(ckv_c), per_b(kr_c)]
        args += [ckv_c, kr_c]
    in_specs += [full(wuk), full(wuvt)]
    args += [wuk, wuvt]
    return pl.pallas_call(
        functools.partial(_mla_kernel, n_new=t, n_ctx=n_ctx, hp=hp),
        grid=(b, t // tq, MLA_HEADS // hp), in_specs=in_specs,
        out_specs=pl.BlockSpec((1, tq, hp * MLA_V), lambda i, j, g: (i, j, g)),
        out_shape=jax.ShapeDtypeStruct((b, t, MLA_HEADS * MLA_V), BF16),
        scratch_shapes=[pltpu.VMEM((MLA_HEADS, tk, LANE), BF16), pltpu.VMEM((MLA_HEADS * MLA_V, tk), BF16)],
        compiler_params=_cparams(3), name="mla_dec" if n_ctx else "mla_ctx",
    )(*args)


def _swa_kernel(*refs, n_new, n_ctx, tq):
    if n_ctx:
        sink_ref, q_ref, k_ref, vt_ref, kc_ref, vct_ref, o_ref = refs
    else:
        sink_ref, q_ref, k_ref, vt_ref, o_ref = refs
    qi = pl.program_id(1)
    grp = SWA_HEADS // SWA_KV_HEADS
    if n_ctx:
        span = tq + 2 * WINDOW
        q0 = qi * tq
        start = pl.multiple_of(jnp.clip(q0 - WINDOW, 0, n_new - span), LANE)
        kpos = start + lax.broadcasted_iota(jnp.int32, (span, tq), 0)
        qpos = q0 + lax.broadcasted_iota(jnp.int32, (span, tq), 1)
        band = jnp.abs(kpos - qpos) <= WINDOW
    dh = SWA_DH
    chunks = []
    if n_ctx:
        for c0, cn in _key_chunks(span):
            chunks.append((
                lambda g, c0=c0, cn=cn: k_ref[0, pl.ds(start + c0, cn), g * dh:(g + 1) * dh],
                lambda g, c0=c0, cn=cn: vt_ref[0, g * dh:(g + 1) * dh, pl.ds(start + c0, cn)],
                band[c0:c0 + cn]))
        for c0, cn in _key_chunks(n_ctx):
            chunks.append((
                lambda g, c0=c0, cn=cn: kc_ref[0, c0:c0 + cn, g * dh:(g + 1) * dh].astype(BF16),
                lambda g, c0=c0, cn=cn: vct_ref[0, g * dh:(g + 1) * dh, c0:c0 + cn].astype(BF16),
                None))
    else:
        for c0, cn in _key_chunks(n_new):
            chunks.append((
                lambda g, c0=c0, cn=cn: k_ref[0, c0:c0 + cn, g * dh:(g + 1) * dh],
                lambda g, c0=c0, cn=cn: vt_ref[0, g * dh:(g + 1) * dh, c0:c0 + cn],
                None))
    qs = [q_ref[0, :, h * dh:(h + 1) * dh] for h in range(SWA_HEADS)]
    sinks = [sink_ref[h] * LOG2E for h in range(SWA_HEADS)]
    outs = _attend_heads(qs, [h // grp for h in range(SWA_HEADS)], chunks, sinks)
    o_ref[0] = jnp.concatenate(outs, axis=0).T.astype(BF16)


def _swa(sink, q, k, vt, k_c, v_ct, tq):
    b, t, _ = q.shape
    n_ctx = 0 if k_c is None else k_c.shape[1]
    per_b = lambda a: pl.BlockSpec((1,) + a.shape[1:], lambda i, j: (i, 0, 0))
    in_specs = [pl.BlockSpec(memory_space=pltpu.SMEM), pl.BlockSpec((1, tq, 512), lambda i, j: (i, j, 0)),
                per_b(k), per_b(vt)]
    args = [sink, q, k, vt]
    if n_ctx:
        in_specs += [per_b(k_c), per_b(v_ct)]
        args += [k_c, v_ct]
    return pl.pallas_call(
        functools.partial(_swa_kernel, n_new=t, n_ctx=n_ctx, tq=tq),
        grid=(b, t // tq), in_specs=in_specs,
        out_specs=pl.BlockSpec((1, tq, 512), lambda i, j: (i, j, 0)),
        out_shape=jax.ShapeDtypeStruct((b, t, 512), BF16),
        compiler_params=_cparams(2), name="swa_dec" if n_ctx else "swa_ctx",
    )(*args)


def _outproj0_kernel(oa_ref, za_ref, ob_ref, zb_ref, x_ref, mod_ref, w_ref, y_ref):
    ga = (oa_ref[0].astype(F32) * _silu(za_ref[0].astype(F32))).astype(BF16)
    gb = (ob_ref[0].astype(F32) * _silu(zb_ref[0].astype(F32))).astype(BF16)
    y = _dot(ga, w_ref[0:512, :]) + _dot(gb, w_ref[512:1024, :])
    y_ref[0] = x_ref[0] + mod_ref[0, 2:3, :] * y


def _outproj0(oa, za, ob, zb, x, mod, w):
    b, t, d = x.shape
    tr = _row_tile(t)
    bm = mod.shape[0]
    rows = lambda c: pl.BlockSpec((1, tr, c), lambda i, j: (i, j, 0))
    return pl.pallas_call(
        _outproj0_kernel, grid=(b, t // tr),
        in_specs=[rows(512), rows(512), rows(512), rows(512), rows(d),
                  pl.BlockSpec((1, 3, d), (lambda i, j: (i, 0, 0)) if bm > 1 else (lambda i, j: (0, 0, 0))),
                  pl.BlockSpec(w.shape, lambda i, j: (0, 0))],
        out_specs=rows(d), out_shape=jax.ShapeDtypeStruct((b, t, d), F32),
        compiler_params=_cparams(2), name="outproj0",
    )(oa, za, ob, zb, x, mod, w)


L1_OFF = dict(qkv=0, zc=1536, qd=2048, kd=2560, vd=2816, zd=3072, ab=3584)
L1_W = 3712


def _inproj1_kernel(*refs, rope):
    if rope:
        (x_ref, mod_ref, ln_ref, w_ref, wvdt_ref, qn_ref, kn_ref, c_ref, s_ref,
         qkv_ref, zc_ref, qd_ref, kd_ref, vdt_ref, zd_ref, ab_ref) = refs
    else:
        (x_ref, mod_ref, ln_ref, w_ref, wvdt_ref, qn_ref, kn_ref,
         qkv_ref, zc_ref, qd_ref, kd_ref, vdt_ref, zd_ref, ab_ref, kd32_ref, vd32_ref) = refs
    h = _adaln(x_ref[0], mod_ref, ln_ref).astype(BF16)
    u = _dot(h, w_ref[...])
    o = L1_OFF
    qkv_ref[0] = u[:, o["qkv"]:o["qkv"] + 1536].astype(BF16)
    zc_ref[0] = u[:, o["zc"]:o["zc"] + 512].astype(BF16)
    zd_ref[0] = u[:, o["zd"]:o["zd"] + 512].astype(BF16)
    ab_ref[0] = u[:, o["ab"]:o["ab"] + 16]
    qd_scale = ATT_DH ** -0.5 * LOG2E
    for j in range(ATT_HEADS):
        blk = _rms_rows(u[:, o["qd"] + j * LANE:o["qd"] + (j + 1) * LANE], qn_ref[...])
        if rope:
            blk = _rope_block(blk, c_ref[...], s_ref[...], ATT_DH // 4)
        qd_ref[0, :, j * LANE:(j + 1) * LANE] = (blk * qd_scale).astype(BF16)
    for j in range(ATT_KV_HEADS):
        blk = _rms_rows(u[:, o["kd"] + j * LANE:o["kd"] + (j + 1) * LANE], kn_ref[...])
        if rope:
            blk = _rope_block(blk, c_ref[...], s_ref[...], ATT_DH // 4)
        else:
            kd32_ref[0, :, j * LANE:(j + 1) * LANE] = blk
        kd_ref[0, :, j * LANE:(j + 1) * LANE] = blk.astype(BF16)
    if not rope:
        vd32_ref[0] = u[:, o["vd"]:o["vd"] + 256]
    vdt_ref[0] = _dot_nt(wvdt_ref[...], h).astype(BF16)


def _inproj1(x, mod, ln, w, wvdt, qn, kn, tables):
    b, t, d = x.shape
    tr = _row_tile(t)
    rope = tables is not None
    bm = mod.shape[0]
    full = lambda a: pl.BlockSpec(a.shape, lambda i, j: (0,) * a.ndim)
    rows = lambda c: pl.BlockSpec((1, tr, c), lambda i, j: (i, j, 0))
    in_specs = [rows(d), pl.BlockSpec((1, 3, d), (lambda i, j: (i, 0, 0)) if bm > 1 else (lambda i, j: (0, 0, 0))),
                full(ln), full(w), full(wvdt), full(qn), full(kn)]
    args = [x, mod, ln, w, wvdt, qn, kn]
    if rope:
        for tab in tables:
            in_specs.append(pl.BlockSpec((tr, LANE), lambda i, j: (j, 0)))
            args.append(tab)
    out_shape = [jax.ShapeDtypeStruct((b, t, 1536), BF16), jax.ShapeDtypeStruct((b, t, 512), BF16),
                 jax.ShapeDtypeStruct((b, t, 512), BF16), jax.ShapeDtypeStruct((b, t, 256), BF16),
                 jax.ShapeDtypeStruct((b, 256, t), BF16), jax.ShapeDtypeStruct((b, t, 512), BF16),
                 jax.ShapeDtypeStruct((b, t, 16), F32)]
    out_specs = [rows(1536), rows(512), rows(512), rows(256),
                 pl.BlockSpec((1, 256, tr), lambda i, j: (i, 0, j)), rows(512), rows(16)]
    if not rope:
        out_shape += [jax.ShapeDtypeStruct((b, t, 256), F32), jax.ShapeDtypeStruct((b, t, 256), F32)]
        out_specs += [rows(256), rows(256)]
    return pl.pallas_call(
        functools.partial(_inproj1_kernel, rope=rope),
        grid=(b, t // tr), in_specs=in_specs, out_specs=out_specs, out_shape=out_shape,
        compiler_params=_cparams(2), name="inproj1_dec" if rope else "inproj1_ctx",
    )(*args)


def _gdn_local(blocks):
    c = GDN_CHUNK
    row = lax.broadcasted_iota(jnp.int32, (c, c), 0)
    col = lax.broadcasted_iota(jnp.int32, (c, c), 1)
    lane2 = lax.broadcasted_iota(jnp.int32, (c, 2 * c), 1)
    eye = (row == col).astype(F32)
    eye_t = jnp.concatenate([eye, jnp.zeros((c, c), F32)], axis=1).astype(BF16)
    chains = [ch for blk in blocks for ch in blk["dirs"]]
    for blk in blocks:
        for ch in blk["dirs"]:
            causal = (row <= col) if ch["upper"] else (row >= col)
            ch["strict"] = (row < col) if ch["upper"] else (row > col)
            ch["decay"] = jnp.exp(jnp.where(causal, ch["gc_col"] - ch["gc_row"], -jnp.inf))
            ch["kb"] = blk["k"] * ch["beta_col"]
            ch["egc"] = jnp.exp(ch["gc_col"])
    for blk in blocks:
        lhs = jnp.concatenate([ch["kb"] for ch in blk["dirs"]] + [blk["q"]], axis=0).astype(BF16)
        a = _dot_nt(lhs, blk["k"].astype(BF16))
        nd = len(blk["dirs"])
        for di, ch in enumerate(blk["dirs"]):
            x = jnp.where(ch["strict"], -(a[di * c:(di + 1) * c] * ch["decay"]), 0.0)
            ch["intra"] = (a[nd * c:] * ch["decay"]).astype(BF16)
            ch["w"] = jnp.concatenate([eye, x], axis=1)
    for _ in range(6):
        for ch in chains:
            wh, wl = _split(ch["w"])
            ph = pltpu.roll(ch["w"], c, 1).astype(BF16)
            lhs = jnp.concatenate([wh, jnp.where(lane2 < c, ph, wl)], axis=1)
            ch["w"] = _dot(lhs, jnp.concatenate([eye_t, wh, wl, wh], axis=0))
    for blk in blocks:
        for ch in blk["dirs"]:
            rhs = jnp.concatenate([blk["v"] * ch["beta_col"], ch["kb"] * ch["egc"]], axis=1).astype(BF16)
            sol = _dot(ch["w"][:, :c].astype(BF16), rhs)
            ch["u"], ch["wv"] = sol[:, :GDN_DV].astype(BF16), sol[:, GDN_DV:].astype(BF16)
            ch["qe"] = (blk["q"] * ch["egc"]).astype(BF16)
            ch["kd"] = (blk["k"] * jnp.exp(ch["glast"] - ch["gc_col"])).astype(BF16)
            ch["eg"] = jnp.exp(ch["glast"])


def _gdn_scan(chains):
    c = GDN_CHUNK
    for ch in chains:
        ch["sb"] = ch["s"].astype(BF16)
    for ch in chains:
        r = _dot(jnp.concatenate([ch["wv"], ch["qe"]], axis=0), ch["sb"])
        ch["vn"] = (ch["u"].astype(F32) - r[:c]).astype(BF16)
        ch["qs"] = r[c:]
    outs = []
    for ch in chains:
        o = ch["qs"] + _dot(ch["intra"], ch["vn"])
        s_new = ch["s"] * ch["eg"] + _dot_tn(ch["kd"], ch["vn"])
        outs.append((o, s_new))
    return outs


def _gdn_kernel(qkv_ref, ab_ref, abt_ref, cw_ref, al_ref, dt_ref, alt_ref, dtt_ref, s0_ref, o_ref, sf_ref,
                u_s, wv_s, qe_s, kd_s, in_s, eg_s, xc_s, gcol_s, grow_s, beta_s, st_s, *, t):
    c = GDN_CHUNK
    n = t // c
    nh = GDN_HEADS
    ab = ab_ref[0]
    gact = -jnp.exp(al_ref[...]) * jax.nn.softplus(ab + dt_ref[...])
    lane16 = lax.broadcasted_iota(jnp.int32, ab.shape, 1)
    beta_s[...] = jnp.where(lane16 < 2 * nh, gact, jax.nn.sigmoid(ab))
    r64 = lax.broadcasted_iota(jnp.int32, (c, c), 0)
    c64 = lax.broadcasted_iota(jnp.int32, (c, c), 1)
    tril = (r64 >= c64).astype(F32)
    triu = (r64 <= c64).astype(F32)
    lane_c = lax.broadcasted_iota(jnp.int32, (c, 16), 1)
    sub_c = lax.broadcasted_iota(jnp.int32, (16, c), 0)

    def cum_body(i, carry):
        r0 = pl.multiple_of(i * c, c)
        g = beta_s[pl.ds(r0, c), :]
        gcol_s[pl.ds(r0, c), :] = jnp.where(lane_c < nh, _dot_exact(tril, g), _dot_exact(triu, g))
        gt = -jnp.exp(alt_ref[...]) * jax.nn.softplus(abt_ref[0, i] + dtt_ref[...])
        grow_s[i] = jnp.where(sub_c < nh, _dot_exact(gt, triu), _dot_exact(gt, tril))
        return carry

    lax.fori_loop(0, n, cum_body, 0, unroll=4)

    rb = 2 * c
    rowb = lax.broadcasted_iota(jnp.int32, (rb, LANE), 0)

    def conv_block(r0):
        rp = pl.multiple_of(jnp.maximum(r0 - 16, 0), 16)
        rn = pl.multiple_of(jnp.minimum(r0 + rb, t - 16), 16)
        has_prev, has_next = r0 > 0, r0 + rb < t
        ys = []
        for j in range(3 * nh):
            cols = slice(j * LANE, (j + 1) * LANE)
            x = qkv_ref[0, pl.ds(r0, rb), cols].astype(F32)
            prv = jnp.where(has_prev, qkv_ref[0, pl.ds(rp, 16), cols].astype(F32)[15:16], 0.0)
            nxt = jnp.where(has_next, qkv_ref[0, pl.ds(rn, 16), cols].astype(F32)[0:1], 0.0)
            xp = jnp.where(rowb == 0, prv, pltpu.roll(x, 1, 0))
            xn = jnp.where(rowb == rb - 1, nxt, pltpu.roll(x, rb - 1, 0))
            cw = cw_ref[:, cols]
            y = _silu(xp * cw[0:1, :] + x * cw[1:2, :] + xn * cw[2:3, :])
            if j < 2 * nh:
                y = y * lax.rsqrt(jnp.sum(y * y, axis=-1, keepdims=True) + EPS)
            if j < nh:
                y = y * GDN_DK ** -0.5
            ys.append(y)
        return ys

    for j, y in enumerate(conv_block(0)):
        xc_s[:, j * LANE:(j + 1) * LANE] = y

    def local_body(jb, carry):
        r0 = pl.multiple_of(jb * rb, rb)
        xs = [xc_s[:, j * LANE:(j + 1) * LANE] for j in range(3 * nh)]
        xs_next = conv_block(pl.multiple_of(jnp.minimum(jb + 1, n // 2 - 1) * rb, rb))
        gcol = gcol_s[pl.ds(r0, rb), :]
        bet = beta_s[pl.ds(r0, rb), :]
        blocks = []
        for sub in range(2):
            ci = 2 * jb + sub
            rows = slice(sub * c, (sub + 1) * c)
            grow = grow_s[ci]
            for hh in range(nh):
                dirs = []
                for d in range(2):
                    ch = d * nh + hh
                    last = sub * c + (c - 1 if d == 0 else 0)
                    dirs.append(dict(gc_col=gcol[rows, ch:ch + 1], gc_row=grow[ch:ch + 1, :],
                                     beta_col=bet[rows, 2 * nh + ch:2 * nh + ch + 1],
                                     glast=gcol[last:last + 1, ch:ch + 1], upper=(d == 1), ch=ch, ci=ci,
                                     r0=r0 + sub * c))
                blocks.append(dict(q=xs[hh][rows], k=xs[nh + hh][rows], v=xs[2 * nh + hh][rows], dirs=dirs))
        _gdn_local(blocks)
        for blk in blocks:
            for chn in blk["dirs"]:
                ch, rr = chn["ch"], pl.ds(pl.multiple_of(chn["r0"], c), c)
                u_s[ch, rr, :] = chn["u"]
                wv_s[ch, rr, :] = chn["wv"]
                qe_s[ch, rr, :] = chn["qe"]
                kd_s[ch, rr, :] = chn["kd"]
                in_s[ch, chn["ci"]] = chn["intra"]
                eg_s[chn["ci"], ch:ch + 1, :] = jnp.broadcast_to(chn["eg"], (1, LANE))
        for j, y in enumerate(xs_next):
            xc_s[:, j * LANE:(j + 1) * LANE] = y
        return carry

    lax.fori_loop(0, n // 2, local_body, 0)

    for d in range(2):
        for hh in range(nh):
            st_s[d * nh + hh] = s0_ref[0, d, hh]
    o_ref[...] = jnp.zeros_like(o_ref)

    def scan_body(i, carry):
        chains = []
        for d in range(2):
            ci = i if d == 0 else n - 1 - i
            rr = pl.ds(pl.multiple_of(ci * c, c), c)
            eg = eg_s[ci]
            for hh in range(nh):
                ch = d * nh + hh
                chains.append(dict(u=u_s[ch, rr, :], wv=wv_s[ch, rr, :], qe=qe_s[ch, rr, :], kd=kd_s[ch, rr, :],
                                   intra=in_s[ch, ci], eg=eg[ch:ch + 1, :], s=st_s[ch], rr=rr, hh=hh, ch=ch))
        for chn, (o, s_new) in zip(chains, _gdn_scan(chains)):
            st_s[chn["ch"]] = s_new
            o_ref[0, chn["rr"], chn["hh"] * LANE:(chn["hh"] + 1) * LANE] += o
        return carry

    lax.fori_loop(0, n, scan_body, 0)
    for d in range(2):
        for hh in range(nh):
            sf_ref[0, d, hh] = st_s[d * nh + hh]


def _gdn(qkv, ab, abt, cw, al, dt, alt, dtt, s0):
    b, t, _ = qkv.shape
    n = t // GDN_CHUNK
    per_b = lambda a: pl.BlockSpec((1,) + a.shape[1:], lambda i: (i,) + (0,) * (a.ndim - 1))
    full = lambda a: pl.BlockSpec(a.shape, lambda i: (0,) * a.ndim)
    return pl.pallas_call(
        functools.partial(_gdn_kernel, t=t), grid=(b,),
        in_specs=[per_b(qkv), per_b(ab), per_b(abt), full(cw), full(al), full(dt), full(alt), full(dtt), per_b(s0)],
        out_specs=[pl.BlockSpec((1, t, GDN_HEADS * GDN_DV), lambda i: (i, 0, 0)), per_b(s0)],
        out_shape=[jax.ShapeDtypeStruct((b, t, GDN_HEADS * GDN_DV), F32), jax.ShapeDtypeStruct(s0.shape, F32)],
        scratch_shapes=[pltpu.VMEM((2 * GDN_HEADS, t, LANE), BF16), pltpu.VMEM((2 * GDN_HEADS, t, LANE), BF16),
                        pltpu.VMEM((2 * GDN_HEADS, t, LANE), BF16), pltpu.VMEM((2 * GDN_HEADS, t, LANE), BF16),
                        pltpu.VMEM((2 * GDN_HEADS, n, GDN_CHUNK, GDN_CHUNK), BF16),
                        pltpu.VMEM((n, 2 * GDN_HEADS, LANE), F32),
                        pltpu.VMEM((2 * GDN_CHUNK, 3 * GDN_HEADS * LANE), F32),
                        pltpu.VMEM((t, 16), F32), pltpu.VMEM((n, 16, GDN_CHUNK), F32), pltpu.VMEM((t, 16), F32),
                        pltpu.VMEM((2 * GDN_HEADS, GDN_DK, GDN_DV), F32)],
        compiler_params=_cparams(1), name="gdn",
    )(qkv, ab, abt, cw, al, dt, alt, dtt, s0)


def _attd_kernel(*refs, n_ctx):
    if n_ctx:
        q_ref, k_ref, vt_ref, kc_ref, vct_ref, o_ref = refs
    else:
        q_ref, k_ref, vt_ref, o_ref = refs
    n_new = k_ref.shape[1]
    grp = ATT_HEADS // ATT_KV_HEADS
    chunks = []
    for c0, cn in _key_chunks(n_new):
        chunks.append((
            lambda g, c0=c0, cn=cn: k_ref[0, c0:c0 + cn, g * LANE:(g + 1) * LANE],
            lambda g, c0=c0, cn=cn: vt_ref[0, g * LANE:(g + 1) * LANE, c0:c0 + cn],
            None))
    if n_ctx:
        for c0, cn in _key_chunks(n_ctx):
            chunks.append((
                lambda g, c0=c0, cn=cn: kc_ref[0, c0:c0 + cn, g * LANE:(g + 1) * LANE].astype(BF16),
                lambda g, c0=c0, cn=cn: vct_ref[0, g * LANE:(g + 1) * LANE, c0:c0 + cn].astype(BF16),
                None))
    qs = [q_ref[0, :, h * LANE:(h + 1) * LANE] for h in range(ATT_HEADS)]
    outs = _attend_heads(qs, [h // grp for h in range(ATT_HEADS)], chunks)
    o_ref[0] = jnp.concatenate(outs, axis=0).T.astype(BF16)


def _attd(q, k, vt, k_c, v_ct, tq):
    b, t, _ = q.shape
    n_ctx = 0 if k_c is None else k_c.shape[1]
    per_b = lambda a: pl.BlockSpec((1,) + a.shape[1:], lambda i, j: (i, 0, 0))
    in_specs = [pl.BlockSpec((1, tq, 512), lambda i, j: (i, j, 0)), per_b(k), per_b(vt)]
    args = [q, k, vt]
    if n_ctx:
        in_specs += [per_b(k_c), per_b(v_ct)]
        args += [k_c, v_ct]
    return pl.pallas_call(
        functools.partial(_attd_kernel, n_ctx=n_ctx),
        grid=(b, t // tq), in_specs=in_specs,
        out_specs=pl.BlockSpec((1, tq, 512), lambda i, j: (i, j, 0)),
        out_shape=jax.ShapeDtypeStruct((b, t, 512), BF16),
        compiler_params=_cparams(2), name="attd_dec" if n_ctx else "attd_ctx",
    )(*args)


def _outproj1_kernel(oc_ref, zc_ref, od_ref, zd_ref, x_ref, mod_ref, gn_ref, w_ref, lnf_ref, y_ref):
    zc = zc_ref[0].astype(F32)
    parts = []
    for j in range(GDN_HEADS):
        oc = _rms_rows(oc_ref[0, :, j * LANE:(j + 1) * LANE], gn_ref[...])
        parts.append((oc * _silu(zc[:, j * LANE:(j + 1) * LANE])).astype(BF16))
    gc = jnp.concatenate(parts, axis=1)
    gd = (od_ref[0].astype(F32) * _silu(zd_ref[0].astype(F32))).astype(BF16)
    y = _dot(gc, w_ref[0:512, :]) + _dot(gd, w_ref[512:1024, :])
    x2 = x_ref[0] + mod_ref[0, 2:3, :] * y
    y_ref[0] = _rms_rows(x2, lnf_ref[...])


def _outproj1(oc, zc, od, zd, x, mod, gn, w, lnf):
    b, t, d = x.shape
    tr = _row_tile(t)
    bm = mod.shape[0]
    rows = lambda c: pl.BlockSpec((1, tr, c), lambda i, j: (i, j, 0))
    full = lambda a: pl.BlockSpec(a.shape, lambda i, j: (0,) * a.ndim)
    return pl.pallas_call(
        _outproj1_kernel, grid=(b, t // tr),
        in_specs=[rows(512), rows(512), rows(512), rows(512), rows(d),
                  pl.BlockSpec((1, 3, d), (lambda i, j: (i, 0, 0)) if bm > 1 else (lambda i, j: (0, 0, 0))),
                  full(gn), full(w), full(lnf)],
        out_specs=rows(d), out_shape=jax.ShapeDtypeStruct((b, t, d), F32),
        compiler_params=_cparams(2), name="outproj1",
    )(oc, zc, od, zd, x, mod, gn, w, lnf)


def _rope_table(n_tok, rot_dim, used, block=LANE):
    quarter = rot_dim // 4
    inv = ROPE_THETA ** (-jnp.arange(quarter, dtype=F32) / quarter)
    tt = jnp.arange(n_tok)
    pos = jnp.stack([tt // GRID_W, tt % GRID_W], axis=-1).astype(F32)
    ang = pos[:, :, None] * inv
    cos, sin = jnp.cos(ang), jnp.sin(ang)
    c = jnp.concatenate([cos, cos], axis=-1).reshape(n_tok, rot_dim)
    s = jnp.concatenate([-sin, sin], axis=-1).reshape(n_tok, rot_dim)
    return c, s


def _place(tab, fill, off, width):
    n = tab.shape[0]
    left = jnp.full((n, off), fill, F32)
    right = jnp.full((n, width - off - tab.shape[1]), fill, F32)
    return jnp.concatenate([left, tab, right], axis=1)


def _prep_l0(w_in0, w_uq, w_ukv):
    d = w_in0.shape[0]
    cq, ckv, kr, za, qb, kb, vb, zb = jnp.split(w_in0, [384, 640, 672, 1184, 1696, 1824, 1952], axis=1)
    kr_pad = jnp.concatenate([jnp.zeros((d, 64), F32), kr, jnp.zeros((d, 32), F32)], axis=1)
    w = jnp.concatenate([cq, ckv, kr_pad, za, qb, kb, vb, zb], axis=1).astype(BF16)
    wvbt = vb.T.astype(BF16)
    uq = w_uq.reshape(MLA_Q_LORA, MLA_HEADS, MLA_NOPE + MLA_ROPE)
    wuq = jnp.pad(uq, ((0, 0), (0, 0), (0, LANE - MLA_NOPE - MLA_ROPE))).reshape(MLA_Q_LORA, MLA_HEADS * LANE)
    ukv = w_ukv.reshape(MLA_KV_LORA, MLA_HEADS, MLA_NOPE + MLA_V)
    wuk = jnp.pad(ukv[:, :, :MLA_NOPE], ((0, 0), (0, 0), (0, LANE - MLA_NOPE))).reshape(MLA_KV_LORA, MLA_HEADS * LANE)
    wuvt = ukv[:, :, MLA_NOPE:].reshape(MLA_KV_LORA, MLA_HEADS * MLA_V).T
    return w, wvbt, wuq.astype(BF16), wuk.astype(BF16), wuvt.astype(BF16)


def _prep_l1(w_in1):
    d = w_in1.shape[0]
    qkv, a, bb, zc, qd, kd, vd, zd = jnp.split(w_in1, [1536, 1544, 1552, 2064, 2576, 2832, 3088], axis=1)
    w = jnp.concatenate([qkv, zc, qd, kd, vd, zd, a, bb, jnp.zeros((d, 112), F32)], axis=1).astype(BF16)
    return w, vd.T.astype(BF16)


def _chunk_rows(ab):
    b, t, c = ab.shape
    return jnp.swapaxes(ab.reshape(b, t // GDN_CHUNK, GDN_CHUNK, c), 2, 3)


def _trunk(x, mod, caches, p, tables, tq):
    dec = caches is not None
    t0m, t0s, t1 = tables if dec else (None, None, None)
    (qa, ckv, kr, za, qb, kb, vbt, zb, *ctx0) = _inproj0(
        x, mod[0], p["ln0"], p["w0"], p["wvbt"], p["qn"], p["wuq"], p["kvn"], (t0m + t0s) if dec else None)
    if dec:
        ckv_c, kr_c, kb_c, vb_ct, s0, kd_c, vd_ct = caches
    else:
        ckv_c = kr_c = kb_c = vb_ct = kd_c = vd_ct = None
        s0 = jnp.zeros((x.shape[0], 2, GDN_HEADS, GDN_DK, GDN_DV), F32)
    oa = _mla(qa, ckv, kr, ckv_c, kr_c, p["wuk"], p["wuvt"], tq)
    ob = _swa(p["sink"], qb, kb, vbt, kb_c, vb_ct, tq)
    x1 = _outproj0(oa, za, ob, zb, x, mod[0], p["wout0"])
    (qkv, zc, qd, kd, vdt, zd, ab, *ctx1) = _inproj1(
        x1, mod[1], p["ln1"], p["w1"], p["wvdt"], p["aqn"], p["akn"], t1 if dec else None)
    oc, sfin = _gdn(qkv, ab, _chunk_rows(ab), p["cw"], p["al"], p["dt"], p["alt"], p["dtt"], s0)
    od = _attd(qd, kd, vdt, kd_c, vd_ct, tq)
    y = _outproj1(oc, zc, od, zd, x1, mod[1], p["gn"], p["wout1"], p["lnf"])
    return y, ctx0, sfin, ctx1


def kernel(x_prompt, x_sample, cache_l0_mla_ckv, cache_l0_mla_krope, cache_l0_swa_k, cache_l0_swa_v,
           state_l1_gdn, cache_l1_attn_k, cache_l1_attn_v, c, c_ctx,
           w_mod0, b_mod0, ln0, w_in0, mla_q_norm, w_uq, mla_kv_norm, w_ukv, swa_sink, w_out0,
           w_mod1, b_mod1, ln1, w_in1, gdn_conv, gdn_a_log, gdn_dt_bias, gdn_norm, att_q_norm, att_k_norm, w_out1,
           ln_f):
    d = x_prompt.shape[-1]
    bd, td = x_sample.shape[:2]
    bc, tc = x_prompt.shape[:2]
    past = cache_l0_mla_ckv.shape[1]
    row = lambda v: v.reshape(1, -1)
    w0, wvbt, wuq, wuk, wuvt = _prep_l0(w_in0, w_uq, w_ukv)
    w1, wvdt = _prep_l1(w_in1)
    al8 = gdn_a_log.reshape(1, 2 * GDN_HEADS)
    dt8 = gdn_dt_bias.reshape(1, 2 * GDN_HEADS)
    al16 = jnp.pad(al8, ((0, 0), (0, 8)))
    dt16 = jnp.pad(dt8, ((0, 0), (0, 8)))
    p = dict(ln0=row(ln0), w0=w0, wvbt=wvbt, qn=row(mla_q_norm), wuq=wuq, kvn=row(mla_kv_norm), wuk=wuk, wuvt=wuvt,
             sink=swa_sink, wout0=w_out0.astype(BF16), ln1=row(ln1), w1=w1, wvdt=wvdt, aqn=row(att_q_norm),
             akn=row(att_k_norm), cw=gdn_conv, al=al16, dt=dt16, alt=al16.T, dtt=dt16.T, gn=row(gdn_norm),
             wout1=w_out1.astype(BF16), lnf=row(ln_f))
    n_rows = -(-(bd + 1) // 8) * 8
    c_rows = jnp.concatenate([c, c_ctx[None, :], jnp.zeros((n_rows - bd - 1, d), F32)], axis=0)
    mods = [_mod(c_rows, w_mod0, b_mod0), _mod(c_rows, w_mod1, b_mod1)]
    mod_dec = [m[:bd].reshape(bd, 3, d) for m in mods]
    mod_ctx = [m[bd:bd + 1].reshape(1, 3, d) for m in mods]
    cm, sm = _rope_table(td, MLA_ROPE, LANE)
    t0m = (_place(cm, 1.0, MLA_NOPE, LANE), _place(sm, 0.0, MLA_NOPE, LANE))
    cs, ss = _rope_table(td, SWA_DH, SWA_DH)
    t0s = (jnp.tile(cs, (1, LANE // SWA_DH)), jnp.tile(ss, (1, LANE // SWA_DH)))
    t1 = _rope_table(td, ATT_DH, LANE)
    caches = (cache_l0_mla_ckv,
              jnp.pad(cache_l0_mla_krope, ((0, 0), (0, 0), (MLA_NOPE, LANE - MLA_NOPE - MLA_ROPE))),
              cache_l0_swa_k.reshape(bd, past, SWA_KV_HEADS * SWA_DH),
              jnp.swapaxes(cache_l0_swa_v.reshape(bd, past, SWA_KV_HEADS * SWA_DH), 1, 2),
              state_l1_gdn,
              cache_l1_attn_k.reshape(bd, past, ATT_KV_HEADS * ATT_DH),
              jnp.swapaxes(cache_l1_attn_v.reshape(bd, past, ATT_KV_HEADS * ATT_DH), 1, 2))
    y_prompt, ctx0, sfin, ctx1 = _trunk(x_prompt, mod_ctx, None, p, None, tq=tc)
    y_sample, _, _, _ = _trunk(x_sample, mod_dec, caches, p, (t0m, t0s, t1), tq=256)
    ckv32, kr32, kb32, vb32 = ctx0
    kd32, vd32 = ctx1
    return (y_prompt, y_sample, ckv32, kr32,
            kb32.reshape(bc, tc, SWA_KV_HEADS, SWA_DH), vb32.reshape(bc, tc, SWA_KV_HEADS, SWA_DH),
            sfin, kd32.reshape(bc, tc, ATT_KV_HEADS, ATT_DH), vd32.reshape(bc, tc, ATT_KV_HEADS, ATT_DH))
```

```python
import functools
import math

import jax
import jax.numpy as jnp
import numpy as np
from jax import lax
from jax.experimental import pallas as pl
from jax.experimental.pallas import tpu as pltpu

F32 = jnp.float32
BF16 = jnp.bfloat16

GRID_W = 64
ROPE_THETA = 10000.0
EPS = 1e-6
WINDOW = 128
MLA_HEADS, MLA_NOPE, MLA_ROPE, MLA_V = 8, 64, 32, 64
MLA_Q_LORA, MLA_KV_LORA = 384, 256
SWA_HEADS, SWA_KV_HEADS, SWA_DH = 8, 2, 64
GDN_HEADS, GDN_DK, GDN_DV, CONV_K, GDN_CHUNK = 4, 128, 128, 3, 64
GDN_LOCAL_CHUNKS = 4
ATT_HEADS, ATT_KV_HEADS, ATT_DH = 4, 2, 128
LANE = 128
LOG2E = math.log2(math.e)
NEG = -1e30
VMEM_LIMIT = 56 * 1024 * 1024


def _cparams(n_axes):
    return pltpu.CompilerParams(dimension_semantics=("arbitrary",) * n_axes, vmem_limit_bytes=VMEM_LIMIT)


def _dot(a, b):
    return jnp.dot(a, b, preferred_element_type=F32)


def _dot_nt(a, b):
    return lax.dot_general(a, b, (((1,), (1,)), ((), ())), preferred_element_type=F32)


def _dot_tn(a, b):
    return lax.dot_general(a, b, (((0,), (0,)), ((), ())), preferred_element_type=F32)


def _dot_exact(a, b):
    return jnp.dot(a, b, preferred_element_type=F32, precision=lax.Precision.HIGHEST)


def _split(a):
    hi = a.astype(BF16)
    return hi, (a - hi.astype(F32)).astype(BF16)


def _silu(x):
    return x * jax.nn.sigmoid(x)


def _rms_rows(x, g):
    return x * lax.rsqrt(jnp.mean(x * x, axis=-1, keepdims=True) + EPS) * g


def _rope_block(x, cos, sin, half):
    lane = lax.broadcasted_iota(jnp.int32, x.shape, 1)
    first = (lane // half) % 2 == 0
    partner = jnp.where(first, pltpu.roll(x, LANE - half, 1), pltpu.roll(x, half, 1))
    return x * cos + partner * sin


def _mod_kernel(c_ref, w_ref, b_ref, o_ref):
    a = _silu(c_ref[...]).astype(BF16)
    o_ref[...] = _dot(a, w_ref[...].astype(BF16)) + b_ref[...]


def _mod(c_rows, w_mod, b_mod):
    r, d = c_rows.shape
    n = w_mod.shape[1]
    tn = 1024
    return pl.pallas_call(
        _mod_kernel,
        grid=(n // tn,),
        in_specs=[pl.BlockSpec((r, d), lambda j: (0, 0)),
                  pl.BlockSpec((d, tn), lambda j: (0, j)),
                  pl.BlockSpec((1, tn), lambda j: (0, j))],
        out_specs=pl.BlockSpec((r, tn), lambda j: (0, j)),
        out_shape=jax.ShapeDtypeStruct((r, n), F32),
        compiler_params=_cparams(1),
        name="mod",
    )(c_rows, w_mod, b_mod.reshape(1, n))


def _adaln(x, mod_ref, ln_ref):
    h = _rms_rows(x, ln_ref[...])
    return h * (1.0 + mod_ref[0, 1:2, :]) + mod_ref[0, 0:1, :]


L0_OFF = dict(cq=0, ckv=384, kr=640, za=768, qb=1280, kb=1792, vb=1920, zb=2048)
L0_W = 2560


def _inproj0_kernel(*refs, rope):
    if rope:
        (x_ref, mod_ref, ln_ref, w_ref, wvbt_ref, qn_ref, wuq_ref, kvn_ref, cm_ref, sm_ref, cs_ref, ss_ref,
         qa_ref, ckv_ref, kr_ref, za_ref, qb_ref, kb_ref, vbt_ref, zb_ref) = refs
    else:
        (x_ref, mod_ref, ln_ref, w_ref, wvbt_ref, qn_ref, wuq_ref, kvn_ref,
         qa_ref, ckv_ref, kr_ref, za_ref, qb_ref, kb_ref, vbt_ref, zb_ref,
         ckv32_ref, kr32_ref, kb32_ref, vb32_ref) = refs
    h = _adaln(x_ref[0], mod_ref, ln_ref).astype(BF16)
    u = _dot(h, w_ref[...])
    o = L0_OFF
    cq = _rms_rows(u[:, o["cq"]:o["cq"] + 384], qn_ref[...]).astype(BF16)
    qa = _dot(cq, wuq_ref[...])
    ckv = _rms_rows(u[:, o["ckv"]:o["ckv"] + 256], kvn_ref[...])
    kr = u[:, o["kr"]:o["kr"] + 128]
    qb = u[:, o["qb"]:o["qb"] + 512]
    kb = u[:, o["kb"]:o["kb"] + 128]
    vb = u[:, o["vb"]:o["vb"] + 128]
    if not rope:
        ckv32_ref[0] = ckv
        kr32_ref[0] = kr[:, 64:96]
        kb32_ref[0] = kb
        vb32_ref[0] = vb
    qa_scale = (MLA_NOPE + MLA_ROPE) ** -0.5 * LOG2E
    qb_scale = SWA_DH ** -0.5 * LOG2E
    for j in range(MLA_HEADS):
        blk = qa[:, j * LANE:(j + 1) * LANE]
        if rope:
            blk = _rope_block(blk, cm_ref[...], sm_ref[...], MLA_ROPE // 4)
        qa_ref[0, :, j * LANE:(j + 1) * LANE] = (blk * qa_scale).astype(BF16)
    for j in range(SWA_HEADS * SWA_DH // LANE):
        blk = qb[:, j * LANE:(j + 1) * LANE]
        if rope:
            blk = _rope_block(blk, cs_ref[...], ss_ref[...], SWA_DH // 4)
        qb_ref[0, :, j * LANE:(j + 1) * LANE] = (blk * qb_scale).astype(BF16)
    if rope:
        kr = _rope_block(kr, cm_ref[...], sm_ref[...], MLA_ROPE // 4)
        kb = _rope_block(kb, cs_ref[...], ss_ref[...], SWA_DH // 4)
    ckv_ref[0] = ckv.astype(BF16)
    kr_ref[0] = kr.astype(BF16)
    kb_ref[0] = kb.astype(BF16)
    za_ref[0] = u[:, o["za"]:o["za"] + 512].astype(BF16)
    zb_ref[0] = u[:, o["zb"]:o["zb"] + 512].astype(BF16)
    vbt_ref[0] = _dot_nt(wvbt_ref[...], h).astype(BF16)


def _row_tile(t):
    return 512 if t % 512 == 0 else 256


def _inproj0(x, mod, ln, w, wvbt, qn, wuq, kvn, tables):
    b, t, d = x.shape
    tr = _row_tile(t)
    rope = tables is not None
    bm = mod.shape[0]
    full = lambda a: pl.BlockSpec(a.shape, lambda i, j: (0,) * a.ndim)
    rows = lambda c: pl.BlockSpec((1, tr, c), lambda i, j: (i, j, 0))
    in_specs = [rows(d), pl.BlockSpec((1, 3, d), (lambda i, j: (i, 0, 0)) if bm > 1 else (lambda i, j: (0, 0, 0))),
                full(ln), full(w), full(wvbt), full(qn), full(wuq), full(kvn)]
    args = [x, mod, ln, w, wvbt, qn, wuq, kvn]
    if rope:
        for tab in tables:
            in_specs.append(pl.BlockSpec((tr, LANE), lambda i, j: (j, 0)))
            args.append(tab)
    out_shape = [jax.ShapeDtypeStruct((b, t, 1024), BF16), jax.ShapeDtypeStruct((b, t, 256), BF16),
                 jax.ShapeDtypeStruct((b, t, 128), BF16), jax.ShapeDtypeStruct((b, t, 512), BF16),
                 jax.ShapeDtypeStruct((b, t, 512), BF16), jax.ShapeDtypeStruct((b, t, 128), BF16),
                 jax.ShapeDtypeStruct((b, 128, t), BF16), jax.ShapeDtypeStruct((b, t, 512), BF16)]
    out_specs = [rows(1024), rows(256), rows(128), rows(512), rows(512), rows(128),
                 pl.BlockSpec((1, 128, tr), lambda i, j: (i, 0, j)), rows(512)]
    if not rope:
        out_shape += [jax.ShapeDtypeStruct((b, t, 256), F32), jax.ShapeDtypeStruct((b, t, 32), F32),
                      jax.ShapeDtypeStruct((b, t, 128), F32), jax.ShapeDtypeStruct((b, t, 128), F32)]
        out_specs += [rows(256), rows(32), rows(128), rows(128)]
    return pl.pallas_call(
        functools.partial(_inproj0_kernel, rope=rope),
        grid=(b, t // tr), in_specs=in_specs, out_specs=out_specs, out_shape=out_shape,
        compiler_params=_cparams(2), name="inproj0_dec" if rope else "inproj0_ctx",
    )(*args)


KEY_CHUNK = 512
ATT_LOOKAHEAD = 4


SUM_ROWS = 16


def _attend_heads(qs, kv_of, chunks, dv, sinks=None):
    nh = len(qs)
    tq = qs[0].shape[0]
    m = [None] * nh
    acc = [None] * nh
    items = [(ci, i) for ci in range(len(chunks)) for i in range(nh)]
    loaded = {}

    def kv(ci, src):
        if (ci, src) not in loaded:
            vt = chunks[ci][1](src)
            ones = (lax.broadcasted_iota(jnp.int32, (SUM_ROWS, vt.shape[1]), 0) == 0).astype(BF16)
            loaded[(ci, src)] = (chunks[ci][0](src), jnp.concatenate([vt, ones], axis=0))
        return loaded[(ci, src)]

    if sinks is not None:
        m = [jnp.full((1, tq), sk, F32) for sk in sinks]
        unit = (lax.broadcasted_iota(jnp.int32, (dv + SUM_ROWS, tq), 0) == dv).astype(F32)
        acc = [unit for _ in sinks]

    scores = {}
    for t in range(len(items) + ATT_LOOKAHEAD):
        if t < len(items):
            ci, i = items[t]
            scores[t] = _dot_nt(kv(ci, kv_of[i])[0], qs[i])
        t0 = t - ATT_LOOKAHEAD
        if t0 < 0:
            continue
        ci, i = items[t0]
        mask = chunks[ci][2]
        si = scores.pop(t0)
        if mask is not None:
            si = jnp.where(mask, si, NEG)
        cm = si.max(axis=0, keepdims=True)
        alpha = None
        if m[i] is None:
            m_new = cm
        else:
            m_new = jnp.maximum(m[i], cm)
            alpha = jnp.exp2(m[i] - m_new)
        p = jnp.exp2(si - m_new)
        m[i] = m_new
        pv = _dot(kv(ci, kv_of[i])[1], p.astype(BF16))
        acc[i] = pv if acc[i] is None else acc[i] * alpha + pv
    return [acc[i][:dv] * (1.0 / acc[i][dv:dv + 1]) for i in range(nh)]


def _key_chunks(n):
    step = KEY_CHUNK if n % KEY_CHUNK == 0 else n
    return [(c0, step) for c0 in range(0, n, step)]


def _mla_kernel(*refs, n_new, n_ctx, hp):
    if n_ctx:
        q_ref, ckv_ref, kr_ref, ckvc_ref, krc_ref, wuk_ref, wuvt_ref, o_ref, k_s, vt_s = refs
    else:
        q_ref, ckv_ref, kr_ref, wuk_ref, wuvt_ref, o_ref, k_s, vt_s = refs
    qi, gi = pl.program_id(1), pl.program_id(2)

    @pl.when((qi == 0) & (gi == 0))
    def _():
        def expand(ckv, kr, r0, n):
            kn = _dot(ckv, wuk_ref[...])
            for j in range(MLA_HEADS):
                k_s[j, r0:r0 + n, :] = (kn[:, j * LANE:(j + 1) * LANE] + kr).astype(BF16)
            vt_s[:, r0:r0 + n] = _dot_nt(wuvt_ref[...], ckv).astype(BF16)

        blk = 512 if n_new % 512 == 0 else 256
        for r0 in range(0, n_new, blk):
            expand(ckv_ref[0, r0:r0 + blk, :], kr_ref[0, r0:r0 + blk, :].astype(F32), r0, blk)
        if n_ctx:
            expand(ckvc_ref[0].astype(BF16), krc_ref[0], n_new, n_ctx)

    qs = [q_ref[0, :, j * LANE:(j + 1) * LANE] for j in range(hp)]
    chunks = []
    for c0, cn in _key_chunks(n_new + n_ctx):
        chunks.append((
            lambda j, c0=c0, cn=cn: k_s[gi * hp + j, c0:c0 + cn, :],
            lambda j, c0=c0, cn=cn: vt_s[pl.ds(pl.multiple_of((gi * hp + j) * MLA_V, MLA_V), MLA_V), c0:c0 + cn],
            None))
    outs = _attend_heads(qs, list(range(hp)), chunks, MLA_V)
    o_ref[0] = jnp.concatenate(outs, axis=0).T.astype(BF16)


def _mla(q, ckv, kr, ckv_c, kr_c, wuk, wuvt, tq):
    b, t, _ = q.shape
    n_ctx = 0 if ckv_c is None else ckv_c.shape[1]
    hp = 4
    tk = t + n_ctx
    rows_q = pl.BlockSpec((1, tq, hp * LANE), lambda i, j, g: (i, j, g))
    per_b = lambda a: pl.BlockSpec((1,) + a.shape[1:], lambda i, j, g: (i, 0, 0))
    full = lambda a: pl.BlockSpec(a.shape, lambda i, j, g: (0, 0))
    in_specs = [rows_q, per_b(ckv), per_b(kr)]
    args = [q, ckv, kr]
    if n_ctx:
        in_specs += [per_b(ckv_c), per_b(kr_c)]
        args += [ckv_c, kr_c]
    in_specs += [full(wuk), full(wuvt)]
    args += [wuk, wuvt]
    return pl.pallas_call(
        functools.partial(_mla_kernel, n_new=t, n_ctx=n_ctx, hp=hp),
        grid=(b, t // tq, MLA_HEADS // hp), in_specs=in_specs,
        out_specs=pl.BlockSpec((1, tq, hp * MLA_V), lambda i, j, g: (i, j, g)),
        out_shape=jax.ShapeDtypeStruct((b, t, MLA_HEADS * MLA_V), BF16),
        scratch_shapes=[pltpu.VMEM((MLA_HEADS, tk, LANE), BF16), pltpu.VMEM((MLA_HEADS * MLA_V, tk), BF16)],
        compiler_params=_cparams(3), name="mla_dec" if n_ctx else "mla_ctx",
    )(*args)


def _swa_kernel(*refs, n_new, n_ctx, tq):
    if n_ctx:
        sink_ref, q_ref, k_ref, vt_ref, kc_ref, vct_ref, o_ref = refs
    else:
        sink_ref, q_ref, k_ref, vt_ref, o_ref = refs
    qi = pl.program_id(1)
    grp = SWA_HEADS // SWA_KV_HEADS
    if n_ctx:
        span = tq + 2 * WINDOW
        q0 = qi * tq
        start = pl.multiple_of(jnp.clip(q0 - WINDOW, 0, n_new - span), LANE)
        kpos = start + lax.broadcasted_iota(jnp.int32, (span, tq), 0)
        qpos = q0 + lax.broadcasted_iota(jnp.int32, (span, tq), 1)
        band = jnp.abs(kpos - qpos) <= WINDOW
    dh = SWA_DH
    chunks = []
    if n_ctx:
        for c0, cn in _key_chunks(span):
            chunks.append((
                lambda g, c0=c0, cn=cn: k_ref[0, pl.ds(start + c0, cn), g * dh:(g + 1) * dh],
                lambda g, c0=c0, cn=cn: vt_ref[0, g * dh:(g + 1) * dh, pl.ds(start + c0, cn)],
                band[c0:c0 + cn]))
        for c0, cn in _key_chunks(n_ctx):
            chunks.append((
                lambda g, c0=c0, cn=cn: kc_ref[0, c0:c0 + cn, g * dh:(g + 1) * dh].astype(BF16),
                lambda g, c0=c0, cn=cn: vct_ref[0, g * dh:(g + 1) * dh, c0:c0 + cn].astype(BF16),
                None))
    else:
        for c0, cn in _key_chunks(n_new):
            chunks.append((
                lambda g, c0=c0, cn=cn: k_ref[0, c0:c0 + cn, g * dh:(g + 1) * dh],
                lambda g, c0=c0, cn=cn: vt_ref[0, g * dh:(g + 1) * dh, c0:c0 + cn],
                None))
    qs = [q_ref[0, :, h * dh:(h + 1) * dh] for h in range(SWA_HEADS)]
    sinks = [sink_ref[h] * LOG2E for h in range(SWA_HEADS)]
    outs = _attend_heads(qs, [h // grp for h in range(SWA_HEADS)], chunks, SWA_DH, sinks)
    o_ref[0] = jnp.concatenate(outs, axis=0).T.astype(BF16)


def _swa(sink, q, k, vt, k_c, v_ct, tq):
    b, t, _ = q.shape
    n_ctx = 0 if k_c is None else k_c.shape[1]
    per_b = lambda a: pl.BlockSpec((1,) + a.shape[1:], lambda i, j: (i, 0, 0))
    in_specs = [pl.BlockSpec(memory_space=pltpu.SMEM), pl.BlockSpec((1, tq, 512), lambda i, j: (i, j, 0)),
                per_b(k), per_b(vt)]
    args = [sink, q, k, vt]
    if n_ctx:
        in_specs += [per_b(k_c), per_b(v_ct)]
        args += [k_c, v_ct]
    return pl.pallas_call(
        functools.partial(_swa_kernel, n_new=t, n_ctx=n_ctx, tq=tq),
        grid=(b, t // tq), in_specs=in_specs,
        out_specs=pl.BlockSpec((1, tq, 512), lambda i, j: (i, j, 0)),
        out_shape=jax.ShapeDtypeStruct((b, t, 512), BF16),
        compiler_params=_cparams(2), name="swa_dec" if n_ctx else "swa_ctx",
    )(*args)


def _outproj0_kernel(oa_ref, za_ref, ob_ref, zb_ref, x_ref, mod_ref, w_ref, y_ref):
    ga = (oa_ref[0].astype(F32) * _silu(za_ref[0].astype(F32))).astype(BF16)
    gb = (ob_ref[0].astype(F32) * _silu(zb_ref[0].astype(F32))).astype(BF16)
    y = _dot(ga, w_ref[0:512, :]) + _dot(gb, w_ref[512:1024, :])
    y_ref[0] = x_ref[0] + mod_ref[0, 2:3, :] * y


def _outproj0(oa, za, ob, zb, x, mod, w):
    b, t, d = x.shape
    tr = _row_tile(t)
    bm = mod.shape[0]
    rows = lambda c: pl.BlockSpec((1, tr, c), lambda i, j: (i, j, 0))
    return pl.pallas_call(
        _outproj0_kernel, grid=(b, t // tr),
        in_specs=[rows(512), rows(512), rows(512), rows(512), rows(d),
                  pl.BlockSpec((1, 3, d), (lambda i, j: (i, 0, 0)) if bm > 1 else (lambda i, j: (0, 0, 0))),
                  pl.BlockSpec(w.shape, lambda i, j: (0, 0))],
        out_specs=rows(d), out_shape=jax.ShapeDtypeStruct((b, t, d), F32),
        compiler_params=_cparams(2), name="outproj0",
    )(oa, za, ob, zb, x, mod, w)


L1_OFF = dict(qkv=0, zc=1536, qd=2048, kd=2560, vd=2816, zd=3072, ab=3584)
L1_W = 3712


def _inproj1_kernel(*refs, rope):
    if rope:
        (x_ref, mod_ref, ln_ref, w_ref, wvdt_ref, qn_ref, kn_ref, c_ref, s_ref,
         qkv_ref, zc_ref, qd_ref, kd_ref, vdt_ref, zd_ref, ab_ref) = refs
    else:
        (x_ref, mod_ref, ln_ref, w_ref, wvdt_ref, qn_ref, kn_ref,
         qkv_ref, zc_ref, qd_ref, kd_ref, vdt_ref, zd_ref, ab_ref, kd32_ref, vd32_ref) = refs
    h = _adaln(x_ref[0], mod_ref, ln_ref).astype(BF16)
    u = _dot(h, w_ref[...])
    o = L1_OFF
    qkv_ref[0] = u[:, o["qkv"]:o["qkv"] + 1536].astype(BF16)
    zc_ref[0] = u[:, o["zc"]:o["zc"] + 512].astype(BF16)
    zd_ref[0] = u[:, o["zd"]:o["zd"] + 512].astype(BF16)
    ab_ref[0] = u[:, o["ab"]:o["ab"] + 16]
    qd_scale = ATT_DH ** -0.5 * LOG2E
    for j in range(ATT_HEADS):
        blk = _rms_rows(u[:, o["qd"] + j * LANE:o["qd"] + (j + 1) * LANE], qn_ref[...])
        if rope:
            blk = _rope_block(blk, c_ref[...], s_ref[...], ATT_DH // 4)
        qd_ref[0, :, j * LANE:(j + 1) * LANE] = (blk * qd_scale).astype(BF16)
    for j in range(ATT_KV_HEADS):
        blk = _rms_rows(u[:, o["kd"] + j * LANE:o["kd"] + (j + 1) * LANE], kn_ref[...])
        if rope:
            blk = _rope_block(blk, c_ref[...], s_ref[...], ATT_DH // 4)
        else:
            kd32_ref[0, :, j * LANE:(j + 1) * LANE] = blk
        kd_ref[0, :, j * LANE:(j + 1) * LANE] = blk.astype(BF16)
    if not rope:
        vd32_ref[0] = u[:, o["vd"]:o["vd"] + 256]
    vdt_ref[0] = _dot_nt(wvdt_ref[...], h).astype(BF16)


def _inproj1(x, mod, ln, w, wvdt, qn, kn, tables):
    b, t, d = x.shape
    tr = _row_tile(t)
    rope = tables is not None
    bm = mod.shape[0]
    full = lambda a: pl.BlockSpec(a.shape, lambda i, j: (0,) * a.ndim)
    rows = lambda c: pl.BlockSpec((1, tr, c), lambda i, j: (i, j, 0))
    in_specs = [rows(d), pl.BlockSpec((1, 3, d), (lambda i, j: (i, 0, 0)) if bm > 1 else (lambda i, j: (0, 0, 0))),
                full(ln), full(w), full(wvdt), full(qn), full(kn)]
    args = [x, mod, ln, w, wvdt, qn, kn]
    if rope:
        for tab in tables:
            in_specs.append(pl.BlockSpec((tr, LANE), lambda i, j: (j, 0)))
            args.append(tab)
    out_shape = [jax.ShapeDtypeStruct((b, t, 1536), BF16), jax.ShapeDtypeStruct((b, t, 512), BF16),
                 jax.ShapeDtypeStruct((b, t, 512), BF16), jax.ShapeDtypeStruct((b, t, 256), BF16),
                 jax.ShapeDtypeStruct((b, 256, t), BF16), jax.ShapeDtypeStruct((b, t, 512), BF16),
                 jax.ShapeDtypeStruct((b, t, 16), F32)]
    out_specs = [rows(1536), rows(512), rows(512), rows(256),
                 pl.BlockSpec((1, 256, tr), lambda i, j: (i, 0, j)), rows(512), rows(16)]
    if not rope:
        out_shape += [jax.ShapeDtypeStruct((b, t, 256), F32), jax.ShapeDtypeStruct((b, t, 256), F32)]
        out_specs += [rows(256), rows(256)]
    return pl.pallas_call(
        functools.partial(_inproj1_kernel, rope=rope),
        grid=(b, t // tr), in_specs=in_specs, out_specs=out_specs, out_shape=out_shape,
        compiler_params=_cparams(2), name="inproj1_dec" if rope else "inproj1_ctx",
    )(*args)


def _gdn_local(blocks):
    c = GDN_CHUNK
    row = lax.broadcasted_iota(jnp.int32, (c, c), 0)
    col = lax.broadcasted_iota(jnp.int32, (c, c), 1)
    lane2 = lax.broadcasted_iota(jnp.int32, (c, 2 * c), 1)
    eye = (row == col).astype(F32)
    eye_t = jnp.concatenate([eye, jnp.zeros((c, c), F32)], axis=1).astype(BF16)
    chains = [ch for blk in blocks for ch in blk["dirs"]]
    for blk in blocks:
        for ch in blk["dirs"]:
            causal = (row <= col) if ch["upper"] else (row >= col)
            ch["strict"] = (row < col) if ch["upper"] else (row > col)
            ch["decay"] = jnp.exp(jnp.where(causal, ch["gc_col"] - ch["gc_row"], -jnp.inf))
            ch["kb"] = blk["k"] * ch["beta_col"]
            ch["egc"] = jnp.exp(ch["gc_col"])
    for blk in blocks:
        lhs = jnp.concatenate([ch["kb"] for ch in blk["dirs"]] + [blk["q"]], axis=0).astype(BF16)
        a = _dot_nt(lhs, blk["k"].astype(BF16))
        nd = len(blk["dirs"])
        for di, ch in enumerate(blk["dirs"]):
            x = jnp.where(ch["strict"], -(a[di * c:(di + 1) * c] * ch["decay"]), 0.0)
            ch["intra"] = (a[nd * c:] * ch["decay"]).astype(BF16)
            ch["w"] = jnp.concatenate([eye, x], axis=1)
    for _ in range(6):
        for ch in chains:
            w = ch["w"]
            wh = w.astype(BF16)
            lo = w - wh.astype(F32)
            php = jnp.where(lane2 < c, pltpu.roll(w, c, 1), lo).astype(BF16)
            ch["w"] = _dot(jnp.concatenate([wh, php], axis=1),
                           jnp.concatenate([eye_t, wh, lo.astype(BF16), wh], axis=0))
    for blk in blocks:
        for ch in blk["dirs"]:
            rhs = jnp.concatenate([blk["v"] * ch["beta_col"], ch["kb"] * ch["egc"]], axis=1).astype(BF16)
            sol = _dot(ch["w"][:, :c].astype(BF16), rhs)
            ch["u"], ch["wv"] = sol[:, :GDN_DV].astype(BF16), sol[:, GDN_DV:].astype(BF16)
            ch["qe"] = (blk["q"] * ch["egc"]).astype(BF16)
            ch["kd"] = (blk["k"] * jnp.exp(ch["glast"] - ch["gc_col"])).astype(BF16)
            ch["eg"] = jnp.exp(ch["glast"])


def _gdn_scan(chains):
    c = GDN_CHUNK
    for ch in chains:
        ch["sb"] = ch["s"].astype(BF16)
    for ch in chains:
        r = _dot(jnp.concatenate([ch["wv"], ch["qe"]], axis=0), ch["sb"])
        ch["vn"] = (ch["u"].astype(F32) - r[:c]).astype(BF16)
        ch["qs"] = r[c:]
    outs = []
    for ch in chains:
        o = ch["qs"] + _dot(ch["intra"], ch["vn"])
        s_new = ch["s"] * ch["eg"] + _dot_tn(ch["kd"], ch["vn"])
        outs.append((o, s_new))
    return outs


def _gdn_kernel(qkv_ref, ab_ref, abt_ref, cw_ref, al_ref, dt_ref, alt_ref, dtt_ref, s0_ref, o_ref, sf_ref,
                u_s, wv_s, qe_s, kd_s, in_s, eg_s, xc_s, gcol_s, grow_s, beta_s, st_s, *, t):
    c = GDN_CHUNK
    n = t // c
    nh = GDN_HEADS
    ab = ab_ref[0]
    gact = -jnp.exp(al_ref[...]) * jax.nn.softplus(ab + dt_ref[...])
    lane16 = lax.broadcasted_iota(jnp.int32, ab.shape, 1)
    beta_s[...] = jnp.where(lane16 < 2 * nh, gact, jax.nn.sigmoid(ab))
    r64 = lax.broadcasted_iota(jnp.int32, (c, c), 0)
    c64 = lax.broadcasted_iota(jnp.int32, (c, c), 1)
    tril = (r64 >= c64).astype(F32)
    triu = (r64 <= c64).astype(F32)
    lane_c = lax.broadcasted_iota(jnp.int32, (c, 16), 1)
    sub_c = lax.broadcasted_iota(jnp.int32, (16, c), 0)

    def cum_body(i, carry):
        r0 = pl.multiple_of(i * c, c)
        g = beta_s[pl.ds(r0, c), :]
        gcol_s[pl.ds(r0, c), :] = jnp.where(lane_c < nh, _dot_exact(tril, g), _dot_exact(triu, g))
        gt = -jnp.exp(alt_ref[...]) * jax.nn.softplus(abt_ref[0, i] + dtt_ref[...])
        grow_s[i] = jnp.where(sub_c < nh, _dot_exact(gt, triu), _dot_exact(gt, tril))
        return carry

    lax.fori_loop(0, n, cum_body, 0, unroll=4)

    ncs = GDN_LOCAL_CHUNKS
    rb = ncs * c
    rowb = lax.broadcasted_iota(jnp.int32, (rb, LANE), 0)

    def conv_block(r0):
        rp = pl.multiple_of(jnp.maximum(r0 - 16, 0), 16)
        rn = pl.multiple_of(jnp.minimum(r0 + rb, t - 16), 16)
        has_prev, has_next = r0 > 0, r0 + rb < t
        ys = []
        for j in range(3 * nh):
            cols = slice(j * LANE, (j + 1) * LANE)
            x = qkv_ref[0, pl.ds(r0, rb), cols].astype(F32)
            prv = jnp.where(has_prev, qkv_ref[0, pl.ds(rp, 16), cols].astype(F32)[15:16], 0.0)
            nxt = jnp.where(has_next, qkv_ref[0, pl.ds(rn, 16), cols].astype(F32)[0:1], 0.0)
            xp = jnp.where(rowb == 0, prv, pltpu.roll(x, 1, 0))
            xn = jnp.where(rowb == rb - 1, nxt, pltpu.roll(x, rb - 1, 0))
            cw = cw_ref[:, cols]
            y = _silu(xp * cw[0:1, :] + x * cw[1:2, :] + xn * cw[2:3, :])
            if j < 2 * nh:
                y = y * lax.rsqrt(jnp.sum(y * y, axis=-1, keepdims=True) + EPS)
            if j < nh:
                y = y * GDN_DK ** -0.5
            ys.append(y)
        return ys

    for j, y in enumerate(conv_block(0)):
        xc_s[:, j * LANE:(j + 1) * LANE] = y

    def local_body(jb, carry):
        r0 = pl.multiple_of(jb * rb, rb)
        xs = [xc_s[:, j * LANE:(j + 1) * LANE] for j in range(3 * nh)]
        xs_next = conv_block(pl.multiple_of(jnp.minimum(jb + 1, n // ncs - 1) * rb, rb))
        gcol = gcol_s[pl.ds(r0, rb), :]
        bet = beta_s[pl.ds(r0, rb), :]
        blocks = []
        for sub in range(ncs):
            ci = ncs * jb + sub
            rows = slice(sub * c, (sub + 1) * c)
            grow = grow_s[ci]
            for hh in range(nh):
                dirs = []
                for d in range(2):
                    ch = d * nh + hh
                    last = sub * c + (c - 1 if d == 0 else 0)
                    dirs.append(dict(gc_col=gcol[rows, ch:ch + 1], gc_row=grow[ch:ch + 1, :],
                                     beta_col=bet[rows, 2 * nh + ch:2 * nh + ch + 1],
                                     glast=gcol[last:last + 1, ch:ch + 1], upper=(d == 1), ch=ch, ci=ci,
                                     r0=r0 + sub * c))
                blocks.append(dict(q=xs[hh][rows], k=xs[nh + hh][rows], v=xs[2 * nh + hh][rows], dirs=dirs))
        _gdn_local(blocks)
        for blk in blocks:
            for chn in blk["dirs"]:
                ch, rr = chn["ch"], pl.ds(pl.multiple_of(chn["r0"], c), c)
                u_s[ch, rr, :] = chn["u"]
                wv_s[ch, rr, :] = chn["wv"]
                qe_s[ch, rr, :] = chn["qe"]
                kd_s[ch, rr, :] = chn["kd"]
                in_s[ch, chn["ci"]] = chn["intra"]
                eg_s[chn["ci"], ch:ch + 1, :] = jnp.broadcast_to(chn["eg"], (1, LANE))
        for j, y in enumerate(xs_next):
            xc_s[:, j * LANE:(j + 1) * LANE] = y
        return carry

    lax.fori_loop(0, n // ncs, local_body, 0)

    for d in range(2):
        for hh in range(nh):
            st_s[d * nh + hh] = s0_ref[0, d, hh]
    o_ref[...] = jnp.zeros_like(o_ref)

    def scan_body(i, carry):
        chains = []
        for d in range(2):
            ci = i if d == 0 else n - 1 - i
            rr = pl.ds(pl.multiple_of(ci * c, c), c)
            eg = eg_s[ci]
            for hh in range(nh):
                ch = d * nh + hh
                chains.append(dict(u=u_s[ch, rr, :], wv=wv_s[ch, rr, :], qe=qe_s[ch, rr, :], kd=kd_s[ch, rr, :],
                                   intra=in_s[ch, ci], eg=eg[ch:ch + 1, :], s=st_s[ch], rr=rr, hh=hh, ch=ch))
        for chn, (o, s_new) in zip(chains, _gdn_scan(chains)):
            st_s[chn["ch"]] = s_new
            o_ref[0, chn["rr"], chn["hh"] * LANE:(chn["hh"] + 1) * LANE] += o
        return carry

    lax.fori_loop(0, n, scan_body, 0)
    for d in range(2):
        for hh in range(nh):
            sf_ref[0, d, hh] = st_s[d * nh + hh]


def _gdn(qkv, ab, abt, cw, al, dt, alt, dtt, s0):
    b, t, _ = qkv.shape
    n = t // GDN_CHUNK
    per_b = lambda a: pl.BlockSpec((1,) + a.shape[1:], lambda i: (i,) + (0,) * (a.ndim - 1))
    full = lambda a: pl.BlockSpec(a.shape, lambda i: (0,) * a.ndim)
    return pl.pallas_call(
        functools.partial(_gdn_kernel, t=t), grid=(b,),
        in_specs=[per_b(qkv), per_b(ab), per_b(abt), full(cw), full(al), full(dt), full(alt), full(dtt), per_b(s0)],
        out_specs=[pl.BlockSpec((1, t, GDN_HEADS * GDN_DV), lambda i: (i, 0, 0)), per_b(s0)],
        out_shape=[jax.ShapeDtypeStruct((b, t, GDN_HEADS * GDN_DV), F32), jax.ShapeDtypeStruct(s0.shape, F32)],
        scratch_shapes=[pltpu.VMEM((2 * GDN_HEADS, t, LANE), BF16), pltpu.VMEM((2 * GDN_HEADS, t, LANE), BF16),
                        pltpu.VMEM((2 * GDN_HEADS, t, LANE), BF16), pltpu.VMEM((2 * GDN_HEADS, t, LANE), BF16),
                        pltpu.VMEM((2 * GDN_HEADS, n, GDN_CHUNK, GDN_CHUNK), BF16),
                        pltpu.VMEM((n, 2 * GDN_HEADS, LANE), F32),
                        pltpu.VMEM((GDN_LOCAL_CHUNKS * GDN_CHUNK, 3 * GDN_HEADS * LANE), F32),
                        pltpu.VMEM((t, 16), F32), pltpu.VMEM((n, 16, GDN_CHUNK), F32), pltpu.VMEM((t, 16), F32),
                        pltpu.VMEM((2 * GDN_HEADS, GDN_DK, GDN_DV), F32)],
        compiler_params=_cparams(1), name="gdn",
    )(qkv, ab, abt, cw, al, dt, alt, dtt, s0)


def _attd_kernel(*refs, n_ctx):
    if n_ctx:
        q_ref, k_ref, vt_ref, kc_ref, vct_ref, o_ref = refs
    else:
        q_ref, k_ref, vt_ref, o_ref = refs
    n_new = k_ref.shape[1]
    grp = ATT_HEADS // ATT_KV_HEADS
    chunks = []
    for c0, cn in _key_chunks(n_new):
        chunks.append((
            lambda g, c0=c0, cn=cn: k_ref[0, c0:c0 + cn, g * LANE:(g + 1) * LANE],
            lambda g, c0=c0, cn=cn: vt_ref[0, g * LANE:(g + 1) * LANE, c0:c0 + cn],
            None))
    if n_ctx:
        for c0, cn in _key_chunks(n_ctx):
            chunks.append((
                lambda g, c0=c0, cn=cn: kc_ref[0, c0:c0 + cn, g * LANE:(g + 1) * LANE].astype(BF16),
                lambda g, c0=c0, cn=cn: vct_ref[0, g * LANE:(g + 1) * LANE, c0:c0 + cn].astype(BF16),
                None))
    qs = [q_ref[0, :, h * LANE:(h + 1) * LANE] for h in range(ATT_HEADS)]
    outs = _attend_heads(qs, [h // grp for h in range(ATT_HEADS)], chunks, ATT_DH)
    o_ref[0] = jnp.concatenate(outs, axis=0).T.astype(BF16)


def _attd(q, k, vt, k_c, v_ct, tq):
    b, t, _ = q.shape
    n_ctx = 0 if k_c is None else k_c.shape[1]
    per_b = lambda a: pl.BlockSpec((1,) + a.shape[1:], lambda i, j: (i, 0, 0))
    in_specs = [pl.BlockSpec((1, tq, 512), lambda i, j: (i, j, 0)), per_b(k), per_b(vt)]
    args = [q, k, vt]
    if n_ctx:
        in_specs += [per_b(k_c), per_b(v_ct)]
        args += [k_c, v_ct]
    return pl.pallas_call(
        functools.partial(_attd_kernel, n_ctx=n_ctx),
        grid=(b, t // tq), in_specs=in_specs,
        out_specs=pl.BlockSpec((1, tq, 512), lambda i, j: (i, j, 0)),
        out_shape=jax.ShapeDtypeStruct((b, t, 512), BF16),
        compiler_params=_cparams(2), name="attd_dec" if n_ctx else "attd_ctx",
    )(*args)


def _outproj1_kernel(oc_ref, zc_ref, od_ref, zd_ref, x_ref, mod_ref, gn_ref, w_ref, lnf_ref, y_ref):
    zc = zc_ref[0].astype(F32)
    parts = []
    for j in range(GDN_HEADS):
        oc = _rms_rows(oc_ref[0, :, j * LANE:(j + 1) * LANE], gn_ref[...])
        parts.append((oc * _silu(zc[:, j * LANE:(j + 1) * LANE])).astype(BF16))
    gc = jnp.concatenate(parts, axis=1)
    gd = (od_ref[0].astype(F32) * _silu(zd_ref[0].astype(F32))).astype(BF16)
    y = _dot(gc, w_ref[0:512, :]) + _dot(gd, w_ref[512:1024, :])
    x2 = x_ref[0] + mod_ref[0, 2:3, :] * y
    y_ref[0] = _rms_rows(x2, lnf_ref[...])


def _outproj1(oc, zc, od, zd, x, mod, gn, w, lnf):
    b, t, d = x.shape
    tr = _row_tile(t)
    bm = mod.shape[0]
    rows = lambda c: pl.BlockSpec((1, tr, c), lambda i, j: (i, j, 0))
    full = lambda a: pl.BlockSpec(a.shape, lambda i, j: (0,) * a.ndim)
    return pl.pallas_call(
        _outproj1_kernel, grid=(b, t // tr),
        in_specs=[rows(512), rows(512), rows(512), rows(512), rows(d),
                  pl.BlockSpec((1, 3, d), (lambda i, j: (i, 0, 0)) if bm > 1 else (lambda i, j: (0, 0, 0))),
                  full(gn), full(w), full(lnf)],
        out_specs=rows(d), out_shape=jax.ShapeDtypeStruct((b, t, d), F32),
        compiler_params=_cparams(2), name="outproj1",
    )(oc, zc, od, zd, x, mod, gn, w, lnf)


def _rope_table(n_tok, rot_dim):
    quarter = rot_dim // 4
    inv = np.float32(ROPE_THETA) ** (-np.arange(quarter, dtype=np.float32) / np.float32(quarter))
    tt = np.arange(n_tok)
    pos = np.stack([tt // GRID_W, tt % GRID_W], axis=-1).astype(np.float32)
    ang = (pos[:, :, None] * inv).astype(np.float32)
    cos, sin = np.cos(ang), np.sin(ang)
    c = np.concatenate([cos, cos], axis=-1).reshape(n_tok, rot_dim)
    s = np.concatenate([-sin, sin], axis=-1).reshape(n_tok, rot_dim)
    return c.astype(np.float32), s.astype(np.float32)


def _place(tab, fill, off, width):
    out = np.full((tab.shape[0], width), fill, np.float32)
    out[:, off:off + tab.shape[1]] = tab
    return out


def _prep_l0(w_in0, w_uq, w_ukv):
    d = w_in0.shape[0]
    cq, ckv, kr, za, qb, kb, vb, zb = jnp.split(w_in0, [384, 640, 672, 1184, 1696, 1824, 1952], axis=1)
    kr_pad = jnp.concatenate([jnp.zeros((d, 64), F32), kr, jnp.zeros((d, 32), F32)], axis=1)
    w = jnp.concatenate([cq, ckv, kr_pad, za, qb, kb, vb, zb], axis=1).astype(BF16)
    wvbt = vb.T.astype(BF16)
    uq = w_uq.reshape(MLA_Q_LORA, MLA_HEADS, MLA_NOPE + MLA_ROPE)
    wuq = jnp.pad(uq, ((0, 0), (0, 0), (0, LANE - MLA_NOPE - MLA_ROPE))).reshape(MLA_Q_LORA, MLA_HEADS * LANE)
    ukv = w_ukv.reshape(MLA_KV_LORA, MLA_HEADS, MLA_NOPE + MLA_V)
    wuk = jnp.pad(ukv[:, :, :MLA_NOPE], ((0, 0), (0, 0), (0, LANE - MLA_NOPE))).reshape(MLA_KV_LORA, MLA_HEADS * LANE)
    wuvt = ukv[:, :, MLA_NOPE:].reshape(MLA_KV_LORA, MLA_HEADS * MLA_V).T
    return w, wvbt, wuq.astype(BF16), wuk.astype(BF16), wuvt.astype(BF16)


def _prep_l1(w_in1):
    d = w_in1.shape[0]
    qkv, a, bb, zc, qd, kd, vd, zd = jnp.split(w_in1, [1536, 1544, 1552, 2064, 2576, 2832, 3088], axis=1)
    w = jnp.concatenate([qkv, zc, qd, kd, vd, zd, a, bb, jnp.zeros((d, 112), F32)], axis=1).astype(BF16)
    return w, vd.T.astype(BF16)


def _chunk_rows(ab):
    b, t, c = ab.shape
    return jnp.swapaxes(ab.reshape(b, t // GDN_CHUNK, GDN_CHUNK, c), 2, 3)


def _trunk(x, mod, caches, p, tables, tq):
    dec = caches is not None
    t0m, t0s, t1 = tables if dec else (None, None, None)
    (qa, ckv, kr, za, qb, kb, vbt, zb, *ctx0) = _inproj0(
        x, mod[0], p["ln0"], p["w0"], p["wvbt"], p["qn"], p["wuq"], p["kvn"], (t0m + t0s) if dec else None)
    if dec:
        ckv_c, kr_c, kb_c, vb_ct, s0, kd_c, vd_ct = caches
    else:
        ckv_c = kr_c = kb_c = vb_ct = kd_c = vd_ct = None
        s0 = jnp.zeros((x.shape[0], 2, GDN_HEADS, GDN_DK, GDN_DV), F32)
    oa = _mla(qa, ckv, kr, ckv_c, kr_c, p["wuk"], p["wuvt"], tq)
    ob = _swa(p["sink"], qb, kb, vbt, kb_c, vb_ct, tq)
    x1 = _outproj0(oa, za, ob, zb, x, mod[0], p["wout0"])
    (qkv, zc, qd, kd, vdt, zd, ab, *ctx1) = _inproj1(
        x1, mod[1], p["ln1"], p["w1"], p["wvdt"], p["aqn"], p["akn"], t1 if dec else None)
    oc, sfin = _gdn(qkv, ab, _chunk_rows(ab), p["cw"], p["al"], p["dt"], p["alt"], p["dtt"], s0)
    od = _attd(qd, kd, vdt, kd_c, vd_ct, tq)
    y = _outproj1(oc, zc, od, zd, x1, mod[1], p["gn"], p["wout1"], p["lnf"])
    return y, ctx0, sfin, ctx1


def kernel(x_prompt, x_sample, cache_l0_mla_ckv, cache_l0_mla_krope, cache_l0_swa_k, cache_l0_swa_v,
           state_l1_gdn, cache_l1_attn_k, cache_l1_attn_v, c, c_ctx,
           w_mod0, b_mod0, ln0, w_in0, mla_q_norm, w_uq, mla_kv_norm, w_ukv, swa_sink, w_out0,
           w_mod1, b_mod1, ln1, w_in1, gdn_conv, gdn_a_log, gdn_dt_bias, gdn_norm, att_q_norm, att_k_norm, w_out1,
           ln_f):
    d = x_prompt.shape[-1]
    bd, td = x_sample.shape[:2]
    bc, tc = x_prompt.shape[:2]
    past = cache_l0_mla_ckv.shape[1]
    row = lambda v: v.reshape(1, -1)
    w0, wvbt, wuq, wuk, wuvt = _prep_l0(w_in0, w_uq, w_ukv)
    w1, wvdt = _prep_l1(w_in1)
    al8 = gdn_a_log.reshape(1, 2 * GDN_HEADS)
    dt8 = gdn_dt_bias.reshape(1, 2 * GDN_HEADS)
    al16 = jnp.pad(al8, ((0, 0), (0, 8)))
    dt16 = jnp.pad(dt8, ((0, 0), (0, 8)))
    p = dict(ln0=row(ln0), w0=w0, wvbt=wvbt, qn=row(mla_q_norm), wuq=wuq, kvn=row(mla_kv_norm), wuk=wuk, wuvt=wuvt,
             sink=swa_sink, wout0=w_out0.astype(BF16), ln1=row(ln1), w1=w1, wvdt=wvdt, aqn=row(att_q_norm),
             akn=row(att_k_norm), cw=gdn_conv, al=al16, dt=dt16, alt=al16.T, dtt=dt16.T, gn=row(gdn_norm),
             wout1=w_out1.astype(BF16), lnf=row(ln_f))
    n_rows = -(-(bd + 1) // 8) * 8
    c_rows = jnp.concatenate([c, c_ctx[None, :], jnp.zeros((n_rows - bd - 1, d), F32)], axis=0)
    mods = [_mod(c_rows, w_mod0, b_mod0), _mod(c_rows, w_mod1, b_mod1)]
    mod_dec = [m[:bd].reshape(bd, 3, d) for m in mods]
    mod_ctx = [m[bd:bd + 1].reshape(1, 3, d) for m in mods]
    cm, sm = _rope_table(td, MLA_ROPE)
    t0m = (jnp.asarray(_place(cm, 1.0, MLA_NOPE, LANE)), jnp.asarray(_place(sm, 0.0, MLA_NOPE, LANE)))
    cs, ss = _rope_table(td, SWA_DH)
    t0s = (jnp.asarray(np.tile(cs, (1, LANE // SWA_DH))), jnp.asarray(np.tile(ss, (1, LANE // SWA_DH))))
    t1 = tuple(jnp.asarray(a) for a in _rope_table(td, ATT_DH))
    caches = (cache_l0_mla_ckv,
              jnp.pad(cache_l0_mla_krope, ((0, 0), (0, 0), (MLA_NOPE, LANE - MLA_NOPE - MLA_ROPE))),
              cache_l0_swa_k.reshape(bd, past, SWA_KV_HEADS * SWA_DH),
              jnp.swapaxes(cache_l0_swa_v.reshape(bd, past, SWA_KV_HEADS * SWA_DH), 1, 2),
              state_l1_gdn,
              cache_l1_attn_k.reshape(bd, past, ATT_KV_HEADS * ATT_DH),
              jnp.swapaxes(cache_l1_attn_v.reshape(bd, past, ATT_KV_HEADS * ATT_DH), 1, 2))
    y_prompt, ctx0, sfin, ctx1 = _trunk(x_prompt, mod_ctx, None, p, None, tq=tc)
    y_sample, _, _, _ = _trunk(x_sample, mod_dec, caches, p, (t0m, t0s, t1), tq=256)
    ckv32, kr32, kb32, vb32 = ctx0
    kd32, vd32 = ctx1
    return (y_prompt, y_sample, ckv32, kr32,
            kb32.reshape(bc, tc, SWA_KV_HEADS, SWA_DH), vb32.reshape(bc, tc, SWA_KV_HEADS, SWA_DH),
            sfin, kd32.reshape(bc, tc, ATT_KV_HEADS, ATT_DH), vd32.reshape(bc, tc, ATT_KV_HEADS, ATT_DH))
```

```python
import functools
import math

import jax
import jax.numpy as jnp
import numpy as np
from jax import lax
from jax.experimental import pallas as pl
from jax.experimental.pallas import tpu as pltpu

F32 = jnp.float32
BF16 = jnp.bfloat16

GRID_W = 64
ROPE_THETA = 10000.0
EPS = 1e-6
WINDOW = 128
MLA_HEADS, MLA_NOPE, MLA_ROPE, MLA_V = 8, 64, 32, 64
MLA_Q_LORA, MLA_KV_LORA = 384, 256
SWA_HEADS, SWA_KV_HEADS, SWA_DH = 8, 2, 64
GDN_HEADS, GDN_DK, GDN_DV, CONV_K, GDN_CHUNK = 4, 128, 128, 3, 64
GDN_LOCAL_CHUNKS = 4
ATT_HEADS, ATT_KV_HEADS, ATT_DH = 4, 2, 128
LANE = 128
LOG2E = math.log2(math.e)
NEG = -1e30
VMEM_LIMIT = 56 * 1024 * 1024


def _cparams(n_axes):
    return pltpu.CompilerParams(dimension_semantics=("arbitrary",) * n_axes, vmem_limit_bytes=VMEM_LIMIT)


def _dot(a, b):
    return jnp.dot(a, b, preferred_element_type=F32)


def _dot_nt(a, b):
    return lax.dot_general(a, b, (((1,), (1,)), ((), ())), preferred_element_type=F32)


def _dot_tn(a, b):
    return lax.dot_general(a, b, (((0,), (0,)), ((), ())), preferred_element_type=F32)


def _dot_exact(a, b):
    return jnp.dot(a, b, preferred_element_type=F32, precision=lax.Precision.HIGHEST)


def _split(a):
    hi = a.astype(BF16)
    return hi, (a - hi.astype(F32)).astype(BF16)


def _silu(x):
    return x * jax.nn.sigmoid(x)


def _rms_rows(x, g):
    return x * lax.rsqrt(jnp.mean(x * x, axis=-1, keepdims=True) + EPS) * g


def _rope_block(x, cos, sin, half):
    lane = lax.broadcasted_iota(jnp.int32, x.shape, 1)
    first = (lane // half) % 2 == 0
    partner = jnp.where(first, pltpu.roll(x, LANE - half, 1), pltpu.roll(x, half, 1))
    return x * cos + partner * sin


def _mod_kernel(c_ref, w_ref, b_ref, o_ref):
    a = _silu(c_ref[...]).astype(BF16)
    o_ref[...] = _dot(a, w_ref[...].astype(BF16)) + b_ref[...]


def _mod(c_rows, w_mod, b_mod):
    r, d = c_rows.shape
    n = w_mod.shape[1]
    tn = 1024
    return pl.pallas_call(
        _mod_kernel,
        grid=(n // tn,),
        in_specs=[pl.BlockSpec((r, d), lambda j: (0, 0)),
                  pl.BlockSpec((d, tn), lambda j: (0, j)),
                  pl.BlockSpec((1, tn), lambda j: (0, j))],
        out_specs=pl.BlockSpec((r, tn), lambda j: (0, j)),
        out_shape=jax.ShapeDtypeStruct((r, n), F32),
        compiler_params=_cparams(1),
        name="mod",
    )(c_rows, w_mod, b_mod.reshape(1, n))


def _adaln(x, mod_ref, ln_ref):
    h = _rms_rows(x, ln_ref[...])
    return h * (1.0 + mod_ref[0, 1:2, :]) + mod_ref[0, 0:1, :]


L0_OFF = dict(cq=0, ckv=384, kr=640, za=768, qb=1280, kb=1792, vb=1920, zb=2048)
L0_W = 2560


def _inproj0_kernel(*refs, rope):
    if rope:
        (x_ref, mod_ref, ln_ref, w_ref, qn_ref, wuq_ref, kvn_ref, cm_ref, sm_ref, cs_ref, ss_ref,
         qa_ref, ckv_ref, kr_ref, za_ref, qb_ref, kb_ref, vbt_ref, zb_ref) = refs
    else:
        (x_ref, mod_ref, ln_ref, w_ref, qn_ref, wuq_ref, kvn_ref,
         qa_ref, ckv_ref, kr_ref, za_ref, qb_ref, kb_ref, vbt_ref, zb_ref,
         ckv32_ref, kr32_ref, kb32_ref, vb32_ref) = refs
    h = _adaln(x_ref[0], mod_ref, ln_ref).astype(BF16)
    u = _dot(h, w_ref[...])
    o = L0_OFF
    cq = _rms_rows(u[:, o["cq"]:o["cq"] + 384], qn_ref[...]).astype(BF16)
    qa = _dot(cq, wuq_ref[...])
    ckv = _rms_rows(u[:, o["ckv"]:o["ckv"] + 256], kvn_ref[...])
    kr = u[:, o["kr"]:o["kr"] + 128]
    qb = u[:, o["qb"]:o["qb"] + 512]
    kb = u[:, o["kb"]:o["kb"] + 128]
    vb = u[:, o["vb"]:o["vb"] + 128]
    if not rope:
        ckv32_ref[0] = ckv
        kr32_ref[0] = kr[:, 64:96]
        kb32_ref[0] = kb
        vb32_ref[0] = vb
    qa_scale = (MLA_NOPE + MLA_ROPE) ** -0.5 * LOG2E
    qb_scale = SWA_DH ** -0.5 * LOG2E
    for j in range(MLA_HEADS):
        blk = qa[:, j * LANE:(j + 1) * LANE]
        if rope:
            blk = _rope_block(blk, cm_ref[...], sm_ref[...], MLA_ROPE // 4)
        qa_ref[0, :, j * LANE:(j + 1) * LANE] = (blk * qa_scale).astype(BF16)
    for j in range(SWA_HEADS * SWA_DH // LANE):
        blk = qb[:, j * LANE:(j + 1) * LANE]
        if rope:
            blk = _rope_block(blk, cs_ref[...], ss_ref[...], SWA_DH // 4)
        qb_ref[0, :, j * LANE:(j + 1) * LANE] = (blk * qb_scale).astype(BF16)
    if rope:
        kr = _rope_block(kr, cm_ref[...], sm_ref[...], MLA_ROPE // 4)
        kb = _rope_block(kb, cs_ref[...], ss_ref[...], SWA_DH // 4)
    ckv_ref[0] = ckv.astype(BF16)
    kr_ref[0] = kr.astype(BF16)
    kb_ref[0] = kb.astype(BF16)
    za_ref[0] = u[:, o["za"]:o["za"] + 512].astype(BF16)
    zb_ref[0] = u[:, o["zb"]:o["zb"] + 512].astype(BF16)
    vbt_ref[0] = vb.T.astype(BF16)


def _row_tile(t):
    return 512 if t % 512 == 0 else 256


def _inproj0(x, mod, ln, w, qn, wuq, kvn, tables):
    b, t, d = x.shape
    tr = _row_tile(t)
    rope = tables is not None
    bm = mod.shape[0]
    full = lambda a: pl.BlockSpec(a.shape, lambda i, j: (0,) * a.ndim)
    rows = lambda c: pl.BlockSpec((1, tr, c), lambda i, j: (i, j, 0))
    in_specs = [rows(d), pl.BlockSpec((1, 3, d), (lambda i, j: (i, 0, 0)) if bm > 1 else (lambda i, j: (0, 0, 0))),
                full(ln), full(w), full(qn), full(wuq), full(kvn)]
    args = [x, mod, ln, w, qn, wuq, kvn]
    if rope:
        for tab in tables:
            in_specs.append(pl.BlockSpec((tr, LANE), lambda i, j: (j, 0)))
            args.append(tab)
    out_shape = [jax.ShapeDtypeStruct((b, t, 1024), BF16), jax.ShapeDtypeStruct((b, t, 256), BF16),
                 jax.ShapeDtypeStruct((b, t, 128), BF16), jax.ShapeDtypeStruct((b, t, 512), BF16),
                 jax.ShapeDtypeStruct((b, t, 512), BF16), jax.ShapeDtypeStruct((b, t, 128), BF16),
                 jax.ShapeDtypeStruct((b, 128, t), BF16), jax.ShapeDtypeStruct((b, t, 512), BF16)]
    out_specs = [rows(1024), rows(256), rows(128), rows(512), rows(512), rows(128),
                 pl.BlockSpec((1, 128, tr), lambda i, j: (i, 0, j)), rows(512)]
    if not rope:
        out_shape += [jax.ShapeDtypeStruct((b, t, 256), F32), jax.ShapeDtypeStruct((b, t, 32), F32),
                      jax.ShapeDtypeStruct((b, t, 128), F32), jax.ShapeDtypeStruct((b, t, 128), F32)]
        out_specs += [rows(256), rows(32), rows(128), rows(128)]
    return pl.pallas_call(
        functools.partial(_inproj0_kernel, rope=rope),
        grid=(b, t // tr), in_specs=in_specs, out_specs=out_specs, out_shape=out_shape,
        compiler_params=_cparams(2), name="inproj0_dec" if rope else "inproj0_ctx",
    )(*args)


KEY_CHUNK = 512
ATT_LOOKAHEAD = 4


SUM_ROWS = 16


def _attend_heads(qs, kv_of, chunks, dv, sinks=None):
    nh = len(qs)
    tq = qs[0].shape[0]
    m = [None] * nh
    acc = [None] * nh
    items = [(ci, i) for ci in range(len(chunks)) for i in range(nh)]
    loaded = {}

    def kv(ci, src):
        if (ci, src) not in loaded:
            vt = chunks[ci][1](src)
            ones = (lax.broadcasted_iota(jnp.int32, (SUM_ROWS, vt.shape[1]), 0) == 0).astype(BF16)
            loaded[(ci, src)] = (chunks[ci][0](src), jnp.concatenate([vt, ones], axis=0))
        return loaded[(ci, src)]

    if sinks is not None:
        m = [jnp.full((1, tq), sk, F32) for sk in sinks]
        unit = (lax.broadcasted_iota(jnp.int32, (dv + SUM_ROWS, tq), 0) == dv).astype(F32)
        acc = [unit for _ in sinks]

    scores = {}
    for t in range(len(items) + ATT_LOOKAHEAD):
        if t < len(items):
            ci, i = items[t]
            scores[t] = _dot_nt(kv(ci, kv_of[i])[0], qs[i])
        t0 = t - ATT_LOOKAHEAD
        if t0 < 0:
            continue
        ci, i = items[t0]
        mask = chunks[ci][2]
        si = scores.pop(t0)
        if mask is not None:
            si = jnp.where(mask, si, NEG)
        cm = si.max(axis=0, keepdims=True)
        alpha = None
        if m[i] is None:
            m_new = cm
        else:
            m_new = jnp.maximum(m[i], cm)
            alpha = jnp.exp2(m[i] - m_new)
        p = jnp.exp2(si - m_new)
        m[i] = m_new
        pv = _dot(kv(ci, kv_of[i])[1], p.astype(BF16))
        acc[i] = pv if acc[i] is None else acc[i] * alpha + pv
    return [acc[i][:dv] * (1.0 / acc[i][dv:dv + 1]) for i in range(nh)]


def _key_chunks(n):
    step = KEY_CHUNK if n % KEY_CHUNK == 0 else n
    return [(c0, step) for c0 in range(0, n, step)]


def _mla_kernel(*refs, n_new, n_ctx, hp):
    if n_ctx:
        q_ref, ckv_ref, kr_ref, ckvc_ref, krc_ref, wuk_ref, wuvt_ref, o_ref, k_s, vt_s = refs
    else:
        q_ref, ckv_ref, kr_ref, wuk_ref, wuvt_ref, o_ref, k_s, vt_s = refs
    qi, gi = pl.program_id(1), pl.program_id(2)

    @pl.when((qi == 0) & (gi == 0))
    def _():
        def expand(ckv, kr, r0, n):
            kn = _dot(ckv, wuk_ref[...])
            for j in range(MLA_HEADS):
                k_s[j, r0:r0 + n, :] = (kn[:, j * LANE:(j + 1) * LANE] + kr).astype(BF16)
            vt_s[:, r0:r0 + n] = _dot_nt(wuvt_ref[...], ckv).astype(BF16)

        blk = 512 if n_new % 512 == 0 else 256
        for r0 in range(0, n_new, blk):
            expand(ckv_ref[0, r0:r0 + blk, :], kr_ref[0, r0:r0 + blk, :].astype(F32), r0, blk)
        if n_ctx:
            expand(ckvc_ref[0].astype(BF16), krc_ref[0], n_new, n_ctx)

    qs = [q_ref[0, :, j * LANE:(j + 1) * LANE] for j in range(hp)]
    chunks = []
    for c0, cn in _key_chunks(n_new + n_ctx):
        chunks.append((
            lambda j, c0=c0, cn=cn: k_s[gi * hp + j, c0:c0 + cn, :],
            lambda j, c0=c0, cn=cn: vt_s[pl.ds(pl.multiple_of((gi * hp + j) * MLA_V, MLA_V), MLA_V), c0:c0 + cn],
            None))
    outs = _attend_heads(qs, list(range(hp)), chunks, MLA_V)
    o_ref[0] = jnp.concatenate(outs, axis=0).T.astype(BF16)


def _mla(q, ckv, kr, ckv_c, kr_c, wuk, wuvt, tq):
    b, t, _ = q.shape
    n_ctx = 0 if ckv_c is None else ckv_c.shape[1]
    hp = 8
    tk = t + n_ctx
    rows_q = pl.BlockSpec((1, tq, hp * LANE), lambda i, j, g: (i, j, g))
    per_b = lambda a: pl.BlockSpec((1,) + a.shape[1:], lambda i, j, g: (i, 0, 0))
    full = lambda a: pl.BlockSpec(a.shape, lambda i, j, g: (0, 0))
    in_specs = [rows_q, per_b(ckv), per_b(kr)]
    args = [q, ckv, kr]
    if n_ctx:
        in_specs += [per_b(ckv_c), per_b(kr_c)]
        args += [ckv_c, kr_c]
    in_specs += [full(wuk), full(wuvt)]
    args += [wuk, wuvt]
    return pl.pallas_call(
        functools.partial(_mla_kernel, n_new=t, n_ctx=n_ctx, hp=hp),
        grid=(b, t // tq, MLA_HEADS // hp), in_specs=in_specs,
        out_specs=pl.BlockSpec((1, tq, hp * MLA_V), lambda i, j, g: (i, j, g)),
        out_shape=jax.ShapeDtypeStruct((b, t, MLA_HEADS * MLA_V), BF16),
        scratch_shapes=[pltpu.VMEM((MLA_HEADS, tk, LANE), BF16), pltpu.VMEM((MLA_HEADS * MLA_V, tk), BF16)],
        compiler_params=_cparams(3), name="mla_dec" if n_ctx else "mla_ctx",
    )(*args)


def _swa_kernel(*refs, n_new, n_ctx, tq):
    if n_ctx:
        sink_ref, q_ref, k_ref, vt_ref, kc_ref, vct_ref, o_ref = refs
    else:
        sink_ref, q_ref, k_ref, vt_ref, o_ref = refs
    qi = pl.program_id(1)
    grp = SWA_HEADS // SWA_KV_HEADS
    if n_ctx:
        span = tq + 2 * WINDOW
        q0 = qi * tq
        start = pl.multiple_of(jnp.clip(q0 - WINDOW, 0, n_new - span), LANE)
        kpos = start + lax.broadcasted_iota(jnp.int32, (span, tq), 0)
        qpos = q0 + lax.broadcasted_iota(jnp.int32, (span, tq), 1)
        band = jnp.abs(kpos - qpos) <= WINDOW
    dh = SWA_DH
    chunks = []
    if n_ctx:
        for c0, cn in _key_chunks(span):
            chunks.append((
                lambda g, c0=c0, cn=cn: k_ref[0, pl.ds(start + c0, cn), g * dh:(g + 1) * dh],
                lambda g, c0=c0, cn=cn: vt_ref[0, g * dh:(g + 1) * dh, pl.ds(start + c0, cn)],
                band[c0:c0 + cn]))
        for c0, cn in _key_chunks(n_ctx):
            chunks.append((
                lambda g, c0=c0, cn=cn: kc_ref[0, c0:c0 + cn, g * dh:(g + 1) * dh].astype(BF16),
                lambda g, c0=c0, cn=cn: vct_ref[0, g * dh:(g + 1) * dh, c0:c0 + cn].astype(BF16),
                None))
    else:
        for c0, cn in _key_chunks(n_new):
            chunks.append((
                lambda g, c0=c0, cn=cn: k_ref[0, c0:c0 + cn, g * dh:(g + 1) * dh],
                lambda g, c0=c0, cn=cn: vt_ref[0, g * dh:(g + 1) * dh, c0:c0 + cn],
                None))
    qs = [q_ref[0, :, h * dh:(h + 1) * dh] for h in range(SWA_HEADS)]
    sinks = [sink_ref[h] * LOG2E for h in range(SWA_HEADS)]
    outs = _attend_heads(qs, [h // grp for h in range(SWA_HEADS)], chunks, SWA_DH, sinks)
    o_ref[0] = jnp.concatenate(outs, axis=0).T.astype(BF16)


def _swa(sink, q, k, vt, k_c, v_ct, tq):
    b, t, _ = q.shape
    n_ctx = 0 if k_c is None else k_c.shape[1]
    per_b = lambda a: pl.BlockSpec((1,) + a.shape[1:], lambda i, j: (i, 0, 0))
    in_specs = [pl.BlockSpec(memory_space=pltpu.SMEM), pl.BlockSpec((1, tq, 512), lambda i, j: (i, j, 0)),
                per_b(k), per_b(vt)]
    args = [sink, q, k, vt]
    if n_ctx:
        in_specs += [per_b(k_c), per_b(v_ct)]
        args += [k_c, v_ct]
    return pl.pallas_call(
        functools.partial(_swa_kernel, n_new=t, n_ctx=n_ctx, tq=tq),
        grid=(b, t // tq), in_specs=in_specs,
        out_specs=pl.BlockSpec((1, tq, 512), lambda i, j: (i, j, 0)),
        out_shape=jax.ShapeDtypeStruct((b, t, 512), BF16),
        compiler_params=_cparams(2), name="swa_dec" if n_ctx else "swa_ctx",
    )(*args)


def _outproj0_kernel(oa_ref, za_ref, ob_ref, zb_ref, x_ref, mod_ref, w_ref, y_ref):
    ga = (oa_ref[0].astype(F32) * _silu(za_ref[0].astype(F32))).astype(BF16)
    gb = (ob_ref[0].astype(F32) * _silu(zb_ref[0].astype(F32))).astype(BF16)
    y = _dot(ga, w_ref[0:512, :]) + _dot(gb, w_ref[512:1024, :])
    y_ref[0] = x_ref[0] + mod_ref[0, 2:3, :] * y


def _outproj0(oa, za, ob, zb, x, mod, w):
    b, t, d = x.shape
    tr = _row_tile(t)
    bm = mod.shape[0]
    rows = lambda c: pl.BlockSpec((1, tr, c), lambda i, j: (i, j, 0))
    return pl.pallas_call(
        _outproj0_kernel, grid=(b, t // tr),
        in_specs=[rows(512), rows(512), rows(512), rows(512), rows(d),
                  pl.BlockSpec((1, 3, d), (lambda i, j: (i, 0, 0)) if bm > 1 else (lambda i, j: (0, 0, 0))),
                  pl.BlockSpec(w.shape, lambda i, j: (0, 0))],
        out_specs=rows(d), out_shape=jax.ShapeDtypeStruct((b, t, d), F32),
        compiler_params=_cparams(2), name="outproj0",
    )(oa, za, ob, zb, x, mod, w)


L1_OFF = dict(qkv=0, zc=1536, qd=2048, kd=2560, vd=2816, zd=3072, ab=3584)
L1_W = 3712


def _inproj1_kernel(*refs, rope):
    if rope:
        (x_ref, mod_ref, ln_ref, w_ref, qn_ref, kn_ref, c_ref, s_ref,
         qkv_ref, zc_ref, qd_ref, kd_ref, vdt_ref, zd_ref, ab_ref) = refs
    else:
        (x_ref, mod_ref, ln_ref, w_ref, qn_ref, kn_ref,
         qkv_ref, zc_ref, qd_ref, kd_ref, vdt_ref, zd_ref, ab_ref, kd32_ref, vd32_ref) = refs
    h = _adaln(x_ref[0], mod_ref, ln_ref).astype(BF16)
    u = _dot(h, w_ref[...])
    o = L1_OFF
    qkv_ref[0] = u[:, o["qkv"]:o["qkv"] + 1536].astype(BF16)
    zc_ref[0] = u[:, o["zc"]:o["zc"] + 512].astype(BF16)
    zd_ref[0] = u[:, o["zd"]:o["zd"] + 512].astype(BF16)
    ab_ref[0] = u[:, o["ab"]:o["ab"] + 16]
    qd_scale = ATT_DH ** -0.5 * LOG2E
    for j in range(ATT_HEADS):
        blk = _rms_rows(u[:, o["qd"] + j * LANE:o["qd"] + (j + 1) * LANE], qn_ref[...])
        if rope:
            blk = _rope_block(blk, c_ref[...], s_ref[...], ATT_DH // 4)
        qd_ref[0, :, j * LANE:(j + 1) * LANE] = (blk * qd_scale).astype(BF16)
    for j in range(ATT_KV_HEADS):
        blk = _rms_rows(u[:, o["kd"] + j * LANE:o["kd"] + (j + 1) * LANE], kn_ref[...])
        if rope:
            blk = _rope_block(blk, c_ref[...], s_ref[...], ATT_DH // 4)
        else:
            kd32_ref[0, :, j * LANE:(j + 1) * LANE] = blk
        kd_ref[0, :, j * LANE:(j + 1) * LANE] = blk.astype(BF16)
    vd = u[:, o["vd"]:o["vd"] + 256]
    if not rope:
        vd32_ref[0] = vd
    vdt_ref[0] = vd.T.astype(BF16)


def _inproj1(x, mod, ln, w, qn, kn, tables):
    b, t, d = x.shape
    tr = _row_tile(t)
    rope = tables is not None
    bm = mod.shape[0]
    full = lambda a: pl.BlockSpec(a.shape, lambda i, j: (0,) * a.ndim)
    rows = lambda c: pl.BlockSpec((1, tr, c), lambda i, j: (i, j, 0))
    in_specs = [rows(d), pl.BlockSpec((1, 3, d), (lambda i, j: (i, 0, 0)) if bm > 1 else (lambda i, j: (0, 0, 0))),
                full(ln), full(w), full(qn), full(kn)]
    args = [x, mod, ln, w, qn, kn]
    if rope:
        for tab in tables:
            in_specs.append(pl.BlockSpec((tr, LANE), lambda i, j: (j, 0)))
            args.append(tab)
    out_shape = [jax.ShapeDtypeStruct((b, t, 1536), BF16), jax.ShapeDtypeStruct((b, t, 512), BF16),
                 jax.ShapeDtypeStruct((b, t, 512), BF16), jax.ShapeDtypeStruct((b, t, 256), BF16),
                 jax.ShapeDtypeStruct((b, 256, t), BF16), jax.ShapeDtypeStruct((b, t, 512), BF16),
                 jax.ShapeDtypeStruct((b, t, 16), F32)]
    out_specs = [rows(1536), rows(512), rows(512), rows(256),
                 pl.BlockSpec((1, 256, tr), lambda i, j: (i, 0, j)), rows(512), rows(16)]
    if not rope:
        out_shape += [jax.ShapeDtypeStruct((b, t, 256), F32), jax.ShapeDtypeStruct((b, t, 256), F32)]
        out_specs += [rows(256), rows(256)]
    return pl.pallas_call(
        functools.partial(_inproj1_kernel, rope=rope),
        grid=(b, t // tr), in_specs=in_specs, out_specs=out_specs, out_shape=out_shape,
        compiler_params=_cparams(2), name="inproj1_dec" if rope else "inproj1_ctx",
    )(*args)


def _gdn_local(blocks):
    c = GDN_CHUNK
    row = lax.broadcasted_iota(jnp.int32, (c, c), 0)
    col = lax.broadcasted_iota(jnp.int32, (c, c), 1)
    lane2 = lax.broadcasted_iota(jnp.int32, (c, 2 * c), 1)
    eye = (row == col).astype(F32)
    eye_t = jnp.concatenate([eye, jnp.zeros((c, c), F32)], axis=1).astype(BF16)
    chains = [ch for blk in blocks for ch in blk["dirs"]]
    for blk in blocks:
        for ch in blk["dirs"]:
            causal = (row <= col) if ch["upper"] else (row >= col)
            ch["strict"] = (row < col) if ch["upper"] else (row > col)
            ch["decay"] = jnp.exp(jnp.where(causal, ch["gc_col"] - ch["gc_row"], -jnp.inf))
            ch["kb"] = blk["k"] * ch["beta_col"]
            ch["egc"] = jnp.exp(ch["gc_col"])
    for blk in blocks:
        lhs = jnp.concatenate([ch["kb"] for ch in blk["dirs"]] + [blk["q"]], axis=0).astype(BF16)
        a = _dot_nt(lhs, blk["k"].astype(BF16))
        nd = len(blk["dirs"])
        for di, ch in enumerate(blk["dirs"]):
            x = jnp.where(ch["strict"], -(a[di * c:(di + 1) * c] * ch["decay"]), 0.0)
            ch["intra"] = (a[nd * c:] * ch["decay"]).astype(BF16)
            ch["w"] = jnp.concatenate([eye, x], axis=1)
    for _ in range(6):
        for ch in chains:
            w = ch["w"]
            wh = w.astype(BF16)
            lo = w - wh.astype(F32)
            php = jnp.where(lane2 < c, pltpu.roll(w, c, 1), lo).astype(BF16)
            ch["w"] = _dot(jnp.concatenate([wh, php], axis=1),
                           jnp.concatenate([eye_t, wh, lo.astype(BF16), wh], axis=0))
    for blk in blocks:
        for ch in blk["dirs"]:
            rhs = jnp.concatenate([blk["v"] * ch["beta_col"], ch["kb"] * ch["egc"]], axis=1).astype(BF16)
            sol = _dot(ch["w"][:, :c].astype(BF16), rhs)
            ch["u"], ch["wv"] = sol[:, :GDN_DV].astype(BF16), sol[:, GDN_DV:].astype(BF16)
            ch["qe"] = (blk["q"] * ch["egc"]).astype(BF16)
            ch["kd"] = (blk["k"] * jnp.exp(ch["glast"] - ch["gc_col"])).astype(BF16)
            ch["eg"] = jnp.exp(ch["glast"])


def _gdn_scan(chains):
    c = GDN_CHUNK
    for ch in chains:
        ch["sb"] = ch["s"].astype(BF16)
    for ch in chains:
        r = _dot(jnp.concatenate([ch["wv"], ch["qe"]], axis=0), ch["sb"])
        ch["vn"] = (ch["u"].astype(F32) - r[:c]).astype(BF16)
        ch["qs"] = r[c:]
    outs = []
    for ch in chains:
        o = ch["qs"] + _dot(ch["intra"], ch["vn"])
        s_new = ch["s"] * ch["eg"] + _dot_tn(ch["kd"], ch["vn"])
        outs.append((o, s_new))
    return outs


def _gdn_kernel(qkv_ref, ab_ref, abt_ref, cw_ref, al_ref, dt_ref, alt_ref, dtt_ref, s0_ref, o_ref, sf_ref,
                u_s, wv_s, qe_s, kd_s, in_s, eg_s, xc_s, gcol_s, grow_s, beta_s, st_s, *, t):
    c = GDN_CHUNK
    n = t // c
    nh = GDN_HEADS
    ab = ab_ref[0]
    gact = -jnp.exp(al_ref[...]) * jax.nn.softplus(ab + dt_ref[...])
    lane16 = lax.broadcasted_iota(jnp.int32, ab.shape, 1)
    beta_s[...] = jnp.where(lane16 < 2 * nh, gact, jax.nn.sigmoid(ab))
    r64 = lax.broadcasted_iota(jnp.int32, (c, c), 0)
    c64 = lax.broadcasted_iota(jnp.int32, (c, c), 1)
    tril = (r64 >= c64).astype(F32)
    triu = (r64 <= c64).astype(F32)
    lane_c = lax.broadcasted_iota(jnp.int32, (c, 16), 1)
    sub_c = lax.broadcasted_iota(jnp.int32, (16, c), 0)

    def cum_body(i, carry):
        r0 = pl.multiple_of(i * c, c)
        g = beta_s[pl.ds(r0, c), :]
        gcol_s[pl.ds(r0, c), :] = jnp.where(lane_c < nh, _dot_exact(tril, g), _dot_exact(triu, g))
        gt = -jnp.exp(alt_ref[...]) * jax.nn.softplus(abt_ref[0, i] + dtt_ref[...])
        grow_s[i] = jnp.where(sub_c < nh, _dot_exact(gt, triu), _dot_exact(gt, tril))
        return carry

    lax.fori_loop(0, n, cum_body, 0, unroll=4)

    ncs = GDN_LOCAL_CHUNKS
    rb = ncs * c
    rowb = lax.broadcasted_iota(jnp.int32, (rb, LANE), 0)

    def conv_block(r0):
        rp = pl.multiple_of(jnp.maximum(r0 - 16, 0), 16)
        rn = pl.multiple_of(jnp.minimum(r0 + rb, t - 16), 16)
        has_prev, has_next = r0 > 0, r0 + rb < t
        ys = []
        for j in range(3 * nh):
            cols = slice(j * LANE, (j + 1) * LANE)
            x = qkv_ref[0, pl.ds(r0, rb), cols].astype(F32)
            prv = jnp.where(has_prev, qkv_ref[0, pl.ds(rp, 16), cols].astype(F32)[15:16], 0.0)
            nxt = jnp.where(has_next, qkv_ref[0, pl.ds(rn, 16), cols].astype(F32)[0:1], 0.0)
            xp = jnp.where(rowb == 0, prv, pltpu.roll(x, 1, 0))
            xn = jnp.where(rowb == rb - 1, nxt, pltpu.roll(x, rb - 1, 0))
            cw = cw_ref[:, cols]
            y = _silu(xp * cw[0:1, :] + x * cw[1:2, :] + xn * cw[2:3, :])
            if j < 2 * nh:
                y = y * lax.rsqrt(jnp.sum(y * y, axis=-1, keepdims=True) + EPS)
            if j < nh:
                y = y * GDN_DK ** -0.5
            ys.append(y)
        return ys

    for j, y in enumerate(conv_block(0)):
        xc_s[:, j * LANE:(j + 1) * LANE] = y

    def local_body(jb, carry):
        r0 = pl.multiple_of(jb * rb, rb)
        xs = [xc_s[:, j * LANE:(j + 1) * LANE] for j in range(3 * nh)]
        xs_next = conv_block(pl.multiple_of(jnp.minimum(jb + 1, n // ncs - 1) * rb, rb))
        gcol = gcol_s[pl.ds(r0, rb), :]
        bet = beta_s[pl.ds(r0, rb), :]
        blocks = []
        for sub in range(ncs):
            ci = ncs * jb + sub
            rows = slice(sub * c, (sub + 1) * c)
            grow = grow_s[ci]
            for hh in range(nh):
                dirs = []
                for d in range(2):
                    ch = d * nh + hh
                    last = sub * c + (c - 1 if d == 0 else 0)
                    dirs.append(dict(gc_col=gcol[rows, ch:ch + 1], gc_row=grow[ch:ch + 1, :],
                                     beta_col=bet[rows, 2 * nh + ch:2 * nh + ch + 1],
                                     glast=gcol[last:last + 1, ch:ch + 1], upper=(d == 1), ch=ch, ci=ci,
                                     r0=r0 + sub * c))
                blocks.append(dict(q=xs[hh][rows], k=xs[nh + hh][rows], v=xs[2 * nh + hh][rows], dirs=dirs))
        _gdn_local(blocks)
        for blk in blocks:
            for chn in blk["dirs"]:
                ch, rr = chn["ch"], pl.ds(pl.multiple_of(chn["r0"], c), c)
                u_s[ch, rr, :] = chn["u"]
                wv_s[ch, rr, :] = chn["wv"]
                qe_s[ch, rr, :] = chn["qe"]
                kd_s[ch, rr, :] = chn["kd"]
                in_s[ch, chn["ci"]] = chn["intra"]
                eg_s[chn["ci"], ch:ch + 1, :] = jnp.broadcast_to(chn["eg"], (1, LANE))
        for j, y in enumerate(xs_next):
            xc_s[:, j * LANE:(j + 1) * LANE] = y
        return carry

    lax.fori_loop(0, n // ncs, local_body, 0)

    for d in range(2):
        for hh in range(nh):
            st_s[d * nh + hh] = s0_ref[0, d, hh]
    o_ref[...] = jnp.zeros_like(o_ref)

    def scan_body(i, carry):
        chains = []
        for d in range(2):
            ci = i if d == 0 else n - 1 - i
            rr = pl.ds(pl.multiple_of(ci * c, c), c)
            eg = eg_s[ci]
            for hh in range(nh):
                ch = d * nh + hh
                chains.append(dict(u=u_s[ch, rr, :], wv=wv_s[ch, rr, :], qe=qe_s[ch, rr, :], kd=kd_s[ch, rr, :],
                                   intra=in_s[ch, ci], eg=eg[ch:ch + 1, :], s=st_s[ch], rr=rr, hh=hh, ch=ch))
        for chn, (o, s_new) in zip(chains, _gdn_scan(chains)):
            st_s[chn["ch"]] = s_new
            o_ref[0, chn["rr"], chn["hh"] * LANE:(chn["hh"] + 1) * LANE] += o
        return carry

    lax.fori_loop(0, n, scan_body, 0)
    for d in range(2):
        for hh in range(nh):
            sf_ref[0, d, hh] = st_s[d * nh + hh]


def _gdn(qkv, ab, abt, cw, al, dt, alt, dtt, s0):
    b, t, _ = qkv.shape
    n = t // GDN_CHUNK
    per_b = lambda a: pl.BlockSpec((1,) + a.shape[1:], lambda i: (i,) + (0,) * (a.ndim - 1))
    full = lambda a: pl.BlockSpec(a.shape, lambda i: (0,) * a.ndim)
    return pl.pallas_call(
        functools.partial(_gdn_kernel, t=t), grid=(b,),
        in_specs=[per_b(qkv), per_b(ab), per_b(abt), full(cw), full(al), full(dt), full(alt), full(dtt), per_b(s0)],
        out_specs=[pl.BlockSpec((1, t, GDN_HEADS * GDN_DV), lambda i: (i, 0, 0)), per_b(s0)],
        out_shape=[jax.ShapeDtypeStruct((b, t, GDN_HEADS * GDN_DV), F32), jax.ShapeDtypeStruct(s0.shape, F32)],
        scratch_shapes=[pltpu.VMEM((2 * GDN_HEADS, t, LANE), BF16), pltpu.VMEM((2 * GDN_HEADS, t, LANE), BF16),
                        pltpu.VMEM((2 * GDN_HEADS, t, LANE), BF16), pltpu.VMEM((2 * GDN_HEADS, t, LANE), BF16),
                        pltpu.VMEM((2 * GDN_HEADS, n, GDN_CHUNK, GDN_CHUNK), BF16),
                        pltpu.VMEM((n, 2 * GDN_HEADS, LANE), F32),
                        pltpu.VMEM((GDN_LOCAL_CHUNKS * GDN_CHUNK, 3 * GDN_HEADS * LANE), F32),
                        pltpu.VMEM((t, 16), F32), pltpu.VMEM((n, 16, GDN_CHUNK), F32), pltpu.VMEM((t, 16), F32),
                        pltpu.VMEM((2 * GDN_HEADS, GDN_DK, GDN_DV), F32)],
        compiler_params=_cparams(1), name="gdn",
    )(qkv, ab, abt, cw, al, dt, alt, dtt, s0)


def _attd_kernel(*refs, n_ctx):
    if n_ctx:
        q_ref, k_ref, vt_ref, kc_ref, vct_ref, o_ref = refs
    else:
        q_ref, k_ref, vt_ref, o_ref = refs
    n_new = k_ref.shape[1]
    grp = ATT_HEADS // ATT_KV_HEADS
    chunks = []
    for c0, cn in _key_chunks(n_new):
        chunks.append((
            lambda g, c0=c0, cn=cn: k_ref[0, c0:c0 + cn, g * LANE:(g + 1) * LANE],
            lambda g, c0=c0, cn=cn: vt_ref[0, g * LANE:(g + 1) * LANE, c0:c0 + cn],
            None))
    if n_ctx:
        for c0, cn in _key_chunks(n_ctx):
            chunks.append((
                lambda g, c0=c0, cn=cn: kc_ref[0, c0:c0 + cn, g * LANE:(g + 1) * LANE].astype(BF16),
                lambda g, c0=c0, cn=cn: vct_ref[0, g * LANE:(g + 1) * LANE, c0:c0 + cn].astype(BF16),
                None))
    qs = [q_ref[0, :, h * LANE:(h + 1) * LANE] for h in range(ATT_HEADS)]
    outs = _attend_heads(qs, [h // grp for h in range(ATT_HEADS)], chunks, ATT_DH)
    o_ref[0] = jnp.concatenate(outs, axis=0).T.astype(BF16)


def _attd(q, k, vt, k_c, v_ct, tq):
    b, t, _ = q.shape
    n_ctx = 0 if k_c is None else k_c.shape[1]
    per_b = lambda a: pl.BlockSpec((1,) + a.shape[1:], lambda i, j: (i, 0, 0))
    in_specs = [pl.BlockSpec((1, tq, 512), lambda i, j: (i, j, 0)), per_b(k), per_b(vt)]
    args = [q, k, vt]
    if n_ctx:
        in_specs += [per_b(k_c), per_b(v_ct)]
        args += [k_c, v_ct]
    return pl.pallas_call(
        functools.partial(_attd_kernel, n_ctx=n_ctx),
        grid=(b, t // tq), in_specs=in_specs,
        out_specs=pl.BlockSpec((1, tq, 512), lambda i, j: (i, j, 0)),
        out_shape=jax.ShapeDtypeStruct((b, t, 512), BF16),
        compiler_params=_cparams(2), name="attd_dec" if n_ctx else "attd_ctx",
    )(*args)


def _outproj1_kernel(oc_ref, zc_ref, od_ref, zd_ref, x_ref, mod_ref, gn_ref, w_ref, lnf_ref, y_ref):
    zc = zc_ref[0].astype(F32)
    parts = []
    for j in range(GDN_HEADS):
        oc = _rms_rows(oc_ref[0, :, j * LANE:(j + 1) * LANE], gn_ref[...])
        parts.append((oc * _silu(zc[:, j * LANE:(j + 1) * LANE])).astype(BF16))
    gc = jnp.concatenate(parts, axis=1)
    gd = (od_ref[0].astype(F32) * _silu(zd_ref[0].astype(F32))).astype(BF16)
    y = _dot(gc, w_ref[0:512, :]) + _dot(gd, w_ref[512:1024, :])
    x2 = x_ref[0] + mod_ref[0, 2:3, :] * y
    y_ref[0] = _rms_rows(x2, lnf_ref[...])


def _outproj1(oc, zc, od, zd, x, mod, gn, w, lnf):
    b, t, d = x.shape
    tr = _row_tile(t)
    bm = mod.shape[0]
    rows = lambda c: pl.BlockSpec((1, tr, c), lambda i, j: (i, j, 0))
    full = lambda a: pl.BlockSpec(a.shape, lambda i, j: (0,) * a.ndim)
    return pl.pallas_call(
        _outproj1_kernel, grid=(b, t // tr),
        in_specs=[rows(512), rows(512), rows(512), rows(512), rows(d),
                  pl.BlockSpec((1, 3, d), (lambda i, j: (i, 0, 0)) if bm > 1 else (lambda i, j: (0, 0, 0))),
                  full(gn), full(w), full(lnf)],
        out_specs=rows(d), out_shape=jax.ShapeDtypeStruct((b, t, d), F32),
        compiler_params=_cparams(2), name="outproj1",
    )(oc, zc, od, zd, x, mod, gn, w, lnf)


def _rope_table(n_tok, rot_dim):
    quarter = rot_dim // 4
    inv = np.float32(ROPE_THETA) ** (-np.arange(quarter, dtype=np.float32) / np.float32(quarter))
    tt = np.arange(n_tok)
    pos = np.stack([tt // GRID_W, tt % GRID_W], axis=-1).astype(np.float32)
    ang = (pos[:, :, None] * inv).astype(np.float32)
    cos, sin = np.cos(ang), np.sin(ang)
    c = np.concatenate([cos, cos], axis=-1).reshape(n_tok, rot_dim)
    s = np.concatenate([-sin, sin], axis=-1).reshape(n_tok, rot_dim)
    return c.astype(np.float32), s.astype(np.float32)


def _place(tab, fill, off, width):
    out = np.full((tab.shape[0], width), fill, np.float32)
    out[:, off:off + tab.shape[1]] = tab
    return out


def _pack_kernel(w_ref, o_ref, *, segs):
    parts = []
    for start, width in segs:
        if start is None:
            parts.append(jnp.zeros((w_ref.shape[0], width), F32))
        else:
            parts.append(w_ref[:, start:start + width])
    o_ref[...] = jnp.concatenate(parts, axis=1).astype(BF16)


def _pack_cols(w, segs):
    d, n_in = w.shape
    n_out = sum(width for _, width in segs)
    tr = 256
    return pl.pallas_call(
        functools.partial(_pack_kernel, segs=segs), grid=(d // tr,),
        in_specs=[pl.BlockSpec((tr, n_in), lambda i: (i, 0))],
        out_specs=pl.BlockSpec((tr, n_out), lambda i: (i, 0)),
        out_shape=jax.ShapeDtypeStruct((d, n_out), BF16),
        compiler_params=_cparams(1), name="pack_cols",
    )(w)


def _prep_l0(w_in0, w_uq, w_ukv):
    w = _pack_cols(w_in0, [(0, 640), (None, 64), (640, 32), (None, 32), (672, 1792)])
    uq = w_uq.reshape(MLA_Q_LORA, MLA_HEADS, MLA_NOPE + MLA_ROPE)
    wuq = jnp.pad(uq, ((0, 0), (0, 0), (0, LANE - MLA_NOPE - MLA_ROPE))).reshape(MLA_Q_LORA, MLA_HEADS * LANE)
    ukv = w_ukv.reshape(MLA_KV_LORA, MLA_HEADS, MLA_NOPE + MLA_V)
    wuk = jnp.pad(ukv[:, :, :MLA_NOPE], ((0, 0), (0, 0), (0, LANE - MLA_NOPE))).reshape(MLA_KV_LORA, MLA_HEADS * LANE)
    wuvt = ukv[:, :, MLA_NOPE:].reshape(MLA_KV_LORA, MLA_HEADS * MLA_V).T
    return w, wuq.astype(BF16), wuk.astype(BF16), wuvt.astype(BF16)


def _prep_l1(w_in1):
    return _pack_cols(w_in1, [(0, 1536), (1552, 2048), (1536, 16), (None, 112)])


def _chunk_rows(ab):
    b, t, c = ab.shape
    return jnp.swapaxes(ab.reshape(b, t // GDN_CHUNK, GDN_CHUNK, c), 2, 3)


def _trunk(x, mod, caches, p, tables, tq):
    dec = caches is not None
    t0m, t0s, t1 = tables if dec else (None, None, None)
    (qa, ckv, kr, za, qb, kb, vbt, zb, *ctx0) = _inproj0(
        x, mod[0], p["ln0"], p["w0"], p["qn"], p["wuq"], p["kvn"], (t0m + t0s) if dec else None)
    if dec:
        ckv_c, kr_c, kb_c, vb_ct, s0, kd_c, vd_ct = caches
    else:
        ckv_c = kr_c = kb_c = vb_ct = kd_c = vd_ct = None
        s0 = jnp.zeros((x.shape[0], 2, GDN_HEADS, GDN_DK, GDN_DV), F32)
    oa = _mla(qa, ckv, kr, ckv_c, kr_c, p["wuk"], p["wuvt"], tq)
    ob = _swa(p["sink"], qb, kb, vbt, kb_c, vb_ct, tq)
    x1 = _outproj0(oa, za, ob, zb, x, mod[0], p["wout0"])
    (qkv, zc, qd, kd, vdt, zd, ab, *ctx1) = _inproj1(
        x1, mod[1], p["ln1"], p["w1"], p["aqn"], p["akn"], t1 if dec else None)
    oc, sfin = _gdn(qkv, ab, _chunk_rows(ab), p["cw"], p["al"], p["dt"], p["alt"], p["dtt"], s0)
    od = _attd(qd, kd, vdt, kd_c, vd_ct, tq)
    y = _outproj1(oc, zc, od, zd, x1, mod[1], p["gn"], p["wout1"], p["lnf"])
    return y, ctx0, sfin, ctx1


def kernel(x_prompt, x_sample, cache_l0_mla_ckv, cache_l0_mla_krope, cache_l0_swa_k, cache_l0_swa_v,
           state_l1_gdn, cache_l1_attn_k, cache_l1_attn_v, c, c_ctx,
           w_mod0, b_mod0, ln0, w_in0, mla_q_norm, w_uq, mla_kv_norm, w_ukv, swa_sink, w_out0,
           w_mod1, b_mod1, ln1, w_in1, gdn_conv, gdn_a_log, gdn_dt_bias, gdn_norm, att_q_norm, att_k_norm, w_out1,
           ln_f):
    d = x_prompt.shape[-1]
    bd, td = x_sample.shape[:2]
    bc, tc = x_prompt.shape[:2]
    past = cache_l0_mla_ckv.shape[1]
    row = lambda v: v.reshape(1, -1)
    w0, wuq, wuk, wuvt = _prep_l0(w_in0, w_uq, w_ukv)
    w1 = _prep_l1(w_in1)
    al8 = gdn_a_log.reshape(1, 2 * GDN_HEADS)
    dt8 = gdn_dt_bias.reshape(1, 2 * GDN_HEADS)
    al16 = jnp.pad(al8, ((0, 0), (0, 8)))
    dt16 = jnp.pad(dt8, ((0, 0), (0, 8)))
    p = dict(ln0=row(ln0), w0=w0, qn=row(mla_q_norm), wuq=wuq, kvn=row(mla_kv_norm), wuk=wuk, wuvt=wuvt,
             sink=swa_sink, wout0=w_out0.astype(BF16), ln1=row(ln1), w1=w1, aqn=row(att_q_norm),
             akn=row(att_k_norm), cw=gdn_conv, al=al16, dt=dt16, alt=al16.T, dtt=dt16.T, gn=row(gdn_norm),
             wout1=w_out1.astype(BF16), lnf=row(ln_f))
    n_rows = -(-(bd + 1) // 8) * 8
    c_rows = jnp.concatenate([c, c_ctx[None, :], jnp.zeros((n_rows - bd - 1, d), F32)], axis=0)
    mods = [_mod(c_rows, w_mod0, b_mod0), _mod(c_rows, w_mod1, b_mod1)]
    mod_dec = [m[:bd].reshape(bd, 3, d) for m in mods]
    mod_ctx = [m[bd:bd + 1].reshape(1, 3, d) for m in mods]
    cm, sm = _rope_table(td, MLA_ROPE)
    t0m = (jnp.asarray(_place(cm, 1.0, MLA_NOPE, LANE)), jnp.asarray(_place(sm, 0.0, MLA_NOPE, LANE)))
    cs, ss = _rope_table(td, SWA_DH)
    t0s = (jnp.asarray(np.tile(cs, (1, LANE // SWA_DH))), jnp.asarray(np.tile(ss, (1, LANE // SWA_DH))))
    t1 = tuple(jnp.asarray(a) for a in _rope_table(td, ATT_DH))
    caches = (cache_l0_mla_ckv,
              jnp.pad(cache_l0_mla_krope, ((0, 0), (0, 0), (MLA_NOPE, LANE - MLA_NOPE - MLA_ROPE))),
              cache_l0_swa_k.reshape(bd, past, SWA_KV_HEADS * SWA_DH),
              jnp.swapaxes(cache_l0_swa_v.reshape(bd, past, SWA_KV_HEADS * SWA_DH), 1, 2),
              state_l1_gdn,
              cache_l1_attn_k.reshape(bd, past, ATT_KV_HEADS * ATT_DH),
              jnp.swapaxes(cache_l1_attn_v.reshape(bd, past, ATT_KV_HEADS * ATT_DH), 1, 2))
    y_prompt, ctx0, sfin, ctx1 = _trunk(x_prompt, mod_ctx, None, p, None, tq=tc)
    y_sample, _, _, _ = _trunk(x_sample, mod_dec, caches, p, (t0m, t0s, t1), tq=256)
    ckv32, kr32, kb32, vb32 = ctx0
    kd32, vd32 = ctx1
    return (y_prompt, y_sample, ckv32, kr32,
            kb32.reshape(bc, tc, SWA_KV_HEADS, SWA_DH), vb32.reshape(bc, tc, SWA_KV_HEADS, SWA_DH),
            sfin, kd32.reshape(bc, tc, ATT_KV_HEADS, ATT_DH), vd32.reshape(bc, tc, ATT_KV_HEADS, ATT_DH))
```

```python
import functools
import math

import jax
import jax.numpy as jnp
import numpy as np
from jax import lax
from jax.experimental import pallas as pl
from jax.experimental.pallas import tpu as pltpu

F32 = jnp.float32
BF16 = jnp.bfloat16

GRID_W = 64
ROPE_THETA = 10000.0
EPS = 1e-6
WINDOW = 128
MLA_HEADS, MLA_NOPE, MLA_ROPE, MLA_V = 8, 64, 32, 64
MLA_Q_LORA, MLA_KV_LORA = 384, 256
SWA_HEADS, SWA_KV_HEADS, SWA_DH = 8, 2, 64
GDN_HEADS, GDN_DK, GDN_DV, CONV_K, GDN_CHUNK = 4, 128, 128, 3, 64
GDN_LOCAL_CHUNKS = 4
ATT_HEADS, ATT_KV_HEADS, ATT_DH = 4, 2, 128
LANE = 128
LOG2E = math.log2(math.e)
NEG = -1e30
VMEM_LIMIT = 56 * 1024 * 1024


def _cparams(n_axes):
    return pltpu.CompilerParams(dimension_semantics=("arbitrary",) * n_axes, vmem_limit_bytes=VMEM_LIMIT)


def _dot(a, b):
    return jnp.dot(a, b, preferred_element_type=F32)


def _dot_nt(a, b):
    return lax.dot_general(a, b, (((1,), (1,)), ((), ())), preferred_element_type=F32)


def _dot_tn(a, b):
    return lax.dot_general(a, b, (((0,), (0,)), ((), ())), preferred_element_type=F32)


def _dot_exact(a, b):
    return jnp.dot(a, b, preferred_element_type=F32, precision=lax.Precision.HIGHEST)


def _split(a):
    hi = a.astype(BF16)
    return hi, (a - hi.astype(F32)).astype(BF16)


def _silu(x):
    return x * jax.nn.sigmoid(x)


def _rms_rows(x, g):
    return x * lax.rsqrt(jnp.mean(x * x, axis=-1, keepdims=True) + EPS) * g


def _rope_block(x, cos, sin, half):
    lane = lax.broadcasted_iota(jnp.int32, x.shape, 1)
    first = (lane // half) % 2 == 0
    partner = jnp.where(first, pltpu.roll(x, LANE - half, 1), pltpu.roll(x, half, 1))
    return x * cos + partner * sin


def _mod_kernel(c_ref, w_ref, b_ref, o_ref):
    a = _silu(c_ref[...]).astype(BF16)
    o_ref[...] = _dot(a, w_ref[...].astype(BF16)) + b_ref[...]


def _mod(c_rows, w_mod, b_mod):
    r, d = c_rows.shape
    n = w_mod.shape[1]
    tn = 1024
    return pl.pallas_call(
        _mod_kernel,
        grid=(n // tn,),
        in_specs=[pl.BlockSpec((r, d), lambda j: (0, 0)),
                  pl.BlockSpec((d, tn), lambda j: (0, j)),
                  pl.BlockSpec((1, tn), lambda j: (0, j))],
        out_specs=pl.BlockSpec((r, tn), lambda j: (0, j)),
        out_shape=jax.ShapeDtypeStruct((r, n), F32),
        compiler_params=_cparams(1),
        name="mod",
    )(c_rows, w_mod, b_mod.reshape(1, n))


def _adaln(x, mod_ref, ln_ref):
    h = _rms_rows(x, ln_ref[...])
    return h * (1.0 + mod_ref[0, 1:2, :]) + mod_ref[0, 0:1, :]


L0_OFF = dict(cq=0, ckv=384, kr=640, za=768, qb=1280, kb=1792, vb=1920, zb=2048)
L0_W = 2560


def _inproj0_kernel(*refs, rope):
    if rope:
        (x_ref, mod_ref, ln_ref, w_ref, qn_ref, wuq_ref, kvn_ref, cm_ref, sm_ref, cs_ref, ss_ref,
         qa_ref, ckv_ref, kr_ref, za_ref, qb_ref, kb_ref, vbt_ref, zb_ref) = refs
    else:
        (x_ref, mod_ref, ln_ref, w_ref, qn_ref, wuq_ref, kvn_ref,
         qa_ref, ckv_ref, kr_ref, za_ref, qb_ref, kb_ref, vbt_ref, zb_ref,
         ckv32_ref, kr32_ref, kb32_ref, vb32_ref) = refs
    h = _adaln(x_ref[0], mod_ref, ln_ref).astype(BF16)
    u = _dot(h, w_ref[...])
    o = L0_OFF
    cq = _rms_rows(u[:, o["cq"]:o["cq"] + 384], qn_ref[...]).astype(BF16)
    qa = _dot(cq, wuq_ref[...])
    ckv = _rms_rows(u[:, o["ckv"]:o["ckv"] + 256], kvn_ref[...])
    kr = u[:, o["kr"]:o["kr"] + 128]
    qb = u[:, o["qb"]:o["qb"] + 512]
    kb = u[:, o["kb"]:o["kb"] + 128]
    vb = u[:, o["vb"]:o["vb"] + 128]
    if not rope:
        ckv32_ref[0] = ckv
        kr32_ref[0] = kr[:, 64:96]
        kb32_ref[0] = kb
        vb32_ref[0] = vb
    qa_scale = (MLA_NOPE + MLA_ROPE) ** -0.5 * LOG2E
    qb_scale = SWA_DH ** -0.5 * LOG2E
    for j in range(MLA_HEADS):
        blk = qa[:, j * LANE:(j + 1) * LANE]
        if rope:
            blk = _rope_block(blk, cm_ref[...], sm_ref[...], MLA_ROPE // 4)
        qa_ref[0, :, j * LANE:(j + 1) * LANE] = (blk * qa_scale).astype(BF16)
    for j in range(SWA_HEADS * SWA_DH // LANE):
        blk = qb[:, j * LANE:(j + 1) * LANE]
        if rope:
            blk = _rope_block(blk, cs_ref[...], ss_ref[...], SWA_DH // 4)
        qb_ref[0, :, j * LANE:(j + 1) * LANE] = (blk * qb_scale).astype(BF16)
    if rope:
        kr = _rope_block(kr, cm_ref[...], sm_ref[...], MLA_ROPE // 4)
        kb = _rope_block(kb, cs_ref[...], ss_ref[...], SWA_DH // 4)
    ckv_ref[0] = ckv.astype(BF16)
    kr_ref[0] = kr.astype(BF16)
    kb_ref[0] = kb.astype(BF16)
    za_ref[0] = u[:, o["za"]:o["za"] + 512].astype(BF16)
    zb_ref[0] = u[:, o["zb"]:o["zb"] + 512].astype(BF16)
    vbt_ref[0] = vb.T.astype(BF16)


def _row_tile(t):
    return 512 if t % 512 == 0 else 256


def _inproj0(x, mod, ln, w, qn, wuq, kvn, tables):
    b, t, d = x.shape
    tr = _row_tile(t)
    rope = tables is not None
    bm = mod.shape[0]
    full = lambda a: pl.BlockSpec(a.shape, lambda i, j: (0,) * a.ndim)
    rows = lambda c: pl.BlockSpec((1, tr, c), lambda i, j: (i, j, 0))
    in_specs = [rows(d), pl.BlockSpec((1, 3, d), (lambda i, j: (i, 0, 0)) if bm > 1 else (lambda i, j: (0, 0, 0))),
                full(ln), full(w), full(qn), full(wuq), full(kvn)]
    args = [x, mod, ln, w, qn, wuq, kvn]
    if rope:
        for tab in tables:
            in_specs.append(pl.BlockSpec((tr, LANE), lambda i, j: (j, 0)))
            args.append(tab)
    out_shape = [jax.ShapeDtypeStruct((b, t, 1024), BF16), jax.ShapeDtypeStruct((b, t, 256), BF16),
                 jax.ShapeDtypeStruct((b, t, 128), BF16), jax.ShapeDtypeStruct((b, t, 512), BF16),
                 jax.ShapeDtypeStruct((b, t, 512), BF16), jax.ShapeDtypeStruct((b, t, 128), BF16),
                 jax.ShapeDtypeStruct((b, 128, t), BF16), jax.ShapeDtypeStruct((b, t, 512), BF16)]
    out_specs = [rows(1024), rows(256), rows(128), rows(512), rows(512), rows(128),
                 pl.BlockSpec((1, 128, tr), lambda i, j: (i, 0, j)), rows(512)]
    if not rope:
        out_shape += [jax.ShapeDtypeStruct((b, t, 256), F32), jax.ShapeDtypeStruct((b, t, 32), F32),
                      jax.ShapeDtypeStruct((b, t, 128), F32), jax.ShapeDtypeStruct((b, t, 128), F32)]
        out_specs += [rows(256), rows(32), rows(128), rows(128)]
    return pl.pallas_call(
        functools.partial(_inproj0_kernel, rope=rope),
        grid=(b, t // tr), in_specs=in_specs, out_specs=out_specs, out_shape=out_shape,
        compiler_params=_cparams(2), name="inproj0_dec" if rope else "inproj0_ctx",
    )(*args)


KEY_CHUNK = 512
ATT_LOOKAHEAD = 4


SUM_ROWS = 16


def _attend_heads(qs, kv_of, chunks, dv, sinks=None):
    nh = len(qs)
    tq = qs[0].shape[0]
    m = [None] * nh
    acc = [None] * nh
    items = [(ci, i) for ci in range(len(chunks)) for i in range(nh)]
    loaded = {}

    def kv(ci, src):
        if (ci, src) not in loaded:
            vt = chunks[ci][1](src)
            ones = (lax.broadcasted_iota(jnp.int32, (SUM_ROWS, vt.shape[1]), 0) == 0).astype(BF16)
            loaded[(ci, src)] = (chunks[ci][0](src), jnp.concatenate([vt, ones], axis=0))
        return loaded[(ci, src)]

    if sinks is not None:
        m = [jnp.full((1, tq), sk, F32) for sk in sinks]
        unit = (lax.broadcasted_iota(jnp.int32, (dv + SUM_ROWS, tq), 0) == dv).astype(F32)
        acc = [unit for _ in sinks]

    scores = {}
    for t in range(len(items) + ATT_LOOKAHEAD):
        if t < len(items):
            ci, i = items[t]
            scores[t] = _dot_nt(kv(ci, kv_of[i])[0], qs[i])
        t0 = t - ATT_LOOKAHEAD
        if t0 < 0:
            continue
        ci, i = items[t0]
        mask = chunks[ci][2]
        si = scores.pop(t0)
        if mask is not None:
            si = jnp.where(mask, si, NEG)
        cm = si.max(axis=0, keepdims=True)
        alpha = None
        if m[i] is None:
            m_new = cm
        else:
            m_new = jnp.maximum(m[i], cm)
            alpha = jnp.exp2(m[i] - m_new)
        p = jnp.exp2(si - m_new)
        m[i] = m_new
        pv = _dot(kv(ci, kv_of[i])[1], p.astype(BF16))
        acc[i] = pv if acc[i] is None else acc[i] * alpha + pv
    return [acc[i][:dv] * (1.0 / acc[i][dv:dv + 1]) for i in range(nh)]


def _key_chunks(n):
    step = KEY_CHUNK if n % KEY_CHUNK == 0 else n
    return [(c0, step) for c0 in range(0, n, step)]


def _mla_kernel(*refs, n_new, n_ctx, hp):
    if n_ctx:
        q_ref, ckv_ref, kr_ref, ckvc_ref, krc_ref, wuk_ref, wuvt_ref, o_ref, k_s, vt_s = refs
    else:
        q_ref, ckv_ref, kr_ref, wuk_ref, wuvt_ref, o_ref, k_s, vt_s = refs
    qi, gi = pl.program_id(1), pl.program_id(2)

    @pl.when((qi == 0) & (gi == 0))
    def _():
        def expand(ckv, kr, r0, n):
            kn = _dot(ckv, wuk_ref[...])
            for j in range(MLA_HEADS):
                k_s[j, r0:r0 + n, :] = (kn[:, j * LANE:(j + 1) * LANE] + kr).astype(BF16)
            vt_s[:, r0:r0 + n] = _dot_nt(wuvt_ref[...], ckv).astype(BF16)

        blk = 512 if n_new % 512 == 0 else 256
        for r0 in range(0, n_new, blk):
            expand(ckv_ref[0, r0:r0 + blk, :], kr_ref[0, r0:r0 + blk, :].astype(F32), r0, blk)
        if n_ctx:
            expand(ckvc_ref[0].astype(BF16), krc_ref[0], n_new, n_ctx)

    qs = [q_ref[0, :, j * LANE:(j + 1) * LANE] for j in range(hp)]
    chunks = []
    for c0, cn in _key_chunks(n_new + n_ctx):
        chunks.append((
            lambda j, c0=c0, cn=cn: k_s[gi * hp + j, c0:c0 + cn, :],
            lambda j, c0=c0, cn=cn: vt_s[pl.ds(pl.multiple_of((gi * hp + j) * MLA_V, MLA_V), MLA_V), c0:c0 + cn],
            None))
    outs = _attend_heads(qs, list(range(hp)), chunks, MLA_V)
    o_ref[0] = jnp.concatenate(outs, axis=0).T.astype(BF16)


def _mla(q, ckv, kr, ckv_c, kr_c, wuk, wuvt, tq):
    b, t, _ = q.shape
    n_ctx = 0 if ckv_c is None else ckv_c.shape[1]
    hp = 8
    tk = t + n_ctx
    rows_q = pl.BlockSpec((1, tq, hp * LANE), lambda i, j, g: (i, j, g))
    per_b = lambda a: pl.BlockSpec((1,) + a.shape[1:], lambda i, j, g: (i, 0, 0))
    full = lambda a: pl.BlockSpec(a.shape, lambda i, j, g: (0, 0))
    in_specs = [rows_q, per_b(ckv), per_b(kr)]
    args = [q, ckv, kr]
    if n_ctx:
        in_specs += [per_b(ckv_c), per_b(kr_c)]
        args += [ckv_c, kr_c]
    in_specs += [full(wuk), full(wuvt)]
    args += [wuk, wuvt]
    return pl.pallas_call(
        functools.partial(_mla_kernel, n_new=t, n_ctx=n_ctx, hp=hp),
        grid=(b, t // tq, MLA_HEADS // hp), in_specs=in_specs,
        out_specs=pl.BlockSpec((1, tq, hp * MLA_V), lambda i, j, g: (i, j, g)),
        out_shape=jax.ShapeDtypeStruct((b, t, MLA_HEADS * MLA_V), BF16),
        scratch_shapes=[pltpu.VMEM((MLA_HEADS, tk, LANE), BF16), pltpu.VMEM((MLA_HEADS * MLA_V, tk), BF16)],
        compiler_params=_cparams(3), name="mla_dec" if n_ctx else "mla_ctx",
    )(*args)


def _swa_kernel(*refs, n_new, n_ctx, tq):
    if n_ctx:
        sink_ref, q_ref, k_ref, vt_ref, kc_ref, vct_ref, o_ref = refs
    else:
        sink_ref, q_ref, k_ref, vt_ref, o_ref = refs
    qi = pl.program_id(1)
    grp = SWA_HEADS // SWA_KV_HEADS
    if n_ctx:
        span = tq + 2 * WINDOW
        q0 = qi * tq
        start = pl.multiple_of(jnp.clip(q0 - WINDOW, 0, n_new - span), LANE)
        kpos = start + lax.broadcasted_iota(jnp.int32, (span, tq), 0)
        qpos = q0 + lax.broadcasted_iota(jnp.int32, (span, tq), 1)
        band = jnp.abs(kpos - qpos) <= WINDOW
    dh = SWA_DH
    chunks = []
    if n_ctx:
        for c0, cn in _key_chunks(span):
            chunks.append((
                lambda g, c0=c0, cn=cn: k_ref[0, pl.ds(start + c0, cn), g * dh:(g + 1) * dh],
                lambda g, c0=c0, cn=cn: vt_ref[0, g * dh:(g + 1) * dh, pl.ds(start + c0, cn)],
                band[c0:c0 + cn]))
        for c0, cn in _key_chunks(n_ctx):
            chunks.append((
                lambda g, c0=c0, cn=cn: kc_ref[0, c0:c0 + cn, g * dh:(g + 1) * dh].astype(BF16),
                lambda g, c0=c0, cn=cn: vct_ref[0, g * dh:(g + 1) * dh, c0:c0 + cn].astype(BF16),
                None))
    else:
        for c0, cn in _key_chunks(n_new):
            chunks.append((
                lambda g, c0=c0, cn=cn: k_ref[0, c0:c0 + cn, g * dh:(g + 1) * dh],
                lambda g, c0=c0, cn=cn: vt_ref[0, g * dh:(g + 1) * dh, c0:c0 + cn],
                None))
    qs = [q_ref[0, :, h * dh:(h + 1) * dh] for h in range(SWA_HEADS)]
    sinks = [sink_ref[h] * LOG2E for h in range(SWA_HEADS)]
    outs = _attend_heads(qs, [h // grp for h in range(SWA_HEADS)], chunks, SWA_DH, sinks)
    o_ref[0] = jnp.concatenate(outs, axis=0).T.astype(BF16)


def _swa(sink, q, k, vt, k_c, v_ct, tq):
    b, t, _ = q.shape
    n_ctx = 0 if k_c is None else k_c.shape[1]
    per_b = lambda a: pl.BlockSpec((1,) + a.shape[1:], lambda i, j: (i, 0, 0))
    in_specs = [pl.BlockSpec(memory_space=pltpu.SMEM), pl.BlockSpec((1, tq, 512), lambda i, j: (i, j, 0)),
                per_b(k), per_b(vt)]
    args = [sink, q, k, vt]
    if n_ctx:
        in_specs += [per_b(k_c), per_b(v_ct)]
        args += [k_c, v_ct]
    return pl.pallas_call(
        functools.partial(_swa_kernel, n_new=t, n_ctx=n_ctx, tq=tq),
        grid=(b, t // tq), in_specs=in_specs,
        out_specs=pl.BlockSpec((1, tq, 512), lambda i, j: (i, j, 0)),
        out_shape=jax.ShapeDtypeStruct((b, t, 512), BF16),
        compiler_params=_cparams(2), name="swa_dec" if n_ctx else "swa_ctx",
    )(*args)


def _silu_gate(o, z):
    return o * (z / (1.0 + jnp.exp(-z)))


def _outproj0_kernel(oa_ref, za_ref, ob_ref, zb_ref, x_ref, mod_ref, w_ref, y_ref):
    ga = _silu_gate(oa_ref[0], za_ref[0])
    gb = _silu_gate(ob_ref[0], zb_ref[0])
    y = _dot(ga, w_ref[0:512, :]) + _dot(gb, w_ref[512:1024, :])
    y_ref[0] = x_ref[0] + mod_ref[0, 2:3, :] * y


def _outproj0(oa, za, ob, zb, x, mod, w):
    b, t, d = x.shape
    tr = _row_tile(t)
    bm = mod.shape[0]
    rows = lambda c: pl.BlockSpec((1, tr, c), lambda i, j: (i, j, 0))
    return pl.pallas_call(
        _outproj0_kernel, grid=(b, t // tr),
        in_specs=[rows(512), rows(512), rows(512), rows(512), rows(d),
                  pl.BlockSpec((1, 3, d), (lambda i, j: (i, 0, 0)) if bm > 1 else (lambda i, j: (0, 0, 0))),
                  pl.BlockSpec(w.shape, lambda i, j: (0, 0))],
        out_specs=rows(d), out_shape=jax.ShapeDtypeStruct((b, t, d), F32),
        compiler_params=_cparams(2), name="outproj0",
    )(oa, za, ob, zb, x, mod, w)


L1_OFF = dict(qkv=0, zc=1536, qd=2048, kd=2560, vd=2816, zd=3072, ab=3584)
L1_W = 3712


def _inproj1_kernel(*refs, rope):
    if rope:
        (x_ref, xp_ref, xn_ref, mod_ref, ln_ref, w_ref, cw_ref, qn_ref, kn_ref, c_ref, s_ref,
         qkv_ref, zc_ref, qd_ref, kd_ref, vdt_ref, zd_ref, ab_ref) = refs
    else:
        (x_ref, xp_ref, xn_ref, mod_ref, ln_ref, w_ref, cw_ref, qn_ref, kn_ref,
         qkv_ref, zc_ref, qd_ref, kd_ref, vdt_ref, zd_ref, ab_ref, kd32_ref, vd32_ref) = refs
    h = _adaln(x_ref[0], mod_ref, ln_ref).astype(BF16)
    u = _dot(h, w_ref[...])
    o = L1_OFF
    tr = u.shape[0]
    nq = GDN_HEADS * GDN_DK
    n_qkv = 2 * nq + GDN_HEADS * GDN_DV
    halo = jnp.concatenate([xp_ref[0], xn_ref[0]], axis=0)
    uh = _dot(_adaln(halo, mod_ref, ln_ref).astype(BF16), w_ref[:, 0:n_qkv])
    jrow = pl.program_id(1)
    prv_all = jnp.where(jrow > 0, uh[7:8], 0.0)
    nxt_all = jnp.where(jrow < pl.num_programs(1) - 1, uh[8:9], 0.0)
    rowi = lax.broadcasted_iota(jnp.int32, (tr, LANE), 0)
    for j in range(n_qkv // LANE):
        cols = slice(j * LANE, (j + 1) * LANE)
        xj = u[:, cols]
        xp = jnp.where(rowi == 0, prv_all[:, cols], pltpu.roll(xj, 1, 0))
        xn = jnp.where(rowi == tr - 1, nxt_all[:, cols], pltpu.roll(xj, tr - 1, 0))
        cw = cw_ref[:, cols]
        y = _silu(xp * cw[0:1, :] + xj * cw[1:2, :] + xn * cw[2:3, :])
        if j < 2 * nq // LANE:
            y = y * lax.rsqrt(jnp.sum(y * y, axis=-1, keepdims=True) + EPS)
        if j < nq // LANE:
            y = y * GDN_DK ** -0.5
        qkv_ref[0, :, cols] = y.astype(BF16)
    zc_ref[0] = u[:, o["zc"]:o["zc"] + 512].astype(BF16)
    zd_ref[0] = u[:, o["zd"]:o["zd"] + 512].astype(BF16)
    ab_ref[0] = u[:, o["ab"]:o["ab"] + 16]
    qd_scale = ATT_DH ** -0.5 * LOG2E
    for j in range(ATT_HEADS):
        blk = _rms_rows(u[:, o["qd"] + j * LANE:o["qd"] + (j + 1) * LANE], qn_ref[...])
        if rope:
            blk = _rope_block(blk, c_ref[...], s_ref[...], ATT_DH // 4)
        qd_ref[0, :, j * LANE:(j + 1) * LANE] = (blk * qd_scale).astype(BF16)
    for j in range(ATT_KV_HEADS):
        blk = _rms_rows(u[:, o["kd"] + j * LANE:o["kd"] + (j + 1) * LANE], kn_ref[...])
        if rope:
            blk = _rope_block(blk, c_ref[...], s_ref[...], ATT_DH // 4)
        else:
            kd32_ref[0, :, j * LANE:(j + 1) * LANE] = blk
        kd_ref[0, :, j * LANE:(j + 1) * LANE] = blk.astype(BF16)
    vd = u[:, o["vd"]:o["vd"] + 256]
    if not rope:
        vd32_ref[0] = vd
    vdt_ref[0] = vd.T.astype(BF16)


def _inproj1(x, mod, ln, w, cw, qn, kn, tables):
    b, t, d = x.shape
    tr = _row_tile(t)
    rope = tables is not None
    bm = mod.shape[0]
    full = lambda a: pl.BlockSpec(a.shape, lambda i, j: (0,) * a.ndim)
    rows = lambda c: pl.BlockSpec((1, tr, c), lambda i, j: (i, j, 0))
    g8 = tr // 8
    in_specs = [rows(d),
                pl.BlockSpec((1, 8, d), lambda i, j: (i, jnp.maximum(j * g8 - 1, 0), 0)),
                pl.BlockSpec((1, 8, d), lambda i, j: (i, jnp.minimum((j + 1) * g8, t // 8 - 1), 0)),
                pl.BlockSpec((1, 3, d), (lambda i, j: (i, 0, 0)) if bm > 1 else (lambda i, j: (0, 0, 0))),
                full(ln), full(w), full(cw), full(qn), full(kn)]
    args = [x, x, x, mod, ln, w, cw, qn, kn]
    if rope:
        for tab in tables:
            in_specs.append(pl.BlockSpec((tr, LANE), lambda i, j: (j, 0)))
            args.append(tab)
    out_shape = [jax.ShapeDtypeStruct((b, t, 1536), BF16), jax.ShapeDtypeStruct((b, t, 512), BF16),
                 jax.ShapeDtypeStruct((b, t, 512), BF16), jax.ShapeDtypeStruct((b, t, 256), BF16),
                 jax.ShapeDtypeStruct((b, 256, t), BF16), jax.ShapeDtypeStruct((b, t, 512), BF16),
                 jax.ShapeDtypeStruct((b, t, 16), F32)]
    out_specs = [rows(1536), rows(512), rows(512), rows(256),
                 pl.BlockSpec((1, 256, tr), lambda i, j: (i, 0, j)), rows(512), rows(16)]
    if not rope:
        out_shape += [jax.ShapeDtypeStruct((b, t, 256), F32), jax.ShapeDtypeStruct((b, t, 256), F32)]
        out_specs += [rows(256), rows(256)]
    return pl.pallas_call(
        functools.partial(_inproj1_kernel, rope=rope),
        grid=(b, t // tr), in_specs=in_specs, out_specs=out_specs, out_shape=out_shape,
        compiler_params=_cparams(2), name="inproj1_dec" if rope else "inproj1_ctx",
    )(*args)


def _gdn_local(blocks):
    c = GDN_CHUNK
    row = lax.broadcasted_iota(jnp.int32, (c, c), 0)
    col = lax.broadcasted_iota(jnp.int32, (c, c), 1)
    lane2 = lax.broadcasted_iota(jnp.int32, (c, 2 * c), 1)
    eye = (row == col).astype(F32)
    eye_t = jnp.concatenate([eye, jnp.zeros((c, c), F32)], axis=1).astype(BF16)
    chains = [ch for blk in blocks for ch in blk["dirs"]]
    for blk in blocks:
        for ch in blk["dirs"]:
            causal = (row <= col) if ch["upper"] else (row >= col)
            ch["strict"] = (row < col) if ch["upper"] else (row > col)
            ch["decay"] = jnp.exp(jnp.where(causal, ch["gc_col"] - ch["gc_row"], -jnp.inf))
            ch["kb"] = blk["k"] * ch["beta_col"]
            ch["egc"] = jnp.exp(ch["gc_col"])
    for blk in blocks:
        lhs = jnp.concatenate([ch["kb"] for ch in blk["dirs"]] + [blk["q"]], axis=0).astype(BF16)
        a = _dot_nt(lhs, blk["k"].astype(BF16))
        nd = len(blk["dirs"])
        for di, ch in enumerate(blk["dirs"]):
            x = jnp.where(ch["strict"], -(a[di * c:(di + 1) * c] * ch["decay"]), 0.0)
            ch["intra"] = (a[nd * c:] * ch["decay"]).astype(BF16)
            ch["w"] = jnp.concatenate([eye, x], axis=1)
    for _ in range(6):
        for ch in chains:
            w = ch["w"]
            wh = w.astype(BF16)
            lo = w - wh.astype(F32)
            php = jnp.where(lane2 < c, pltpu.roll(w, c, 1), lo).astype(BF16)
            ch["w"] = _dot(jnp.concatenate([wh, php], axis=1),
                           jnp.concatenate([eye_t, wh, lo.astype(BF16), wh], axis=0))
    for blk in blocks:
        for ch in blk["dirs"]:
            rhs = jnp.concatenate([blk["v"] * ch["beta_col"], ch["kb"] * ch["egc"]], axis=1).astype(BF16)
            sol = _dot(ch["w"][:, :c].astype(BF16), rhs)
            ch["u"], ch["wv"] = sol[:, :GDN_DV].astype(BF16), sol[:, GDN_DV:].astype(BF16)
            ch["qe"] = (blk["q"] * ch["egc"]).astype(BF16)
            ch["kd"] = (blk["k"] * jnp.exp(ch["glast"] - ch["gc_col"])).astype(BF16)
            ch["eg"] = jnp.exp(ch["glast"])


def _gdn_scan(chains):
    c = GDN_CHUNK
    for ch in chains:
        ch["sb"] = ch["s"].astype(BF16)
    for ch in chains:
        r = _dot(jnp.concatenate([ch["wv"], ch["qe"]], axis=0), ch["sb"])
        ch["vn"] = (ch["u"].astype(F32) - r[:c]).astype(BF16)
        ch["qs"] = r[c:]
    outs = []
    for ch in chains:
        o = ch["qs"] + _dot(ch["intra"], ch["vn"])
        s_new = ch["s"] * ch["eg"] + _dot_tn(ch["kd"], ch["vn"])
        outs.append((o, s_new))
    return outs


def _gdn_kernel(qkv_ref, ab_ref, abt_ref, al_ref, dt_ref, alt_ref, dtt_ref, s0_ref, o_ref, sf_ref,
                u_s, wv_s, qe_s, kd_s, in_s, eg_s, gcol_s, grow_s, beta_s, st_s, *, t):
    c = GDN_CHUNK
    n = t // c
    nh = GDN_HEADS
    ab = ab_ref[0]
    gact = -jnp.exp(al_ref[...]) * jax.nn.softplus(ab + dt_ref[...])
    lane16 = lax.broadcasted_iota(jnp.int32, ab.shape, 1)
    beta_s[...] = jnp.where(lane16 < 2 * nh, gact, jax.nn.sigmoid(ab))
    r64 = lax.broadcasted_iota(jnp.int32, (c, c), 0)
    c64 = lax.broadcasted_iota(jnp.int32, (c, c), 1)
    tril = (r64 >= c64).astype(F32)
    triu = (r64 <= c64).astype(F32)
    lane_c = lax.broadcasted_iota(jnp.int32, (c, 16), 1)
    sub_c = lax.broadcasted_iota(jnp.int32, (16, c), 0)

    def cum_chunk(i):
        r0 = pl.multiple_of(i * c, c)
        g = beta_s[pl.ds(r0, c), :]
        gcol_s[pl.ds(r0, c), :] = jnp.where(lane_c < nh, _dot_exact(tril, g), _dot_exact(triu, g))
        gt = -jnp.exp(alt_ref[...]) * jax.nn.softplus(abt_ref[0, i] + dtt_ref[...])
        grow_s[i] = jnp.where(sub_c < nh, _dot_exact(gt, triu), _dot_exact(gt, tril))

    ncs = GDN_LOCAL_CHUNKS
    rb = ncs * c
    for sub in range(ncs):
        cum_chunk(sub)

    def local_body(jb, carry):
        r0 = pl.multiple_of(jb * rb, rb)
        xs = [qkv_ref[0, pl.ds(r0, rb), j * LANE:(j + 1) * LANE].astype(F32) for j in range(3 * nh)]
        gcol = gcol_s[pl.ds(r0, rb), :]
        bet = beta_s[pl.ds(r0, rb), :]
        blocks = []
        for sub in range(ncs):
            ci = ncs * jb + sub
            rows = slice(sub * c, (sub + 1) * c)
            grow = grow_s[ci]
            for hh in range(nh):
                dirs = []
                for d in range(2):
                    ch = d * nh + hh
                    last = sub * c + (c - 1 if d == 0 else 0)
                    dirs.append(dict(gc_col=gcol[rows, ch:ch + 1], gc_row=grow[ch:ch + 1, :],
                                     beta_col=bet[rows, 2 * nh + ch:2 * nh + ch + 1],
                                     glast=gcol[last:last + 1, ch:ch + 1], upper=(d == 1), ch=ch, ci=ci,
                                     r0=r0 + sub * c))
                blocks.append(dict(q=xs[hh][rows], k=xs[nh + hh][rows], v=xs[2 * nh + hh][rows], dirs=dirs))
        _gdn_local(blocks)
        for blk in blocks:
            for chn in blk["dirs"]:
                ch, rr = chn["ch"], pl.ds(pl.multiple_of(chn["r0"], c), c)
                u_s[ch, rr, :] = chn["u"]
                wv_s[ch, rr, :] = chn["wv"]
                qe_s[ch, rr, :] = chn["qe"]
                kd_s[ch, rr, :] = chn["kd"]
                in_s[ch, chn["ci"]] = chn["intra"]
                eg_s[chn["ci"], ch:ch + 1, :] = jnp.broadcast_to(chn["eg"], (1, LANE))
        nxt = jnp.minimum(jb + 1, n // ncs - 1) * ncs
        for sub in range(ncs):
            cum_chunk(nxt + sub)
        return carry

    lax.fori_loop(0, n // ncs, local_body, 0)

    for d in range(2):
        for hh in range(nh):
            st_s[d * nh + hh] = s0_ref[0, d, hh]
    o_ref[...] = jnp.zeros_like(o_ref)

    def scan_body(i, carry):
        chains = []
        for d in range(2):
            ci = i if d == 0 else n - 1 - i
            rr = pl.ds(pl.multiple_of(ci * c, c), c)
            eg = eg_s[ci]
            for hh in range(nh):
                ch = d * nh + hh
                chains.append(dict(u=u_s[ch, rr, :], wv=wv_s[ch, rr, :], qe=qe_s[ch, rr, :], kd=kd_s[ch, rr, :],
                                   intra=in_s[ch, ci], eg=eg[ch:ch + 1, :], s=st_s[ch], rr=rr, hh=hh, ch=ch))
        for chn, (o, s_new) in zip(chains, _gdn_scan(chains)):
            st_s[chn["ch"]] = s_new
            o_ref[0, chn["rr"], chn["hh"] * LANE:(chn["hh"] + 1) * LANE] += o
        return carry

    lax.fori_loop(0, n, scan_body, 0)
    for d in range(2):
        for hh in range(nh):
            sf_ref[0, d, hh] = st_s[d * nh + hh]


def _gdn(qkv, ab, abt, al, dt, alt, dtt, s0):
    b, t, _ = qkv.shape
    n = t // GDN_CHUNK
    per_b = lambda a: pl.BlockSpec((1,) + a.shape[1:], lambda i: (i,) + (0,) * (a.ndim - 1))
    full = lambda a: pl.BlockSpec(a.shape, lambda i: (0,) * a.ndim)
    return pl.pallas_call(
        functools.partial(_gdn_kernel, t=t), grid=(b,),
        in_specs=[per_b(qkv), per_b(ab), per_b(abt), full(al), full(dt), full(alt), full(dtt), per_b(s0)],
        out_specs=[pl.BlockSpec((1, t, GDN_HEADS * GDN_DV), lambda i: (i, 0, 0)), per_b(s0)],
        out_shape=[jax.ShapeDtypeStruct((b, t, GDN_HEADS * GDN_DV), F32), jax.ShapeDtypeStruct(s0.shape, F32)],
        scratch_shapes=[pltpu.VMEM((2 * GDN_HEADS, t, LANE), BF16), pltpu.VMEM((2 * GDN_HEADS, t, LANE), BF16),
                        pltpu.VMEM((2 * GDN_HEADS, t, LANE), BF16), pltpu.VMEM((2 * GDN_HEADS, t, LANE), BF16),
                        pltpu.VMEM((2 * GDN_HEADS, n, GDN_CHUNK, GDN_CHUNK), BF16),
                        pltpu.VMEM((n, 2 * GDN_HEADS, LANE), F32),
                        pltpu.VMEM((t, 16), F32), pltpu.VMEM((n, 16, GDN_CHUNK), F32), pltpu.VMEM((t, 16), F32),
                        pltpu.VMEM((2 * GDN_HEADS, GDN_DK, GDN_DV), F32)],
        compiler_params=_cparams(1), name="gdn",
    )(qkv, ab, abt, al, dt, alt, dtt, s0)


def _attd_kernel(*refs, n_ctx):
    if n_ctx:
        q_ref, k_ref, vt_ref, kc_ref, vct_ref, o_ref = refs
    else:
        q_ref, k_ref, vt_ref, o_ref = refs
    n_new = k_ref.shape[1]
    grp = ATT_HEADS // ATT_KV_HEADS
    chunks = []
    for c0, cn in _key_chunks(n_new):
        chunks.append((
            lambda g, c0=c0, cn=cn: k_ref[0, c0:c0 + cn, g * LANE:(g + 1) * LANE],
            lambda g, c0=c0, cn=cn: vt_ref[0, g * LANE:(g + 1) * LANE, c0:c0 + cn],
            None))
    if n_ctx:
        for c0, cn in _key_chunks(n_ctx):
            chunks.append((
                lambda g, c0=c0, cn=cn: kc_ref[0, c0:c0 + cn, g * LANE:(g + 1) * LANE].astype(BF16),
                lambda g, c0=c0, cn=cn: vct_ref[0, g * LANE:(g + 1) * LANE, c0:c0 + cn].astype(BF16),
                None))
    qs = [q_ref[0, :, h * LANE:(h + 1) * LANE] for h in range(ATT_HEADS)]
    outs = _attend_heads(qs, [h // grp for h in range(ATT_HEADS)], chunks, ATT_DH)
    o_ref[0] = jnp.concatenate(outs, axis=0).T.astype(BF16)


def _attd(q, k, vt, k_c, v_ct, tq):
    b, t, _ = q.shape
    n_ctx = 0 if k_c is None else k_c.shape[1]
    per_b = lambda a: pl.BlockSpec((1,) + a.shape[1:], lambda i, j: (i, 0, 0))
    in_specs = [pl.BlockSpec((1, tq, 512), lambda i, j: (i, j, 0)), per_b(k), per_b(vt)]
    args = [q, k, vt]
    if n_ctx:
        in_specs += [per_b(k_c), per_b(v_ct)]
        args += [k_c, v_ct]
    return pl.pallas_call(
        functools.partial(_attd_kernel, n_ctx=n_ctx),
        grid=(b, t // tq), in_specs=in_specs,
        out_specs=pl.BlockSpec((1, tq, 512), lambda i, j: (i, j, 0)),
        out_shape=jax.ShapeDtypeStruct((b, t, 512), BF16),
        compiler_params=_cparams(2), name="attd_dec" if n_ctx else "attd_ctx",
    )(*args)


def _outproj1_kernel(oc_ref, zc_ref, od_ref, zd_ref, x_ref, mod_ref, gn_ref, w_ref, lnf_ref, y_ref):
    parts = []
    for j in range(GDN_HEADS):
        parts.append(_rms_rows(oc_ref[0, :, j * LANE:(j + 1) * LANE], gn_ref[...]).astype(BF16))
    gc = _silu_gate(jnp.concatenate(parts, axis=1), zc_ref[0])
    gd = _silu_gate(od_ref[0], zd_ref[0])
    y = _dot(gc, w_ref[0:512, :]) + _dot(gd, w_ref[512:1024, :])
    x2 = x_ref[0] + mod_ref[0, 2:3, :] * y
    y_ref[0] = _rms_rows(x2, lnf_ref[...])


def _outproj1(oc, zc, od, zd, x, mod, gn, w, lnf):
    b, t, d = x.shape
    tr = _row_tile(t)
    bm = mod.shape[0]
    rows = lambda c: pl.BlockSpec((1, tr, c), lambda i, j: (i, j, 0))
    full = lambda a: pl.BlockSpec(a.shape, lambda i, j: (0,) * a.ndim)
    return pl.pallas_call(
        _outproj1_kernel, grid=(b, t // tr),
        in_specs=[rows(512), rows(512), rows(512), rows(512), rows(d),
                  pl.BlockSpec((1, 3, d), (lambda i, j: (i, 0, 0)) if bm > 1 else (lambda i, j: (0, 0, 0))),
                  full(gn), full(w), full(lnf)],
        out_specs=rows(d), out_shape=jax.ShapeDtypeStruct((b, t, d), F32),
        compiler_params=_cparams(2), name="outproj1",
    )(oc, zc, od, zd, x, mod, gn, w, lnf)


def _rope_table(n_tok, rot_dim):
    quarter = rot_dim // 4
    inv = np.float32(ROPE_THETA) ** (-np.arange(quarter, dtype=np.float32) / np.float32(quarter))
    tt = np.arange(n_tok)
    pos = np.stack([tt // GRID_W, tt % GRID_W], axis=-1).astype(np.float32)
    ang = (pos[:, :, None] * inv).astype(np.float32)
    cos, sin = np.cos(ang), np.sin(ang)
    c = np.concatenate([cos, cos], axis=-1).reshape(n_tok, rot_dim)
    s = np.concatenate([-sin, sin], axis=-1).reshape(n_tok, rot_dim)
    return c.astype(np.float32), s.astype(np.float32)


def _place(tab, fill, off, width):
    out = np.full((tab.shape[0], width), fill, np.float32)
    out[:, off:off + tab.shape[1]] = tab
    return out


def _pack_kernel(w_ref, o_ref, *, segs):
    parts = []
    for start, width in segs:
        if start is None:
            parts.append(jnp.zeros((w_ref.shape[0], width), F32))
        else:
            parts.append(w_ref[:, start:start + width])
    o_ref[...] = jnp.concatenate(parts, axis=1).astype(BF16)


def _pack_cols(w, segs):
    d, n_in = w.shape
    n_out = sum(width for _, width in segs)
    tr = 256
    return pl.pallas_call(
        functools.partial(_pack_kernel, segs=segs), grid=(d // tr,),
        in_specs=[pl.BlockSpec((tr, n_in), lambda i: (i, 0))],
        out_specs=pl.BlockSpec((tr, n_out), lambda i: (i, 0)),
        out_shape=jax.ShapeDtypeStruct((d, n_out), BF16),
        compiler_params=_cparams(1), name="pack_cols",
    )(w)


def _prep_l0(w_in0, w_uq, w_ukv):
    w = _pack_cols(w_in0, [(0, 640), (None, 64), (640, 32), (None, 32), (672, 1792)])
    uq = w_uq.reshape(MLA_Q_LORA, MLA_HEADS, MLA_NOPE + MLA_ROPE)
    wuq = jnp.pad(uq, ((0, 0), (0, 0), (0, LANE - MLA_NOPE - MLA_ROPE))).reshape(MLA_Q_LORA, MLA_HEADS * LANE)
    ukv = w_ukv.reshape(MLA_KV_LORA, MLA_HEADS, MLA_NOPE + MLA_V)
    wuk = jnp.pad(ukv[:, :, :MLA_NOPE], ((0, 0), (0, 0), (0, LANE - MLA_NOPE))).reshape(MLA_KV_LORA, MLA_HEADS * LANE)
    wuvt = ukv[:, :, MLA_NOPE:].reshape(MLA_KV_LORA, MLA_HEADS * MLA_V).T
    return w, wuq.astype(BF16), wuk.astype(BF16), wuvt.astype(BF16)


def _prep_l1(w_in1):
    return _pack_cols(w_in1, [(0, 1536), (1552, 2048), (1536, 16), (None, 112)])


def _chunk_rows(ab):
    b, t, c = ab.shape
    return jnp.swapaxes(ab.reshape(b, t // GDN_CHUNK, GDN_CHUNK, c), 2, 3)


def _trunk(x, mod, caches, p, tables, tq):
    dec = caches is not None
    t0m, t0s, t1 = tables if dec else (None, None, None)
    (qa, ckv, kr, za, qb, kb, vbt, zb, *ctx0) = _inproj0(
        x, mod[0], p["ln0"], p["w0"], p["qn"], p["wuq"], p["kvn"], (t0m + t0s) if dec else None)
    if dec:
        ckv_c, kr_c, kb_c, vb_ct, s0, kd_c, vd_ct = caches
    else:
        ckv_c = kr_c = kb_c = vb_ct = kd_c = vd_ct = None
        s0 = jnp.zeros((x.shape[0], 2, GDN_HEADS, GDN_DK, GDN_DV), F32)
    oa = _mla(qa, ckv, kr, ckv_c, kr_c, p["wuk"], p["wuvt"], tq)
    ob = _swa(p["sink"], qb, kb, vbt, kb_c, vb_ct, tq)
    x1 = _outproj0(oa, za, ob, zb, x, mod[0], p["wout0"])
    (qkv, zc, qd, kd, vdt, zd, ab, *ctx1) = _inproj1(
        x1, mod[1], p["ln1"], p["w1"], p["cw"], p["aqn"], p["akn"], t1 if dec else None)
    oc, sfin = _gdn(qkv, ab, _chunk_rows(ab), p["al"], p["dt"], p["alt"], p["dtt"], s0)
    od = _attd(qd, kd, vdt, kd_c, vd_ct, tq)
    y = _outproj1(oc, zc, od, zd, x1, mod[1], p["gn"], p["wout1"], p["lnf"])
    return y, ctx0, sfin, ctx1


def kernel(x_prompt, x_sample, cache_l0_mla_ckv, cache_l0_mla_krope, cache_l0_swa_k, cache_l0_swa_v,
           state_l1_gdn, cache_l1_attn_k, cache_l1_attn_v, c, c_ctx,
           w_mod0, b_mod0, ln0, w_in0, mla_q_norm, w_uq, mla_kv_norm, w_ukv, swa_sink, w_out0,
           w_mod1, b_mod1, ln1, w_in1, gdn_conv, gdn_a_log, gdn_dt_bias, gdn_norm, att_q_norm, att_k_norm, w_out1,
           ln_f):
    d = x_prompt.shape[-1]
    bd, td = x_sample.shape[:2]
    bc, tc = x_prompt.shape[:2]
    past = cache_l0_mla_ckv.shape[1]
    row = lambda v: v.reshape(1, -1)
    w0, wuq, wuk, wuvt = _prep_l0(w_in0, w_uq, w_ukv)
    w1 = _prep_l1(w_in1)
    al8 = gdn_a_log.reshape(1, 2 * GDN_HEADS)
    dt8 = gdn_dt_bias.reshape(1, 2 * GDN_HEADS)
    al16 = jnp.pad(al8, ((0, 0), (0, 8)))
    dt16 = jnp.pad(dt8, ((0, 0), (0, 8)))
    p = dict(ln0=row(ln0), w0=w0, qn=row(mla_q_norm), wuq=wuq, kvn=row(mla_kv_norm), wuk=wuk, wuvt=wuvt,
             sink=swa_sink, wout0=w_out0.astype(BF16), ln1=row(ln1), w1=w1, aqn=row(att_q_norm),
             akn=row(att_k_norm), cw=gdn_conv, al=al16, dt=dt16, alt=al16.T, dtt=dt16.T, gn=row(gdn_norm),
             wout1=w_out1.astype(BF16), lnf=row(ln_f))
    n_rows = -(-(bd + 1) // 8) * 8
    c_rows = jnp.concatenate([c, c_ctx[None, :], jnp.zeros((n_rows - bd - 1, d), F32)], axis=0)
    mods = [_mod(c_rows, w_mod0, b_mod0), _mod(c_rows, w_mod1, b_mod1)]
    mod_dec = [m[:bd].reshape(bd, 3, d) for m in mods]
    mod_ctx = [m[bd:bd + 1].reshape(1, 3, d) for m in mods]
    cm, sm = _rope_table(td, MLA_ROPE)
    t0m = (jnp.asarray(_place(cm, 1.0, MLA_NOPE, LANE)), jnp.asarray(_place(sm, 0.0, MLA_NOPE, LANE)))
    cs, ss = _rope_table(td, SWA_DH)
    t0s = (jnp.asarray(np.tile(cs, (1, LANE // SWA_DH))), jnp.asarray(np.tile(ss, (1, LANE // SWA_DH))))
    t1 = tuple(jnp.asarray(a) for a in _rope_table(td, ATT_DH))
    caches = (cache_l0_mla_ckv,
              jnp.pad(cache_l0_mla_krope, ((0, 0), (0, 0), (MLA_NOPE, LANE - MLA_NOPE - MLA_ROPE))),
              cache_l0_swa_k.reshape(bd, past, SWA_KV_HEADS * SWA_DH),
              jnp.swapaxes(cache_l0_swa_v.reshape(bd, past, SWA_KV_HEADS * SWA_DH), 1, 2),
              state_l1_gdn,
              cache_l1_attn_k.reshape(bd, past, ATT_KV_HEADS * ATT_DH),
              jnp.swapaxes(cache_l1_attn_v.reshape(bd, past, ATT_KV_HEADS * ATT_DH), 1, 2))
    y_prompt, ctx0, sfin, ctx1 = _trunk(x_prompt, mod_ctx, None, p, None, tq=tc)
    y_sample, _, _, _ = _trunk(x_sample, mod_dec, caches, p, (t0m, t0s, t1), tq=256)
    ckv32, kr32, kb32, vb32 = ctx0
    kd32, vd32 = ctx1
    return (y_prompt, y_sample, ckv32, kr32,
            kb32.reshape(bc, tc, SWA_KV_HEADS, SWA_DH), vb32.reshape(bc, tc, SWA_KV_HEADS, SWA_DH),
            sfin, kd32.reshape(bc, tc, ATT_KV_HEADS, ATT_DH), vd32.reshape(bc, tc, ATT_KV_HEADS, ATT_DH))
```

```python
import functools
import math

import jax
import jax.numpy as jnp
import numpy as np
from jax import lax
from jax.experimental import pallas as pl
from jax.experimental.pallas import tpu as pltpu

F32 = jnp.float32
BF16 = jnp.bfloat16

GRID_W = 64
ROPE_THETA = 10000.0
EPS = 1e-6
WINDOW = 128
MLA_HEADS, MLA_NOPE, MLA_ROPE, MLA_V = 8, 64, 32, 64
MLA_Q_LORA, MLA_KV_LORA = 384, 256
SWA_HEADS, SWA_KV_HEADS, SWA_DH = 8, 2, 64
GDN_HEADS, GDN_DK, GDN_DV, CONV_K, GDN_CHUNK = 4, 128, 128, 3, 64
GDN_LOCAL_CHUNKS = 4
ATT_HEADS, ATT_KV_HEADS, ATT_DH = 4, 2, 128
LANE = 128
LOG2E = math.log2(math.e)
NEG = -1e30
VMEM_LIMIT = 56 * 1024 * 1024


def _cparams(n_axes):
    return pltpu.CompilerParams(dimension_semantics=("arbitrary",) * n_axes, vmem_limit_bytes=VMEM_LIMIT)


def _dot(a, b):
    return jnp.dot(a, b, preferred_element_type=F32)


def _dot_nt(a, b):
    return lax.dot_general(a, b, (((1,), (1,)), ((), ())), preferred_element_type=F32)


def _dot_tn(a, b):
    return lax.dot_general(a, b, (((0,), (0,)), ((), ())), preferred_element_type=F32)


def _dot_exact(a, b):
    return jnp.dot(a, b, preferred_element_type=F32, precision=lax.Precision.HIGHEST)


def _split(a):
    hi = a.astype(BF16)
    return hi, (a - hi.astype(F32)).astype(BF16)


def _silu(x):
    return x * jax.nn.sigmoid(x)


def _rms_rows(x, g):
    return x * lax.rsqrt(jnp.mean(x * x, axis=-1, keepdims=True) + EPS) * g


def _rope_block(x, cos, sin, half):
    lane = lax.broadcasted_iota(jnp.int32, x.shape, 1)
    first = (lane // half) % 2 == 0
    partner = jnp.where(first, pltpu.roll(x, LANE - half, 1), pltpu.roll(x, half, 1))
    return x * cos + partner * sin


def _mod_kernel(c_ref, w_ref, b_ref, o_ref):
    a = _silu(c_ref[...]).astype(BF16)
    o_ref[...] = _dot(a, w_ref[...].astype(BF16)) + b_ref[...]


def _mod(c_rows, w_mod, b_mod):
    r, d = c_rows.shape
    n = w_mod.shape[1]
    tn = 1024
    return pl.pallas_call(
        _mod_kernel,
        grid=(n // tn,),
        in_specs=[pl.BlockSpec((r, d), lambda j: (0, 0)),
                  pl.BlockSpec((d, tn), lambda j: (0, j)),
                  pl.BlockSpec((1, tn), lambda j: (0, j))],
        out_specs=pl.BlockSpec((r, tn), lambda j: (0, j)),
        out_shape=jax.ShapeDtypeStruct((r, n), F32),
        compiler_params=_cparams(1),
        name="mod",
    )(c_rows, w_mod, b_mod.reshape(1, n))


def _adaln(x, mod_ref, ln_ref):
    h = _rms_rows(x, ln_ref[...])
    return h * (1.0 + mod_ref[0, 1:2, :]) + mod_ref[0, 0:1, :]


L0_OFF = dict(cq=0, ckv=384, kr=640, za=768, qb=1280, kb=1792, vb=1920, zb=2048)
L0_W = 2560


def _inproj0_kernel(*refs, rope):
    if rope:
        (x_ref, mod_ref, ln_ref, w_ref, qn_ref, wuq_ref, kvn_ref, cm_ref, sm_ref, cs_ref, ss_ref,
         qa_ref, ckv_ref, kr_ref, za_ref, qb_ref, kb_ref, vbt_ref, zb_ref) = refs
    else:
        (x_ref, mod_ref, ln_ref, w_ref, qn_ref, wuq_ref, kvn_ref,
         qa_ref, ckv_ref, kr_ref, za_ref, qb_ref, kb_ref, vbt_ref, zb_ref,
         ckv32_ref, kr32_ref, kb32_ref, vb32_ref) = refs
    h = _adaln(x_ref[0], mod_ref, ln_ref).astype(BF16)
    u = _dot(h, w_ref[...])
    o = L0_OFF
    cq = _rms_rows(u[:, o["cq"]:o["cq"] + 384], qn_ref[...]).astype(BF16)
    qa = _dot(cq, wuq_ref[...])
    ckv = _rms_rows(u[:, o["ckv"]:o["ckv"] + 256], kvn_ref[...])
    kr = u[:, o["kr"]:o["kr"] + 128]
    qb = u[:, o["qb"]:o["qb"] + 512]
    kb = u[:, o["kb"]:o["kb"] + 128]
    vb = u[:, o["vb"]:o["vb"] + 128]
    if not rope:
        ckv32_ref[0] = ckv
        kr32_ref[0] = kr[:, 64:96]
        kb32_ref[0] = kb
        vb32_ref[0] = vb
    qa_scale = (MLA_NOPE + MLA_ROPE) ** -0.5 * LOG2E
    qb_scale = SWA_DH ** -0.5 * LOG2E
    for j in range(MLA_HEADS):
        blk = qa[:, j * LANE:(j + 1) * LANE]
        if rope:
            blk = _rope_block(blk, cm_ref[...], sm_ref[...], MLA_ROPE // 4)
        qa_ref[0, :, j * LANE:(j + 1) * LANE] = (blk * qa_scale).astype(BF16)
    for j in range(SWA_HEADS * SWA_DH // LANE):
        blk = qb[:, j * LANE:(j + 1) * LANE]
        if rope:
            blk = _rope_block(blk, cs_ref[...], ss_ref[...], SWA_DH // 4)
        qb_ref[0, :, j * LANE:(j + 1) * LANE] = (blk * qb_scale).astype(BF16)
    if rope:
        kr = _rope_block(kr, cm_ref[...], sm_ref[...], MLA_ROPE // 4)
        kb = _rope_block(kb, cs_ref[...], ss_ref[...], SWA_DH // 4)
    ckv_ref[0] = ckv.astype(BF16)
    kr_ref[0] = kr.astype(BF16)
    kb_ref[0] = kb.astype(BF16)
    za_ref[0] = u[:, o["za"]:o["za"] + 512].astype(BF16)
    zb_ref[0] = u[:, o["zb"]:o["zb"] + 512].astype(BF16)
    vbt_ref[0] = vb.T.astype(BF16)


def _row_tile(t):
    return 512 if t % 512 == 0 else 256


def _inproj0(x, mod, ln, w, qn, wuq, kvn, tables):
    b, t, d = x.shape
    tr = _row_tile(t)
    rope = tables is not None
    bm = mod.shape[0]
    full = lambda a: pl.BlockSpec(a.shape, lambda i, j: (0,) * a.ndim)
    rows = lambda c: pl.BlockSpec((1, tr, c), lambda i, j: (i, j, 0))
    in_specs = [rows(d), pl.BlockSpec((1, 3, d), (lambda i, j: (i, 0, 0)) if bm > 1 else (lambda i, j: (0, 0, 0))),
                full(ln), full(w), full(qn), full(wuq), full(kvn)]
    args = [x, mod, ln, w, qn, wuq, kvn]
    if rope:
        for tab in tables:
            in_specs.append(pl.BlockSpec((tr, LANE), lambda i, j: (j, 0)))
            args.append(tab)
    out_shape = [jax.ShapeDtypeStruct((b, t, 1024), BF16), jax.ShapeDtypeStruct((b, t, 256), BF16),
                 jax.ShapeDtypeStruct((b, t, 128), BF16), jax.ShapeDtypeStruct((b, t, 512), BF16),
                 jax.ShapeDtypeStruct((b, t, 512), BF16), jax.ShapeDtypeStruct((b, t, 128), BF16),
                 jax.ShapeDtypeStruct((b, 128, t), BF16), jax.ShapeDtypeStruct((b, t, 512), BF16)]
    out_specs = [rows(1024), rows(256), rows(128), rows(512), rows(512), rows(128),
                 pl.BlockSpec((1, 128, tr), lambda i, j: (i, 0, j)), rows(512)]
    if not rope:
        out_shape += [jax.ShapeDtypeStruct((b, t, 256), F32), jax.ShapeDtypeStruct((b, t, 32), F32),
                      jax.ShapeDtypeStruct((b, t, 128), F32), jax.ShapeDtypeStruct((b, t, 128), F32)]
        out_specs += [rows(256), rows(32), rows(128), rows(128)]
    return pl.pallas_call(
        functools.partial(_inproj0_kernel, rope=rope),
        grid=(b, t // tr), in_specs=in_specs, out_specs=out_specs, out_shape=out_shape,
        compiler_params=_cparams(2), name="inproj0_dec" if rope else "inproj0_ctx",
    )(*args)


KEY_CHUNK = 512
ATT_LOOKAHEAD = 4


SUM_ROWS = 16


def _attend_heads(qs, kv_of, chunks, dv, sinks=None):
    nh = len(qs)
    tq = qs[0].shape[0]
    m = [None] * nh
    acc = [None] * nh
    items = [(ci, i) for ci in range(len(chunks)) for i in range(nh)]
    loaded = {}

    def kv(ci, src):
        if (ci, src) not in loaded:
            vt = chunks[ci][1](src)
            ones = (lax.broadcasted_iota(jnp.int32, (SUM_ROWS, vt.shape[1]), 0) == 0).astype(BF16)
            loaded[(ci, src)] = (chunks[ci][0](src), jnp.concatenate([vt, ones], axis=0))
        return loaded[(ci, src)]

    if sinks is not None:
        m = [jnp.full((1, tq), sk, F32) for sk in sinks]
        unit = (lax.broadcasted_iota(jnp.int32, (dv + SUM_ROWS, tq), 0) == dv).astype(F32)
        acc = [unit for _ in sinks]

    scores = {}
    for t in range(len(items) + ATT_LOOKAHEAD):
        if t < len(items):
            ci, i = items[t]
            scores[t] = _dot_nt(kv(ci, kv_of[i])[0], qs[i])
        t0 = t - ATT_LOOKAHEAD
        if t0 < 0:
            continue
        ci, i = items[t0]
        mask = chunks[ci][2]
        si = scores.pop(t0)
        if mask is not None:
            si = jnp.where(mask, si, NEG)
        cm = si.max(axis=0, keepdims=True)
        alpha = None
        if m[i] is None:
            m_new = cm
        else:
            m_new = jnp.maximum(m[i], cm)
            alpha = jnp.exp2(m[i] - m_new)
        p = jnp.exp2(si - m_new)
        m[i] = m_new
        pv = _dot(kv(ci, kv_of[i])[1], p.astype(BF16))
        acc[i] = pv if acc[i] is None else acc[i] * alpha + pv
    return [acc[i][:dv] * (1.0 / acc[i][dv:dv + 1]) for i in range(nh)]


def _key_chunks(n):
    step = KEY_CHUNK if n % KEY_CHUNK == 0 else n
    return [(c0, step) for c0 in range(0, n, step)]


def _mla_kernel(*refs, n_new, n_ctx, hp):
    if n_ctx:
        q_ref, ckv_ref, kr_ref, ckvc_ref, krc_ref, wuk_ref, wuvt_ref, o_ref, k_s, vt_s = refs
    else:
        q_ref, ckv_ref, kr_ref, wuk_ref, wuvt_ref, o_ref, k_s, vt_s = refs
    qi, gi = pl.program_id(1), pl.program_id(2)

    @pl.when((qi == 0) & (gi == 0))
    def _():
        def expand(ckv, kr, r0, n):
            kn = _dot(ckv, wuk_ref[...])
            for j in range(MLA_HEADS):
                k_s[j, r0:r0 + n, :] = (kn[:, j * LANE:(j + 1) * LANE] + kr).astype(BF16)
            vt_s[:, r0:r0 + n] = _dot_nt(wuvt_ref[...], ckv).astype(BF16)

        blk = 512 if n_new % 512 == 0 else 256
        for r0 in range(0, n_new, blk):
            expand(ckv_ref[0, r0:r0 + blk, :], kr_ref[0, r0:r0 + blk, :].astype(F32), r0, blk)
        if n_ctx:
            expand(ckvc_ref[0].astype(BF16), krc_ref[0], n_new, n_ctx)

    qs = [q_ref[0, :, j * LANE:(j + 1) * LANE] for j in range(hp)]
    chunks = []
    for c0, cn in _key_chunks(n_new + n_ctx):
        chunks.append((
            lambda j, c0=c0, cn=cn: k_s[gi * hp + j, c0:c0 + cn, :],
            lambda j, c0=c0, cn=cn: vt_s[pl.ds(pl.multiple_of((gi * hp + j) * MLA_V, MLA_V), MLA_V), c0:c0 + cn],
            None))
    outs = _attend_heads(qs, list(range(hp)), chunks, MLA_V)
    o_ref[0] = jnp.concatenate(outs, axis=0).T.astype(BF16)


def _mla(q, ckv, kr, ckv_c, kr_c, wuk, wuvt, tq):
    b, t, _ = q.shape
    n_ctx = 0 if ckv_c is None else ckv_c.shape[1]
    hp = 8
    tk = t + n_ctx
    rows_q = pl.BlockSpec((1, tq, hp * LANE), lambda i, j, g: (i, j, g))
    per_b = lambda a: pl.BlockSpec((1,) + a.shape[1:], lambda i, j, g: (i, 0, 0))
    full = lambda a: pl.BlockSpec(a.shape, lambda i, j, g: (0, 0))
    in_specs = [rows_q, per_b(ckv), per_b(kr)]
    args = [q, ckv, kr]
    if n_ctx:
        in_specs += [per_b(ckv_c), per_b(kr_c)]
        args += [ckv_c, kr_c]
    in_specs += [full(wuk), full(wuvt)]
    args += [wuk, wuvt]
    return pl.pallas_call(
        functools.partial(_mla_kernel, n_new=t, n_ctx=n_ctx, hp=hp),
        grid=(b, t // tq, MLA_HEADS // hp), in_specs=in_specs,
        out_specs=pl.BlockSpec((1, tq, hp * MLA_V), lambda i, j, g: (i, j, g)),
        out_shape=jax.ShapeDtypeStruct((b, t, MLA_HEADS * MLA_V), BF16),
        scratch_shapes=[pltpu.VMEM((MLA_HEADS, tk, LANE), BF16), pltpu.VMEM((MLA_HEADS * MLA_V, tk), BF16)],
        compiler_params=_cparams(3), name="mla_dec" if n_ctx else "mla_ctx",
    )(*args)


def _swa_kernel(*refs, n_new, n_ctx, tq):
    if n_ctx:
        sink_ref, q_ref, k_ref, vt_ref, kc_ref, vct_ref, o_ref = refs
    else:
        sink_ref, q_ref, k_ref, vt_ref, o_ref = refs
    qi = pl.program_id(1)
    grp = SWA_HEADS // SWA_KV_HEADS
    if n_ctx:
        span = tq + 2 * WINDOW
        q0 = qi * tq
        start = pl.multiple_of(jnp.clip(q0 - WINDOW, 0, n_new - span), LANE)
        kpos = start + lax.broadcasted_iota(jnp.int32, (span, tq), 0)
        qpos = q0 + lax.broadcasted_iota(jnp.int32, (span, tq), 1)
        band = jnp.abs(kpos - qpos) <= WINDOW
    dh = SWA_DH
    chunks = []
    if n_ctx:
        for c0, cn in _key_chunks(span):
            chunks.append((
                lambda g, c0=c0, cn=cn: k_ref[0, pl.ds(start + c0, cn), g * dh:(g + 1) * dh],
                lambda g, c0=c0, cn=cn: vt_ref[0, g * dh:(g + 1) * dh, pl.ds(start + c0, cn)],
                band[c0:c0 + cn]))
        for c0, cn in _key_chunks(n_ctx):
            chunks.append((
                lambda g, c0=c0, cn=cn: kc_ref[0, c0:c0 + cn, g * dh:(g + 1) * dh].astype(BF16),
                lambda g, c0=c0, cn=cn: vct_ref[0, g * dh:(g + 1) * dh, c0:c0 + cn].astype(BF16),
                None))
    else:
        for c0, cn in _key_chunks(n_new):
            chunks.append((
                lambda g, c0=c0, cn=cn: k_ref[0, c0:c0 + cn, g * dh:(g + 1) * dh],
                lambda g, c0=c0, cn=cn: vt_ref[0, g * dh:(g + 1) * dh, c0:c0 + cn],
                None))
    qs = [q_ref[0, :, h * dh:(h + 1) * dh] for h in range(SWA_HEADS)]
    sinks = [sink_ref[h] * LOG2E for h in range(SWA_HEADS)]
    outs = _attend_heads(qs, [h // grp for h in range(SWA_HEADS)], chunks, SWA_DH, sinks)
    o_ref[0] = jnp.concatenate(outs, axis=0).T.astype(BF16)


def _swa(sink, q, k, vt, k_c, v_ct, tq):
    b, t, _ = q.shape
    n_ctx = 0 if k_c is None else k_c.shape[1]
    per_b = lambda a: pl.BlockSpec((1,) + a.shape[1:], lambda i, j: (i, 0, 0))
    in_specs = [pl.BlockSpec(memory_space=pltpu.SMEM), pl.BlockSpec((1, tq, 512), lambda i, j: (i, j, 0)),
                per_b(k), per_b(vt)]
    args = [sink, q, k, vt]
    if n_ctx:
        in_specs += [per_b(k_c), per_b(v_ct)]
        args += [k_c, v_ct]
    return pl.pallas_call(
        functools.partial(_swa_kernel, n_new=t, n_ctx=n_ctx, tq=tq),
        grid=(b, t // tq), in_specs=in_specs,
        out_specs=pl.BlockSpec((1, tq, 512), lambda i, j: (i, j, 0)),
        out_shape=jax.ShapeDtypeStruct((b, t, 512), BF16),
        compiler_params=_cparams(2), name="swa_dec" if n_ctx else "swa_ctx",
    )(*args)


def _silu_gate(o, z):
    return o * (z / (1.0 + jnp.exp(-z)))


L1_OFF = dict(qkv=0, zc=1536, qd=2048, kd=2560, vd=2816, zd=3072, ab=3584)
L1_W = 3712


def _inproj1_kernel(*refs, rope):
    if rope:
        (oa_ref, za_ref, ob_ref, zb_ref, x_ref, *halo_refs, mod0_ref, wo_ref, mod_ref, ln_ref, w_ref, cw_ref,
         qn_ref, kn_ref, c_ref, s_ref,
         x1_ref, qkv_ref, zc_ref, qd_ref, kd_ref, vdt_ref, zd_ref, ab_ref) = refs
    else:
        (oa_ref, za_ref, ob_ref, zb_ref, x_ref, *halo_refs, mod0_ref, wo_ref, mod_ref, ln_ref, w_ref, cw_ref,
         qn_ref, kn_ref,
         x1_ref, qkv_ref, zc_ref, qd_ref, kd_ref, vdt_ref, zd_ref, ab_ref, kd32_ref, vd32_ref) = refs

    def residual(oa, za, ob, zb, x):
        y0 = _dot(_silu_gate(oa, za), wo_ref[0:512, :]) + _dot(_silu_gate(ob, zb), wo_ref[512:1024, :])
        return x + mod0_ref[0, 2:3, :] * y0

    x1 = residual(oa_ref[0], za_ref[0], ob_ref[0], zb_ref[0], x_ref[0])
    x1_ref[0] = x1
    h = _adaln(x1, mod_ref, ln_ref).astype(BF16)
    u = _dot(h, w_ref[...])
    o = L1_OFF
    tr = u.shape[0]
    nq = GDN_HEADS * GDN_DK
    n_qkv = 2 * nq + GDN_HEADS * GDN_DV
    hp, hn = halo_refs[:5], halo_refs[5:]
    xh = residual(*[jnp.concatenate([a[0], b[0]], axis=0) for a, b in zip(hp, hn)])
    uh = _dot(_adaln(xh, mod_ref, ln_ref).astype(BF16), w_ref[:, 0:n_qkv])
    jrow = pl.program_id(1)
    prv_all = jnp.where(jrow > 0, uh[HALO - 1:HALO], 0.0)
    nxt_all = jnp.where(jrow < pl.num_programs(1) - 1, uh[HALO:HALO + 1], 0.0)
    rowi = lax.broadcasted_iota(jnp.int32, (tr, LANE), 0)
    for j in range(n_qkv // LANE):
        cols = slice(j * LANE, (j + 1) * LANE)
        xj = u[:, cols]
        xp = jnp.where(rowi == 0, prv_all[:, cols], pltpu.roll(xj, 1, 0))
        xn = jnp.where(rowi == tr - 1, nxt_all[:, cols], pltpu.roll(xj, tr - 1, 0))
        cw = cw_ref[:, cols]
        y = _silu(xp * cw[0:1, :] + xj * cw[1:2, :] + xn * cw[2:3, :])
        if j < 2 * nq // LANE:
            y = y * lax.rsqrt(jnp.sum(y * y, axis=-1, keepdims=True) + EPS)
        if j < nq // LANE:
            y = y * GDN_DK ** -0.5
        qkv_ref[0, :, cols] = y.astype(BF16)
    zc_ref[0] = u[:, o["zc"]:o["zc"] + 512].astype(BF16)
    zd_ref[0] = u[:, o["zd"]:o["zd"] + 512].astype(BF16)
    ab_ref[0] = u[:, o["ab"]:o["ab"] + 16]
    qd_scale = ATT_DH ** -0.5 * LOG2E
    for j in range(ATT_HEADS):
        blk = _rms_rows(u[:, o["qd"] + j * LANE:o["qd"] + (j + 1) * LANE], qn_ref[...])
        if rope:
            blk = _rope_block(blk, c_ref[...], s_ref[...], ATT_DH // 4)
        qd_ref[0, :, j * LANE:(j + 1) * LANE] = (blk * qd_scale).astype(BF16)
    for j in range(ATT_KV_HEADS):
        blk = _rms_rows(u[:, o["kd"] + j * LANE:o["kd"] + (j + 1) * LANE], kn_ref[...])
        if rope:
            blk = _rope_block(blk, c_ref[...], s_ref[...], ATT_DH // 4)
        else:
            kd32_ref[0, :, j * LANE:(j + 1) * LANE] = blk
        kd_ref[0, :, j * LANE:(j + 1) * LANE] = blk.astype(BF16)
    vd = u[:, o["vd"]:o["vd"] + 256]
    if not rope:
        vd32_ref[0] = vd
    vdt_ref[0] = vd.T.astype(BF16)


HALO = 16


def _inproj1(oa, za, ob, zb, x, mod0, wo, mod, ln, w, cw, qn, kn, tables):
    b, t, d = x.shape
    tr = _row_tile(t)
    rope = tables is not None
    bm = mod.shape[0]
    full = lambda a: pl.BlockSpec(a.shape, lambda i, j: (0,) * a.ndim)
    rows = lambda c: pl.BlockSpec((1, tr, c), lambda i, j: (i, j, 0))
    gpb = tr // HALO
    prev = lambda c: pl.BlockSpec((1, HALO, c), lambda i, j: (i, jnp.maximum(j * gpb - 1, 0), 0))
    nxt = lambda c: pl.BlockSpec((1, HALO, c), lambda i, j: (i, jnp.minimum((j + 1) * gpb, t // HALO - 1), 0))
    mod_spec = pl.BlockSpec((1, 3, d), (lambda i, j: (i, 0, 0)) if bm > 1 else (lambda i, j: (0, 0, 0)))
    streams = [oa, za, ob, zb, x]
    widths = [512, 512, 512, 512, d]
    in_specs = ([rows(c) for c in widths] + [prev(c) for c in widths] + [nxt(c) for c in widths]
                + [mod_spec, full(wo), mod_spec, full(ln), full(w), full(cw), full(qn), full(kn)])
    args = streams * 3 + [mod0, wo, mod, ln, w, cw, qn, kn]
    if rope:
        for tab in tables:
            in_specs.append(pl.BlockSpec((tr, LANE), lambda i, j: (j, 0)))
            args.append(tab)
    out_shape = [jax.ShapeDtypeStruct((b, t, d), F32),
                 jax.ShapeDtypeStruct((b, t, 1536), BF16), jax.ShapeDtypeStruct((b, t, 512), BF16),
                 jax.ShapeDtypeStruct((b, t, 512), BF16), jax.ShapeDtypeStruct((b, t, 256), BF16),
                 jax.ShapeDtypeStruct((b, 256, t), BF16), jax.ShapeDtypeStruct((b, t, 512), BF16),
                 jax.ShapeDtypeStruct((b, t, 16), F32)]
    out_specs = [rows(d), rows(1536), rows(512), rows(512), rows(256),
                 pl.BlockSpec((1, 256, tr), lambda i, j: (i, 0, j)), rows(512), rows(16)]
    if not rope:
        out_shape += [jax.ShapeDtypeStruct((b, t, 256), F32), jax.ShapeDtypeStruct((b, t, 256), F32)]
        out_specs += [rows(256), rows(256)]
    return pl.pallas_call(
        functools.partial(_inproj1_kernel, rope=rope),
        grid=(b, t // tr), in_specs=in_specs, out_specs=out_specs, out_shape=out_shape,
        compiler_params=_cparams(2), name="inproj1_dec" if rope else "inproj1_ctx",
    )(*args)


def _gdn_local(blocks):
    c = GDN_CHUNK
    row = lax.broadcasted_iota(jnp.int32, (c, c), 0)
    col = lax.broadcasted_iota(jnp.int32, (c, c), 1)
    lane2 = lax.broadcasted_iota(jnp.int32, (c, 2 * c), 1)
    eye = (row == col).astype(F32)
    eye_t = jnp.concatenate([eye, jnp.zeros((c, c), F32)], axis=1).astype(BF16)
    chains = [ch for blk in blocks for ch in blk["dirs"]]
    for blk in blocks:
        for ch in blk["dirs"]:
            causal = (row <= col) if ch["upper"] else (row >= col)
            ch["strict"] = (row < col) if ch["upper"] else (row > col)
            ch["decay"] = jnp.exp(jnp.where(causal, ch["gc_col"] - ch["gc_row"], -jnp.inf))
            ch["kb"] = blk["k"] * ch["beta_col"]
            ch["egc"] = jnp.exp(ch["gc_col"])
    for blk in blocks:
        lhs = jnp.concatenate([ch["kb"] for ch in blk["dirs"]] + [blk["q"]], axis=0).astype(BF16)
        a = _dot_nt(lhs, blk["k"].astype(BF16))
        nd = len(blk["dirs"])
        for di, ch in enumerate(blk["dirs"]):
            x = jnp.where(ch["strict"], -(a[di * c:(di + 1) * c] * ch["decay"]), 0.0)
            ch["intra"] = (a[nd * c:] * ch["decay"]).astype(BF16)
            ch["w"] = jnp.concatenate([eye, x], axis=1)
    for _ in range(6):
        for ch in chains:
            w = ch["w"]
            wh = w.astype(BF16)
            lo = w - wh.astype(F32)
            php = jnp.where(lane2 < c, pltpu.roll(w, c, 1), lo).astype(BF16)
            ch["w"] = _dot(jnp.concatenate([wh, php], axis=1),
                           jnp.concatenate([eye_t, wh, lo.astype(BF16), wh], axis=0))
    for blk in blocks:
        for ch in blk["dirs"]:
            rhs = jnp.concatenate([blk["v"] * ch["beta_col"], ch["kb"] * ch["egc"]], axis=1).astype(BF16)
            sol = _dot(ch["w"][:, :c].astype(BF16), rhs)
            ch["u"], ch["wv"] = sol[:, :GDN_DV].astype(BF16), sol[:, GDN_DV:].astype(BF16)
            ch["qe"] = (blk["q"] * ch["egc"]).astype(BF16)
            ch["kd"] = (blk["k"] * jnp.exp(ch["glast"] - ch["gc_col"])).astype(BF16)
            ch["eg"] = jnp.exp(ch["glast"])


def _gdn_scan(chains):
    c = GDN_CHUNK
    for ch in chains:
        ch["sb"] = ch["s"].astype(BF16)
    for ch in chains:
        r = _dot(jnp.concatenate([ch["wv"], ch["qe"]], axis=0), ch["sb"])
        ch["vn"] = (ch["u"].astype(F32) - r[:c]).astype(BF16)
        ch["qs"] = r[c:]
    outs = []
    for ch in chains:
        o = ch["qs"] + _dot(ch["intra"], ch["vn"])
        s_new = ch["s"] * ch["eg"] + _dot_tn(ch["kd"], ch["vn"])
        outs.append((o, s_new))
    return outs


def _gdn_kernel(qkv_ref, ab_ref, abt_ref, al_ref, dt_ref, alt_ref, dtt_ref, s0_ref, o_ref, sf_ref,
                u_s, wv_s, qe_s, kd_s, in_s, eg_s, gcol_s, grow_s, beta_s, st_s, *, t):
    c = GDN_CHUNK
    n = t // c
    nh = GDN_HEADS
    ab = ab_ref[0]
    gact = -jnp.exp(al_ref[...]) * jax.nn.softplus(ab + dt_ref[...])
    lane16 = lax.broadcasted_iota(jnp.int32, ab.shape, 1)
    beta_s[...] = jnp.where(lane16 < 2 * nh, gact, jax.nn.sigmoid(ab))
    r64 = lax.broadcasted_iota(jnp.int32, (c, c), 0)
    c64 = lax.broadcasted_iota(jnp.int32, (c, c), 1)
    tril = (r64 >= c64).astype(F32)
    triu = (r64 <= c64).astype(F32)
    lane_c = lax.broadcasted_iota(jnp.int32, (c, 16), 1)
    sub_c = lax.broadcasted_iota(jnp.int32, (16, c), 0)

    def cum_chunk(i):
        r0 = pl.multiple_of(i * c, c)
        g = beta_s[pl.ds(r0, c), :]
        gcol_s[pl.ds(r0, c), :] = jnp.where(lane_c < nh, _dot_exact(tril, g), _dot_exact(triu, g))
        gt = -jnp.exp(alt_ref[...]) * jax.nn.softplus(abt_ref[0, i] + dtt_ref[...])
        grow_s[i] = jnp.where(sub_c < nh, _dot_exact(gt, triu), _dot_exact(gt, tril))

    ncs = GDN_LOCAL_CHUNKS
    rb = ncs * c
    for sub in range(ncs):
        cum_chunk(sub)

    def local_body(jb, carry):
        r0 = pl.multiple_of(jb * rb, rb)
        nxt_blk = jnp.minimum(jb + 1, n // ncs - 1)
        xs = [qkv_ref[0, pl.ds(r0, rb), j * LANE:(j + 1) * LANE].astype(F32) for j in range(3 * nh)]
        gcol = gcol_s[pl.ds(r0, rb), :]
        bet = beta_s[pl.ds(r0, rb), :]
        blocks = []
        for sub in range(ncs):
            ci = ncs * jb + sub
            rows = slice(sub * c, (sub + 1) * c)
            grow = grow_s[ci]
            for hh in range(nh):
                dirs = []
                for d in range(2):
                    ch = d * nh + hh
                    last = sub * c + (c - 1 if d == 0 else 0)
                    dirs.append(dict(gc_col=gcol[rows, ch:ch + 1], gc_row=grow[ch:ch + 1, :],
                                     beta_col=bet[rows, 2 * nh + ch:2 * nh + ch + 1],
                                     glast=gcol[last:last + 1, ch:ch + 1], upper=(d == 1), ch=ch, ci=ci,
                                     r0=r0 + sub * c))
                blocks.append(dict(q=xs[hh][rows], k=xs[nh + hh][rows], v=xs[2 * nh + hh][rows], dirs=dirs))
        _gdn_local(blocks)
        for blk in blocks:
            for chn in blk["dirs"]:
                ch, rr = chn["ch"], pl.ds(pl.multiple_of(chn["r0"], c), c)
                u_s[ch, rr, :] = chn["u"]
                wv_s[ch, rr, :] = chn["wv"]
                qe_s[ch, rr, :] = chn["qe"]
                kd_s[ch, rr, :] = chn["kd"]
                in_s[ch, chn["ci"]] = chn["intra"]
                eg_s[chn["ci"], ch:ch + 1, :] = jnp.broadcast_to(chn["eg"], (1, LANE))
        for sub in range(ncs):
            cum_chunk(nxt_blk * ncs + sub)
        return carry

    lax.fori_loop(0, n // ncs, local_body, 0)

    for d in range(2):
        for hh in range(nh):
            st_s[d * nh + hh] = s0_ref[0, d, hh]
    o_ref[...] = jnp.zeros_like(o_ref)

    def scan_body(i, carry):
        chains = []
        for d in range(2):
            ci = i if d == 0 else n - 1 - i
            rr = pl.ds(pl.multiple_of(ci * c, c), c)
            eg = eg_s[ci]
            for hh in range(nh):
                ch = d * nh + hh
                chains.append(dict(u=u_s[ch, rr, :], wv=wv_s[ch, rr, :], qe=qe_s[ch, rr, :], kd=kd_s[ch, rr, :],
                                   intra=in_s[ch, ci], eg=eg[ch:ch + 1, :], s=st_s[ch], rr=rr, hh=hh, ch=ch))
        for chn, (o, s_new) in zip(chains, _gdn_scan(chains)):
            st_s[chn["ch"]] = s_new
            cols = slice(chn["hh"] * LANE, (chn["hh"] + 1) * LANE)
            o_ref[0, chn["rr"], cols] = (o_ref[0, chn["rr"], cols].astype(F32) + o).astype(o_ref.dtype)
        return carry

    lax.fori_loop(0, n, scan_body, 0)
    for d in range(2):
        for hh in range(nh):
            sf_ref[0, d, hh] = st_s[d * nh + hh]


def _gdn(qkv, ab, abt, al, dt, alt, dtt, s0):
    b, t, _ = qkv.shape
    n = t // GDN_CHUNK
    per_b = lambda a: pl.BlockSpec((1,) + a.shape[1:], lambda i: (i,) + (0,) * (a.ndim - 1))
    full = lambda a: pl.BlockSpec(a.shape, lambda i: (0,) * a.ndim)
    return pl.pallas_call(
        functools.partial(_gdn_kernel, t=t), grid=(b,),
        in_specs=[per_b(qkv), per_b(ab), per_b(abt), full(al), full(dt), full(alt), full(dtt), per_b(s0)],
        out_specs=[pl.BlockSpec((1, t, GDN_HEADS * GDN_DV), lambda i: (i, 0, 0)), per_b(s0)],
        out_shape=[jax.ShapeDtypeStruct((b, t, GDN_HEADS * GDN_DV), BF16), jax.ShapeDtypeStruct(s0.shape, F32)],
        scratch_shapes=[pltpu.VMEM((2 * GDN_HEADS, t, LANE), BF16), pltpu.VMEM((2 * GDN_HEADS, t, LANE), BF16),
                        pltpu.VMEM((2 * GDN_HEADS, t, LANE), BF16), pltpu.VMEM((2 * GDN_HEADS, t, LANE), BF16),
                        pltpu.VMEM((2 * GDN_HEADS, n, GDN_CHUNK, GDN_CHUNK), BF16),
                        pltpu.VMEM((n, 2 * GDN_HEADS, LANE), F32),
                        pltpu.VMEM((t, 16), F32), pltpu.VMEM((n, 16, GDN_CHUNK), F32), pltpu.VMEM((t, 16), F32),
                        pltpu.VMEM((2 * GDN_HEADS, GDN_DK, GDN_DV), F32)],
        compiler_params=_cparams(1), name="gdn",
    )(qkv, ab, abt, al, dt, alt, dtt, s0)


def _attd_kernel(*refs, n_ctx):
    if n_ctx:
        q_ref, k_ref, vt_ref, kc_ref, vct_ref, o_ref = refs
    else:
        q_ref, k_ref, vt_ref, o_ref = refs
    n_new = k_ref.shape[1]
    grp = ATT_HEADS // ATT_KV_HEADS
    chunks = []
    for c0, cn in _key_chunks(n_new):
        chunks.append((
            lambda g, c0=c0, cn=cn: k_ref[0, c0:c0 + cn, g * LANE:(g + 1) * LANE],
            lambda g, c0=c0, cn=cn: vt_ref[0, g * LANE:(g + 1) * LANE, c0:c0 + cn],
            None))
    if n_ctx:
        for c0, cn in _key_chunks(n_ctx):
            chunks.append((
                lambda g, c0=c0, cn=cn: kc_ref[0, c0:c0 + cn, g * LANE:(g + 1) * LANE].astype(BF16),
                lambda g, c0=c0, cn=cn: vct_ref[0, g * LANE:(g + 1) * LANE, c0:c0 + cn].astype(BF16),
                None))
    qs = [q_ref[0, :, h * LANE:(h + 1) * LANE] for h in range(ATT_HEADS)]
    outs = _attend_heads(qs, [h // grp for h in range(ATT_HEADS)], chunks, ATT_DH)
    o_ref[0] = jnp.concatenate(outs, axis=0).T.astype(BF16)


def _attd(q, k, vt, k_c, v_ct, tq):
    b, t, _ = q.shape
    n_ctx = 0 if k_c is None else k_c.shape[1]
    per_b = lambda a: pl.BlockSpec((1,) + a.shape[1:], lambda i, j: (i, 0, 0))
    in_specs = [pl.BlockSpec((1, tq, 512), lambda i, j: (i, j, 0)), per_b(k), per_b(vt)]
    args = [q, k, vt]
    if n_ctx:
        in_specs += [per_b(k_c), per_b(v_ct)]
        args += [k_c, v_ct]
    return pl.pallas_call(
        functools.partial(_attd_kernel, n_ctx=n_ctx),
        grid=(b, t // tq), in_specs=in_specs,
        out_specs=pl.BlockSpec((1, tq, 512), lambda i, j: (i, j, 0)),
        out_shape=jax.ShapeDtypeStruct((b, t, 512), BF16),
        compiler_params=_cparams(2), name="attd_dec" if n_ctx else "attd_ctx",
    )(*args)


def _outproj1_kernel(oc_ref, zc_ref, od_ref, zd_ref, x_ref, mod_ref, gn_ref, w_ref, lnf_ref, y_ref):
    parts = []
    for j in range(GDN_HEADS):
        parts.append(_rms_rows(oc_ref[0, :, j * LANE:(j + 1) * LANE].astype(F32), gn_ref[...]).astype(BF16))
    gc = _silu_gate(jnp.concatenate(parts, axis=1), zc_ref[0])
    gd = _silu_gate(od_ref[0], zd_ref[0])
    y = _dot(gc, w_ref[0:512, :]) + _dot(gd, w_ref[512:1024, :])
    x2 = x_ref[0] + mod_ref[0, 2:3, :] * y
    y_ref[0] = _rms_rows(x2, lnf_ref[...])


def _outproj1(oc, zc, od, zd, x, mod, gn, w, lnf):
    b, t, d = x.shape
    tr = _row_tile(t)
    bm = mod.shape[0]
    rows = lambda c: pl.BlockSpec((1, tr, c), lambda i, j: (i, j, 0))
    full = lambda a: pl.BlockSpec(a.shape, lambda i, j: (0,) * a.ndim)
    return pl.pallas_call(
        _outproj1_kernel, grid=(b, t // tr),
        in_specs=[rows(512), rows(512), rows(512), rows(512), rows(d),
                  pl.BlockSpec((1, 3, d), (lambda i, j: (i, 0, 0)) if bm > 1 else (lambda i, j: (0, 0, 0))),
                  full(gn), full(w), full(lnf)],
        out_specs=rows(d), out_shape=jax.ShapeDtypeStruct((b, t, d), F32),
        compiler_params=_cparams(2), name="outproj1",
    )(oc, zc, od, zd, x, mod, gn, w, lnf)


def _rope_table(n_tok, rot_dim):
    quarter = rot_dim // 4
    inv = np.float32(ROPE_THETA) ** (-np.arange(quarter, dtype=np.float32) / np.float32(quarter))
    tt = np.arange(n_tok)
    pos = np.stack([tt // GRID_W, tt % GRID_W], axis=-1).astype(np.float32)
    ang = (pos[:, :, None] * inv).astype(np.float32)
    cos, sin = np.cos(ang), np.sin(ang)
    c = np.concatenate([cos, cos], axis=-1).reshape(n_tok, rot_dim)
    s = np.concatenate([-sin, sin], axis=-1).reshape(n_tok, rot_dim)
    return c.astype(np.float32), s.astype(np.float32)


def _place(tab, fill, off, width):
    out = np.full((tab.shape[0], width), fill, np.float32)
    out[:, off:off + tab.shape[1]] = tab
    return out


def _pack_kernel(w_ref, o_ref, *, segs):
    parts = []
    for start, width in segs:
        if start is None:
            parts.append(jnp.zeros((w_ref.shape[0], width), F32))
        else:
            parts.append(w_ref[:, start:start + width])
    o_ref[...] = jnp.concatenate(parts, axis=1).astype(BF16)


def _pack_cols(w, segs):
    d, n_in = w.shape
    n_out = sum(width for _, width in segs)
    tr = 256
    return pl.pallas_call(
        functools.partial(_pack_kernel, segs=segs), grid=(d // tr,),
        in_specs=[pl.BlockSpec((tr, n_in), lambda i: (i, 0))],
        out_specs=pl.BlockSpec((tr, n_out), lambda i: (i, 0)),
        out_shape=jax.ShapeDtypeStruct((d, n_out), BF16),
        compiler_params=_cparams(1), name="pack_cols",
    )(w)


def _prep_l0(w_in0, w_uq, w_ukv):
    w = _pack_cols(w_in0, [(0, 640), (None, 64), (640, 32), (None, 32), (672, 1792)])
    uq = w_uq.reshape(MLA_Q_LORA, MLA_HEADS, MLA_NOPE + MLA_ROPE)
    wuq = jnp.pad(uq, ((0, 0), (0, 0), (0, LANE - MLA_NOPE - MLA_ROPE))).reshape(MLA_Q_LORA, MLA_HEADS * LANE)
    ukv = w_ukv.reshape(MLA_KV_LORA, MLA_HEADS, MLA_NOPE + MLA_V)
    wuk = jnp.pad(ukv[:, :, :MLA_NOPE], ((0, 0), (0, 0), (0, LANE - MLA_NOPE))).reshape(MLA_KV_LORA, MLA_HEADS * LANE)
    wuvt = ukv[:, :, MLA_NOPE:].reshape(MLA_KV_LORA, MLA_HEADS * MLA_V).T
    return w, wuq.astype(BF16), wuk.astype(BF16), wuvt.astype(BF16)


def _prep_l1(w_in1):
    return _pack_cols(w_in1, [(0, 1536), (1552, 2048), (1536, 16), (None, 112)])


def _chunk_rows(ab):
    b, t, c = ab.shape
    return jnp.swapaxes(ab.reshape(b, t // GDN_CHUNK, GDN_CHUNK, c), 2, 3)


def _trunk(x, mod, caches, p, tables, tq):
    dec = caches is not None
    t0m, t0s, t1 = tables if dec else (None, None, None)
    (qa, ckv, kr, za, qb, kb, vbt, zb, *ctx0) = _inproj0(
        x, mod[0], p["ln0"], p["w0"], p["qn"], p["wuq"], p["kvn"], (t0m + t0s) if dec else None)
    if dec:
        ckv_c, kr_c, kb_c, vb_ct, s0, kd_c, vd_ct = caches
    else:
        ckv_c = kr_c = kb_c = vb_ct = kd_c = vd_ct = None
        s0 = jnp.zeros((x.shape[0], 2, GDN_HEADS, GDN_DK, GDN_DV), F32)
    oa = _mla(qa, ckv, kr, ckv_c, kr_c, p["wuk"], p["wuvt"], tq)
    ob = _swa(p["sink"], qb, kb, vbt, kb_c, vb_ct, tq)
    (x1, qkv, zc, qd, kd, vdt, zd, ab, *ctx1) = _inproj1(
        oa, za, ob, zb, x, mod[0], p["wout0"], mod[1], p["ln1"], p["w1"], p["cw"], p["aqn"], p["akn"],
        t1 if dec else None)
    oc, sfin = _gdn(qkv, ab, _chunk_rows(ab), p["al"], p["dt"], p["alt"], p["dtt"], s0)
    od = _attd(qd, kd, vdt, kd_c, vd_ct, tq)
    y = _outproj1(oc, zc, od, zd, x1, mod[1], p["gn"], p["wout1"], p["lnf"])
    return y, ctx0, sfin, ctx1


def kernel(x_prompt, x_sample, cache_l0_mla_ckv, cache_l0_mla_krope, cache_l0_swa_k, cache_l0_swa_v,
           state_l1_gdn, cache_l1_attn_k, cache_l1_attn_v, c, c_ctx,
           w_mod0, b_mod0, ln0, w_in0, mla_q_norm, w_uq, mla_kv_norm, w_ukv, swa_sink, w_out0,
           w_mod1, b_mod1, ln1, w_in1, gdn_conv, gdn_a_log, gdn_dt_bias, gdn_norm, att_q_norm, att_k_norm, w_out1,
           ln_f):
    d = x_prompt.shape[-1]
    bd, td = x_sample.shape[:2]
    bc, tc = x_prompt.shape[:2]
    past = cache_l0_mla_ckv.shape[1]
    row = lambda v: v.reshape(1, -1)
    w0, wuq, wuk, wuvt = _prep_l0(w_in0, w_uq, w_ukv)
    w1 = _prep_l1(w_in1)
    al8 = gdn_a_log.reshape(1, 2 * GDN_HEADS)
    dt8 = gdn_dt_bias.reshape(1, 2 * GDN_HEADS)
    al16 = jnp.pad(al8, ((0, 0), (0, 8)))
    dt16 = jnp.pad(dt8, ((0, 0), (0, 8)))
    p = dict(ln0=row(ln0), w0=w0, qn=row(mla_q_norm), wuq=wuq, kvn=row(mla_kv_norm), wuk=wuk, wuvt=wuvt,
             sink=swa_sink, wout0=w_out0.astype(BF16), ln1=row(ln1), w1=w1, aqn=row(att_q_norm),
             akn=row(att_k_norm), cw=gdn_conv, al=al16, dt=dt16, alt=al16.T, dtt=dt16.T, gn=row(gdn_norm),
             wout1=w_out1.astype(BF16), lnf=row(ln_f))
    n_rows = -(-(bd + 1) // 8) * 8
    c_rows = jnp.concatenate([c, c_ctx[None, :], jnp.zeros((n_rows - bd - 1, d), F32)], axis=0)
    mods = [_mod(c_rows, w_mod0, b_mod0), _mod(c_rows, w_mod1, b_mod1)]
    mod_dec = [m[:bd].reshape(bd, 3, d) for m in mods]
    mod_ctx = [m[bd:bd + 1].reshape(1, 3, d) for m in mods]
    cm, sm = _rope_table(td, MLA_ROPE)
    t0m = (jnp.asarray(_place(cm, 1.0, MLA_NOPE, LANE)), jnp.asarray(_place(sm, 0.0, MLA_NOPE, LANE)))
    cs, ss = _rope_table(td, SWA_DH)
    t0s = (jnp.asarray(np.tile(cs, (1, LANE // SWA_DH))), jnp.asarray(np.tile(ss, (1, LANE // SWA_DH))))
    t1 = tuple(jnp.asarray(a) for a in _rope_table(td, ATT_DH))
    caches = (cache_l0_mla_ckv,
              jnp.pad(cache_l0_mla_krope, ((0, 0), (0, 0), (MLA_NOPE, LANE - MLA_NOPE - MLA_ROPE))),
              cache_l0_swa_k.reshape(bd, past, SWA_KV_HEADS * SWA_DH),
              jnp.swapaxes(cache_l0_swa_v.reshape(bd, past, SWA_KV_HEADS * SWA_DH), 1, 2),
              state_l1_gdn,
              cache_l1_attn_k.reshape(bd, past, ATT_KV_HEADS * ATT_DH),
              jnp.swapaxes(cache_l1_attn_v.reshape(bd, past, ATT_KV_HEADS * ATT_DH), 1, 2))
    y_prompt, ctx0, sfin, ctx1 = _trunk(x_prompt, mod_ctx, None, p, None, tq=tc)
    y_sample, _, _, _ = _trunk(x_sample, mod_dec, caches, p, (t0m, t0s, t1), tq=256)
    ckv32, kr32, kb32, vb32 = ctx0
    kd32, vd32 = ctx1
    return (y_prompt, y_sample, ckv32, kr32,
            kb32.reshape(bc, tc, SWA_KV_HEADS, SWA_DH), vb32.reshape(bc, tc, SWA_KV_HEADS, SWA_DH),
            sfin, kd32.reshape(bc, tc, ATT_KV_HEADS, ATT_DH), vd32.reshape(bc, tc, ATT_KV_HEADS, ATT_DH))
```

```python
import functools
import math

import jax
import jax.numpy as jnp
import numpy as np
from jax import lax
from jax.experimental import pallas as pl
from jax.experimental.pallas import tpu as pltpu

F32 = jnp.float32
BF16 = jnp.bfloat16

GRID_W = 64
ROPE_THETA = 10000.0
EPS = 1e-6
WINDOW = 128
MLA_HEADS, MLA_NOPE, MLA_ROPE, MLA_V = 8, 64, 32, 64
MLA_Q_LORA, MLA_KV_LORA = 384, 256
SWA_HEADS, SWA_KV_HEADS, SWA_DH = 8, 2, 64
GDN_HEADS, GDN_DK, GDN_DV, CONV_K, GDN_CHUNK = 4, 128, 128, 3, 64
GDN_LOCAL_CHUNKS = 4
ATT_HEADS, ATT_KV_HEADS, ATT_DH = 4, 2, 128
LANE = 128
LOG2E = math.log2(math.e)
NEG = -1e30
VMEM_LIMIT = 56 * 1024 * 1024


def _cparams(n_axes):
    return pltpu.CompilerParams(dimension_semantics=("arbitrary",) * n_axes, vmem_limit_bytes=VMEM_LIMIT)


def _dot(a, b):
    return jnp.dot(a, b, preferred_element_type=F32)


def _dot_nt(a, b):
    return lax.dot_general(a, b, (((1,), (1,)), ((), ())), preferred_element_type=F32)


def _dot_tn(a, b):
    return lax.dot_general(a, b, (((0,), (0,)), ((), ())), preferred_element_type=F32)


def _dot_exact(a, b):
    return jnp.dot(a, b, preferred_element_type=F32, precision=lax.Precision.HIGHEST)


def _split(a):
    hi = a.astype(BF16)
    return hi, (a - hi.astype(F32)).astype(BF16)


def _silu(x):
    return x * jax.nn.sigmoid(x)


def _rms_rows(x, g):
    return x * lax.rsqrt(jnp.mean(x * x, axis=-1, keepdims=True) + EPS) * g


def _rope_block(x, cos, sin, half):
    lane = lax.broadcasted_iota(jnp.int32, x.shape, 1)
    first = (lane // half) % 2 == 0
    partner = jnp.where(first, pltpu.roll(x, LANE - half, 1), pltpu.roll(x, half, 1))
    return x * cos + partner * sin


def _mod_kernel(c_ref, w_ref, b_ref, o_ref):
    a = _silu(c_ref[...]).astype(BF16)
    o_ref[...] = _dot(a, w_ref[...].astype(BF16)) + b_ref[...]


def _mod(c_rows, w_mod, b_mod):
    r, d = c_rows.shape
    n = w_mod.shape[1]
    tn = 1024
    return pl.pallas_call(
        _mod_kernel,
        grid=(n // tn,),
        in_specs=[pl.BlockSpec((r, d), lambda j: (0, 0)),
                  pl.BlockSpec((d, tn), lambda j: (0, j)),
                  pl.BlockSpec((1, tn), lambda j: (0, j))],
        out_specs=pl.BlockSpec((r, tn), lambda j: (0, j)),
        out_shape=jax.ShapeDtypeStruct((r, n), F32),
        compiler_params=_cparams(1),
        name="mod",
    )(c_rows, w_mod, b_mod.reshape(1, n))


def _adaln(x, mod_ref, ln_ref):
    h = _rms_rows(x, ln_ref[...])
    return h * (1.0 + mod_ref[0, 1:2, :]) + mod_ref[0, 0:1, :]


L0_OFF = dict(cq=0, ckv=384, kr=640, za=768, qb=1280, kb=1792, vb=1920, zb=2048)
L0_W = 2560


def _inproj0_kernel(*refs, rope):
    if rope:
        (x_ref, mod_ref, ln_ref, w_ref, qn_ref, wuq_ref, kvn_ref, cm_ref, sm_ref, cs_ref, ss_ref,
         qa_ref, ckv_ref, kr_ref, za_ref, qb_ref, kb_ref, vbt_ref, zb_ref) = refs
    else:
        (x_ref, mod_ref, ln_ref, w_ref, qn_ref, wuq_ref, kvn_ref,
         qa_ref, ckv_ref, kr_ref, za_ref, qb_ref, kb_ref, vbt_ref, zb_ref,
         ckv32_ref, kr32_ref, kb32_ref, vb32_ref) = refs
    h = _adaln(x_ref[0], mod_ref, ln_ref).astype(BF16)
    u = _dot(h, w_ref[...])
    o = L0_OFF
    cq = _rms_rows(u[:, o["cq"]:o["cq"] + 384], qn_ref[...]).astype(BF16)
    qa = _dot(cq, wuq_ref[...])
    ckv = _rms_rows(u[:, o["ckv"]:o["ckv"] + 256], kvn_ref[...])
    kr = u[:, o["kr"]:o["kr"] + 128]
    qb = u[:, o["qb"]:o["qb"] + 512]
    kb = u[:, o["kb"]:o["kb"] + 128]
    vb = u[:, o["vb"]:o["vb"] + 128]
    if not rope:
        ckv32_ref[0] = ckv
        kr32_ref[0] = kr[:, 64:96]
        kb32_ref[0] = kb
        vb32_ref[0] = vb
    qa_scale = (MLA_NOPE + MLA_ROPE) ** -0.5 * LOG2E
    qb_scale = SWA_DH ** -0.5 * LOG2E
    for j in range(MLA_HEADS):
        blk = qa[:, j * LANE:(j + 1) * LANE]
        if rope:
            blk = _rope_block(blk, cm_ref[...], sm_ref[...], MLA_ROPE // 4)
        qa_ref[0, :, j * LANE:(j + 1) * LANE] = (blk * qa_scale).astype(BF16)
    for j in range(SWA_HEADS * SWA_DH // LANE):
        blk = qb[:, j * LANE:(j + 1) * LANE]
        if rope:
            blk = _rope_block(blk, cs_ref[...], ss_ref[...], SWA_DH // 4)
        qb_ref[0, :, j * LANE:(j + 1) * LANE] = (blk * qb_scale).astype(BF16)
    if rope:
        kr = _rope_block(kr, cm_ref[...], sm_ref[...], MLA_ROPE // 4)
        kb = _rope_block(kb, cs_ref[...], ss_ref[...], SWA_DH // 4)
    ckv_ref[0] = ckv.astype(BF16)
    kr_ref[0] = kr.astype(BF16)
    kb_ref[0] = kb.astype(BF16)
    za_ref[0] = u[:, o["za"]:o["za"] + 512].astype(BF16)
    zb_ref[0] = u[:, o["zb"]:o["zb"] + 512].astype(BF16)
    vbt_ref[0] = vb.T.astype(BF16)


def _row_tile(t):
    return 512 if t % 512 == 0 else 256


def _inproj0(x, mod, ln, w, qn, wuq, kvn, tables):
    b, t, d = x.shape
    tr = _row_tile(t)
    rope = tables is not None
    bm = mod.shape[0]
    full = lambda a: pl.BlockSpec(a.shape, lambda i, j: (0,) * a.ndim)
    rows = lambda c: pl.BlockSpec((1, tr, c), lambda i, j: (i, j, 0))
    in_specs = [rows(d), pl.BlockSpec((1, 3, d), (lambda i, j: (i, 0, 0)) if bm > 1 else (lambda i, j: (0, 0, 0))),
                full(ln), full(w), full(qn), full(wuq), full(kvn)]
    args = [x, mod, ln, w, qn, wuq, kvn]
    if rope:
        for tab in tables:
            in_specs.append(pl.BlockSpec((tr, LANE), lambda i, j: (j, 0)))
            args.append(tab)
    out_shape = [jax.ShapeDtypeStruct((b, t, 1024), BF16), jax.ShapeDtypeStruct((b, t, 256), BF16),
                 jax.ShapeDtypeStruct((b, t, 128), BF16), jax.ShapeDtypeStruct((b, t, 512), BF16),
                 jax.ShapeDtypeStruct((b, t, 512), BF16), jax.ShapeDtypeStruct((b, t, 128), BF16),
                 jax.ShapeDtypeStruct((b, 128, t), BF16), jax.ShapeDtypeStruct((b, t, 512), BF16)]
    out_specs = [rows(1024), rows(256), rows(128), rows(512), rows(512), rows(128),
                 pl.BlockSpec((1, 128, tr), lambda i, j: (i, 0, j)), rows(512)]
    if not rope:
        out_shape += [jax.ShapeDtypeStruct((b, t, 256), F32), jax.ShapeDtypeStruct((b, t, 32), F32),
                      jax.ShapeDtypeStruct((b, t, 128), F32), jax.ShapeDtypeStruct((b, t, 128), F32)]
        out_specs += [rows(256), rows(32), rows(128), rows(128)]
    return pl.pallas_call(
        functools.partial(_inproj0_kernel, rope=rope),
        grid=(b, t // tr), in_specs=in_specs, out_specs=out_specs, out_shape=out_shape,
        compiler_params=_cparams(2), name="inproj0_dec" if rope else "inproj0_ctx",
    )(*args)


KEY_CHUNK = 512
ATT_LOOKAHEAD = 4


SUM_ROWS = 16


def _attend_heads(qs, kv_of, chunks, dv, sinks=None):
    nh = len(qs)
    tq = qs[0].shape[0]
    m = [None] * nh
    acc = [None] * nh
    items = [(ci, i) for ci in range(len(chunks)) for i in range(nh)]
    loaded = {}

    def kv(ci, src):
        if (ci, src) not in loaded:
            vt = chunks[ci][1](src)
            ones = (lax.broadcasted_iota(jnp.int32, (SUM_ROWS, vt.shape[1]), 0) == 0).astype(BF16)
            loaded[(ci, src)] = (chunks[ci][0](src), jnp.concatenate([vt, ones], axis=0))
        return loaded[(ci, src)]

    if sinks is not None:
        m = [jnp.full((1, tq), sk, F32) for sk in sinks]
        unit = (lax.broadcasted_iota(jnp.int32, (dv + SUM_ROWS, tq), 0) == dv).astype(F32)
        acc = [unit for _ in sinks]

    scores = {}
    for t in range(len(items) + ATT_LOOKAHEAD):
        if t < len(items):
            ci, i = items[t]
            scores[t] = _dot_nt(kv(ci, kv_of[i])[0], qs[i])
        t0 = t - ATT_LOOKAHEAD
        if t0 < 0:
            continue
        ci, i = items[t0]
        mask = chunks[ci][2]
        si = scores.pop(t0)
        if mask is not None:
            si = jnp.where(mask, si, NEG)
        cm = si.max(axis=0, keepdims=True)
        alpha = None
        if m[i] is None:
            m_new = cm
        else:
            m_new = jnp.maximum(m[i], cm)
            alpha = jnp.exp2(m[i] - m_new)
        p = jnp.exp2(si - m_new)
        m[i] = m_new
        pv = _dot(kv(ci, kv_of[i])[1], p.astype(BF16))
        acc[i] = pv if acc[i] is None else acc[i] * alpha + pv
    return [acc[i][:dv] * (1.0 / acc[i][dv:dv + 1]) for i in range(nh)]


def _key_chunks(n):
    step = KEY_CHUNK if n % KEY_CHUNK == 0 else n
    return [(c0, step) for c0 in range(0, n, step)]


def _mla_kernel(*refs, n_new, n_ctx, hp):
    if n_ctx:
        q_ref, ckv_ref, kr_ref, ckvc_ref, krc_ref, wuk_ref, wuvt_ref, o_ref, k_s, vt_s = refs
    else:
        q_ref, ckv_ref, kr_ref, wuk_ref, wuvt_ref, o_ref, k_s, vt_s = refs
    qi, gi = pl.program_id(1), pl.program_id(2)

    @pl.when((qi == 0) & (gi == 0))
    def _():
        def expand(ckv, kr, r0, n):
            kn = _dot(ckv, wuk_ref[...])
            for j in range(MLA_HEADS):
                k_s[j, r0:r0 + n, :] = (kn[:, j * LANE:(j + 1) * LANE] + kr).astype(BF16)
            vt_s[:, r0:r0 + n] = _dot_nt(wuvt_ref[...], ckv).astype(BF16)

        blk = 512 if n_new % 512 == 0 else 256
        for r0 in range(0, n_new, blk):
            expand(ckv_ref[0, r0:r0 + blk, :], kr_ref[0, r0:r0 + blk, :].astype(F32), r0, blk)
        if n_ctx:
            expand(ckvc_ref[0].astype(BF16), krc_ref[0], n_new, n_ctx)

    qs = [q_ref[0, :, j * LANE:(j + 1) * LANE] for j in range(hp)]
    chunks = []
    for c0, cn in _key_chunks(n_new + n_ctx):
        chunks.append((
            lambda j, c0=c0, cn=cn: k_s[gi * hp + j, c0:c0 + cn, :],
            lambda j, c0=c0, cn=cn: vt_s[pl.ds(pl.multiple_of((gi * hp + j) * MLA_V, MLA_V), MLA_V), c0:c0 + cn],
            None))
    outs = _attend_heads(qs, list(range(hp)), chunks, MLA_V)
    o_ref[0] = jnp.concatenate(outs, axis=0).T.astype(BF16)


def _mla(q, ckv, kr, ckv_c, kr_c, wuk, wuvt, tq):
    b, t, _ = q.shape
    n_ctx = 0 if ckv_c is None else ckv_c.shape[1]
    hp = 8
    tk = t + n_ctx
    rows_q = pl.BlockSpec((1, tq, hp * LANE), lambda i, j, g: (i, j, g))
    per_b = lambda a: pl.BlockSpec((1,) + a.shape[1:], lambda i, j, g: (i, 0, 0))
    full = lambda a: pl.BlockSpec(a.shape, lambda i, j, g: (0, 0))
    in_specs = [rows_q, per_b(ckv), per_b(kr)]
    args = [q, ckv, kr]
    if n_ctx:
        in_specs += [per_b(ckv_c), per_b(kr_c)]
        args += [ckv_c, kr_c]
    in_specs += [full(wuk), full(wuvt)]
    args += [wuk, wuvt]
    return pl.pallas_call(
        functools.partial(_mla_kernel, n_new=t, n_ctx=n_ctx, hp=hp),
        grid=(b, t // tq, MLA_HEADS // hp), in_specs=in_specs,
        out_specs=pl.BlockSpec((1, tq, hp * MLA_V), lambda i, j, g: (i, j, g)),
        out_shape=jax.ShapeDtypeStruct((b, t, MLA_HEADS * MLA_V), BF16),
        scratch_shapes=[pltpu.VMEM((MLA_HEADS, tk, LANE), BF16), pltpu.VMEM((MLA_HEADS * MLA_V, tk), BF16)],
        compiler_params=_cparams(3), name="mla_dec" if n_ctx else "mla_ctx",
    )(*args)


def _swa_kernel(*refs, n_new, n_ctx, tq):
    if n_ctx:
        sink_ref, q_ref, k_ref, vt_ref, kc_ref, vct_ref, o_ref = refs
    else:
        sink_ref, q_ref, k_ref, vt_ref, o_ref = refs
    qi = pl.program_id(1)
    grp = SWA_HEADS // SWA_KV_HEADS
    if n_ctx:
        span = tq + 2 * WINDOW
        q0 = qi * tq
        start = pl.multiple_of(jnp.clip(q0 - WINDOW, 0, n_new - span), LANE)
        kpos = start + lax.broadcasted_iota(jnp.int32, (span, tq), 0)
        qpos = q0 + lax.broadcasted_iota(jnp.int32, (span, tq), 1)
        band = jnp.abs(kpos - qpos) <= WINDOW
    dh = SWA_DH
    chunks = []
    if n_ctx:
        for c0, cn in _key_chunks(span):
            chunks.append((
                lambda g, c0=c0, cn=cn: k_ref[0, pl.ds(start + c0, cn), g * dh:(g + 1) * dh],
                lambda g, c0=c0, cn=cn: vt_ref[0, g * dh:(g + 1) * dh, pl.ds(start + c0, cn)],
                band[c0:c0 + cn]))
        for c0, cn in _key_chunks(n_ctx):
            chunks.append((
                lambda g, c0=c0, cn=cn: kc_ref[0, c0:c0 + cn, g * dh:(g + 1) * dh].astype(BF16),
                lambda g, c0=c0, cn=cn: vct_ref[0, g * dh:(g + 1) * dh, c0:c0 + cn].astype(BF16),
                None))
    else:
        for c0, cn in _key_chunks(n_new):
            chunks.append((
                lambda g, c0=c0, cn=cn: k_ref[0, c0:c0 + cn, g * dh:(g + 1) * dh],
                lambda g, c0=c0, cn=cn: vt_ref[0, g * dh:(g + 1) * dh, c0:c0 + cn],
                None))
    qs = [q_ref[0, :, h * dh:(h + 1) * dh] for h in range(SWA_HEADS)]
    sinks = [sink_ref[h] * LOG2E for h in range(SWA_HEADS)]
    outs = _attend_heads(qs, [h // grp for h in range(SWA_HEADS)], chunks, SWA_DH, sinks)
    o_ref[0] = jnp.concatenate(outs, axis=0).T.astype(BF16)


def _swa(sink, q, k, vt, k_c, v_ct, tq):
    b, t, _ = q.shape
    n_ctx = 0 if k_c is None else k_c.shape[1]
    per_b = lambda a: pl.BlockSpec((1,) + a.shape[1:], lambda i, j: (i, 0, 0))
    in_specs = [pl.BlockSpec(memory_space=pltpu.SMEM), pl.BlockSpec((1, tq, 512), lambda i, j: (i, j, 0)),
                per_b(k), per_b(vt)]
    args = [sink, q, k, vt]
    if n_ctx:
        in_specs += [per_b(k_c), per_b(v_ct)]
        args += [k_c, v_ct]
    return pl.pallas_call(
        functools.partial(_swa_kernel, n_new=t, n_ctx=n_ctx, tq=tq),
        grid=(b, t // tq), in_specs=in_specs,
        out_specs=pl.BlockSpec((1, tq, 512), lambda i, j: (i, j, 0)),
        out_shape=jax.ShapeDtypeStruct((b, t, 512), BF16),
        compiler_params=_cparams(2), name="swa_dec" if n_ctx else "swa_ctx",
    )(*args)


def _silu_gate(o, z):
    return o * (z / (1.0 + jnp.exp(-z)))


L1_OFF = dict(qkv=0, zc=1536, qd=2048, kd=2560, vd=2816, zd=3072, ab=3584)
L1_W = 3712


def _inproj1_kernel(*refs, rope):
    if rope:
        (oa_ref, za_ref, ob_ref, zb_ref, x_ref, *halo_refs, mod0_ref, wo_ref, mod_ref, ln_ref, w_ref, cw_ref,
         qn_ref, kn_ref, c_ref, s_ref,
         x1_ref, qkv_ref, zc_ref, qd_ref, kd_ref, vdt_ref, zd_ref, ab_ref) = refs
    else:
        (oa_ref, za_ref, ob_ref, zb_ref, x_ref, *halo_refs, mod0_ref, wo_ref, mod_ref, ln_ref, w_ref, cw_ref,
         qn_ref, kn_ref,
         x1_ref, qkv_ref, zc_ref, qd_ref, kd_ref, vdt_ref, zd_ref, ab_ref, kd32_ref, vd32_ref) = refs

    def residual(oa, za, ob, zb, x):
        y0 = _dot(_silu_gate(oa, za), wo_ref[0:512, :]) + _dot(_silu_gate(ob, zb), wo_ref[512:1024, :])
        return x + mod0_ref[0, 2:3, :] * y0

    x1 = residual(oa_ref[0], za_ref[0], ob_ref[0], zb_ref[0], x_ref[0])
    x1_ref[0] = x1
    h = _adaln(x1, mod_ref, ln_ref).astype(BF16)
    u = _dot(h, w_ref[...])
    o = L1_OFF
    tr = u.shape[0]
    nq = GDN_HEADS * GDN_DK
    n_qkv = 2 * nq + GDN_HEADS * GDN_DV
    hp, hn = halo_refs[:5], halo_refs[5:]
    xh = residual(*[jnp.concatenate([a[0], b[0]], axis=0) for a, b in zip(hp, hn)])
    uh = _dot(_adaln(xh, mod_ref, ln_ref).astype(BF16), w_ref[:, 0:n_qkv])
    jrow = pl.program_id(1)
    prv_all = jnp.where(jrow > 0, uh[HALO - 1:HALO], 0.0)
    nxt_all = jnp.where(jrow < pl.num_programs(1) - 1, uh[HALO:HALO + 1], 0.0)
    rowi = lax.broadcasted_iota(jnp.int32, (tr, LANE), 0)
    for j in range(n_qkv // LANE):
        cols = slice(j * LANE, (j + 1) * LANE)
        xj = u[:, cols]
        xp = jnp.where(rowi == 0, prv_all[:, cols], pltpu.roll(xj, 1, 0))
        xn = jnp.where(rowi == tr - 1, nxt_all[:, cols], pltpu.roll(xj, tr - 1, 0))
        cw = cw_ref[:, cols]
        y = _silu(xp * cw[0:1, :] + xj * cw[1:2, :] + xn * cw[2:3, :])
        if j < 2 * nq // LANE:
            y = y * lax.rsqrt(jnp.sum(y * y, axis=-1, keepdims=True) + EPS)
        if j < nq // LANE:
            y = y * GDN_DK ** -0.5
        qkv_ref[0, :, cols] = y.astype(BF16)
    zc_ref[0] = u[:, o["zc"]:o["zc"] + 512].astype(BF16)
    zd_ref[0] = u[:, o["zd"]:o["zd"] + 512].astype(BF16)
    ab_ref[0] = u[:, o["ab"]:o["ab"] + 16]
    qd_scale = ATT_DH ** -0.5 * LOG2E
    for j in range(ATT_HEADS):
        blk = _rms_rows(u[:, o["qd"] + j * LANE:o["qd"] + (j + 1) * LANE], qn_ref[...])
        if rope:
            blk = _rope_block(blk, c_ref[...], s_ref[...], ATT_DH // 4)
        qd_ref[0, :, j * LANE:(j + 1) * LANE] = (blk * qd_scale).astype(BF16)
    for j in range(ATT_KV_HEADS):
        blk = _rms_rows(u[:, o["kd"] + j * LANE:o["kd"] + (j + 1) * LANE], kn_ref[...])
        if rope:
            blk = _rope_block(blk, c_ref[...], s_ref[...], ATT_DH // 4)
        else:
            kd32_ref[0, :, j * LANE:(j + 1) * LANE] = blk
        kd_ref[0, :, j * LANE:(j + 1) * LANE] = blk.astype(BF16)
    vd = u[:, o["vd"]:o["vd"] + 256]
    if not rope:
        vd32_ref[0] = vd
    vdt_ref[0] = vd.T.astype(BF16)


HALO = 16


def _inproj1(oa, za, ob, zb, x, mod0, wo, mod, ln, w, cw, qn, kn, tables):
    b, t, d = x.shape
    tr = _row_tile(t)
    rope = tables is not None
    bm = mod.shape[0]
    full = lambda a: pl.BlockSpec(a.shape, lambda i, j: (0,) * a.ndim)
    rows = lambda c: pl.BlockSpec((1, tr, c), lambda i, j: (i, j, 0))
    gpb = tr // HALO
    prev = lambda c: pl.BlockSpec((1, HALO, c), lambda i, j: (i, jnp.maximum(j * gpb - 1, 0), 0))
    nxt = lambda c: pl.BlockSpec((1, HALO, c), lambda i, j: (i, jnp.minimum((j + 1) * gpb, t // HALO - 1), 0))
    mod_spec = pl.BlockSpec((1, 3, d), (lambda i, j: (i, 0, 0)) if bm > 1 else (lambda i, j: (0, 0, 0)))
    streams = [oa, za, ob, zb, x]
    widths = [512, 512, 512, 512, d]
    in_specs = ([rows(c) for c in widths] + [prev(c) for c in widths] + [nxt(c) for c in widths]
                + [mod_spec, full(wo), mod_spec, full(ln), full(w), full(cw), full(qn), full(kn)])
    args = streams * 3 + [mod0, wo, mod, ln, w, cw, qn, kn]
    if rope:
        for tab in tables:
            in_specs.append(pl.BlockSpec((tr, LANE), lambda i, j: (j, 0)))
            args.append(tab)
    out_shape = [jax.ShapeDtypeStruct((b, t, d), F32),
                 jax.ShapeDtypeStruct((b, t, 1536), BF16), jax.ShapeDtypeStruct((b, t, 512), BF16),
                 jax.ShapeDtypeStruct((b, t, 512), BF16), jax.ShapeDtypeStruct((b, t, 256), BF16),
                 jax.ShapeDtypeStruct((b, 256, t), BF16), jax.ShapeDtypeStruct((b, t, 512), BF16),
                 jax.ShapeDtypeStruct((b, t, 16), F32)]
    out_specs = [rows(d), rows(1536), rows(512), rows(512), rows(256),
                 pl.BlockSpec((1, 256, tr), lambda i, j: (i, 0, j)), rows(512), rows(16)]
    if not rope:
        out_shape += [jax.ShapeDtypeStruct((b, t, 256), F32), jax.ShapeDtypeStruct((b, t, 256), F32)]
        out_specs += [rows(256), rows(256)]
    return pl.pallas_call(
        functools.partial(_inproj1_kernel, rope=rope),
        grid=(b, t // tr), in_specs=in_specs, out_specs=out_specs, out_shape=out_shape,
        compiler_params=_cparams(2), name="inproj1_dec" if rope else "inproj1_ctx",
    )(*args)


def _gdn_local(blocks):
    c = GDN_CHUNK
    row = lax.broadcasted_iota(jnp.int32, (c, c), 0)
    col = lax.broadcasted_iota(jnp.int32, (c, c), 1)
    lane2 = lax.broadcasted_iota(jnp.int32, (c, 2 * c), 1)
    eye = (row == col).astype(F32)
    eye_t = jnp.concatenate([eye, jnp.zeros((c, c), F32)], axis=1).astype(BF16)
    chains = [ch for blk in blocks for ch in blk["dirs"]]
    for blk in blocks:
        for ch in blk["dirs"]:
            causal = (row <= col) if ch["upper"] else (row >= col)
            ch["strict"] = (row < col) if ch["upper"] else (row > col)
            ch["decay"] = jnp.exp(jnp.where(causal, ch["gc_col"] - ch["gc_row"], -jnp.inf))
            ch["kb"] = blk["k"] * ch["beta_col"]
            ch["egc"] = jnp.exp(ch["gc_col"])
    for blk in blocks:
        lhs = jnp.concatenate([ch["kb"] for ch in blk["dirs"]] + [blk["q"]], axis=0).astype(BF16)
        a = _dot_nt(lhs, blk["k"].astype(BF16))
        nd = len(blk["dirs"])
        for di, ch in enumerate(blk["dirs"]):
            x = jnp.where(ch["strict"], -(a[di * c:(di + 1) * c] * ch["decay"]), 0.0)
            ch["intra"] = (a[nd * c:] * ch["decay"]).astype(BF16)
            ch["w"] = jnp.concatenate([eye, x], axis=1)
    for _ in range(6):
        for ch in chains:
            w = ch["w"]
            wh = w.astype(BF16)
            lo = w - wh.astype(F32)
            php = jnp.where(lane2 < c, pltpu.roll(w, c, 1), lo).astype(BF16)
            ch["w"] = _dot(jnp.concatenate([wh, php], axis=1),
                           jnp.concatenate([eye_t, wh, lo.astype(BF16), wh], axis=0))
    for blk in blocks:
        for ch in blk["dirs"]:
            rhs = jnp.concatenate([blk["v"] * ch["beta_col"], ch["kb"] * ch["egc"]], axis=1).astype(BF16)
            sol = _dot(ch["w"][:, :c].astype(BF16), rhs)
            ch["u"], ch["wv"] = sol[:, :GDN_DV].astype(BF16), sol[:, GDN_DV:].astype(BF16)
            ch["qe"] = (blk["q"] * ch["egc"]).astype(BF16)
            ch["kd"] = (blk["k"] * jnp.exp(ch["glast"] - ch["gc_col"])).astype(BF16)
            ch["eg"] = jnp.exp(ch["glast"])


def _gdn_scan(chains):
    c = GDN_CHUNK
    for ch in chains:
        ch["sb"] = ch["s"].astype(BF16)
    for ch in chains:
        r = _dot(jnp.concatenate([ch["wv"], ch["qe"]], axis=0), ch["sb"])
        ch["vn"] = (ch["u"].astype(F32) - r[:c]).astype(BF16)
        ch["qs"] = r[c:]
    outs = []
    for ch in chains:
        o = ch["qs"] + _dot(ch["intra"], ch["vn"])
        s_new = ch["s"] * ch["eg"] + _dot_tn(ch["kd"], ch["vn"])
        outs.append((o, s_new))
    return outs


def _gdn_kernel(qkv_ref, ab_ref, abt_ref, al_ref, dt_ref, alt_ref, dtt_ref, s0_ref, o_ref, sf_ref,
                u_s, wv_s, qe_s, kd_s, in_s, eg_s, gcol_s, grow_s, beta_s, st_s, *, t):
    c = GDN_CHUNK
    n = t // c
    nh = GDN_HEADS
    ab = ab_ref[0]
    gact = -jnp.exp(al_ref[...]) * jax.nn.softplus(ab + dt_ref[...])
    lane16 = lax.broadcasted_iota(jnp.int32, ab.shape, 1)
    beta_s[...] = jnp.where(lane16 < 2 * nh, gact, jax.nn.sigmoid(ab))
    r64 = lax.broadcasted_iota(jnp.int32, (c, c), 0)
    c64 = lax.broadcasted_iota(jnp.int32, (c, c), 1)
    tril = (r64 >= c64).astype(F32)
    triu = (r64 <= c64).astype(F32)
    lane_c = lax.broadcasted_iota(jnp.int32, (c, 16), 1)
    sub_c = lax.broadcasted_iota(jnp.int32, (16, c), 0)

    def cum_chunk(i):
        r0 = pl.multiple_of(i * c, c)
        g = beta_s[pl.ds(r0, c), :]
        gcol_s[pl.ds(r0, c), :] = jnp.where(lane_c < nh, _dot_exact(tril, g), _dot_exact(triu, g))
        gt = -jnp.exp(alt_ref[...]) * jax.nn.softplus(abt_ref[0, i] + dtt_ref[...])
        grow_s[i] = jnp.where(sub_c < nh, _dot_exact(gt, triu), _dot_exact(gt, tril))

    ncs = GDN_LOCAL_CHUNKS
    rb = ncs * c
    for sub in range(ncs):
        cum_chunk(sub)

    def local_body(jb, carry):
        r0 = pl.multiple_of(jb * rb, rb)
        nxt_blk = jnp.minimum(jb + 1, n // ncs - 1)
        xs = [qkv_ref[0, pl.ds(r0, rb), j * LANE:(j + 1) * LANE].astype(F32) for j in range(3 * nh)]
        gcol = gcol_s[pl.ds(r0, rb), :]
        bet = beta_s[pl.ds(r0, rb), :]
        blocks = []
        for sub in range(ncs):
            ci = ncs * jb + sub
            rows = slice(sub * c, (sub + 1) * c)
            grow = grow_s[ci]
            for hh in range(nh):
                dirs = []
                for d in range(2):
                    ch = d * nh + hh
                    last = sub * c + (c - 1 if d == 0 else 0)
                    dirs.append(dict(gc_col=gcol[rows, ch:ch + 1], gc_row=grow[ch:ch + 1, :],
                                     beta_col=bet[rows, 2 * nh + ch:2 * nh + ch + 1],
                                     glast=gcol[last:last + 1, ch:ch + 1], upper=(d == 1), ch=ch, ci=ci,
                                     r0=r0 + sub * c))
                blocks.append(dict(q=xs[hh][rows], k=xs[nh + hh][rows], v=xs[2 * nh + hh][rows], dirs=dirs))
        _gdn_local(blocks)
        for blk in blocks:
            for chn in blk["dirs"]:
                ch, rr = chn["ch"], pl.ds(pl.multiple_of(chn["r0"], c), c)
                u_s[ch, rr, :] = chn["u"]
                wv_s[ch, rr, :] = chn["wv"]
                qe_s[ch, rr, :] = chn["qe"]
                kd_s[ch, rr, :] = chn["kd"]
                in_s[ch, chn["ci"]] = chn["intra"]
                eg_s[chn["ci"], ch:ch + 1, :] = jnp.broadcast_to(chn["eg"], (1, LANE))
        for sub in range(ncs):
            cum_chunk(nxt_blk * ncs + sub)
        return carry

    lax.fori_loop(0, n // ncs, local_body, 0)

    for d in range(2):
        for hh in range(nh):
            st_s[d * nh + hh] = s0_ref[0, d, hh]
    o_ref[...] = jnp.zeros_like(o_ref)

    def scan_body(i, carry):
        chains = []
        for d in range(2):
            ci = i if d == 0 else n - 1 - i
            rr = pl.ds(pl.multiple_of(ci * c, c), c)
            eg = eg_s[ci]
            for hh in range(nh):
                ch = d * nh + hh
                chains.append(dict(u=u_s[ch, rr, :], wv=wv_s[ch, rr, :], qe=qe_s[ch, rr, :], kd=kd_s[ch, rr, :],
                                   intra=in_s[ch, ci], eg=eg[ch:ch + 1, :], s=st_s[ch], rr=rr, hh=hh, ch=ch))
        for chn, (o, s_new) in zip(chains, _gdn_scan(chains)):
            st_s[chn["ch"]] = s_new
            cols = slice(chn["hh"] * LANE, (chn["hh"] + 1) * LANE)
            o_ref[0, chn["rr"], cols] = (o_ref[0, chn["rr"], cols].astype(F32) + o).astype(o_ref.dtype)
        return carry

    lax.fori_loop(0, n, scan_body, 0)
    for d in range(2):
        for hh in range(nh):
            sf_ref[0, d, hh] = st_s[d * nh + hh]


def _gdn(qkv, ab, abt, al, dt, alt, dtt, s0):
    b, t, _ = qkv.shape
    n = t // GDN_CHUNK
    per_b = lambda a: pl.BlockSpec((1,) + a.shape[1:], lambda i: (i,) + (0,) * (a.ndim - 1))
    full = lambda a: pl.BlockSpec(a.shape, lambda i: (0,) * a.ndim)
    return pl.pallas_call(
        functools.partial(_gdn_kernel, t=t), grid=(b,),
        in_specs=[per_b(qkv), per_b(ab), per_b(abt), full(al), full(dt), full(alt), full(dtt), per_b(s0)],
        out_specs=[pl.BlockSpec((1, t, GDN_HEADS * GDN_DV), lambda i: (i, 0, 0)), per_b(s0)],
        out_shape=[jax.ShapeDtypeStruct((b, t, GDN_HEADS * GDN_DV), BF16), jax.ShapeDtypeStruct(s0.shape, F32)],
        scratch_shapes=[pltpu.VMEM((2 * GDN_HEADS, t, LANE), BF16), pltpu.VMEM((2 * GDN_HEADS, t, LANE), BF16),
                        pltpu.VMEM((2 * GDN_HEADS, t, LANE), BF16), pltpu.VMEM((2 * GDN_HEADS, t, LANE), BF16),
                        pltpu.VMEM((2 * GDN_HEADS, n, GDN_CHUNK, GDN_CHUNK), BF16),
                        pltpu.VMEM((n, 2 * GDN_HEADS, LANE), F32),
                        pltpu.VMEM((t, 16), F32), pltpu.VMEM((n, 16, GDN_CHUNK), F32), pltpu.VMEM((t, 16), F32),
                        pltpu.VMEM((2 * GDN_HEADS, GDN_DK, GDN_DV), F32)],
        compiler_params=_cparams(1), name="gdn",
    )(qkv, ab, abt, al, dt, alt, dtt, s0)


def _attd_kernel(*refs, n_ctx):
    if n_ctx:
        q_ref, k_ref, vt_ref, kc_ref, vct_ref, o_ref = refs
    else:
        q_ref, k_ref, vt_ref, o_ref = refs
    n_new = k_ref.shape[1]
    grp = ATT_HEADS // ATT_KV_HEADS
    chunks = []
    for c0, cn in _key_chunks(n_new):
        chunks.append((
            lambda g, c0=c0, cn=cn: k_ref[0, c0:c0 + cn, g * LANE:(g + 1) * LANE],
            lambda g, c0=c0, cn=cn: vt_ref[0, g * LANE:(g + 1) * LANE, c0:c0 + cn],
            None))
    if n_ctx:
        for c0, cn in _key_chunks(n_ctx):
            chunks.append((
                lambda g, c0=c0, cn=cn: kc_ref[0, c0:c0 + cn, g * LANE:(g + 1) * LANE].astype(BF16),
                lambda g, c0=c0, cn=cn: vct_ref[0, g * LANE:(g + 1) * LANE, c0:c0 + cn].astype(BF16),
                None))
    qs = [q_ref[0, :, h * LANE:(h + 1) * LANE] for h in range(ATT_HEADS)]
    outs = _attend_heads(qs, [h // grp for h in range(ATT_HEADS)], chunks, ATT_DH)
    o_ref[0] = jnp.concatenate(outs, axis=0).T.astype(BF16)


def _attd(q, k, vt, k_c, v_ct, tq):
    b, t, _ = q.shape
    n_ctx = 0 if k_c is None else k_c.shape[1]
    per_b = lambda a: pl.BlockSpec((1,) + a.shape[1:], lambda i, j: (i, 0, 0))
    in_specs = [pl.BlockSpec((1, tq, 512), lambda i, j: (i, j, 0)), per_b(k), per_b(vt)]
    args = [q, k, vt]
    if n_ctx:
        in_specs += [per_b(k_c), per_b(v_ct)]
        args += [k_c, v_ct]
    return pl.pallas_call(
        functools.partial(_attd_kernel, n_ctx=n_ctx),
        grid=(b, t // tq), in_specs=in_specs,
        out_specs=pl.BlockSpec((1, tq, 512), lambda i, j: (i, j, 0)),
        out_shape=jax.ShapeDtypeStruct((b, t, 512), BF16),
        compiler_params=_cparams(2), name="attd_dec" if n_ctx else "attd_ctx",
    )(*args)


def _outproj1_kernel(oc_ref, zc_ref, od_ref, zd_ref, x_ref, mod_ref, gn_ref, w_ref, lnf_ref, y_ref):
    parts = []
    for j in range(GDN_HEADS):
        parts.append(_rms_rows(oc_ref[0, :, j * LANE:(j + 1) * LANE].astype(F32), gn_ref[...]).astype(BF16))
    gc = _silu_gate(jnp.concatenate(parts, axis=1), zc_ref[0])
    gd = _silu_gate(od_ref[0], zd_ref[0])
    y = _dot(gc, w_ref[0:512, :]) + _dot(gd, w_ref[512:1024, :])
    x2 = x_ref[0] + mod_ref[0, 2:3, :] * y
    y_ref[0] = _rms_rows(x2, lnf_ref[...])


def _outproj1(oc, zc, od, zd, x, mod, gn, w, lnf):
    b, t, d = x.shape
    tr = 1024 if t % 1024 == 0 else _row_tile(t)
    bm = mod.shape[0]
    rows = lambda c: pl.BlockSpec((1, tr, c), lambda i, j: (i, j, 0))
    full = lambda a: pl.BlockSpec(a.shape, lambda i, j: (0,) * a.ndim)
    return pl.pallas_call(
        _outproj1_kernel, grid=(b, t // tr),
        in_specs=[rows(512), rows(512), rows(512), rows(512), rows(d),
                  pl.BlockSpec((1, 3, d), (lambda i, j: (i, 0, 0)) if bm > 1 else (lambda i, j: (0, 0, 0))),
                  full(gn), full(w), full(lnf)],
        out_specs=rows(d), out_shape=jax.ShapeDtypeStruct((b, t, d), F32),
        compiler_params=_cparams(2), name="outproj1",
    )(oc, zc, od, zd, x, mod, gn, w, lnf)


def _rope_table(n_tok, rot_dim):
    quarter = rot_dim // 4
    inv = np.float32(ROPE_THETA) ** (-np.arange(quarter, dtype=np.float32) / np.float32(quarter))
    tt = np.arange(n_tok)
    pos = np.stack([tt // GRID_W, tt % GRID_W], axis=-1).astype(np.float32)
    ang = (pos[:, :, None] * inv).astype(np.float32)
    cos, sin = np.cos(ang), np.sin(ang)
    c = np.concatenate([cos, cos], axis=-1).reshape(n_tok, rot_dim)
    s = np.concatenate([-sin, sin], axis=-1).reshape(n_tok, rot_dim)
    return c.astype(np.float32), s.astype(np.float32)


def _place(tab, fill, off, width):
    out = np.full((tab.shape[0], width), fill, np.float32)
    out[:, off:off + tab.shape[1]] = tab
    return out


PACK_STEP = 256


def _pack_kernel(wt_ref, o_ref, *, segs):
    d = wt_ref.shape[1]
    lane = lax.broadcasted_iota(jnp.int32, (d, LANE), 1)
    off = 0
    for start, width, lane_off in segs:
        if width % LANE == 0:
            for c0 in range(0, width, PACK_STEP):
                cw = min(PACK_STEP, width - c0)
                o_ref[:, off + c0:off + c0 + cw] = wt_ref[start + c0:start + c0 + cw, :].T.astype(BF16)
            off += width
        else:
            blk = jnp.where(lane < width, wt_ref[start:start + LANE, :].T, 0.0)
            if lane_off:
                blk = pltpu.roll(blk, lane_off, 1)
            o_ref[:, off:off + LANE] = blk.astype(BF16)
            off += LANE


def _pack_cols(w, segs):
    d, n_in = w.shape
    n_out = sum(width if width % LANE == 0 else LANE for _, width, _ in segs)
    return pl.pallas_call(
        functools.partial(_pack_kernel, segs=segs), grid=(1,),
        in_specs=[pl.BlockSpec((n_in, d), lambda i: (0, 0))],
        out_specs=pl.BlockSpec((d, n_out), lambda i: (0, 0)),
        out_shape=jax.ShapeDtypeStruct((d, n_out), BF16),
        compiler_params=_cparams(1), name="pack_cols",
    )(w.T)


def _prep_l0(w_in0, w_uq, w_ukv):
    w = _pack_cols(w_in0, [(0, 640, 0), (640, MLA_ROPE, MLA_NOPE), (672, 1792, 0)])
    uq = w_uq.reshape(MLA_Q_LORA, MLA_HEADS, MLA_NOPE + MLA_ROPE)
    wuq = jnp.pad(uq, ((0, 0), (0, 0), (0, LANE - MLA_NOPE - MLA_ROPE))).reshape(MLA_Q_LORA, MLA_HEADS * LANE)
    ukv = w_ukv.reshape(MLA_KV_LORA, MLA_HEADS, MLA_NOPE + MLA_V)
    wuk = jnp.pad(ukv[:, :, :MLA_NOPE], ((0, 0), (0, 0), (0, LANE - MLA_NOPE))).reshape(MLA_KV_LORA, MLA_HEADS * LANE)
    wuvt = ukv[:, :, MLA_NOPE:].reshape(MLA_KV_LORA, MLA_HEADS * MLA_V).T
    return w, wuq.astype(BF16), wuk.astype(BF16), wuvt.astype(BF16)


def _prep_l1(w_in1):
    return _pack_cols(w_in1, [(0, 1536, 0), (1552, 2048, 0), (1536, 16, 0)])


def _chunk_rows(ab):
    b, t, c = ab.shape
    return jnp.swapaxes(ab.reshape(b, t // GDN_CHUNK, GDN_CHUNK, c), 2, 3)


def _trunk(x, mod, caches, p, tables, tq):
    dec = caches is not None
    t0m, t0s, t1 = tables if dec else (None, None, None)
    (qa, ckv, kr, za, qb, kb, vbt, zb, *ctx0) = _inproj0(
        x, mod[0], p["ln0"], p["w0"], p["qn"], p["wuq"], p["kvn"], (t0m + t0s) if dec else None)
    if dec:
        ckv_c, kr_c, kb_c, vb_ct, s0, kd_c, vd_ct = caches
    else:
        ckv_c = kr_c = kb_c = vb_ct = kd_c = vd_ct = None
        s0 = jnp.zeros((x.shape[0], 2, GDN_HEADS, GDN_DK, GDN_DV), F32)
    oa = _mla(qa, ckv, kr, ckv_c, kr_c, p["wuk"], p["wuvt"], tq)
    ob = _swa(p["sink"], qb, kb, vbt, kb_c, vb_ct, tq)
    (x1, qkv, zc, qd, kd, vdt, zd, ab, *ctx1) = _inproj1(
        oa, za, ob, zb, x, mod[0], p["wout0"], mod[1], p["ln1"], p["w1"], p["cw"], p["aqn"], p["akn"],
        t1 if dec else None)
    oc, sfin = _gdn(qkv, ab, _chunk_rows(ab), p["al"], p["dt"], p["alt"], p["dtt"], s0)
    od = _attd(qd, kd, vdt, kd_c, vd_ct, tq)
    y = _outproj1(oc, zc, od, zd, x1, mod[1], p["gn"], p["wout1"], p["lnf"])
    return y, ctx0, sfin, ctx1


def kernel(x_prompt, x_sample, cache_l0_mla_ckv, cache_l0_mla_krope, cache_l0_swa_k, cache_l0_swa_v,
           state_l1_gdn, cache_l1_attn_k, cache_l1_attn_v, c, c_ctx,
           w_mod0, b_mod0, ln0, w_in0, mla_q_norm, w_uq, mla_kv_norm, w_ukv, swa_sink, w_out0,
           w_mod1, b_mod1, ln1, w_in1, gdn_conv, gdn_a_log, gdn_dt_bias, gdn_norm, att_q_norm, att_k_norm, w_out1,
           ln_f):
    d = x_prompt.shape[-1]
    bd, td = x_sample.shape[:2]
    bc, tc = x_prompt.shape[:2]
    past = cache_l0_mla_ckv.shape[1]
    row = lambda v: v.reshape(1, -1)
    w0, wuq, wuk, wuvt = _prep_l0(w_in0, w_uq, w_ukv)
    w1 = _prep_l1(w_in1)
    al8 = gdn_a_log.reshape(1, 2 * GDN_HEADS)
    dt8 = gdn_dt_bias.reshape(1, 2 * GDN_HEADS)
    al16 = jnp.pad(al8, ((0, 0), (0, 8)))
    dt16 = jnp.pad(dt8, ((0, 0), (0, 8)))
    p = dict(ln0=row(ln0), w0=w0, qn=row(mla_q_norm), wuq=wuq, kvn=row(mla_kv_norm), wuk=wuk, wuvt=wuvt,
             sink=swa_sink, wout0=w_out0.astype(BF16), ln1=row(ln1), w1=w1, aqn=row(att_q_norm),
             akn=row(att_k_norm), cw=gdn_conv, al=al16, dt=dt16, alt=al16.T, dtt=dt16.T, gn=row(gdn_norm),
             wout1=w_out1.astype(BF16), lnf=row(ln_f))
    n_rows = -(-(bd + 1) // 8) * 8
    c_rows = jnp.concatenate([c, c_ctx[None, :], jnp.zeros((n_rows - bd - 1, d), F32)], axis=0)
    mods = [_mod(c_rows, w_mod0, b_mod0), _mod(c_rows, w_mod1, b_mod1)]
    mod_dec = [m[:bd].reshape(bd, 3, d) for m in mods]
    mod_ctx = [m[bd:bd + 1].reshape(1, 3, d) for m in mods]
    cm, sm = _rope_table(td, MLA_ROPE)
    t0m = (jnp.asarray(_place(cm, 1.0, MLA_NOPE, LANE)), jnp.asarray(_place(sm, 0.0, MLA_NOPE, LANE)))
    cs, ss = _rope_table(td, SWA_DH)
    t0s = (jnp.asarray(np.tile(cs, (1, LANE // SWA_DH))), jnp.asarray(np.tile(ss, (1, LANE // SWA_DH))))
    t1 = tuple(jnp.asarray(a) for a in _rope_table(td, ATT_DH))
    caches = (cache_l0_mla_ckv,
              jnp.pad(cache_l0_mla_krope, ((0, 0), (0, 0), (MLA_NOPE, LANE - MLA_NOPE - MLA_ROPE))),
              cache_l0_swa_k.reshape(bd, past, SWA_KV_HEADS * SWA_DH),
              jnp.swapaxes(cache_l0_swa_v.reshape(bd, past, SWA_KV_HEADS * SWA_DH), 1, 2),
              state_l1_gdn,
              cache_l1_attn_k.reshape(bd, past, ATT_KV_HEADS * ATT_DH),
              jnp.swapaxes(cache_l1_attn_v.reshape(bd, past, ATT_KV_HEADS * ATT_DH), 1, 2))
    y_prompt, ctx0, sfin, ctx1 = _trunk(x_prompt, mod_ctx, None, p, None, tq=tc)
    y_sample, _, _, _ = _trunk(x_sample, mod_dec, caches, p, (t0m, t0s, t1), tq=256)
    ckv32, kr32, kb32, vb32 = ctx0
    kd32, vd32 = ctx1
    return (y_prompt, y_sample, ckv32, kr32,
            kb32.reshape(bc, tc, SWA_KV_HEADS, SWA_DH), vb32.reshape(bc, tc, SWA_KV_HEADS, SWA_DH),
            sfin, kd32.reshape(bc, tc, ATT_KV_HEADS, ATT_DH), vd32.reshape(bc, tc, ATT_KV_HEADS, ATT_DH))
```

```python
import functools
import math

import jax
import jax.numpy as jnp
import numpy as np
from jax import lax
from jax.experimental import pallas as pl
from jax.experimental.pallas import tpu as pltpu

F32 = jnp.float32
BF16 = jnp.bfloat16

GRID_W = 64
ROPE_THETA = 10000.0
EPS = 1e-6
WINDOW = 128
MLA_HEADS, MLA_NOPE, MLA_ROPE, MLA_V = 8, 64, 32, 64
MLA_Q_LORA, MLA_KV_LORA = 384, 256
SWA_HEADS, SWA_KV_HEADS, SWA_DH = 8, 2, 64
GDN_HEADS, GDN_DK, GDN_DV, CONV_K, GDN_CHUNK = 4, 128, 128, 3, 64
GDN_LOCAL_CHUNKS = 4
ATT_HEADS, ATT_KV_HEADS, ATT_DH = 4, 2, 128
LANE = 128
LOG2E = math.log2(math.e)
NEG = -1e30
VMEM_LIMIT = 56 * 1024 * 1024


def _cparams(n_axes):
    return pltpu.CompilerParams(dimension_semantics=("arbitrary",) * n_axes, vmem_limit_bytes=VMEM_LIMIT)


def _dot(a, b):
    return jnp.dot(a, b, preferred_element_type=F32)


def _dot_nt(a, b):
    return lax.dot_general(a, b, (((1,), (1,)), ((), ())), preferred_element_type=F32)


def _dot_tn(a, b):
    return lax.dot_general(a, b, (((0,), (0,)), ((), ())), preferred_element_type=F32)


def _dot_exact(a, b):
    return jnp.dot(a, b, preferred_element_type=F32, precision=lax.Precision.HIGHEST)


def _split(a):
    hi = a.astype(BF16)
    return hi, (a - hi.astype(F32)).astype(BF16)


def _silu(x):
    return x * jax.nn.sigmoid(x)


def _rms_rows(x, g):
    return x * lax.rsqrt(jnp.mean(x * x, axis=-1, keepdims=True) + EPS) * g


def _rope_block(x, cos, sin, half):
    lane = lax.broadcasted_iota(jnp.int32, x.shape, 1)
    first = (lane // half) % 2 == 0
    partner = jnp.where(first, pltpu.roll(x, LANE - half, 1), pltpu.roll(x, half, 1))
    return x * cos + partner * sin


def _mod_kernel(c_ref, w_ref, b_ref, o_ref):
    a = _silu(c_ref[...]).astype(BF16)
    o_ref[...] = _dot(a, w_ref[...].astype(BF16)) + b_ref[...]


def _mod(c_rows, w_mod, b_mod):
    r, d = c_rows.shape
    n = w_mod.shape[1]
    tn = 1024
    return pl.pallas_call(
        _mod_kernel,
        grid=(n // tn,),
        in_specs=[pl.BlockSpec((r, d), lambda j: (0, 0)),
                  pl.BlockSpec((d, tn), lambda j: (0, j)),
                  pl.BlockSpec((1, tn), lambda j: (0, j))],
        out_specs=pl.BlockSpec((r, tn), lambda j: (0, j)),
        out_shape=jax.ShapeDtypeStruct((r, n), F32),
        compiler_params=_cparams(1),
        name="mod",
    )(c_rows, w_mod, b_mod.reshape(1, n))


def _adaln(x, mod_ref, ln_ref):
    h = _rms_rows(x, ln_ref[...])
    return h * (1.0 + mod_ref[0, 1:2, :]) + mod_ref[0, 0:1, :]


L0_OFF = dict(cq=0, ckv=384, kr=640, za=768, qb=1280, kb=1792, vb=1920, zb=2048)
L0_W = 2560


def _inproj0_kernel(*refs, rope):
    if rope:
        (x_ref, mod_ref, ln_ref, w_ref, qn_ref, wuq_ref, kvn_ref, cm_ref, sm_ref, cs_ref, ss_ref,
         qa_ref, ckv_ref, kr_ref, za_ref, qb_ref, kb_ref, vbt_ref, zb_ref) = refs
    else:
        (x_ref, mod_ref, ln_ref, w_ref, qn_ref, wuq_ref, kvn_ref,
         qa_ref, ckv_ref, kr_ref, za_ref, qb_ref, kb_ref, vbt_ref, zb_ref,
         ckv32_ref, kr32_ref, kb32_ref, vb32_ref) = refs
    h = _adaln(x_ref[0], mod_ref, ln_ref).astype(BF16)
    u = _dot(h, w_ref[...])
    o = L0_OFF
    cq = _rms_rows(u[:, o["cq"]:o["cq"] + 384], qn_ref[...]).astype(BF16)
    qa = _dot(cq, wuq_ref[...])
    ckv = _rms_rows(u[:, o["ckv"]:o["ckv"] + 256], kvn_ref[...])
    kr = u[:, o["kr"]:o["kr"] + 128]
    qb = u[:, o["qb"]:o["qb"] + 512]
    kb = u[:, o["kb"]:o["kb"] + 128]
    vb = u[:, o["vb"]:o["vb"] + 128]
    if not rope:
        ckv32_ref[0] = ckv
        kr32_ref[0] = kr[:, 64:96]
        kb32_ref[0] = kb
        vb32_ref[0] = vb
    qa_scale = (MLA_NOPE + MLA_ROPE) ** -0.5 * LOG2E
    qb_scale = SWA_DH ** -0.5 * LOG2E
    for j in range(MLA_HEADS):
        blk = qa[:, j * LANE:(j + 1) * LANE]
        if rope:
            blk = _rope_block(blk, cm_ref[...], sm_ref[...], MLA_ROPE // 4)
        qa_ref[0, :, j * LANE:(j + 1) * LANE] = (blk * qa_scale).astype(BF16)
    for j in range(SWA_HEADS * SWA_DH // LANE):
        blk = qb[:, j * LANE:(j + 1) * LANE]
        if rope:
            blk = _rope_block(blk, cs_ref[...], ss_ref[...], SWA_DH // 4)
        qb_ref[0, :, j * LANE:(j + 1) * LANE] = (blk * qb_scale).astype(BF16)
    if rope:
        kr = _rope_block(kr, cm_ref[...], sm_ref[...], MLA_ROPE // 4)
        kb = _rope_block(kb, cs_ref[...], ss_ref[...], SWA_DH // 4)
    ckv_ref[0] = ckv.astype(BF16)
    kr_ref[0] = kr.astype(BF16)
    kb_ref[0] = kb.astype(BF16)
    za_ref[0] = u[:, o["za"]:o["za"] + 512].astype(BF16)
    zb_ref[0] = u[:, o["zb"]:o["zb"] + 512].astype(BF16)
    vbt_ref[0] = vb.T.astype(BF16)


def _row_tile(t):
    return 512 if t % 512 == 0 else 256


def _inproj0(x, mod, ln, w, qn, wuq, kvn, tables):
    b, t, d = x.shape
    tr = _row_tile(t)
    rope = tables is not None
    bm = mod.shape[0]
    full = lambda a: pl.BlockSpec(a.shape, lambda i, j: (0,) * a.ndim)
    rows = lambda c: pl.BlockSpec((1, tr, c), lambda i, j: (i, j, 0))
    in_specs = [rows(d), pl.BlockSpec((1, 3, d), (lambda i, j: (i, 0, 0)) if bm > 1 else (lambda i, j: (0, 0, 0))),
                full(ln), full(w), full(qn), full(wuq), full(kvn)]
    args = [x, mod, ln, w, qn, wuq, kvn]
    if rope:
        for tab in tables:
            in_specs.append(pl.BlockSpec((tr, LANE), lambda i, j: (j, 0)))
            args.append(tab)
    out_shape = [jax.ShapeDtypeStruct((b, t, 1024), BF16), jax.ShapeDtypeStruct((b, t, 256), BF16),
                 jax.ShapeDtypeStruct((b, t, 128), BF16), jax.ShapeDtypeStruct((b, t, 512), BF16),
                 jax.ShapeDtypeStruct((b, t, 512), BF16), jax.ShapeDtypeStruct((b, t, 128), BF16),
                 jax.ShapeDtypeStruct((b, 128, t), BF16), jax.ShapeDtypeStruct((b, t, 512), BF16)]
    out_specs = [rows(1024), rows(256), rows(128), rows(512), rows(512), rows(128),
                 pl.BlockSpec((1, 128, tr), lambda i, j: (i, 0, j)), rows(512)]
    if not rope:
        out_shape += [jax.ShapeDtypeStruct((b, t, 256), F32), jax.ShapeDtypeStruct((b, t, 32), F32),
                      jax.ShapeDtypeStruct((b, t, 128), F32), jax.ShapeDtypeStruct((b, t, 128), F32)]
        out_specs += [rows(256), rows(32), rows(128), rows(128)]
    return pl.pallas_call(
        functools.partial(_inproj0_kernel, rope=rope),
        grid=(b, t // tr), in_specs=in_specs, out_specs=out_specs, out_shape=out_shape,
        compiler_params=_cparams(2), name="inproj0_dec" if rope else "inproj0_ctx",
    )(*args)


KEY_CHUNK = 512
ATT_LOOKAHEAD = 4


SUM_ROWS = 16


def _attend_heads(qs, kv_of, chunks, dv, sinks=None):
    nh = len(qs)
    tq = qs[0].shape[0]
    m = [None] * nh
    acc = [None] * nh
    items = [(ci, i) for ci in range(len(chunks)) for i in range(nh)]
    loaded = {}

    def kv(ci, src):
        if (ci, src) not in loaded:
            vt = chunks[ci][1](src)
            ones = (lax.broadcasted_iota(jnp.int32, (SUM_ROWS, vt.shape[1]), 0) == 0).astype(BF16)
            loaded[(ci, src)] = (chunks[ci][0](src), jnp.concatenate([vt, ones], axis=0))
        return loaded[(ci, src)]

    if sinks is not None:
        m = [jnp.full((1, tq), sk, F32) for sk in sinks]
        unit = (lax.broadcasted_iota(jnp.int32, (dv + SUM_ROWS, tq), 0) == dv).astype(F32)
        acc = [unit for _ in sinks]

    scores = {}
    for t in range(len(items) + ATT_LOOKAHEAD):
        if t < len(items):
            ci, i = items[t]
            scores[t] = _dot_nt(kv(ci, kv_of[i])[0], qs[i])
        t0 = t - ATT_LOOKAHEAD
        if t0 < 0:
            continue
        ci, i = items[t0]
        mask = chunks[ci][2]
        si = scores.pop(t0)
        if mask is not None:
            si = jnp.where(mask, si, NEG)
        cm = si.max(axis=0, keepdims=True)
        alpha = None
        if m[i] is None:
            m_new = cm
        else:
            m_new = jnp.maximum(m[i], cm)
            alpha = jnp.exp2(m[i] - m_new)
        p = jnp.exp2(si - m_new)
        m[i] = m_new
        pv = _dot(kv(ci, kv_of[i])[1], p.astype(BF16))
        acc[i] = pv if acc[i] is None else acc[i] * alpha + pv
    return [acc[i][:dv] * (1.0 / acc[i][dv:dv + 1]) for i in range(nh)]


def _key_chunks(n):
    step = KEY_CHUNK if n % KEY_CHUNK == 0 else n
    return [(c0, step) for c0 in range(0, n, step)]


def _mla_kernel(*refs, n_new, n_ctx, hp):
    if n_ctx:
        q_ref, ckv_ref, kr_ref, ckvc_ref, krc_ref, wuk_ref, wuvt_ref, o_ref, k_s, vt_s = refs
    else:
        q_ref, ckv_ref, kr_ref, wuk_ref, wuvt_ref, o_ref, k_s, vt_s = refs
    qi, gi = pl.program_id(1), pl.program_id(2)

    @pl.when((qi == 0) & (gi == 0))
    def _():
        def expand(ckv, kr, r0, n):
            kn = _dot(ckv, wuk_ref[...])
            for j in range(MLA_HEADS):
                k_s[j, r0:r0 + n, :] = (kn[:, j * LANE:(j + 1) * LANE] + kr).astype(BF16)
            vt_s[:, r0:r0 + n] = _dot_nt(wuvt_ref[...], ckv).astype(BF16)

        blk = 512 if n_new % 512 == 0 else 256
        for r0 in range(0, n_new, blk):
            expand(ckv_ref[0, r0:r0 + blk, :], kr_ref[0, r0:r0 + blk, :].astype(F32), r0, blk)
        if n_ctx:
            expand(ckvc_ref[0].astype(BF16), krc_ref[0], n_new, n_ctx)

    qs = [q_ref[0, :, j * LANE:(j + 1) * LANE] for j in range(hp)]
    chunks = []
    for c0, cn in _key_chunks(n_new + n_ctx):
        chunks.append((
            lambda j, c0=c0, cn=cn: k_s[gi * hp + j, c0:c0 + cn, :],
            lambda j, c0=c0, cn=cn: vt_s[pl.ds(pl.multiple_of((gi * hp + j) * MLA_V, MLA_V), MLA_V), c0:c0 + cn],
            None))
    outs = _attend_heads(qs, list(range(hp)), chunks, MLA_V)
    o_ref[0] = jnp.concatenate(outs, axis=0).T.astype(BF16)


def _mla(q, ckv, kr, ckv_c, kr_c, wuk, wuvt, tq):
    b, t, _ = q.shape
    n_ctx = 0 if ckv_c is None else ckv_c.shape[1]
    hp = 8
    tk = t + n_ctx
    rows_q = pl.BlockSpec((1, tq, hp * LANE), lambda i, j, g: (i, j, g))
    per_b = lambda a: pl.BlockSpec((1,) + a.shape[1:], lambda i, j, g: (i, 0, 0))
    full = lambda a: pl.BlockSpec(a.shape, lambda i, j, g: (0, 0))
    in_specs = [rows_q, per_b(ckv), per_b(kr)]
    args = [q, ckv, kr]
    if n_ctx:
        in_specs += [per_b(ckv_c), per_b(kr_c)]
        args += [ckv_c, kr_c]
    in_specs += [full(wuk), full(wuvt)]
    args += [wuk, wuvt]
    return pl.pallas_call(
        functools.partial(_mla_kernel, n_new=t, n_ctx=n_ctx, hp=hp),
        grid=(b, t // tq, MLA_HEADS // hp), in_specs=in_specs,
        out_specs=pl.BlockSpec((1, tq, hp * MLA_V), lambda i, j, g: (i, j, g)),
        out_shape=jax.ShapeDtypeStruct((b, t, MLA_HEADS * MLA_V), BF16),
        scratch_shapes=[pltpu.VMEM((MLA_HEADS, tk, LANE), BF16), pltpu.VMEM((MLA_HEADS * MLA_V, tk), BF16)],
        compiler_params=_cparams(3), name="mla_dec" if n_ctx else "mla_ctx",
    )(*args)


def _swa_kernel(*refs, n_new, n_ctx, tq):
    if n_ctx:
        sink_ref, q_ref, k_ref, vt_ref, kc_ref, vct_ref, o_ref = refs
    else:
        sink_ref, q_ref, k_ref, vt_ref, o_ref = refs
    qi = pl.program_id(1)
    grp = SWA_HEADS // SWA_KV_HEADS
    if n_ctx:
        span = tq + 2 * WINDOW
        q0 = qi * tq
        start = pl.multiple_of(jnp.clip(q0 - WINDOW, 0, n_new - span), LANE)
        kpos = start + lax.broadcasted_iota(jnp.int32, (span, tq), 0)
        qpos = q0 + lax.broadcasted_iota(jnp.int32, (span, tq), 1)
        band = jnp.abs(kpos - qpos) <= WINDOW
    dh = SWA_DH
    chunks = []
    if n_ctx:
        for c0, cn in _key_chunks(span):
            chunks.append((
                lambda g, c0=c0, cn=cn: k_ref[0, pl.ds(start + c0, cn), g * dh:(g + 1) * dh],
                lambda g, c0=c0, cn=cn: vt_ref[0, g * dh:(g + 1) * dh, pl.ds(start + c0, cn)],
                band[c0:c0 + cn]))
        for c0, cn in _key_chunks(n_ctx):
            chunks.append((
                lambda g, c0=c0, cn=cn: kc_ref[0, c0:c0 + cn, g * dh:(g + 1) * dh].astype(BF16),
                lambda g, c0=c0, cn=cn: vct_ref[0, g * dh:(g + 1) * dh, c0:c0 + cn].astype(BF16),
                None))
    else:
        for c0, cn in _key_chunks(n_new):
            chunks.append((
                lambda g, c0=c0, cn=cn: k_ref[0, c0:c0 + cn, g * dh:(g + 1) * dh],
                lambda g, c0=c0, cn=cn: vt_ref[0, g * dh:(g + 1) * dh, c0:c0 + cn],
                None))
    qs = [q_ref[0, :, h * dh:(h + 1) * dh] for h in range(SWA_HEADS)]
    sinks = [sink_ref[h] * LOG2E for h in range(SWA_HEADS)]
    outs = _attend_heads(qs, [h // grp for h in range(SWA_HEADS)], chunks, SWA_DH, sinks)
    o_ref[0] = jnp.concatenate(outs, axis=0).T.astype(BF16)


def _swa(sink, q, k, vt, k_c, v_ct, tq):
    b, t, _ = q.shape
    n_ctx = 0 if k_c is None else k_c.shape[1]
    per_b = lambda a: pl.BlockSpec((1,) + a.shape[1:], lambda i, j: (i, 0, 0))
    in_specs = [pl.BlockSpec(memory_space=pltpu.SMEM), pl.BlockSpec((1, tq, 512), lambda i, j: (i, j, 0)),
                per_b(k), per_b(vt)]
    args = [sink, q, k, vt]
    if n_ctx:
        in_specs += [per_b(k_c), per_b(v_ct)]
        args += [k_c, v_ct]
    return pl.pallas_call(
        functools.partial(_swa_kernel, n_new=t, n_ctx=n_ctx, tq=tq),
        grid=(b, t // tq), in_specs=in_specs,
        out_specs=pl.BlockSpec((1, tq, 512), lambda i, j: (i, j, 0)),
        out_shape=jax.ShapeDtypeStruct((b, t, 512), BF16),
        compiler_params=_cparams(2), name="swa_dec" if n_ctx else "swa_ctx",
    )(*args)


def _silu_gate(o, z):
    return o * (z / (1.0 + jnp.exp(-z)))


L1_OFF = dict(qkv=0, zc=1536, qd=2048, kd=2560, vd=2816, zd=3072, ab=3584)
L1_W = 3712


def _inproj1_kernel(*refs, rope, nj):
    if rope:
        (oa_ref, za_ref, ob_ref, zb_ref, x_ref, mod0_ref, wo_ref, mod_ref, ln_ref, w_ref, cw_ref,
         qn_ref, kn_ref, c_ref, s_ref,
         x1_ref, qkv_ref, zc_ref, qd_ref, kd_ref, vdt_ref, zd_ref, ab_ref, us_ref, pr_ref) = refs
    else:
        (oa_ref, za_ref, ob_ref, zb_ref, x_ref, mod0_ref, wo_ref, mod_ref, ln_ref, w_ref, cw_ref,
         qn_ref, kn_ref,
         x1_ref, qkv_ref, zc_ref, qd_ref, kd_ref, vdt_ref, zd_ref, ab_ref, kd32_ref, vd32_ref, us_ref, pr_ref) = refs
    g = pl.program_id(0)
    tr = us_ref.shape[0]
    nq = GDN_HEADS * GDN_DK
    n_qkv = 2 * nq + GDN_HEADS * GDN_DV
    o = L1_OFF

    @pl.when(g == 0)
    def _():
        us_ref[...] = jnp.zeros_like(us_ref)
        pr_ref[...] = jnp.zeros_like(pr_ref)

    def conv_held_block(next_row):
        jb = lax.rem(g + nj - 1, nj)
        prv = jnp.where(jb > 0, pr_ref[7:8, :], 0.0)
        nxt = jnp.where(jb < nj - 1, next_row, 0.0)
        row8 = lax.broadcasted_iota(jnp.int32, (8, LANE), 0)
        for j in range(n_qkv // LANE):
            cols = slice(j * LANE, (j + 1) * LANE)
            xj = us_ref[:, cols]
            xp = pltpu.roll(xj, 1, 0)
            xp = jnp.concatenate([jnp.where(row8 == 0, prv[:, cols], xp[0:8]), xp[8:]], axis=0)
            xn = pltpu.roll(xj, tr - 1, 0)
            xn = jnp.concatenate([xn[:tr - 8], jnp.where(row8 == 7, nxt[:, cols], xn[tr - 8:])], axis=0)
            cw = cw_ref[:, cols]
            y = _silu(xp * cw[0:1, :] + xj * cw[1:2, :] + xn * cw[2:3, :])
            if j < 2 * nq // LANE:
                y = y * lax.rsqrt(jnp.sum(y * y, axis=-1, keepdims=True) + EPS)
            if j < nq // LANE:
                y = y * GDN_DK ** -0.5
            qkv_ref[0, :, cols] = y.astype(BF16)
        pr_ref[...] = us_ref[tr - 8:tr, 0:n_qkv]
        zc_ref[0] = us_ref[:, o["zc"]:o["zc"] + 512].astype(BF16)
        zd_ref[0] = us_ref[:, o["zd"]:o["zd"] + 512].astype(BF16)
        ab_ref[0] = us_ref[:, o["ab"]:o["ab"] + 16]
        qd_scale = ATT_DH ** -0.5 * LOG2E
        for j in range(ATT_HEADS):
            blk = _rms_rows(us_ref[:, o["qd"] + j * LANE:o["qd"] + (j + 1) * LANE], qn_ref[...])
            if rope:
                blk = _rope_block(blk, c_ref[...], s_ref[...], ATT_DH // 4)
            qd_ref[0, :, j * LANE:(j + 1) * LANE] = (blk * qd_scale).astype(BF16)
        for j in range(ATT_KV_HEADS):
            blk = _rms_rows(us_ref[:, o["kd"] + j * LANE:o["kd"] + (j + 1) * LANE], kn_ref[...])
            if rope:
                blk = _rope_block(blk, c_ref[...], s_ref[...], ATT_DH // 4)
            else:
                kd32_ref[0, :, j * LANE:(j + 1) * LANE] = blk
            kd_ref[0, :, j * LANE:(j + 1) * LANE] = blk.astype(BF16)
        vd = us_ref[:, o["vd"]:o["vd"] + 256]
        if not rope:
            vd32_ref[0] = vd
        vdt_ref[0] = vd.T.astype(BF16)

    @pl.when(g == pl.num_programs(0) - 1)
    def _():
        conv_held_block(jnp.zeros((1, n_qkv), F32))

    @pl.when(g < pl.num_programs(0) - 1)
    def _():
        _inproj1_block(refs, rope, conv_held_block)


def _inproj1_block(refs, rope, conv_held_block):
    if rope:
        (oa_ref, za_ref, ob_ref, zb_ref, x_ref, mod0_ref, wo_ref, mod_ref, ln_ref, w_ref, cw_ref,
         qn_ref, kn_ref, c_ref, s_ref,
         x1_ref, qkv_ref, zc_ref, qd_ref, kd_ref, vdt_ref, zd_ref, ab_ref, us_ref, pr_ref) = refs
    else:
        (oa_ref, za_ref, ob_ref, zb_ref, x_ref, mod0_ref, wo_ref, mod_ref, ln_ref, w_ref, cw_ref,
         qn_ref, kn_ref,
         x1_ref, qkv_ref, zc_ref, qd_ref, kd_ref, vdt_ref, zd_ref, ab_ref, kd32_ref, vd32_ref, us_ref, pr_ref) = refs
    n_qkv = pr_ref.shape[1]
    y0 = (_dot(_silu_gate(oa_ref[0], za_ref[0]), wo_ref[0:512, :])
          + _dot(_silu_gate(ob_ref[0], zb_ref[0]), wo_ref[512:1024, :]))
    x1 = x_ref[0] + mod0_ref[0, 2:3, :] * y0
    x1_ref[0] = x1
    h = _adaln(x1, mod_ref, ln_ref).astype(BF16)
    conv_held_block(_dot(h[0:8], w_ref[:, 0:n_qkv])[0:1])
    us_ref[...] = _dot(h, w_ref[...])


def _inproj1(oa, za, ob, zb, x, mod0, wo, mod, ln, w, cw, qn, kn, tables):
    b, t, d = x.shape
    tr = _row_tile(t)
    nj = t // tr
    n_blocks = b * nj
    rope = tables is not None
    bm = mod.shape[0]
    n_qkv = cw.shape[1]
    cur = lambda g: jnp.minimum(g, n_blocks - 1)
    held = lambda g: jnp.maximum(g - 1, 0)
    full = lambda a: pl.BlockSpec(a.shape, lambda g: (0,) * a.ndim)
    rows = lambda c: pl.BlockSpec((1, tr, c), lambda g: (cur(g) // nj, cur(g) % nj, 0))
    mod_spec = pl.BlockSpec((1, 3, d), (lambda g: (cur(g) // nj, 0, 0)) if bm > 1 else (lambda g: (0, 0, 0)))
    widths = [512, 512, 512, 512, d]
    in_specs = ([rows(c) for c in widths]
                + [mod_spec, full(wo), mod_spec, full(ln), full(w), full(cw), full(qn), full(kn)])
    args = [oa, za, ob, zb, x, mod0, wo, mod, ln, w, cw, qn, kn]
    hrows = lambda c: pl.BlockSpec((1, tr, c), lambda g: (held(g) // nj, held(g) % nj, 0))
    if rope:
        for tab in tables:
            in_specs.append(pl.BlockSpec((tr, LANE), lambda g: (held(g) % nj, 0)))
            args.append(tab)
    out_shape = [jax.ShapeDtypeStruct((b, t, d), F32),
                 jax.ShapeDtypeStruct((b, t, n_qkv), BF16), jax.ShapeDtypeStruct((b, t, 512), BF16),
                 jax.ShapeDtypeStruct((b, t, 512), BF16), jax.ShapeDtypeStruct((b, t, 256), BF16),
                 jax.ShapeDtypeStruct((b, 256, t), BF16), jax.ShapeDtypeStruct((b, t, 512), BF16),
                 jax.ShapeDtypeStruct((b, t, 16), F32)]
    out_specs = [rows(d), hrows(n_qkv), hrows(512), hrows(512), hrows(256),
                 pl.BlockSpec((1, 256, tr), lambda g: (held(g) // nj, 0, held(g) % nj)), hrows(512), hrows(16)]
    if not rope:
        out_shape += [jax.ShapeDtypeStruct((b, t, 256), F32), jax.ShapeDtypeStruct((b, t, 256), F32)]
        out_specs += [hrows(256), hrows(256)]
    return pl.pallas_call(
        functools.partial(_inproj1_kernel, rope=rope, nj=nj),
        grid=(n_blocks + 1,), in_specs=in_specs, out_specs=out_specs, out_shape=out_shape,
        scratch_shapes=[pltpu.VMEM((tr, L1_W), F32), pltpu.VMEM((8, n_qkv), F32)],
        compiler_params=_cparams(1), name="inproj1_dec" if rope else "inproj1_ctx",
    )(*args)


def _gdn_local(blocks):
    c = GDN_CHUNK
    row = lax.broadcasted_iota(jnp.int32, (c, c), 0)
    col = lax.broadcasted_iota(jnp.int32, (c, c), 1)
    lane2 = lax.broadcasted_iota(jnp.int32, (c, 2 * c), 1)
    eye = (row == col).astype(F32)
    eye_t = jnp.concatenate([eye, jnp.zeros((c, c), F32)], axis=1).astype(BF16)
    chains = [ch for blk in blocks for ch in blk["dirs"]]
    for blk in blocks:
        for ch in blk["dirs"]:
            causal = (row <= col) if ch["upper"] else (row >= col)
            ch["strict"] = (row < col) if ch["upper"] else (row > col)
            ch["decay"] = jnp.exp(jnp.where(causal, ch["gc_col"] - ch["gc_row"], -jnp.inf))
            ch["kb"] = blk["k"] * ch["beta_col"]
            ch["egc"] = jnp.exp(ch["gc_col"])
    for blk in blocks:
        lhs = jnp.concatenate([ch["kb"] for ch in blk["dirs"]] + [blk["q"]], axis=0).astype(BF16)
        a = _dot_nt(lhs, blk["k"].astype(BF16))
        nd = len(blk["dirs"])
        for di, ch in enumerate(blk["dirs"]):
            x = jnp.where(ch["strict"], -(a[di * c:(di + 1) * c] * ch["decay"]), 0.0)
            ch["intra"] = (a[nd * c:] * ch["decay"]).astype(BF16)
            ch["w"] = jnp.concatenate([eye, x], axis=1)
    for _ in range(6):
        for ch in chains:
            w = ch["w"]
            wh = w.astype(BF16)
            lo = w - wh.astype(F32)
            php = jnp.where(lane2 < c, pltpu.roll(w, c, 1), lo).astype(BF16)
            ch["w"] = _dot(jnp.concatenate([wh, php], axis=1),
                           jnp.concatenate([eye_t, wh, lo.astype(BF16), wh], axis=0))
    for blk in blocks:
        for ch in blk["dirs"]:
            rhs = jnp.concatenate([blk["v"] * ch["beta_col"], ch["kb"] * ch["egc"]], axis=1).astype(BF16)
            sol = _dot(ch["w"][:, :c].astype(BF16), rhs)
            ch["u"], ch["wv"] = sol[:, :GDN_DV].astype(BF16), sol[:, GDN_DV:].astype(BF16)
            ch["qe"] = (blk["q"] * ch["egc"]).astype(BF16)
            ch["kd"] = (blk["k"] * jnp.exp(ch["glast"] - ch["gc_col"])).astype(BF16)
            ch["eg"] = jnp.exp(ch["glast"])


def _gdn_scan(chains):
    c = GDN_CHUNK
    for ch in chains:
        ch["sb"] = ch["s"].astype(BF16)
    for ch in chains:
        r = _dot(jnp.concatenate([ch["wv"], ch["qe"]], axis=0), ch["sb"])
        ch["vn"] = (ch["u"].astype(F32) - r[:c]).astype(BF16)
        ch["qs"] = r[c:]
    outs = []
    for ch in chains:
        o = ch["qs"] + _dot(ch["intra"], ch["vn"])
        s_new = ch["s"] * ch["eg"] + _dot_tn(ch["kd"], ch["vn"])
        outs.append((o, s_new))
    return outs


def _gdn_kernel(qkv_ref, ab_ref, abt_ref, al_ref, dt_ref, alt_ref, dtt_ref, s0_ref, o_ref, sf_ref,
                u_s, wv_s, qe_s, kd_s, in_s, eg_s, gcol_s, grow_s, beta_s, st_s, *, t):
    c = GDN_CHUNK
    n = t // c
    nh = GDN_HEADS
    ab = ab_ref[0]
    gact = -jnp.exp(al_ref[...]) * jax.nn.softplus(ab + dt_ref[...])
    lane16 = lax.broadcasted_iota(jnp.int32, ab.shape, 1)
    beta_s[...] = jnp.where(lane16 < 2 * nh, gact, jax.nn.sigmoid(ab))
    r64 = lax.broadcasted_iota(jnp.int32, (c, c), 0)
    c64 = lax.broadcasted_iota(jnp.int32, (c, c), 1)
    tril = (r64 >= c64).astype(F32)
    triu = (r64 <= c64).astype(F32)
    lane_c = lax.broadcasted_iota(jnp.int32, (c, 16), 1)
    sub_c = lax.broadcasted_iota(jnp.int32, (16, c), 0)

    def cum_chunk(i):
        r0 = pl.multiple_of(i * c, c)
        g = beta_s[pl.ds(r0, c), :]
        gcol_s[pl.ds(r0, c), :] = jnp.where(lane_c < nh, _dot_exact(tril, g), _dot_exact(triu, g))
        gt = -jnp.exp(alt_ref[...]) * jax.nn.softplus(abt_ref[0, i] + dtt_ref[...])
        grow_s[i] = jnp.where(sub_c < nh, _dot_exact(gt, triu), _dot_exact(gt, tril))

    ncs = GDN_LOCAL_CHUNKS
    rb = ncs * c
    for sub in range(ncs):
        cum_chunk(sub)

    def local_body(jb, carry):
        r0 = pl.multiple_of(jb * rb, rb)
        nxt_blk = jnp.minimum(jb + 1, n // ncs - 1)
        xs = [qkv_ref[0, pl.ds(r0, rb), j * LANE:(j + 1) * LANE].astype(F32) for j in range(3 * nh)]
        gcol = gcol_s[pl.ds(r0, rb), :]
        bet = beta_s[pl.ds(r0, rb), :]
        blocks = []
        for sub in range(ncs):
            ci = ncs * jb + sub
            rows = slice(sub * c, (sub + 1) * c)
            grow = grow_s[ci]
            for hh in range(nh):
                dirs = []
                for d in range(2):
                    ch = d * nh + hh
                    last = sub * c + (c - 1 if d == 0 else 0)
                    dirs.append(dict(gc_col=gcol[rows, ch:ch + 1], gc_row=grow[ch:ch + 1, :],
                                     beta_col=bet[rows, 2 * nh + ch:2 * nh + ch + 1],
                                     glast=gcol[last:last + 1, ch:ch + 1], upper=(d == 1), ch=ch, ci=ci,
                                     r0=r0 + sub * c))
                blocks.append(dict(q=xs[hh][rows], k=xs[nh + hh][rows], v=xs[2 * nh + hh][rows], dirs=dirs))
        _gdn_local(blocks)
        for blk in blocks:
            for chn in blk["dirs"]:
                ch, rr = chn["ch"], pl.ds(pl.multiple_of(chn["r0"], c), c)
                u_s[ch, rr, :] = chn["u"]
                wv_s[ch, rr, :] = chn["wv"]
                qe_s[ch, rr, :] = chn["qe"]
                kd_s[ch, rr, :] = chn["kd"]
                in_s[ch, chn["ci"]] = chn["intra"]
                eg_s[chn["ci"], ch:ch + 1, :] = jnp.broadcast_to(chn["eg"], (1, LANE))
        for sub in range(ncs):
            cum_chunk(nxt_blk * ncs + sub)
        return carry

    lax.fori_loop(0, n // ncs, local_body, 0)

    for d in range(2):
        for hh in range(nh):
            st_s[d * nh + hh] = s0_ref[0, d, hh]
    o_ref[...] = jnp.zeros_like(o_ref)

    def scan_body(i, carry):
        chains = []
        for d in range(2):
            ci = i if d == 0 else n - 1 - i
            rr = pl.ds(pl.multiple_of(ci * c, c), c)
            eg = eg_s[ci]
            for hh in range(nh):
                ch = d * nh + hh
                chains.append(dict(u=u_s[ch, rr, :], wv=wv_s[ch, rr, :], qe=qe_s[ch, rr, :], kd=kd_s[ch, rr, :],
                                   intra=in_s[ch, ci], eg=eg[ch:ch + 1, :], s=st_s[ch], rr=rr, hh=hh, ch=ch))
        for chn, (o, s_new) in zip(chains, _gdn_scan(chains)):
            st_s[chn["ch"]] = s_new
            cols = slice(chn["hh"] * LANE, (chn["hh"] + 1) * LANE)
            o_ref[0, chn["rr"], cols] = (o_ref[0, chn["rr"], cols].astype(F32) + o).astype(o_ref.dtype)
        return carry

    lax.fori_loop(0, n, scan_body, 0)
    for d in range(2):
        for hh in range(nh):
            sf_ref[0, d, hh] = st_s[d * nh + hh]


def _gdn(qkv, ab, abt, al, dt, alt, dtt, s0):
    b, t, _ = qkv.shape
    n = t // GDN_CHUNK
    per_b = lambda a: pl.BlockSpec((1,) + a.shape[1:], lambda i: (i,) + (0,) * (a.ndim - 1))
    full = lambda a: pl.BlockSpec(a.shape, lambda i: (0,) * a.ndim)
    return pl.pallas_call(
        functools.partial(_gdn_kernel, t=t), grid=(b,),
        in_specs=[per_b(qkv), per_b(ab), per_b(abt), full(al), full(dt), full(alt), full(dtt), per_b(s0)],
        out_specs=[pl.BlockSpec((1, t, GDN_HEADS * GDN_DV), lambda i: (i, 0, 0)), per_b(s0)],
        out_shape=[jax.ShapeDtypeStruct((b, t, GDN_HEADS * GDN_DV), BF16), jax.ShapeDtypeStruct(s0.shape, F32)],
        scratch_shapes=[pltpu.VMEM((2 * GDN_HEADS, t, LANE), BF16), pltpu.VMEM((2 * GDN_HEADS, t, LANE), BF16),
                        pltpu.VMEM((2 * GDN_HEADS, t, LANE), BF16), pltpu.VMEM((2 * GDN_HEADS, t, LANE), BF16),
                        pltpu.VMEM((2 * GDN_HEADS, n, GDN_CHUNK, GDN_CHUNK), BF16),
                        pltpu.VMEM((n, 2 * GDN_HEADS, LANE), F32),
                        pltpu.VMEM((t, 16), F32), pltpu.VMEM((n, 16, GDN_CHUNK), F32), pltpu.VMEM((t, 16), F32),
                        pltpu.VMEM((2 * GDN_HEADS, GDN_DK, GDN_DV), F32)],
        compiler_params=_cparams(1), name="gdn",
    )(qkv, ab, abt, al, dt, alt, dtt, s0)


def _attd_kernel(*refs, n_ctx):
    if n_ctx:
        q_ref, k_ref, vt_ref, kc_ref, vct_ref, o_ref = refs
    else:
        q_ref, k_ref, vt_ref, o_ref = refs
    n_new = k_ref.shape[1]
    grp = ATT_HEADS // ATT_KV_HEADS
    chunks = []
    for c0, cn in _key_chunks(n_new):
        chunks.append((
            lambda g, c0=c0, cn=cn: k_ref[0, c0:c0 + cn, g * LANE:(g + 1) * LANE],
            lambda g, c0=c0, cn=cn: vt_ref[0, g * LANE:(g + 1) * LANE, c0:c0 + cn],
            None))
    if n_ctx:
        for c0, cn in _key_chunks(n_ctx):
            chunks.append((
                lambda g, c0=c0, cn=cn: kc_ref[0, c0:c0 + cn, g * LANE:(g + 1) * LANE].astype(BF16),
                lambda g, c0=c0, cn=cn: vct_ref[0, g * LANE:(g + 1) * LANE, c0:c0 + cn].astype(BF16),
                None))
    qs = [q_ref[0, :, h * LANE:(h + 1) * LANE] for h in range(ATT_HEADS)]
    outs = _attend_heads(qs, [h // grp for h in range(ATT_HEADS)], chunks, ATT_DH)
    o_ref[0] = jnp.concatenate(outs, axis=0).T.astype(BF16)


def _attd(q, k, vt, k_c, v_ct, tq):
    b, t, _ = q.shape
    n_ctx = 0 if k_c is None else k_c.shape[1]
    per_b = lambda a: pl.BlockSpec((1,) + a.shape[1:], lambda i, j: (i, 0, 0))
    in_specs = [pl.BlockSpec((1, tq, 512), lambda i, j: (i, j, 0)), per_b(k), per_b(vt)]
    args = [q, k, vt]
    if n_ctx:
        in_specs += [per_b(k_c), per_b(v_ct)]
        args += [k_c, v_ct]
    return pl.pallas_call(
        functools.partial(_attd_kernel, n_ctx=n_ctx),
        grid=(b, t // tq), in_specs=in_specs,
        out_specs=pl.BlockSpec((1, tq, 512), lambda i, j: (i, j, 0)),
        out_shape=jax.ShapeDtypeStruct((b, t, 512), BF16),
        compiler_params=_cparams(2), name="attd_dec" if n_ctx else "attd_ctx",
    )(*args)


def _outproj1_kernel(oc_ref, zc_ref, od_ref, zd_ref, x_ref, mod_ref, gn_ref, w_ref, lnf_ref, y_ref):
    parts = []
    for j in range(GDN_HEADS):
        parts.append(_rms_rows(oc_ref[0, :, j * LANE:(j + 1) * LANE].astype(F32), gn_ref[...]).astype(BF16))
    gc = _silu_gate(jnp.concatenate(parts, axis=1), zc_ref[0])
    gd = _silu_gate(od_ref[0], zd_ref[0])
    y = _dot(gc, w_ref[0:512, :]) + _dot(gd, w_ref[512:1024, :])
    x2 = x_ref[0] + mod_ref[0, 2:3, :] * y
    y_ref[0] = _rms_rows(x2, lnf_ref[...])


def _outproj1(oc, zc, od, zd, x, mod, gn, w, lnf):
    b, t, d = x.shape
    tr = 1024 if t % 1024 == 0 else _row_tile(t)
    bm = mod.shape[0]
    rows = lambda c: pl.BlockSpec((1, tr, c), lambda i, j: (i, j, 0))
    full = lambda a: pl.BlockSpec(a.shape, lambda i, j: (0,) * a.ndim)
    return pl.pallas_call(
        _outproj1_kernel, grid=(b, t // tr),
        in_specs=[rows(512), rows(512), rows(512), rows(512), rows(d),
                  pl.BlockSpec((1, 3, d), (lambda i, j: (i, 0, 0)) if bm > 1 else (lambda i, j: (0, 0, 0))),
                  full(gn), full(w), full(lnf)],
        out_specs=rows(d), out_shape=jax.ShapeDtypeStruct((b, t, d), F32),
        compiler_params=_cparams(2), name="outproj1",
    )(oc, zc, od, zd, x, mod, gn, w, lnf)


def _rope_table(n_tok, rot_dim):
    quarter = rot_dim // 4
    inv = np.float32(ROPE_THETA) ** (-np.arange(quarter, dtype=np.float32) / np.float32(quarter))
    tt = np.arange(n_tok)
    pos = np.stack([tt // GRID_W, tt % GRID_W], axis=-1).astype(np.float32)
    ang = (pos[:, :, None] * inv).astype(np.float32)
    cos, sin = np.cos(ang), np.sin(ang)
    c = np.concatenate([cos, cos], axis=-1).reshape(n_tok, rot_dim)
    s = np.concatenate([-sin, sin], axis=-1).reshape(n_tok, rot_dim)
    return c.astype(np.float32), s.astype(np.float32)


def _place(tab, fill, off, width):
    out = np.full((tab.shape[0], width), fill, np.float32)
    out[:, off:off + tab.shape[1]] = tab
    return out


PACK_STEP = 256


def _pack_kernel(wt_ref, o_ref, *, segs):
    d = wt_ref.shape[1]
    lane = lax.broadcasted_iota(jnp.int32, (d, LANE), 1)
    off = 0
    for start, width, lane_off in segs:
        if width % LANE == 0:
            for c0 in range(0, width, PACK_STEP):
                cw = min(PACK_STEP, width - c0)
                o_ref[:, off + c0:off + c0 + cw] = wt_ref[start + c0:start + c0 + cw, :].T.astype(BF16)
            off += width
        else:
            blk = jnp.where(lane < width, wt_ref[start:start + LANE, :].T, 0.0)
            if lane_off:
                blk = pltpu.roll(blk, lane_off, 1)
            o_ref[:, off:off + LANE] = blk.astype(BF16)
            off += LANE


def _pack_cols(w, segs):
    d, n_in = w.shape
    n_out = sum(width if width % LANE == 0 else LANE for _, width, _ in segs)
    return pl.pallas_call(
        functools.partial(_pack_kernel, segs=segs), grid=(1,),
        in_specs=[pl.BlockSpec((n_in, d), lambda i: (0, 0))],
        out_specs=pl.BlockSpec((d, n_out), lambda i: (0, 0)),
        out_shape=jax.ShapeDtypeStruct((d, n_out), BF16),
        compiler_params=_cparams(1), name="pack_cols",
    )(w.T)


def _prep_l0(w_in0, w_uq, w_ukv):
    w = _pack_cols(w_in0, [(0, 640, 0), (640, MLA_ROPE, MLA_NOPE), (672, 1792, 0)])
    uq = w_uq.reshape(MLA_Q_LORA, MLA_HEADS, MLA_NOPE + MLA_ROPE)
    wuq = jnp.pad(uq, ((0, 0), (0, 0), (0, LANE - MLA_NOPE - MLA_ROPE))).reshape(MLA_Q_LORA, MLA_HEADS * LANE)
    ukv = w_ukv.reshape(MLA_KV_LORA, MLA_HEADS, MLA_NOPE + MLA_V)
    wuk = jnp.pad(ukv[:, :, :MLA_NOPE], ((0, 0), (0, 0), (0, LANE - MLA_NOPE))).reshape(MLA_KV_LORA, MLA_HEADS * LANE)
    wuvt = ukv[:, :, MLA_NOPE:].reshape(MLA_KV_LORA, MLA_HEADS * MLA_V).T
    return w, wuq.astype(BF16), wuk.astype(BF16), wuvt.astype(BF16)


def _prep_l1(w_in1):
    return _pack_cols(w_in1, [(0, 1536, 0), (1552, 2048, 0), (1536, 16, 0)])


def _chunk_rows(ab):
    b, t, c = ab.shape
    return jnp.swapaxes(ab.reshape(b, t // GDN_CHUNK, GDN_CHUNK, c), 2, 3)


def _trunk(x, mod, caches, p, tables, tq):
    dec = caches is not None
    t0m, t0s, t1 = tables if dec else (None, None, None)
    (qa, ckv, kr, za, qb, kb, vbt, zb, *ctx0) = _inproj0(
        x, mod[0], p["ln0"], p["w0"], p["qn"], p["wuq"], p["kvn"], (t0m + t0s) if dec else None)
    if dec:
        ckv_c, kr_c, kb_c, vb_ct, s0, kd_c, vd_ct = caches
    else:
        ckv_c = kr_c = kb_c = vb_ct = kd_c = vd_ct = None
        s0 = jnp.zeros((x.shape[0], 2, GDN_HEADS, GDN_DK, GDN_DV), F32)
    oa = _mla(qa, ckv, kr, ckv_c, kr_c, p["wuk"], p["wuvt"], tq)
    ob = _swa(p["sink"], qb, kb, vbt, kb_c, vb_ct, tq)
    (x1, qkv, zc, qd, kd, vdt, zd, ab, *ctx1) = _inproj1(
        oa, za, ob, zb, x, mod[0], p["wout0"], mod[1], p["ln1"], p["w1"], p["cw"], p["aqn"], p["akn"],
        t1 if dec else None)
    oc, sfin = _gdn(qkv, ab, _chunk_rows(ab), p["al"], p["dt"], p["alt"], p["dtt"], s0)
    od = _attd(qd, kd, vdt, kd_c, vd_ct, tq)
    y = _outproj1(oc, zc, od, zd, x1, mod[1], p["gn"], p["wout1"], p["lnf"])
    return y, ctx0, sfin, ctx1


def kernel(x_prompt, x_sample, cache_l0_mla_ckv, cache_l0_mla_krope, cache_l0_swa_k, cache_l0_swa_v,
           state_l1_gdn, cache_l1_attn_k, cache_l1_attn_v, c, c_ctx,
           w_mod0, b_mod0, ln0, w_in0, mla_q_norm, w_uq, mla_kv_norm, w_ukv, swa_sink, w_out0,
           w_mod1, b_mod1, ln1, w_in1, gdn_conv, gdn_a_log, gdn_dt_bias, gdn_norm, att_q_norm, att_k_norm, w_out1,
           ln_f):
    d = x_prompt.shape[-1]
    bd, td = x_sample.shape[:2]
    bc, tc = x_prompt.shape[:2]
    past = cache_l0_mla_ckv.shape[1]
    row = lambda v: v.reshape(1, -1)
    w0, wuq, wuk, wuvt = _prep_l0(w_in0, w_uq, w_ukv)
    w1 = _prep_l1(w_in1)
    al8 = gdn_a_log.reshape(1, 2 * GDN_HEADS)
    dt8 = gdn_dt_bias.reshape(1, 2 * GDN_HEADS)
    al16 = jnp.pad(al8, ((0, 0), (0, 8)))
    dt16 = jnp.pad(dt8, ((0, 0), (0, 8)))
    p = dict(ln0=row(ln0), w0=w0, qn=row(mla_q_norm), wuq=wuq, kvn=row(mla_kv_norm), wuk=wuk, wuvt=wuvt,
             sink=swa_sink, wout0=w_out0.astype(BF16), ln1=row(ln1), w1=w1, aqn=row(att_q_norm),
             akn=row(att_k_norm), cw=gdn_conv, al=al16, dt=dt16, alt=al16.T, dtt=dt16.T, gn=row(gdn_norm),
             wout1=w_out1.astype(BF16), lnf=row(ln_f))
    n_rows = -(-(bd + 1) // 8) * 8
    c_rows = jnp.concatenate([c, c_ctx[None, :], jnp.zeros((n_rows - bd - 1, d), F32)], axis=0)
    mods = [_mod(c_rows, w_mod0, b_mod0), _mod(c_rows, w_mod1, b_mod1)]
    mod_dec = [m[:bd].reshape(bd, 3, d) for m in mods]
    mod_ctx = [m[bd:bd + 1].reshape(1, 3, d) for m in mods]
    cm, sm = _rope_table(td, MLA_ROPE)
    t0m = (jnp.asarray(_place(cm, 1.0, MLA_NOPE, LANE)), jnp.asarray(_place(sm, 0.0, MLA_NOPE, LANE)))
    cs, ss = _rope_table(td, SWA_DH)
    t0s = (jnp.asarray(np.tile(cs, (1, LANE // SWA_DH))), jnp.asarray(np.tile(ss, (1, LANE // SWA_DH))))
    t1 = tuple(jnp.asarray(a) for a in _rope_table(td, ATT_DH))
    caches = (cache_l0_mla_ckv,
              jnp.pad(cache_l0_mla_krope, ((0, 0), (0, 0), (MLA_NOPE, LANE - MLA_NOPE - MLA_ROPE))),
              cache_l0_swa_k.reshape(bd, past, SWA_KV_HEADS * SWA_DH),
              jnp.swapaxes(cache_l0_swa_v.reshape(bd, past, SWA_KV_HEADS * SWA_DH), 1, 2),
              state_l1_gdn,
              cache_l1_attn_k.reshape(bd, past, ATT_KV_HEADS * ATT_DH),
              jnp.swapaxes(cache_l1_attn_v.reshape(bd, past, ATT_KV_HEADS * ATT_DH), 1, 2))
    y_prompt, ctx0, sfin, ctx1 = _trunk(x_prompt, mod_ctx, None, p, None, tq=tc)
    y_sample, _, _, _ = _trunk(x_sample, mod_dec, caches, p, (t0m, t0s, t1), tq=256)
    ckv32, kr32, kb32, vb32 = ctx0
    kd32, vd32 = ctx1
    return (y_prompt, y_sample, ckv32, kr32,
            kb32.reshape(bc, tc, SWA_KV_HEADS, SWA_DH), vb32.reshape(bc, tc, SWA_KV_HEADS, SWA_DH),
            sfin, kd32.reshape(bc, tc, ATT_KV_HEADS, ATT_DH), vd32.reshape(bc, tc, ATT_KV_HEADS, ATT_DH))
```

```python
import functools
import math

import jax
import jax.numpy as jnp
import numpy as np
from jax import lax
from jax.experimental import pallas as pl
from jax.experimental.pallas import tpu as pltpu

F32 = jnp.float32
BF16 = jnp.bfloat16

GRID_W = 64
ROPE_THETA = 10000.0
EPS = 1e-6
WINDOW = 128
MLA_HEADS, MLA_NOPE, MLA_ROPE, MLA_V = 8, 64, 32, 64
MLA_Q_LORA, MLA_KV_LORA = 384, 256
SWA_HEADS, SWA_KV_HEADS, SWA_DH = 8, 2, 64
GDN_HEADS, GDN_DK, GDN_DV, GDN_CHUNK = 4, 128, 128, 64
GDN_LOCAL_CHUNKS = 4
ATT_HEADS, ATT_KV_HEADS, ATT_DH = 4, 2, 128
LANE = 128
LOG2E = math.log2(math.e)
NEG = -1e30
VMEM_LIMIT = 56 * 1024 * 1024


def _cparams(n_axes):
    return pltpu.CompilerParams(dimension_semantics=("arbitrary",) * n_axes, vmem_limit_bytes=VMEM_LIMIT)


def _dot(a, b):
    return jnp.dot(a, b, preferred_element_type=F32)


def _dot_nt(a, b):
    return lax.dot_general(a, b, (((1,), (1,)), ((), ())), preferred_element_type=F32)


def _dot_tn(a, b):
    return lax.dot_general(a, b, (((0,), (0,)), ((), ())), preferred_element_type=F32)


def _dot_exact(a, b):
    return jnp.dot(a, b, preferred_element_type=F32, precision=lax.Precision.HIGHEST)


def _silu(x):
    return x * jax.nn.sigmoid(x)


def _rms_rows(x, g):
    return x * lax.rsqrt(jnp.mean(x * x, axis=-1, keepdims=True) + EPS) * g


def _rope_block(x, cos, sin, half):
    lane = lax.broadcasted_iota(jnp.int32, x.shape, 1)
    first = (lane // half) % 2 == 0
    partner = jnp.where(first, pltpu.roll(x, LANE - half, 1), pltpu.roll(x, half, 1))
    return x * cos + partner * sin


def _mod_kernel(c_ref, w_ref, b_ref, o_ref):
    a = _silu(c_ref[...]).astype(BF16)
    o_ref[...] = _dot(a, w_ref[...].astype(BF16)) + b_ref[...]


def _mod(c_rows, w_mod, b_mod):
    r, d = c_rows.shape
    n = w_mod.shape[1]
    tn = 1024
    return pl.pallas_call(
        _mod_kernel,
        grid=(n // tn,),
        in_specs=[pl.BlockSpec((r, d), lambda j: (0, 0)),
                  pl.BlockSpec((d, tn), lambda j: (0, j)),
                  pl.BlockSpec((1, tn), lambda j: (0, j))],
        out_specs=pl.BlockSpec((r, tn), lambda j: (0, j)),
        out_shape=jax.ShapeDtypeStruct((r, n), F32),
        compiler_params=_cparams(1),
        name="mod",
    )(c_rows, w_mod, b_mod.reshape(1, n))


def _adaln(x, mod_ref, ln_ref):
    h = _rms_rows(x, ln_ref[...])
    return h * (1.0 + mod_ref[0, 1:2, :]) + mod_ref[0, 0:1, :]


L0_OFF = dict(cq=0, ckv=384, kr=640, za=768, qb=1280, kb=1792, vb=1920, zb=2048)


def _inproj0_kernel(*refs, rope):
    us_ref = refs[-1]
    g = pl.program_id(0)

    @pl.when(g == 0)
    def _():
        us_ref[...] = jnp.zeros_like(us_ref)

    @pl.when(g == pl.num_programs(0) - 1)
    def _():
        _inproj0_finish(refs, rope)

    @pl.when(g < pl.num_programs(0) - 1)
    def _():
        x_ref, mod_ref, ln_ref, w_ref = refs[:4]
        h = _adaln(x_ref[0], mod_ref, ln_ref).astype(BF16)
        _inproj0_finish(refs, rope)
        us_ref[...] = _dot(h, w_ref[...])


def _inproj0_finish(refs, rope):
    if rope:
        (x_ref, mod_ref, ln_ref, w_ref, qn_ref, wuq_ref, kvn_ref, cm_ref, sm_ref, cs_ref, ss_ref,
         qa_ref, ckv_ref, kr_ref, za_ref, qb_ref, kb_ref, vbt_ref, zb_ref, u) = refs
    else:
        (x_ref, mod_ref, ln_ref, w_ref, qn_ref, wuq_ref, kvn_ref,
         qa_ref, ckv_ref, kr_ref, za_ref, qb_ref, kb_ref, vbt_ref, zb_ref,
         ckv32_ref, kr32_ref, kb32_ref, vb32_ref, u) = refs
    o = L0_OFF
    cq = _rms_rows(u[:, o["cq"]:o["cq"] + 384], qn_ref[...]).astype(BF16)
    qa = _dot(cq, wuq_ref[...])
    ckv = _rms_rows(u[:, o["ckv"]:o["ckv"] + 256], kvn_ref[...])
    kr = u[:, o["kr"]:o["kr"] + 128]
    qb = u[:, o["qb"]:o["qb"] + 512]
    kb = u[:, o["kb"]:o["kb"] + 128]
    vb = u[:, o["vb"]:o["vb"] + 128]
    if not rope:
        ckv32_ref[0] = ckv
        kr32_ref[0] = kr[:, 64:96]
        kb32_ref[0] = kb
        vb32_ref[0] = vb
    qa_scale = (MLA_NOPE + MLA_ROPE) ** -0.5 * LOG2E
    qb_scale = SWA_DH ** -0.5 * LOG2E
    for j in range(MLA_HEADS):
        blk = qa[:, j * LANE:(j + 1) * LANE]
        if rope:
            blk = _rope_block(blk, cm_ref[...], sm_ref[...], MLA_ROPE // 4)
        qa_ref[0, :, j * LANE:(j + 1) * LANE] = (blk * qa_scale).astype(BF16)
    for j in range(SWA_HEADS * SWA_DH // LANE):
        blk = qb[:, j * LANE:(j + 1) * LANE]
        if rope:
            blk = _rope_block(blk, cs_ref[...], ss_ref[...], SWA_DH // 4)
        qb_ref[0, :, j * LANE:(j + 1) * LANE] = (blk * qb_scale).astype(BF16)
    if rope:
        kr = _rope_block(kr, cm_ref[...], sm_ref[...], MLA_ROPE // 4)
        kb = _rope_block(kb, cs_ref[...], ss_ref[...], SWA_DH // 4)
    ckv_ref[0] = ckv.astype(BF16)
    kr_ref[0] = kr.astype(BF16)
    kb_ref[0] = kb.astype(BF16)
    za_ref[0] = u[:, o["za"]:o["za"] + 512].astype(BF16)
    zb_ref[0] = u[:, o["zb"]:o["zb"] + 512].astype(BF16)
    vbt_ref[0] = vb.T.astype(BF16)


def _row_tile(t):
    return 512 if t % 512 == 0 else 256


def _inproj0(x, mod, ln, w, qn, wuq, kvn, tables):
    b, t, d = x.shape
    tr = _row_tile(t)
    nj = t // tr
    n_blocks = b * nj
    rope = tables is not None
    bm = mod.shape[0]
    cur = lambda g: jnp.minimum(g, n_blocks - 1)
    held = lambda g: jnp.maximum(g - 1, 0)
    full = lambda a: pl.BlockSpec(a.shape, lambda g: (0,) * a.ndim)
    rows = lambda c: pl.BlockSpec((1, tr, c), lambda g: (held(g) // nj, held(g) % nj, 0))
    in_specs = [pl.BlockSpec((1, tr, d), lambda g: (cur(g) // nj, cur(g) % nj, 0)),
                pl.BlockSpec((1, 3, d), (lambda g: (cur(g) // nj, 0, 0)) if bm > 1 else (lambda g: (0, 0, 0))),
                full(ln), full(w), full(qn), full(wuq), full(kvn)]
    args = [x, mod, ln, w, qn, wuq, kvn]
    if rope:
        for tab in tables:
            in_specs.append(pl.BlockSpec((tr, LANE), lambda g: (held(g) % nj, 0)))
            args.append(tab)
    out_shape = [jax.ShapeDtypeStruct((b, t, 1024), BF16), jax.ShapeDtypeStruct((b, t, 256), BF16),
                 jax.ShapeDtypeStruct((b, t, 128), BF16), jax.ShapeDtypeStruct((b, t, 512), BF16),
                 jax.ShapeDtypeStruct((b, t, 512), BF16), jax.ShapeDtypeStruct((b, t, 128), BF16),
                 jax.ShapeDtypeStruct((b, 128, t), BF16), jax.ShapeDtypeStruct((b, t, 512), BF16)]
    out_specs = [rows(1024), rows(256), rows(128), rows(512), rows(512), rows(128),
                 pl.BlockSpec((1, 128, tr), lambda g: (held(g) // nj, 0, held(g) % nj)), rows(512)]
    if not rope:
        out_shape += [jax.ShapeDtypeStruct((b, t, 256), F32), jax.ShapeDtypeStruct((b, t, 32), F32),
                      jax.ShapeDtypeStruct((b, t, 128), F32), jax.ShapeDtypeStruct((b, t, 128), F32)]
        out_specs += [rows(256), rows(32), rows(128), rows(128)]
    return pl.pallas_call(
        functools.partial(_inproj0_kernel, rope=rope),
        grid=(n_blocks + 1,), in_specs=in_specs, out_specs=out_specs, out_shape=out_shape,
        scratch_shapes=[pltpu.VMEM((tr, w.shape[1]), F32)],
        compiler_params=_cparams(1), name="inproj0_dec" if rope else "inproj0_ctx",
    )(*args)


KEY_CHUNK = 512
ATT_LOOKAHEAD = 4


SUM_ROWS = 16


def _attend_heads(qs, kv_of, chunks, dv, sinks=None):
    nh = len(qs)
    tq = qs[0].shape[0]
    m = [None] * nh
    acc = [None] * nh
    items = [(ci, i) for ci in range(len(chunks)) for i in range(nh)]
    loaded = {}

    def kv(ci, src):
        if (ci, src) not in loaded:
            vt = chunks[ci][1](src)
            ones = (lax.broadcasted_iota(jnp.int32, (SUM_ROWS, vt.shape[1]), 0) == 0).astype(BF16)
            loaded[(ci, src)] = (chunks[ci][0](src), jnp.concatenate([vt, ones], axis=0))
        return loaded[(ci, src)]

    if sinks is not None:
        m = [jnp.full((1, tq), sk, F32) for sk in sinks]
        unit = (lax.broadcasted_iota(jnp.int32, (dv + SUM_ROWS, tq), 0) == dv).astype(F32)
        acc = [unit for _ in sinks]

    scores = {}
    for t in range(len(items) + ATT_LOOKAHEAD):
        if t < len(items):
            ci, i = items[t]
            scores[t] = _dot_nt(kv(ci, kv_of[i])[0], qs[i])
        t0 = t - ATT_LOOKAHEAD
        if t0 < 0:
            continue
        ci, i = items[t0]
        mask = chunks[ci][2]
        si = scores.pop(t0)
        if mask is not None:
            si = jnp.where(mask, si, NEG)
        cm = si.max(axis=0, keepdims=True)
        alpha = None
        if m[i] is None:
            m_new = cm
        else:
            m_new = jnp.maximum(m[i], cm)
            alpha = jnp.exp2(m[i] - m_new)
        p = jnp.exp2(si - m_new)
        m[i] = m_new
        pv = _dot(kv(ci, kv_of[i])[1], p.astype(BF16))
        acc[i] = pv if acc[i] is None else acc[i] * alpha + pv
    return [acc[i][:dv] * (1.0 / acc[i][dv:dv + 1]) for i in range(nh)]


def _key_chunks(n):
    step = KEY_CHUNK if n % KEY_CHUNK == 0 else n
    return [(c0, step) for c0 in range(0, n, step)]


def _mla_kernel(*refs, n_new, n_ctx, hp):
    if n_ctx:
        q_ref, ckv_ref, kr_ref, ckvc_ref, krc_ref, wuk_ref, wuvt_ref, o_ref, k_s, vt_s = refs
    else:
        q_ref, ckv_ref, kr_ref, wuk_ref, wuvt_ref, o_ref, k_s, vt_s = refs
    qi, gi = pl.program_id(1), pl.program_id(2)

    @pl.when((qi == 0) & (gi == 0))
    def _():
        def expand(ckv, kr, r0, n):
            kn = _dot(ckv, wuk_ref[...])
            for j in range(MLA_HEADS):
                k_s[j, r0:r0 + n, :] = (kn[:, j * LANE:(j + 1) * LANE] + kr).astype(BF16)
            vt_s[:, r0:r0 + n] = _dot_nt(wuvt_ref[...], ckv).astype(BF16)

        blk = 512 if n_new % 512 == 0 else 256
        for r0 in range(0, n_new, blk):
            expand(ckv_ref[0, r0:r0 + blk, :], kr_ref[0, r0:r0 + blk, :].astype(F32), r0, blk)
        if n_ctx:
            expand(ckvc_ref[0].astype(BF16), krc_ref[0], n_new, n_ctx)

    qs = [q_ref[0, :, j * LANE:(j + 1) * LANE] for j in range(hp)]
    chunks = []
    for c0, cn in _key_chunks(n_new + n_ctx):
        chunks.append((
            lambda j, c0=c0, cn=cn: k_s[gi * hp + j, c0:c0 + cn, :],
            lambda j, c0=c0, cn=cn: vt_s[pl.ds(pl.multiple_of((gi * hp + j) * MLA_V, MLA_V), MLA_V), c0:c0 + cn],
            None))
    outs = _attend_heads(qs, list(range(hp)), chunks, MLA_V)
    o_ref[0] = jnp.concatenate(outs, axis=0).T.astype(BF16)


def _mla(q, ckv, kr, ckv_c, kr_c, wuk, wuvt, tq):
    b, t, _ = q.shape
    n_ctx = 0 if ckv_c is None else ckv_c.shape[1]
    hp = 8
    tk = t + n_ctx
    rows_q = pl.BlockSpec((1, tq, hp * LANE), lambda i, j, g: (i, j, g))
    per_b = lambda a: pl.BlockSpec((1,) + a.shape[1:], lambda i, j, g: (i, 0, 0))
    full = lambda a: pl.BlockSpec(a.shape, lambda i, j, g: (0, 0))
    in_specs = [rows_q, per_b(ckv), per_b(kr)]
    args = [q, ckv, kr]
    if n_ctx:
        in_specs += [per_b(ckv_c), per_b(kr_c)]
        args += [ckv_c, kr_c]
    in_specs += [full(wuk), full(wuvt)]
    args += [wuk, wuvt]
    return pl.pallas_call(
        functools.partial(_mla_kernel, n_new=t, n_ctx=n_ctx, hp=hp),
        grid=(b, t // tq, MLA_HEADS // hp), in_specs=in_specs,
        out_specs=pl.BlockSpec((1, tq, hp * MLA_V), lambda i, j, g: (i, j, g)),
        out_shape=jax.ShapeDtypeStruct((b, t, MLA_HEADS * MLA_V), BF16),
        scratch_shapes=[pltpu.VMEM((MLA_HEADS, tk, LANE), BF16), pltpu.VMEM((MLA_HEADS * MLA_V, tk), BF16)],
        compiler_params=_cparams(3), name="mla_dec" if n_ctx else "mla_ctx",
    )(*args)


def _swa_kernel(*refs, n_new, n_ctx, tq):
    if n_ctx:
        sink_ref, q_ref, k_ref, vt_ref, kc_ref, vct_ref, o_ref = refs
    else:
        sink_ref, q_ref, k_ref, vt_ref, o_ref = refs
    qi = pl.program_id(1)
    grp = SWA_HEADS // SWA_KV_HEADS
    if n_ctx:
        span = tq + 2 * WINDOW
        q0 = qi * tq
        start = pl.multiple_of(jnp.clip(q0 - WINDOW, 0, n_new - span), LANE)
        kpos = start + lax.broadcasted_iota(jnp.int32, (span, tq), 0)
        qpos = q0 + lax.broadcasted_iota(jnp.int32, (span, tq), 1)
        band = jnp.abs(kpos - qpos) <= WINDOW
    dh = SWA_DH
    chunks = []
    if n_ctx:
        for c0, cn in _key_chunks(span):
            chunks.append((
                lambda g, c0=c0, cn=cn: k_ref[0, pl.ds(start + c0, cn), g * dh:(g + 1) * dh],
                lambda g, c0=c0, cn=cn: vt_ref[0, g * dh:(g + 1) * dh, pl.ds(start + c0, cn)],
                band[c0:c0 + cn]))
        for c0, cn in _key_chunks(n_ctx):
            chunks.append((
                lambda g, c0=c0, cn=cn: kc_ref[0, c0:c0 + cn, g * dh:(g + 1) * dh].astype(BF16),
                lambda g, c0=c0, cn=cn: vct_ref[0, g * dh:(g + 1) * dh, c0:c0 + cn].astype(BF16),
                None))
    else:
        for c0, cn in _key_chunks(n_new):
            chunks.append((
                lambda g, c0=c0, cn=cn: k_ref[0, c0:c0 + cn, g * dh:(g + 1) * dh],
                lambda g, c0=c0, cn=cn: vt_ref[0, g * dh:(g + 1) * dh, c0:c0 + cn],
                None))
    qs = [q_ref[0, :, h * dh:(h + 1) * dh] for h in range(SWA_HEADS)]
    sinks = [sink_ref[h] * LOG2E for h in range(SWA_HEADS)]
    outs = _attend_heads(qs, [h // grp for h in range(SWA_HEADS)], chunks, SWA_DH, sinks)
    o_ref[0] = jnp.concatenate(outs, axis=0).T.astype(BF16)


def _swa(sink, q, k, vt, k_c, v_ct, tq):
    b, t, _ = q.shape
    n_ctx = 0 if k_c is None else k_c.shape[1]
    per_b = lambda a: pl.BlockSpec((1,) + a.shape[1:], lambda i, j: (i, 0, 0))
    in_specs = [pl.BlockSpec(memory_space=pltpu.SMEM), pl.BlockSpec((1, tq, 512), lambda i, j: (i, j, 0)),
                per_b(k), per_b(vt)]
    args = [sink, q, k, vt]
    if n_ctx:
        in_specs += [per_b(k_c), per_b(v_ct)]
        args += [k_c, v_ct]
    return pl.pallas_call(
        functools.partial(_swa_kernel, n_new=t, n_ctx=n_ctx, tq=tq),
        grid=(b, t // tq), in_specs=in_specs,
        out_specs=pl.BlockSpec((1, tq, 512), lambda i, j: (i, j, 0)),
        out_shape=jax.ShapeDtypeStruct((b, t, 512), BF16),
        compiler_params=_cparams(2), name="swa_dec" if n_ctx else "swa_ctx",
    )(*args)


def _silu_gate(o, z):
    return o * (z / (1.0 + jnp.exp(-z)))


L1_OFF = dict(qkv=0, zc=1536, qd=2048, kd=2560, vd=2816, zd=3072, ab=3584)
L1_W = 3712


def _inproj1_kernel(*refs, rope, nj):
    if rope:
        (oa_ref, za_ref, ob_ref, zb_ref, x_ref, mod0_ref, wo_ref, mod_ref, ln_ref, w_ref, cw_ref,
         qn_ref, kn_ref, c_ref, s_ref,
         x1_ref, qkv_ref, zc_ref, qd_ref, kd_ref, vdt_ref, zd_ref, ab_ref, us_ref, pr_ref) = refs
    else:
        (oa_ref, za_ref, ob_ref, zb_ref, x_ref, mod0_ref, wo_ref, mod_ref, ln_ref, w_ref, cw_ref,
         qn_ref, kn_ref,
         x1_ref, qkv_ref, zc_ref, qd_ref, kd_ref, vdt_ref, zd_ref, ab_ref, kd32_ref, vd32_ref, us_ref, pr_ref) = refs
    g = pl.program_id(0)
    tr = us_ref.shape[0]
    nq = GDN_HEADS * GDN_DK
    n_qkv = 2 * nq + GDN_HEADS * GDN_DV
    o = L1_OFF

    @pl.when(g == 0)
    def _():
        us_ref[...] = jnp.zeros_like(us_ref)
        pr_ref[...] = jnp.zeros_like(pr_ref)

    def conv_held_block(next_row):
        jb = lax.rem(g + nj - 1, nj)
        prv = jnp.where(jb > 0, pr_ref[7:8, :], 0.0)
        nxt = jnp.where(jb < nj - 1, next_row, 0.0)
        row8 = lax.broadcasted_iota(jnp.int32, (8, LANE), 0)
        for j in range(n_qkv // LANE):
            cols = slice(j * LANE, (j + 1) * LANE)
            xj = us_ref[:, cols]
            xp = pltpu.roll(xj, 1, 0)
            xp = jnp.concatenate([jnp.where(row8 == 0, prv[:, cols], xp[0:8]), xp[8:]], axis=0)
            xn = pltpu.roll(xj, tr - 1, 0)
            xn = jnp.concatenate([xn[:tr - 8], jnp.where(row8 == 7, nxt[:, cols], xn[tr - 8:])], axis=0)
            cw = cw_ref[:, cols]
            y = _silu(xp * cw[0:1, :] + xj * cw[1:2, :] + xn * cw[2:3, :])
            if j < 2 * nq // LANE:
                y = y * lax.rsqrt(jnp.sum(y * y, axis=-1, keepdims=True) + EPS)
            if j < nq // LANE:
                y = y * GDN_DK ** -0.5
            qkv_ref[0, :, cols] = y.astype(BF16)
        pr_ref[...] = us_ref[tr - 8:tr, 0:n_qkv]
        zc_ref[0] = us_ref[:, o["zc"]:o["zc"] + 512].astype(BF16)
        zd_ref[0] = us_ref[:, o["zd"]:o["zd"] + 512].astype(BF16)
        ab_ref[0] = us_ref[:, o["ab"]:o["ab"] + 16]
        qd_scale = ATT_DH ** -0.5 * LOG2E
        for j in range(ATT_HEADS):
            blk = _rms_rows(us_ref[:, o["qd"] + j * LANE:o["qd"] + (j + 1) * LANE], qn_ref[...])
            if rope:
                blk = _rope_block(blk, c_ref[...], s_ref[...], ATT_DH // 4)
            qd_ref[0, :, j * LANE:(j + 1) * LANE] = (blk * qd_scale).astype(BF16)
        for j in range(ATT_KV_HEADS):
            blk = _rms_rows(us_ref[:, o["kd"] + j * LANE:o["kd"] + (j + 1) * LANE], kn_ref[...])
            if rope:
                blk = _rope_block(blk, c_ref[...], s_ref[...], ATT_DH // 4)
            else:
                kd32_ref[0, :, j * LANE:(j + 1) * LANE] = blk
            kd_ref[0, :, j * LANE:(j + 1) * LANE] = blk.astype(BF16)
        vd = us_ref[:, o["vd"]:o["vd"] + 256]
        if not rope:
            vd32_ref[0] = vd
        vdt_ref[0] = vd.T.astype(BF16)

    @pl.when(g == pl.num_programs(0) - 1)
    def _():
        conv_held_block(jnp.zeros((1, n_qkv), F32))

    @pl.when(g < pl.num_programs(0) - 1)
    def _():
        _inproj1_block(refs, rope, conv_held_block)


def _inproj1_block(refs, rope, conv_held_block):
    if rope:
        (oa_ref, za_ref, ob_ref, zb_ref, x_ref, mod0_ref, wo_ref, mod_ref, ln_ref, w_ref, cw_ref,
         qn_ref, kn_ref, c_ref, s_ref,
         x1_ref, qkv_ref, zc_ref, qd_ref, kd_ref, vdt_ref, zd_ref, ab_ref, us_ref, pr_ref) = refs
    else:
        (oa_ref, za_ref, ob_ref, zb_ref, x_ref, mod0_ref, wo_ref, mod_ref, ln_ref, w_ref, cw_ref,
         qn_ref, kn_ref,
         x1_ref, qkv_ref, zc_ref, qd_ref, kd_ref, vdt_ref, zd_ref, ab_ref, kd32_ref, vd32_ref, us_ref, pr_ref) = refs
    n_qkv = pr_ref.shape[1]
    y0 = (_dot(_silu_gate(oa_ref[0], za_ref[0]), wo_ref[0:512, :])
          + _dot(_silu_gate(ob_ref[0], zb_ref[0]), wo_ref[512:1024, :]))
    x1 = x_ref[0] + mod0_ref[0, 2:3, :] * y0
    x1_ref[0] = x1
    h = _adaln(x1, mod_ref, ln_ref).astype(BF16)
    conv_held_block(_dot(h[0:8], w_ref[:, 0:n_qkv])[0:1])
    us_ref[...] = _dot(h, w_ref[...])


def _inproj1(oa, za, ob, zb, x, mod0, wo, mod, ln, w, cw, qn, kn, tables):
    b, t, d = x.shape
    tr = _row_tile(t)
    nj = t // tr
    n_blocks = b * nj
    rope = tables is not None
    bm = mod.shape[0]
    n_qkv = cw.shape[1]
    cur = lambda g: jnp.minimum(g, n_blocks - 1)
    held = lambda g: jnp.maximum(g - 1, 0)
    full = lambda a: pl.BlockSpec(a.shape, lambda g: (0,) * a.ndim)
    rows = lambda c: pl.BlockSpec((1, tr, c), lambda g: (cur(g) // nj, cur(g) % nj, 0))
    mod_spec = pl.BlockSpec((1, 3, d), (lambda g: (cur(g) // nj, 0, 0)) if bm > 1 else (lambda g: (0, 0, 0)))
    widths = [512, 512, 512, 512, d]
    in_specs = ([rows(c) for c in widths]
                + [mod_spec, full(wo), mod_spec, full(ln), full(w), full(cw), full(qn), full(kn)])
    args = [oa, za, ob, zb, x, mod0, wo, mod, ln, w, cw, qn, kn]
    hrows = lambda c: pl.BlockSpec((1, tr, c), lambda g: (held(g) // nj, held(g) % nj, 0))
    if rope:
        for tab in tables:
            in_specs.append(pl.BlockSpec((tr, LANE), lambda g: (held(g) % nj, 0)))
            args.append(tab)
    out_shape = [jax.ShapeDtypeStruct((b, t, d), F32),
                 jax.ShapeDtypeStruct((b, t, n_qkv), BF16), jax.ShapeDtypeStruct((b, t, 512), BF16),
                 jax.ShapeDtypeStruct((b, t, 512), BF16), jax.ShapeDtypeStruct((b, t, 256), BF16),
                 jax.ShapeDtypeStruct((b, 256, t), BF16), jax.ShapeDtypeStruct((b, t, 512), BF16),
                 jax.ShapeDtypeStruct((b, t, 16), F32)]
    out_specs = [rows(d), hrows(n_qkv), hrows(512), hrows(512), hrows(256),
                 pl.BlockSpec((1, 256, tr), lambda g: (held(g) // nj, 0, held(g) % nj)), hrows(512), hrows(16)]
    if not rope:
        out_shape += [jax.ShapeDtypeStruct((b, t, 256), F32), jax.ShapeDtypeStruct((b, t, 256), F32)]
        out_specs += [hrows(256), hrows(256)]
    return pl.pallas_call(
        functools.partial(_inproj1_kernel, rope=rope, nj=nj),
        grid=(n_blocks + 1,), in_specs=in_specs, out_specs=out_specs, out_shape=out_shape,
        scratch_shapes=[pltpu.VMEM((tr, L1_W), F32), pltpu.VMEM((8, n_qkv), F32)],
        compiler_params=_cparams(1), name="inproj1_dec" if rope else "inproj1_ctx",
    )(*args)


def _gdn_local(blocks):
    c = GDN_CHUNK
    row = lax.broadcasted_iota(jnp.int32, (c, c), 0)
    col = lax.broadcasted_iota(jnp.int32, (c, c), 1)
    lane2 = lax.broadcasted_iota(jnp.int32, (c, 2 * c), 1)
    eye = (row == col).astype(F32)
    eye_t = jnp.concatenate([eye, jnp.zeros((c, c), F32)], axis=1).astype(BF16)
    chains = [ch for blk in blocks for ch in blk["dirs"]]
    for blk in blocks:
        for ch in blk["dirs"]:
            causal = (row <= col) if ch["upper"] else (row >= col)
            ch["strict"] = (row < col) if ch["upper"] else (row > col)
            ch["decay"] = jnp.exp(jnp.where(causal, ch["gc_col"] - ch["gc_row"], -jnp.inf))
            ch["kb"] = blk["k"] * ch["beta_col"]
            ch["egc"] = jnp.exp(ch["gc_col"])
    for blk in blocks:
        lhs = jnp.concatenate([ch["kb"] for ch in blk["dirs"]] + [blk["q"]], axis=0).astype(BF16)
        a = _dot_nt(lhs, blk["k"].astype(BF16))
        nd = len(blk["dirs"])
        for di, ch in enumerate(blk["dirs"]):
            x = jnp.where(ch["strict"], -(a[di * c:(di + 1) * c] * ch["decay"]), 0.0)
            ch["intra"] = (a[nd * c:] * ch["decay"]).astype(BF16)
            ch["w"] = jnp.concatenate([eye, x], axis=1)
    for _ in range(6):
        for ch in chains:
            w = ch["w"]
            wh = w.astype(BF16)
            lo = w - wh.astype(F32)
            php = jnp.where(lane2 < c, pltpu.roll(w, c, 1), lo).astype(BF16)
            ch["w"] = _dot(jnp.concatenate([wh, php], axis=1),
                           jnp.concatenate([eye_t, wh, lo.astype(BF16), wh], axis=0))
    for blk in blocks:
        for ch in blk["dirs"]:
            rhs = jnp.concatenate([blk["v"] * ch["beta_col"], ch["kb"] * ch["egc"]], axis=1).astype(BF16)
            sol = _dot(ch["w"][:, :c].astype(BF16), rhs)
            ch["u"], ch["wv"] = sol[:, :GDN_DV].astype(BF16), sol[:, GDN_DV:].astype(BF16)
            ch["qe"] = (blk["q"] * ch["egc"]).astype(BF16)
            ch["kd"] = (blk["k"] * jnp.exp(ch["glast"] - ch["gc_col"])).astype(BF16)
            ch["eg"] = jnp.exp(ch["glast"])


def _gdn_scan(chains):
    c = GDN_CHUNK
    for ch in chains:
        ch["sb"] = ch["s"].astype(BF16)
    for ch in chains:
        r = _dot(jnp.concatenate([ch["wv"], ch["qe"]], axis=0), ch["sb"])
        ch["vn"] = (ch["u"].astype(F32) - r[:c]).astype(BF16)
        ch["qs"] = r[c:]
    outs = []
    for ch in chains:
        o = ch["qs"] + _dot(ch["intra"], ch["vn"])
        s_new = ch["s"] * ch["eg"] + _dot_tn(ch["kd"], ch["vn"])
        outs.append((o, s_new))
    return outs


def _gdn_kernel(qkv_ref, ab_ref, abt_ref, al_ref, dt_ref, alt_ref, dtt_ref, s0_ref, o_ref, sf_ref,
                u_s, wv_s, qe_s, kd_s, in_s, eg_s, gcol_s, grow_s, beta_s, st_s, *, t):
    c = GDN_CHUNK
    n = t // c
    nh = GDN_HEADS
    ab = ab_ref[0]
    gact = -jnp.exp(al_ref[...]) * jax.nn.softplus(ab + dt_ref[...])
    lane16 = lax.broadcasted_iota(jnp.int32, ab.shape, 1)
    beta_s[...] = jnp.where(lane16 < 2 * nh, gact, jax.nn.sigmoid(ab))
    r64 = lax.broadcasted_iota(jnp.int32, (c, c), 0)
    c64 = lax.broadcasted_iota(jnp.int32, (c, c), 1)
    tril = (r64 >= c64).astype(F32)
    triu = (r64 <= c64).astype(F32)
    lane_c = lax.broadcasted_iota(jnp.int32, (c, 16), 1)
    sub_c = lax.broadcasted_iota(jnp.int32, (16, c), 0)

    def cum_chunk(i):
        r0 = pl.multiple_of(i * c, c)
        g = beta_s[pl.ds(r0, c), :]
        gcol_s[pl.ds(r0, c), :] = jnp.where(lane_c < nh, _dot_exact(tril, g), _dot_exact(triu, g))
        gt = -jnp.exp(alt_ref[...]) * jax.nn.softplus(abt_ref[0, i] + dtt_ref[...])
        grow_s[i] = jnp.where(sub_c < nh, _dot_exact(gt, triu), _dot_exact(gt, tril))

    ncs = GDN_LOCAL_CHUNKS
    rb = ncs * c
    for sub in range(ncs):
        cum_chunk(sub)

    def local_body(jb, carry):
        r0 = pl.multiple_of(jb * rb, rb)
        nxt_blk = jnp.minimum(jb + 1, n // ncs - 1)
        xs = [qkv_ref[0, pl.ds(r0, rb), j * LANE:(j + 1) * LANE].astype(F32) for j in range(3 * nh)]
        gcol = gcol_s[pl.ds(r0, rb), :]
        bet = beta_s[pl.ds(r0, rb), :]
        blocks = []
        for sub in range(ncs):
            ci = ncs * jb + sub
            rows = slice(sub * c, (sub + 1) * c)
            grow = grow_s[ci]
            for hh in range(nh):
                dirs = []
                for d in range(2):
                    ch = d * nh + hh
                    last = sub * c + (c - 1 if d == 0 else 0)
                    dirs.append(dict(gc_col=gcol[rows, ch:ch + 1], gc_row=grow[ch:ch + 1, :],
                                     beta_col=bet[rows, 2 * nh + ch:2 * nh + ch + 1],
                                     glast=gcol[last:last + 1, ch:ch + 1], upper=(d == 1), ch=ch, ci=ci,
                                     r0=r0 + sub * c))
                blocks.append(dict(q=xs[hh][rows], k=xs[nh + hh][rows], v=xs[2 * nh + hh][rows], dirs=dirs))
        _gdn_local(blocks)
        for blk in blocks:
            for chn in blk["dirs"]:
                ch, rr = chn["ch"], pl.ds(pl.multiple_of(chn["r0"], c), c)
                u_s[ch, rr, :] = chn["u"]
                wv_s[ch, rr, :] = chn["wv"]
                qe_s[ch, rr, :] = chn["qe"]
                kd_s[ch, rr, :] = chn["kd"]
                in_s[ch, chn["ci"]] = chn["intra"]
                eg_s[chn["ci"], ch:ch + 1, :] = jnp.broadcast_to(chn["eg"], (1, LANE))
        for sub in range(ncs):
            cum_chunk(nxt_blk * ncs + sub)
        return carry

    lax.fori_loop(0, n // ncs, local_body, 0)

    for d in range(2):
        for hh in range(nh):
            st_s[d * nh + hh] = s0_ref[0, d, hh]
    o_ref[...] = jnp.zeros_like(o_ref)

    def scan_body(i, carry):
        chains = []
        for d in range(2):
            ci = i if d == 0 else n - 1 - i
            rr = pl.ds(pl.multiple_of(ci * c, c), c)
            eg = eg_s[ci]
            for hh in range(nh):
                ch = d * nh + hh
                chains.append(dict(u=u_s[ch, rr, :], wv=wv_s[ch, rr, :], qe=qe_s[ch, rr, :], kd=kd_s[ch, rr, :],
                                   intra=in_s[ch, ci], eg=eg[ch:ch + 1, :], s=st_s[ch], rr=rr, hh=hh, ch=ch))
        for chn, (o, s_new) in zip(chains, _gdn_scan(chains)):
            st_s[chn["ch"]] = s_new
            cols = slice(chn["hh"] * LANE, (chn["hh"] + 1) * LANE)
            o_ref[0, chn["rr"], cols] = (o_ref[0, chn["rr"], cols].astype(F32) + o).astype(o_ref.dtype)
        return carry

    lax.fori_loop(0, n, scan_body, 0)
    for d in range(2):
        for hh in range(nh):
            sf_ref[0, d, hh] = st_s[d * nh + hh]


def _gdn(qkv, ab, abt, al, dt, alt, dtt, s0):
    b, t, _ = qkv.shape
    n = t // GDN_CHUNK
    per_b = lambda a: pl.BlockSpec((1,) + a.shape[1:], lambda i: (i,) + (0,) * (a.ndim - 1))
    full = lambda a: pl.BlockSpec(a.shape, lambda i: (0,) * a.ndim)
    return pl.pallas_call(
        functools.partial(_gdn_kernel, t=t), grid=(b,),
        in_specs=[per_b(qkv), per_b(ab), per_b(abt), full(al), full(dt), full(alt), full(dtt), per_b(s0)],
        out_specs=[pl.BlockSpec((1, t, GDN_HEADS * GDN_DV), lambda i: (i, 0, 0)), per_b(s0)],
        out_shape=[jax.ShapeDtypeStruct((b, t, GDN_HEADS * GDN_DV), BF16), jax.ShapeDtypeStruct(s0.shape, F32)],
        scratch_shapes=[pltpu.VMEM((2 * GDN_HEADS, t, LANE), BF16), pltpu.VMEM((2 * GDN_HEADS, t, LANE), BF16),
                        pltpu.VMEM((2 * GDN_HEADS, t, LANE), BF16), pltpu.VMEM((2 * GDN_HEADS, t, LANE), BF16),
                        pltpu.VMEM((2 * GDN_HEADS, n, GDN_CHUNK, GDN_CHUNK), BF16),
                        pltpu.VMEM((n, 2 * GDN_HEADS, LANE), F32),
                        pltpu.VMEM((t, 16), F32), pltpu.VMEM((n, 16, GDN_CHUNK), F32), pltpu.VMEM((t, 16), F32),
                        pltpu.VMEM((2 * GDN_HEADS, GDN_DK, GDN_DV), F32)],
        compiler_params=_cparams(1), name="gdn",
    )(qkv, ab, abt, al, dt, alt, dtt, s0)


def _attd_kernel(*refs, n_ctx):
    if n_ctx:
        q_ref, k_ref, vt_ref, kc_ref, vct_ref, o_ref = refs
    else:
        q_ref, k_ref, vt_ref, o_ref = refs
    n_new = k_ref.shape[1]
    grp = ATT_HEADS // ATT_KV_HEADS
    chunks = []
    for c0, cn in _key_chunks(n_new):
        chunks.append((
            lambda g, c0=c0, cn=cn: k_ref[0, c0:c0 + cn, g * LANE:(g + 1) * LANE],
            lambda g, c0=c0, cn=cn: vt_ref[0, g * LANE:(g + 1) * LANE, c0:c0 + cn],
            None))
    if n_ctx:
        for c0, cn in _key_chunks(n_ctx):
            chunks.append((
                lambda g, c0=c0, cn=cn: kc_ref[0, c0:c0 + cn, g * LANE:(g + 1) * LANE].astype(BF16),
                lambda g, c0=c0, cn=cn: vct_ref[0, g * LANE:(g + 1) * LANE, c0:c0 + cn].astype(BF16),
                None))
    qs = [q_ref[0, :, h * LANE:(h + 1) * LANE] for h in range(ATT_HEADS)]
    outs = _attend_heads(qs, [h // grp for h in range(ATT_HEADS)], chunks, ATT_DH)
    o_ref[0] = jnp.concatenate(outs, axis=0).T.astype(BF16)


def _attd(q, k, vt, k_c, v_ct, tq):
    b, t, _ = q.shape
    n_ctx = 0 if k_c is None else k_c.shape[1]
    per_b = lambda a: pl.BlockSpec((1,) + a.shape[1:], lambda i, j: (i, 0, 0))
    in_specs = [pl.BlockSpec((1, tq, 512), lambda i, j: (i, j, 0)), per_b(k), per_b(vt)]
    args = [q, k, vt]
    if n_ctx:
        in_specs += [per_b(k_c), per_b(v_ct)]
        args += [k_c, v_ct]
    return pl.pallas_call(
        functools.partial(_attd_kernel, n_ctx=n_ctx),
        grid=(b, t // tq), in_specs=in_specs,
        out_specs=pl.BlockSpec((1, tq, 512), lambda i, j: (i, j, 0)),
        out_shape=jax.ShapeDtypeStruct((b, t, 512), BF16),
        compiler_params=_cparams(2), name="attd_dec" if n_ctx else "attd_ctx",
    )(*args)


def _outproj1_kernel(oc_ref, zc_ref, od_ref, zd_ref, x_ref, mod_ref, gn_ref, w_ref, lnf_ref, y_ref):
    parts = []
    for j in range(GDN_HEADS):
        parts.append(_rms_rows(oc_ref[0, :, j * LANE:(j + 1) * LANE].astype(F32), gn_ref[...]).astype(BF16))
    gc = _silu_gate(jnp.concatenate(parts, axis=1), zc_ref[0])
    gd = _silu_gate(od_ref[0], zd_ref[0])
    y = _dot(gc, w_ref[0:512, :]) + _dot(gd, w_ref[512:1024, :])
    x2 = x_ref[0] + mod_ref[0, 2:3, :] * y
    y_ref[0] = _rms_rows(x2, lnf_ref[...])


def _outproj1(oc, zc, od, zd, x, mod, gn, w, lnf):
    b, t, d = x.shape
    tr = 1024 if t % 1024 == 0 else _row_tile(t)
    bm = mod.shape[0]
    rows = lambda c: pl.BlockSpec((1, tr, c), lambda i, j: (i, j, 0))
    full = lambda a: pl.BlockSpec(a.shape, lambda i, j: (0,) * a.ndim)
    return pl.pallas_call(
        _outproj1_kernel, grid=(b, t // tr),
        in_specs=[rows(512), rows(512), rows(512), rows(512), rows(d),
                  pl.BlockSpec((1, 3, d), (lambda i, j: (i, 0, 0)) if bm > 1 else (lambda i, j: (0, 0, 0))),
                  full(gn), full(w), full(lnf)],
        out_specs=rows(d), out_shape=jax.ShapeDtypeStruct((b, t, d), F32),
        compiler_params=_cparams(2), name="outproj1",
    )(oc, zc, od, zd, x, mod, gn, w, lnf)


def _rope_table(n_tok, rot_dim):
    quarter = rot_dim // 4
    inv = np.float32(ROPE_THETA) ** (-np.arange(quarter, dtype=np.float32) / np.float32(quarter))
    tt = np.arange(n_tok)
    pos = np.stack([tt // GRID_W, tt % GRID_W], axis=-1).astype(np.float32)
    ang = (pos[:, :, None] * inv).astype(np.float32)
    cos, sin = np.cos(ang), np.sin(ang)
    c = np.concatenate([cos, cos], axis=-1).reshape(n_tok, rot_dim)
    s = np.concatenate([-sin, sin], axis=-1).reshape(n_tok, rot_dim)
    return c.astype(np.float32), s.astype(np.float32)


def _place(tab, fill, off, width):
    out = np.full((tab.shape[0], width), fill, np.float32)
    out[:, off:off + tab.shape[1]] = tab
    return out


PACK_STEP = 256


def _pack_kernel(wt_ref, o_ref, *, segs):
    d = wt_ref.shape[1]
    lane = lax.broadcasted_iota(jnp.int32, (d, LANE), 1)
    off = 0
    for start, width, lane_off in segs:
        if width % LANE == 0:
            for c0 in range(0, width, PACK_STEP):
                cw = min(PACK_STEP, width - c0)
                o_ref[:, off + c0:off + c0 + cw] = wt_ref[start + c0:start + c0 + cw, :].T.astype(BF16)
            off += width
        else:
            blk = jnp.where(lane < width, wt_ref[start:start + LANE, :].T, 0.0)
            if lane_off:
                blk = pltpu.roll(blk, lane_off, 1)
            o_ref[:, off:off + LANE] = blk.astype(BF16)
            off += LANE


def _pack_cols(w, segs):
    d, n_in = w.shape
    n_out = sum(width if width % LANE == 0 else LANE for _, width, _ in segs)
    return pl.pallas_call(
        functools.partial(_pack_kernel, segs=segs), grid=(1,),
        in_specs=[pl.BlockSpec((n_in, d), lambda i: (0, 0))],
        out_specs=pl.BlockSpec((d, n_out), lambda i: (0, 0)),
        out_shape=jax.ShapeDtypeStruct((d, n_out), BF16),
        compiler_params=_cparams(1), name="pack_cols",
    )(w.T)


def _prep_l0(w_in0, w_uq, w_ukv):
    w = _pack_cols(w_in0, [(0, 640, 0), (640, MLA_ROPE, MLA_NOPE), (672, 1792, 0)])
    uq = w_uq.reshape(MLA_Q_LORA, MLA_HEADS, MLA_NOPE + MLA_ROPE)
    wuq = jnp.pad(uq, ((0, 0), (0, 0), (0, LANE - MLA_NOPE - MLA_ROPE))).reshape(MLA_Q_LORA, MLA_HEADS * LANE)
    ukv = w_ukv.reshape(MLA_KV_LORA, MLA_HEADS, MLA_NOPE + MLA_V)
    wuk = jnp.pad(ukv[:, :, :MLA_NOPE], ((0, 0), (0, 0), (0, LANE - MLA_NOPE))).reshape(MLA_KV_LORA, MLA_HEADS * LANE)
    wuvt = ukv[:, :, MLA_NOPE:].reshape(MLA_KV_LORA, MLA_HEADS * MLA_V).T
    return w, wuq.astype(BF16), wuk.astype(BF16), wuvt.astype(BF16)


def _prep_l1(w_in1):
    return _pack_cols(w_in1, [(0, 1536, 0), (1552, 2048, 0), (1536, 16, 0)])


def _chunk_rows(ab):
    b, t, c = ab.shape
    return jnp.swapaxes(ab.reshape(b, t // GDN_CHUNK, GDN_CHUNK, c), 2, 3)


def _trunk(x, mod, caches, p, tables, tq):
    dec = caches is not None
    t0m, t0s, t1 = tables if dec else (None, None, None)
    (qa, ckv, kr, za, qb, kb, vbt, zb, *ctx0) = _inproj0(
        x, mod[0], p["ln0"], p["w0"], p["qn"], p["wuq"], p["kvn"], (t0m + t0s) if dec else None)
    if dec:
        ckv_c, kr_c, kb_c, vb_ct, s0, kd_c, vd_ct = caches
    else:
        ckv_c = kr_c = kb_c = vb_ct = kd_c = vd_ct = None
        s0 = jnp.zeros((x.shape[0], 2, GDN_HEADS, GDN_DK, GDN_DV), F32)
    oa = _mla(qa, ckv, kr, ckv_c, kr_c, p["wuk"], p["wuvt"], tq)
    ob = _swa(p["sink"], qb, kb, vbt, kb_c, vb_ct, tq)
    (x1, qkv, zc, qd, kd, vdt, zd, ab, *ctx1) = _inproj1(
        oa, za, ob, zb, x, mod[0], p["wout0"], mod[1], p["ln1"], p["w1"], p["cw"], p["aqn"], p["akn"],
        t1 if dec else None)
    oc, sfin = _gdn(qkv, ab, _chunk_rows(ab), p["al"], p["dt"], p["alt"], p["dtt"], s0)
    od = _attd(qd, kd, vdt, kd_c, vd_ct, tq)
    y = _outproj1(oc, zc, od, zd, x1, mod[1], p["gn"], p["wout1"], p["lnf"])
    return y, ctx0, sfin, ctx1


def kernel(x_prompt, x_sample, cache_l0_mla_ckv, cache_l0_mla_krope, cache_l0_swa_k, cache_l0_swa_v,
           state_l1_gdn, cache_l1_attn_k, cache_l1_attn_v, c, c_ctx,
           w_mod0, b_mod0, ln0, w_in0, mla_q_norm, w_uq, mla_kv_norm, w_ukv, swa_sink, w_out0,
           w_mod1, b_mod1, ln1, w_in1, gdn_conv, gdn_a_log, gdn_dt_bias, gdn_norm, att_q_norm, att_k_norm, w_out1,
           ln_f):
    d = x_prompt.shape[-1]
    bd, td = x_sample.shape[:2]
    bc, tc = x_prompt.shape[:2]
    past = cache_l0_mla_ckv.shape[1]
    row = lambda v: v.reshape(1, -1)
    w0, wuq, wuk, wuvt = _prep_l0(w_in0, w_uq, w_ukv)
    w1 = _prep_l1(w_in1)
    al8 = gdn_a_log.reshape(1, 2 * GDN_HEADS)
    dt8 = gdn_dt_bias.reshape(1, 2 * GDN_HEADS)
    al16 = jnp.pad(al8, ((0, 0), (0, 8)))
    dt16 = jnp.pad(dt8, ((0, 0), (0, 8)))
    p = dict(ln0=row(ln0), w0=w0, qn=row(mla_q_norm), wuq=wuq, kvn=row(mla_kv_norm), wuk=wuk, wuvt=wuvt,
             sink=swa_sink, wout0=w_out0.astype(BF16), ln1=row(ln1), w1=w1, aqn=row(att_q_norm),
             akn=row(att_k_norm), cw=gdn_conv, al=al16, dt=dt16, alt=al16.T, dtt=dt16.T, gn=row(gdn_norm),
             wout1=w_out1.astype(BF16), lnf=row(ln_f))
    n_rows = -(-(bd + 1) // 8) * 8
    c_rows = jnp.concatenate([c, c_ctx[None, :], jnp.zeros((n_rows - bd - 1, d), F32)], axis=0)
    mods = [_mod(c_rows, w_mod0, b_mod0), _mod(c_rows, w_mod1, b_mod1)]
    mod_dec = [m[:bd].reshape(bd, 3, d) for m in mods]
    mod_ctx = [m[bd:bd + 1].reshape(1, 3, d) for m in mods]
    cm, sm = _rope_table(td, MLA_ROPE)
    t0m = (jnp.asarray(_place(cm, 1.0, MLA_NOPE, LANE)), jnp.asarray(_place(sm, 0.0, MLA_NOPE, LANE)))
    cs, ss = _rope_table(td, SWA_DH)
    t0s = (jnp.asarray(np.tile(cs, (1, LANE // SWA_DH))), jnp.asarray(np.tile(ss, (1, LANE // SWA_DH))))
    t1 = tuple(jnp.asarray(a) for a in _rope_table(td, ATT_DH))
    caches = (cache_l0_mla_ckv,
              jnp.pad(cache_l0_mla_krope, ((0, 0), (0, 0), (MLA_NOPE, LANE - MLA_NOPE - MLA_ROPE))),
              cache_l0_swa_k.reshape(bd, past, SWA_KV_HEADS * SWA_DH),
              jnp.swapaxes(cache_l0_swa_v.reshape(bd, past, SWA_KV_HEADS * SWA_DH), 1, 2),
              state_l1_gdn,
              cache_l1_attn_k.reshape(bd, past, ATT_KV_HEADS * ATT_DH),
              jnp.swapaxes(cache_l1_attn_v.reshape(bd, past, ATT_KV_HEADS * ATT_DH), 1, 2))
    y_prompt, ctx0, sfin, ctx1 = _trunk(x_prompt, mod_ctx, None, p, None, tq=tc)
    y_sample, _, _, _ = _trunk(x_sample, mod_dec, caches, p, (t0m, t0s, t1), tq=256)
    ckv32, kr32, kb32, vb32 = ctx0
    kd32, vd32 = ctx1
    return (y_prompt, y_sample, ckv32, kr32,
            kb32.reshape(bc, tc, SWA_KV_HEADS, SWA_DH), vb32.reshape(bc, tc, SWA_KV_HEADS, SWA_DH),
            sfin, kd32.reshape(bc, tc, ATT_KV_HEADS, ATT_DH), vd32.reshape(bc, tc, ATT_KV_HEADS, ATT_DH))
```

```python
import functools
import math

import jax
import jax.numpy as jnp
import numpy as np
from jax import lax
from jax.experimental import pallas as pl
from jax.experimental.pallas import tpu as pltpu

F32 = jnp.float32
BF16 = jnp.bfloat16

GRID_W = 64
ROPE_THETA = 10000.0
EPS = 1e-6
WINDOW = 128
MLA_HEADS, MLA_NOPE, MLA_ROPE, MLA_V = 8, 64, 32, 64
MLA_Q_LORA, MLA_KV_LORA = 384, 256
SWA_HEADS, SWA_KV_HEADS, SWA_DH = 8, 2, 64
GDN_HEADS, GDN_DK, GDN_DV, GDN_CHUNK = 4, 128, 128, 64
GDN_LOCAL_CHUNKS = 4
ATT_HEADS, ATT_KV_HEADS, ATT_DH = 4, 2, 128
LANE = 128
LOG2E = math.log2(math.e)
NEG = -1e30
VMEM_LIMIT = 56 * 1024 * 1024


def _cparams(n_axes):
    return pltpu.CompilerParams(dimension_semantics=("arbitrary",) * n_axes, vmem_limit_bytes=VMEM_LIMIT)


def _dot(a, b):
    return jnp.dot(a, b, preferred_element_type=F32)


def _dot_nt(a, b):
    return lax.dot_general(a, b, (((1,), (1,)), ((), ())), preferred_element_type=F32)


def _dot_tn(a, b):
    return lax.dot_general(a, b, (((0,), (0,)), ((), ())), preferred_element_type=F32)


def _dot_exact(a, b):
    return jnp.dot(a, b, preferred_element_type=F32, precision=lax.Precision.HIGHEST)


def _silu(x):
    return x * jax.nn.sigmoid(x)


def _rms_rows(x, g):
    return x * lax.rsqrt(jnp.mean(x * x, axis=-1, keepdims=True) + EPS) * g


def _rope_block(x, cos, sin, half):
    lane = lax.broadcasted_iota(jnp.int32, x.shape, 1)
    first = (lane // half) % 2 == 0
    partner = jnp.where(first, pltpu.roll(x, LANE - half, 1), pltpu.roll(x, half, 1))
    return x * cos + partner * sin


def _mod_kernel(c_ref, w_ref, b_ref, o_ref):
    a = _silu(c_ref[...]).astype(BF16)
    o_ref[...] = _dot(a, w_ref[...].astype(BF16)) + b_ref[...]


def _mod(c_rows, w_mod, b_mod):
    r, d = c_rows.shape
    n = w_mod.shape[1]
    tn = 1024
    return pl.pallas_call(
        _mod_kernel,
        grid=(n // tn,),
        in_specs=[pl.BlockSpec((r, d), lambda j: (0, 0)),
                  pl.BlockSpec((d, tn), lambda j: (0, j)),
                  pl.BlockSpec((1, tn), lambda j: (0, j))],
        out_specs=pl.BlockSpec((r, tn), lambda j: (0, j)),
        out_shape=jax.ShapeDtypeStruct((r, n), F32),
        compiler_params=_cparams(1),
        name="mod",
    )(c_rows, w_mod, b_mod.reshape(1, n))


def _adaln(x, mod_ref, ln_ref):
    h = _rms_rows(x, ln_ref[...])
    return h * (1.0 + mod_ref[0, 1:2, :]) + mod_ref[0, 0:1, :]


L0_OFF = dict(cq=0, ckv=384, kr=640, za=768, qb=1280, kb=1792, vb=1920, zb=2048)


def _inproj0_kernel(*refs, rope):
    us_ref = refs[-1]
    g = pl.program_id(0)

    @pl.when(g == 0)
    def _():
        us_ref[...] = jnp.zeros_like(us_ref)

    @pl.when(g == pl.num_programs(0) - 1)
    def _():
        _inproj0_finish(refs, rope)

    @pl.when(g < pl.num_programs(0) - 1)
    def _():
        x_ref, mod_ref, ln_ref, w_ref = refs[:4]
        h = _adaln(x_ref[0], mod_ref, ln_ref).astype(BF16)
        _inproj0_finish(refs, rope)
        us_ref[...] = _dot(h, w_ref[...])


def _inproj0_finish(refs, rope):
    if rope:
        (x_ref, mod_ref, ln_ref, w_ref, qn_ref, wuq_ref, kvn_ref, cm_ref, sm_ref, cs_ref, ss_ref,
         qa_ref, ckv_ref, kr_ref, za_ref, qb_ref, kb_ref, vbt_ref, zb_ref, u) = refs
    else:
        (x_ref, mod_ref, ln_ref, w_ref, qn_ref, wuq_ref, kvn_ref,
         qa_ref, ckv_ref, kr_ref, za_ref, qb_ref, kb_ref, vbt_ref, zb_ref,
         ckv32_ref, kr32_ref, kb32_ref, vb32_ref, u) = refs
    o = L0_OFF
    cq = _rms_rows(u[:, o["cq"]:o["cq"] + 384], qn_ref[...]).astype(BF16)
    qa = _dot(cq, wuq_ref[...])
    ckv = _rms_rows(u[:, o["ckv"]:o["ckv"] + 256], kvn_ref[...])
    kr = u[:, o["kr"]:o["kr"] + 128]
    qb = u[:, o["qb"]:o["qb"] + 512]
    kb = u[:, o["kb"]:o["kb"] + 128]
    vb = u[:, o["vb"]:o["vb"] + 128]
    if not rope:
        ckv32_ref[0] = ckv
        kr32_ref[0] = kr[:, 64:96]
        kb32_ref[0] = kb
        vb32_ref[0] = vb
    qa_scale = (MLA_NOPE + MLA_ROPE) ** -0.5 * LOG2E
    qb_scale = SWA_DH ** -0.5 * LOG2E
    for j in range(MLA_HEADS):
        blk = qa[:, j * LANE:(j + 1) * LANE]
        if rope:
            blk = _rope_block(blk, cm_ref[...], sm_ref[...], MLA_ROPE // 4)
        qa_ref[0, :, j * LANE:(j + 1) * LANE] = (blk * qa_scale).astype(BF16)
    for j in range(SWA_HEADS * SWA_DH // LANE):
        blk = qb[:, j * LANE:(j + 1) * LANE]
        if rope:
            blk = _rope_block(blk, cs_ref[...], ss_ref[...], SWA_DH // 4)
        qb_ref[0, :, j * LANE:(j + 1) * LANE] = (blk * qb_scale).astype(BF16)
    if rope:
        kr = _rope_block(kr, cm_ref[...], sm_ref[...], MLA_ROPE // 4)
        kb = _rope_block(kb, cs_ref[...], ss_ref[...], SWA_DH // 4)
    ckv_ref[0] = ckv.astype(BF16)
    kr_ref[0] = kr.astype(BF16)
    kb_ref[0] = kb.astype(BF16)
    za_ref[0] = u[:, o["za"]:o["za"] + 512].astype(BF16)
    zb_ref[0] = u[:, o["zb"]:o["zb"] + 512].astype(BF16)
    vbt_ref[0] = vb.T.astype(BF16)


def _row_tile(t):
    return 512 if t % 512 == 0 else 256


def _inproj0(x, mod, ln, w, qn, wuq, kvn, tables):
    b, t, d = x.shape
    tr = _row_tile(t)
    nj = t // tr
    n_blocks = b * nj
    rope = tables is not None
    bm = mod.shape[0]
    cur = lambda g: jnp.minimum(g, n_blocks - 1)
    held = lambda g: jnp.maximum(g - 1, 0)
    full = lambda a: pl.BlockSpec(a.shape, lambda g: (0,) * a.ndim)
    rows = lambda c: pl.BlockSpec((1, tr, c), lambda g: (held(g) // nj, held(g) % nj, 0))
    in_specs = [pl.BlockSpec((1, tr, d), lambda g: (cur(g) // nj, cur(g) % nj, 0)),
                pl.BlockSpec((1, 3, d), (lambda g: (cur(g) // nj, 0, 0)) if bm > 1 else (lambda g: (0, 0, 0))),
                full(ln), full(w), full(qn), full(wuq), full(kvn)]
    args = [x, mod, ln, w, qn, wuq, kvn]
    if rope:
        for tab in tables:
            in_specs.append(pl.BlockSpec((tr, LANE), lambda g: (held(g) % nj, 0)))
            args.append(tab)
    out_shape = [jax.ShapeDtypeStruct((b, t, 1024), BF16), jax.ShapeDtypeStruct((b, t, 256), BF16),
                 jax.ShapeDtypeStruct((b, t, 128), BF16), jax.ShapeDtypeStruct((b, t, 512), BF16),
                 jax.ShapeDtypeStruct((b, t, 512), BF16), jax.ShapeDtypeStruct((b, t, 128), BF16),
                 jax.ShapeDtypeStruct((b, 128, t), BF16), jax.ShapeDtypeStruct((b, t, 512), BF16)]
    out_specs = [rows(1024), rows(256), rows(128), rows(512), rows(512), rows(128),
                 pl.BlockSpec((1, 128, tr), lambda g: (held(g) // nj, 0, held(g) % nj)), rows(512)]
    if not rope:
        out_shape += [jax.ShapeDtypeStruct((b, t, 256), F32), jax.ShapeDtypeStruct((b, t, 32), F32),
                      jax.ShapeDtypeStruct((b, t, 128), F32), jax.ShapeDtypeStruct((b, t, 128), F32)]
        out_specs += [rows(256), rows(32), rows(128), rows(128)]
    return pl.pallas_call(
        functools.partial(_inproj0_kernel, rope=rope),
        grid=(n_blocks + 1,), in_specs=in_specs, out_specs=out_specs, out_shape=out_shape,
        scratch_shapes=[pltpu.VMEM((tr, w.shape[1]), F32)],
        compiler_params=_cparams(1), name="inproj0_dec" if rope else "inproj0_ctx",
    )(*args)


KEY_CHUNK = 512
ATT_LOOKAHEAD = 4


SUM_ROWS = 16


def _attend_heads(qs, kv_of, chunks, dv, sinks=None):
    nh = len(qs)
    tq = qs[0].shape[0]
    m = [None] * nh
    acc = [None] * nh
    items = [(ci, i) for ci in range(len(chunks)) for i in range(nh)]
    loaded = {}

    def kv(ci, src):
        if (ci, src) not in loaded:
            vt = chunks[ci][1](src)
            ones = (lax.broadcasted_iota(jnp.int32, (SUM_ROWS, vt.shape[1]), 0) == 0).astype(BF16)
            loaded[(ci, src)] = (chunks[ci][0](src), jnp.concatenate([vt, ones], axis=0))
        return loaded[(ci, src)]

    if sinks is not None:
        m = [jnp.full((1, tq), sk, F32) for sk in sinks]
        unit = (lax.broadcasted_iota(jnp.int32, (dv + SUM_ROWS, tq), 0) == dv).astype(F32)
        acc = [unit for _ in sinks]

    scores = {}
    for t in range(len(items) + ATT_LOOKAHEAD):
        if t < len(items):
            ci, i = items[t]
            scores[t] = _dot_nt(kv(ci, kv_of[i])[0], qs[i])
        t0 = t - ATT_LOOKAHEAD
        if t0 < 0:
            continue
        ci, i = items[t0]
        mask = chunks[ci][2]
        si = scores.pop(t0)
        if mask is not None:
            si = jnp.where(mask, si, NEG)
        cm = si.max(axis=0, keepdims=True)
        alpha = None
        if m[i] is None:
            m_new = cm
        else:
            m_new = jnp.maximum(m[i], cm)
            alpha = jnp.exp2(m[i] - m_new)
        p = jnp.exp2(si - m_new)
        m[i] = m_new
        pv = _dot(kv(ci, kv_of[i])[1], p.astype(BF16))
        acc[i] = pv if acc[i] is None else acc[i] * alpha + pv
    return [acc[i][:dv] * (1.0 / acc[i][dv:dv + 1]) for i in range(nh)]


def _key_chunks(n):
    step = KEY_CHUNK if n % KEY_CHUNK == 0 else n
    return [(c0, step) for c0 in range(0, n, step)]


def _mla_kernel(*refs, n_new, n_ctx, hp):
    if n_ctx:
        q_ref, ckv_ref, kr_ref, ckvc_ref, krc_ref, wuk_ref, wuvt_ref, o_ref, k_s, vt_s = refs
    else:
        q_ref, ckv_ref, kr_ref, wuk_ref, wuvt_ref, o_ref, k_s, vt_s = refs
    qi, gi = pl.program_id(1), pl.program_id(2)

    @pl.when((qi == 0) & (gi == 0))
    def _():
        def expand(ckv, kr, r0, n):
            kn = _dot(ckv, wuk_ref[...])
            for j in range(MLA_HEADS):
                k_s[j, r0:r0 + n, :] = (kn[:, j * LANE:(j + 1) * LANE] + kr).astype(BF16)
            vt_s[:, r0:r0 + n] = _dot_nt(wuvt_ref[...], ckv).astype(BF16)

        blk = 512 if n_new % 512 == 0 else 256
        for r0 in range(0, n_new, blk):
            expand(ckv_ref[0, r0:r0 + blk, :], kr_ref[0, r0:r0 + blk, :].astype(F32), r0, blk)
        if n_ctx:
            expand(ckvc_ref[0].astype(BF16), krc_ref[0], n_new, n_ctx)

    qs = [q_ref[0, :, j * LANE:(j + 1) * LANE] for j in range(hp)]
    chunks = []
    for c0, cn in _key_chunks(n_new + n_ctx):
        chunks.append((
            lambda j, c0=c0, cn=cn: k_s[gi * hp + j, c0:c0 + cn, :],
            lambda j, c0=c0, cn=cn: vt_s[pl.ds(pl.multiple_of((gi * hp + j) * MLA_V, MLA_V), MLA_V), c0:c0 + cn],
            None))
    outs = _attend_heads(qs, list(range(hp)), chunks, MLA_V)
    o_ref[0] = jnp.concatenate(outs, axis=0).T.astype(BF16)


def _mla(q, ckv, kr, ckv_c, kr_c, wuk, wuvt, tq):
    b, t, _ = q.shape
    n_ctx = 0 if ckv_c is None else ckv_c.shape[1]
    hp = 8
    tk = t + n_ctx
    rows_q = pl.BlockSpec((1, tq, hp * LANE), lambda i, j, g: (i, j, g))
    per_b = lambda a: pl.BlockSpec((1,) + a.shape[1:], lambda i, j, g: (i, 0, 0))
    full = lambda a: pl.BlockSpec(a.shape, lambda i, j, g: (0, 0))
    in_specs = [rows_q, per_b(ckv), per_b(kr)]
    args = [q, ckv, kr]
    if n_ctx:
        in_specs += [per_b(ckv_c), per_b(kr_c)]
        args += [ckv_c, kr_c]
    in_specs += [full(wuk), full(wuvt)]
    args += [wuk, wuvt]
    return pl.pallas_call(
        functools.partial(_mla_kernel, n_new=t, n_ctx=n_ctx, hp=hp),
        grid=(b, t // tq, MLA_HEADS // hp), in_specs=in_specs,
        out_specs=pl.BlockSpec((1, tq, hp * MLA_V), lambda i, j, g: (i, j, g)),
        out_shape=jax.ShapeDtypeStruct((b, t, MLA_HEADS * MLA_V), BF16),
        scratch_shapes=[pltpu.VMEM((MLA_HEADS, tk, LANE), BF16), pltpu.VMEM((MLA_HEADS * MLA_V, tk), BF16)],
        compiler_params=_cparams(3), name="mla_dec" if n_ctx else "mla_ctx",
    )(*args)


def _swa_kernel(*refs, n_new, n_ctx, tq):
    if n_ctx:
        sink_ref, q_ref, k_ref, vt_ref, kc_ref, vct_ref, o_ref = refs
    else:
        sink_ref, q_ref, k_ref, vt_ref, o_ref = refs
    qi = pl.program_id(1)
    grp = SWA_HEADS // SWA_KV_HEADS
    if n_ctx:
        span = tq + 2 * WINDOW
        q0 = qi * tq
        start = pl.multiple_of(jnp.clip(q0 - WINDOW, 0, n_new - span), LANE)
        kpos = start + lax.broadcasted_iota(jnp.int32, (span, tq), 0)
        qpos = q0 + lax.broadcasted_iota(jnp.int32, (span, tq), 1)
        band = jnp.abs(kpos - qpos) <= WINDOW
    dh = SWA_DH
    chunks = []
    if n_ctx:
        for c0, cn in _key_chunks(span):
            chunks.append((
                lambda g, c0=c0, cn=cn: k_ref[0, pl.ds(start + c0, cn), g * dh:(g + 1) * dh],
                lambda g, c0=c0, cn=cn: vt_ref[0, g * dh:(g + 1) * dh, pl.ds(start + c0, cn)],
                band[c0:c0 + cn]))
        for c0, cn in _key_chunks(n_ctx):
            chunks.append((
                lambda g, c0=c0, cn=cn: kc_ref[0, c0:c0 + cn, g * dh:(g + 1) * dh].astype(BF16),
                lambda g, c0=c0, cn=cn: vct_ref[0, g * dh:(g + 1) * dh, c0:c0 + cn].astype(BF16),
                None))
    else:
        for c0, cn in _key_chunks(n_new):
            chunks.append((
                lambda g, c0=c0, cn=cn: k_ref[0, c0:c0 + cn, g * dh:(g + 1) * dh],
                lambda g, c0=c0, cn=cn: vt_ref[0, g * dh:(g + 1) * dh, c0:c0 + cn],
                None))
    qs = [q_ref[0, :, h * dh:(h + 1) * dh] for h in range(SWA_HEADS)]
    sinks = [sink_ref[h] * LOG2E for h in range(SWA_HEADS)]
    outs = _attend_heads(qs, [h // grp for h in range(SWA_HEADS)], chunks, SWA_DH, sinks)
    o_ref[0] = jnp.concatenate(outs, axis=0).T.astype(BF16)


def _swa(sink, q, k, vt, k_c, v_ct, tq):
    b, t, _ = q.shape
    n_ctx = 0 if k_c is None else k_c.shape[1]
    assert n_ctx == 0 or (t >= tq + 2 * WINDOW and tq % LANE == 0 and WINDOW % LANE == 0)
    per_b = lambda a: pl.BlockSpec((1,) + a.shape[1:], lambda i, j: (i, 0, 0))
    in_specs = [pl.BlockSpec(memory_space=pltpu.SMEM), pl.BlockSpec((1, tq, 512), lambda i, j: (i, j, 0)),
                per_b(k), per_b(vt)]
    args = [sink, q, k, vt]
    if n_ctx:
        in_specs += [per_b(k_c), per_b(v_ct)]
        args += [k_c, v_ct]
    return pl.pallas_call(
        functools.partial(_swa_kernel, n_new=t, n_ctx=n_ctx, tq=tq),
        grid=(b, t // tq), in_specs=in_specs,
        out_specs=pl.BlockSpec((1, tq, 512), lambda i, j: (i, j, 0)),
        out_shape=jax.ShapeDtypeStruct((b, t, 512), BF16),
        compiler_params=_cparams(2), name="swa_dec" if n_ctx else "swa_ctx",
    )(*args)


def _silu_gate(o, z):
    return o * (z / (1.0 + jnp.exp(-z)))


L1_OFF = dict(qkv=0, zc=1536, qd=2048, kd=2560, vd=2816, zd=3072, ab=3584)
L1_W = 3712


def _inproj1_kernel(*refs, rope, nj):
    if rope:
        (oa_ref, za_ref, ob_ref, zb_ref, x_ref, mod0_ref, wo_ref, mod_ref, ln_ref, w_ref, cw_ref,
         qn_ref, kn_ref, c_ref, s_ref,
         x1_ref, qkv_ref, zc_ref, qd_ref, kd_ref, vdt_ref, zd_ref, ab_ref, us_ref, pr_ref) = refs
    else:
        (oa_ref, za_ref, ob_ref, zb_ref, x_ref, mod0_ref, wo_ref, mod_ref, ln_ref, w_ref, cw_ref,
         qn_ref, kn_ref,
         x1_ref, qkv_ref, zc_ref, qd_ref, kd_ref, vdt_ref, zd_ref, ab_ref, kd32_ref, vd32_ref, us_ref, pr_ref) = refs
    g = pl.program_id(0)
    tr = us_ref.shape[0]
    nq = GDN_HEADS * GDN_DK
    n_qkv = 2 * nq + GDN_HEADS * GDN_DV
    o = L1_OFF

    @pl.when(g == 0)
    def _():
        us_ref[...] = jnp.zeros_like(us_ref)
        pr_ref[...] = jnp.zeros_like(pr_ref)

    def conv_held_block(next_row):
        jb = lax.rem(g + nj - 1, nj)
        prv = jnp.where(jb > 0, pr_ref[7:8, :], 0.0)
        nxt = jnp.where(jb < nj - 1, next_row, 0.0)
        row8 = lax.broadcasted_iota(jnp.int32, (8, LANE), 0)
        for j in range(n_qkv // LANE):
            cols = slice(j * LANE, (j + 1) * LANE)
            xj = us_ref[:, cols]
            xp = pltpu.roll(xj, 1, 0)
            xp = jnp.concatenate([jnp.where(row8 == 0, prv[:, cols], xp[0:8]), xp[8:]], axis=0)
            xn = pltpu.roll(xj, tr - 1, 0)
            xn = jnp.concatenate([xn[:tr - 8], jnp.where(row8 == 7, nxt[:, cols], xn[tr - 8:])], axis=0)
            cw = cw_ref[:, cols]
            y = _silu(xp * cw[0:1, :] + xj * cw[1:2, :] + xn * cw[2:3, :])
            if j < 2 * nq // LANE:
                y = y * lax.rsqrt(jnp.sum(y * y, axis=-1, keepdims=True) + EPS)
            if j < nq // LANE:
                y = y * GDN_DK ** -0.5
            qkv_ref[0, :, cols] = y.astype(BF16)
        pr_ref[...] = us_ref[tr - 8:tr, 0:n_qkv]
        zc_ref[0] = us_ref[:, o["zc"]:o["zc"] + 512].astype(BF16)
        zd_ref[0] = us_ref[:, o["zd"]:o["zd"] + 512].astype(BF16)
        ab_ref[0] = us_ref[:, o["ab"]:o["ab"] + 16]
        qd_scale = ATT_DH ** -0.5 * LOG2E
        for j in range(ATT_HEADS):
            blk = _rms_rows(us_ref[:, o["qd"] + j * LANE:o["qd"] + (j + 1) * LANE], qn_ref[...])
            if rope:
                blk = _rope_block(blk, c_ref[...], s_ref[...], ATT_DH // 4)
            qd_ref[0, :, j * LANE:(j + 1) * LANE] = (blk * qd_scale).astype(BF16)
        for j in range(ATT_KV_HEADS):
            blk = _rms_rows(us_ref[:, o["kd"] + j * LANE:o["kd"] + (j + 1) * LANE], kn_ref[...])
            if rope:
                blk = _rope_block(blk, c_ref[...], s_ref[...], ATT_DH // 4)
            else:
                kd32_ref[0, :, j * LANE:(j + 1) * LANE] = blk
            kd_ref[0, :, j * LANE:(j + 1) * LANE] = blk.astype(BF16)
        vd = us_ref[:, o["vd"]:o["vd"] + 256]
        if not rope:
            vd32_ref[0] = vd
        vdt_ref[0] = vd.T.astype(BF16)

    @pl.when(g == pl.num_programs(0) - 1)
    def _():
        conv_held_block(jnp.zeros((1, n_qkv), F32))

    @pl.when(g < pl.num_programs(0) - 1)
    def _():
        _inproj1_block(refs, rope, conv_held_block)


def _inproj1_block(refs, rope, conv_held_block):
    if rope:
        (oa_ref, za_ref, ob_ref, zb_ref, x_ref, mod0_ref, wo_ref, mod_ref, ln_ref, w_ref, cw_ref,
         qn_ref, kn_ref, c_ref, s_ref,
         x1_ref, qkv_ref, zc_ref, qd_ref, kd_ref, vdt_ref, zd_ref, ab_ref, us_ref, pr_ref) = refs
    else:
        (oa_ref, za_ref, ob_ref, zb_ref, x_ref, mod0_ref, wo_ref, mod_ref, ln_ref, w_ref, cw_ref,
         qn_ref, kn_ref,
         x1_ref, qkv_ref, zc_ref, qd_ref, kd_ref, vdt_ref, zd_ref, ab_ref, kd32_ref, vd32_ref, us_ref, pr_ref) = refs
    n_qkv = pr_ref.shape[1]
    y0 = (_dot(_silu_gate(oa_ref[0], za_ref[0]), wo_ref[0:512, :])
          + _dot(_silu_gate(ob_ref[0], zb_ref[0]), wo_ref[512:1024, :]))
    x1 = x_ref[0] + mod0_ref[0, 2:3, :] * y0
    x1_ref[0] = x1
    h = _adaln(x1, mod_ref, ln_ref).astype(BF16)
    conv_held_block(_dot(h[0:8], w_ref[:, 0:n_qkv])[0:1])
    us_ref[...] = _dot(h, w_ref[...])


def _inproj1(oa, za, ob, zb, x, mod0, wo, mod, ln, w, cw, qn, kn, tables):
    b, t, d = x.shape
    tr = _row_tile(t)
    nj = t // tr
    n_blocks = b * nj
    rope = tables is not None
    bm = mod.shape[0]
    n_qkv = cw.shape[1]
    cur = lambda g: jnp.minimum(g, n_blocks - 1)
    held = lambda g: jnp.maximum(g - 1, 0)
    full = lambda a: pl.BlockSpec(a.shape, lambda g: (0,) * a.ndim)
    rows = lambda c: pl.BlockSpec((1, tr, c), lambda g: (cur(g) // nj, cur(g) % nj, 0))
    mod_spec = pl.BlockSpec((1, 3, d), (lambda g: (cur(g) // nj, 0, 0)) if bm > 1 else (lambda g: (0, 0, 0)))
    widths = [512, 512, 512, 512, d]
    in_specs = ([rows(c) for c in widths]
                + [mod_spec, full(wo), mod_spec, full(ln), full(w), full(cw), full(qn), full(kn)])
    args = [oa, za, ob, zb, x, mod0, wo, mod, ln, w, cw, qn, kn]
    hrows = lambda c: pl.BlockSpec((1, tr, c), lambda g: (held(g) // nj, held(g) % nj, 0))
    if rope:
        for tab in tables:
            in_specs.append(pl.BlockSpec((tr, LANE), lambda g: (held(g) % nj, 0)))
            args.append(tab)
    out_shape = [jax.ShapeDtypeStruct((b, t, d), F32),
                 jax.ShapeDtypeStruct((b, t, n_qkv), BF16), jax.ShapeDtypeStruct((b, t, 512), BF16),
                 jax.ShapeDtypeStruct((b, t, 512), BF16), jax.ShapeDtypeStruct((b, t, 256), BF16),
                 jax.ShapeDtypeStruct((b, 256, t), BF16), jax.ShapeDtypeStruct((b, t, 512), BF16),
                 jax.ShapeDtypeStruct((b, t, 16), F32)]
    out_specs = [rows(d), hrows(n_qkv), hrows(512), hrows(512), hrows(256),
                 pl.BlockSpec((1, 256, tr), lambda g: (held(g) // nj, 0, held(g) % nj)), hrows(512), hrows(16)]
    if not rope:
        out_shape += [jax.ShapeDtypeStruct((b, t, 256), F32), jax.ShapeDtypeStruct((b, t, 256), F32)]
        out_specs += [hrows(256), hrows(256)]
    return pl.pallas_call(
        functools.partial(_inproj1_kernel, rope=rope, nj=nj),
        grid=(n_blocks + 1,), in_specs=in_specs, out_specs=out_specs, out_shape=out_shape,
        scratch_shapes=[pltpu.VMEM((tr, L1_W), F32), pltpu.VMEM((8, n_qkv), F32)],
        compiler_params=_cparams(1), name="inproj1_dec" if rope else "inproj1_ctx",
    )(*args)


def _gdn_local(blocks):
    c = GDN_CHUNK
    row = lax.broadcasted_iota(jnp.int32, (c, c), 0)
    col = lax.broadcasted_iota(jnp.int32, (c, c), 1)
    lane2 = lax.broadcasted_iota(jnp.int32, (c, 2 * c), 1)
    eye = (row == col).astype(F32)
    eye_t = jnp.concatenate([eye, jnp.zeros((c, c), F32)], axis=1).astype(BF16)
    chains = [ch for blk in blocks for ch in blk["dirs"]]
    for blk in blocks:
        for ch in blk["dirs"]:
            causal = (row <= col) if ch["upper"] else (row >= col)
            ch["strict"] = (row < col) if ch["upper"] else (row > col)
            ch["decay"] = jnp.exp(jnp.where(causal, ch["gc_col"] - ch["gc_row"], -jnp.inf))
            ch["kb"] = blk["k"] * ch["beta_col"]
            ch["egc"] = jnp.exp(ch["gc_col"])
    for blk in blocks:
        lhs = jnp.concatenate([ch["kb"] for ch in blk["dirs"]] + [blk["q"]], axis=0).astype(BF16)
        a = _dot_nt(lhs, blk["k"].astype(BF16))
        nd = len(blk["dirs"])
        for di, ch in enumerate(blk["dirs"]):
            x = jnp.where(ch["strict"], -(a[di * c:(di + 1) * c] * ch["decay"]), 0.0)
            ch["intra"] = (a[nd * c:] * ch["decay"]).astype(BF16)
            ch["w"] = jnp.concatenate([eye, x], axis=1)
    for _ in range(6):
        for ch in chains:
            w = ch["w"]
            wh = w.astype(BF16)
            lo = w - wh.astype(F32)
            php = jnp.where(lane2 < c, pltpu.roll(w, c, 1), lo).astype(BF16)
            ch["w"] = _dot(jnp.concatenate([wh, php], axis=1),
                           jnp.concatenate([eye_t, wh, lo.astype(BF16), wh], axis=0))
    for blk in blocks:
        for ch in blk["dirs"]:
            rhs = jnp.concatenate([blk["v"] * ch["beta_col"], ch["kb"] * ch["egc"]], axis=1).astype(BF16)
            sol = _dot(ch["w"][:, :c].astype(BF16), rhs)
            ch["u"], ch["wv"] = sol[:, :GDN_DV].astype(BF16), sol[:, GDN_DV:].astype(BF16)
            ch["qe"] = (blk["q"] * ch["egc"]).astype(BF16)
            ch["kd"] = (blk["k"] * jnp.exp(ch["glast"] - ch["gc_col"])).astype(BF16)
            ch["eg"] = jnp.exp(ch["glast"])


def _gdn_scan(chains):
    c = GDN_CHUNK
    for ch in chains:
        ch["sb"] = ch["s"].astype(BF16)
    for ch in chains:
        r = _dot(jnp.concatenate([ch["wv"], ch["qe"]], axis=0), ch["sb"])
        ch["vn"] = (ch["u"].astype(F32) - r[:c]).astype(BF16)
        ch["qs"] = r[c:]
    outs = []
    for ch in chains:
        o = ch["qs"] + _dot(ch["intra"], ch["vn"])
        s_new = ch["s"] * ch["eg"] + _dot_tn(ch["kd"], ch["vn"])
        outs.append((o, s_new))
    return outs


def _gdn_kernel(qkv_ref, ab_ref, abt_ref, al_ref, dt_ref, alt_ref, dtt_ref, s0_ref, o_ref, sf_ref,
                u_s, wv_s, qe_s, kd_s, in_s, eg_s, gcol_s, grow_s, beta_s, st_s, *, t):
    c = GDN_CHUNK
    n = t // c
    nh = GDN_HEADS
    ab = ab_ref[0]
    gact = -jnp.exp(al_ref[...]) * jax.nn.softplus(ab + dt_ref[...])
    lane16 = lax.broadcasted_iota(jnp.int32, ab.shape, 1)
    beta_s[...] = jnp.where(lane16 < 2 * nh, gact, jax.nn.sigmoid(ab))
    r64 = lax.broadcasted_iota(jnp.int32, (c, c), 0)
    c64 = lax.broadcasted_iota(jnp.int32, (c, c), 1)
    tril = (r64 >= c64).astype(F32)
    triu = (r64 <= c64).astype(F32)
    lane_c = lax.broadcasted_iota(jnp.int32, (c, 16), 1)
    sub_c = lax.broadcasted_iota(jnp.int32, (16, c), 0)

    def cum_chunk(i):
        r0 = pl.multiple_of(i * c, c)
        g = beta_s[pl.ds(r0, c), :]
        gcol_s[pl.ds(r0, c), :] = jnp.where(lane_c < nh, _dot_exact(tril, g), _dot_exact(triu, g))
        gt = -jnp.exp(alt_ref[...]) * jax.nn.softplus(abt_ref[0, i] + dtt_ref[...])
        grow_s[i] = jnp.where(sub_c < nh, _dot_exact(gt, triu), _dot_exact(gt, tril))

    ncs = GDN_LOCAL_CHUNKS
    rb = ncs * c
    for sub in range(ncs):
        cum_chunk(sub)

    def local_body(jb, carry):
        r0 = pl.multiple_of(jb * rb, rb)
        nxt_blk = jnp.minimum(jb + 1, n // ncs - 1)
        xs = [qkv_ref[0, pl.ds(r0, rb), j * LANE:(j + 1) * LANE].astype(F32) for j in range(3 * nh)]
        gcol = gcol_s[pl.ds(r0, rb), :]
        bet = beta_s[pl.ds(r0, rb), :]
        blocks = []
        for sub in range(ncs):
            ci = ncs * jb + sub
            rows = slice(sub * c, (sub + 1) * c)
            grow = grow_s[ci]
            for hh in range(nh):
                dirs = []
                for d in range(2):
                    ch = d * nh + hh
                    last = sub * c + (c - 1 if d == 0 else 0)
                    dirs.append(dict(gc_col=gcol[rows, ch:ch + 1], gc_row=grow[ch:ch + 1, :],
                                     beta_col=bet[rows, 2 * nh + ch:2 * nh + ch + 1],
                                     glast=gcol[last:last + 1, ch:ch + 1], upper=(d == 1), ch=ch, ci=ci,
                                     r0=r0 + sub * c))
                blocks.append(dict(q=xs[hh][rows], k=xs[nh + hh][rows], v=xs[2 * nh + hh][rows], dirs=dirs))
        _gdn_local(blocks)
        for blk in blocks:
            for chn in blk["dirs"]:
                ch, rr = chn["ch"], pl.ds(pl.multiple_of(chn["r0"], c), c)
                u_s[ch, rr, :] = chn["u"]
                wv_s[ch, rr, :] = chn["wv"]
                qe_s[ch, rr, :] = chn["qe"]
                kd_s[ch, rr, :] = chn["kd"]
                in_s[ch, chn["ci"]] = chn["intra"]
                eg_s[chn["ci"], ch:ch + 1, :] = jnp.broadcast_to(chn["eg"], (1, LANE))
        for sub in range(ncs):
            cum_chunk(nxt_blk * ncs + sub)
        return carry

    lax.fori_loop(0, n // ncs, local_body, 0)

    for d in range(2):
        for hh in range(nh):
            st_s[d * nh + hh] = s0_ref[0, d, hh]
    o_ref[...] = jnp.zeros_like(o_ref)

    def scan_body(i, carry):
        chains = []
        for d in range(2):
            ci = i if d == 0 else n - 1 - i
            rr = pl.ds(pl.multiple_of(ci * c, c), c)
            eg = eg_s[ci]
            for hh in range(nh):
                ch = d * nh + hh
                chains.append(dict(u=u_s[ch, rr, :], wv=wv_s[ch, rr, :], qe=qe_s[ch, rr, :], kd=kd_s[ch, rr, :],
                                   intra=in_s[ch, ci], eg=eg[ch:ch + 1, :], s=st_s[ch], rr=rr, hh=hh, ch=ch))
        for chn, (o, s_new) in zip(chains, _gdn_scan(chains)):
            st_s[chn["ch"]] = s_new
            cols = slice(chn["hh"] * LANE, (chn["hh"] + 1) * LANE)
            o_ref[0, chn["rr"], cols] = (o_ref[0, chn["rr"], cols].astype(F32) + o).astype(o_ref.dtype)
        return carry

    lax.fori_loop(0, n, scan_body, 0)
    for d in range(2):
        for hh in range(nh):
            sf_ref[0, d, hh] = st_s[d * nh + hh]


def _gdn(qkv, ab, abt, al, dt, alt, dtt, s0):
    b, t, _ = qkv.shape
    n = t // GDN_CHUNK
    assert t % (GDN_CHUNK * GDN_LOCAL_CHUNKS) == 0
    per_b = lambda a: pl.BlockSpec((1,) + a.shape[1:], lambda i: (i,) + (0,) * (a.ndim - 1))
    full = lambda a: pl.BlockSpec(a.shape, lambda i: (0,) * a.ndim)
    return pl.pallas_call(
        functools.partial(_gdn_kernel, t=t), grid=(b,),
        in_specs=[per_b(qkv), per_b(ab), per_b(abt), full(al), full(dt), full(alt), full(dtt), per_b(s0)],
        out_specs=[pl.BlockSpec((1, t, GDN_HEADS * GDN_DV), lambda i: (i, 0, 0)), per_b(s0)],
        out_shape=[jax.ShapeDtypeStruct((b, t, GDN_HEADS * GDN_DV), BF16), jax.ShapeDtypeStruct(s0.shape, F32)],
        scratch_shapes=[pltpu.VMEM((2 * GDN_HEADS, t, LANE), BF16), pltpu.VMEM((2 * GDN_HEADS, t, LANE), BF16),
                        pltpu.VMEM((2 * GDN_HEADS, t, LANE), BF16), pltpu.VMEM((2 * GDN_HEADS, t, LANE), BF16),
                        pltpu.VMEM((2 * GDN_HEADS, n, GDN_CHUNK, GDN_CHUNK), BF16),
                        pltpu.VMEM((n, 2 * GDN_HEADS, LANE), F32),
                        pltpu.VMEM((t, 16), F32), pltpu.VMEM((n, 16, GDN_CHUNK), F32), pltpu.VMEM((t, 16), F32),
                        pltpu.VMEM((2 * GDN_HEADS, GDN_DK, GDN_DV), F32)],
        compiler_params=_cparams(1), name="gdn",
    )(qkv, ab, abt, al, dt, alt, dtt, s0)


def _attd_kernel(*refs, n_ctx):
    if n_ctx:
        q_ref, k_ref, vt_ref, kc_ref, vct_ref, o_ref = refs
    else:
        q_ref, k_ref, vt_ref, o_ref = refs
    n_new = k_ref.shape[1]
    grp = ATT_HEADS // ATT_KV_HEADS
    chunks = []
    for c0, cn in _key_chunks(n_new):
        chunks.append((
            lambda g, c0=c0, cn=cn: k_ref[0, c0:c0 + cn, g * LANE:(g + 1) * LANE],
            lambda g, c0=c0, cn=cn: vt_ref[0, g * LANE:(g + 1) * LANE, c0:c0 + cn],
            None))
    if n_ctx:
        for c0, cn in _key_chunks(n_ctx):
            chunks.append((
                lambda g, c0=c0, cn=cn: kc_ref[0, c0:c0 + cn, g * LANE:(g + 1) * LANE].astype(BF16),
                lambda g, c0=c0, cn=cn: vct_ref[0, g * LANE:(g + 1) * LANE, c0:c0 + cn].astype(BF16),
                None))
    qs = [q_ref[0, :, h * LANE:(h + 1) * LANE] for h in range(ATT_HEADS)]
    outs = _attend_heads(qs, [h // grp for h in range(ATT_HEADS)], chunks, ATT_DH)
    o_ref[0] = jnp.concatenate(outs, axis=0).T.astype(BF16)


def _attd(q, k, vt, k_c, v_ct, tq):
    b, t, _ = q.shape
    n_ctx = 0 if k_c is None else k_c.shape[1]
    per_b = lambda a: pl.BlockSpec((1,) + a.shape[1:], lambda i, j: (i, 0, 0))
    in_specs = [pl.BlockSpec((1, tq, 512), lambda i, j: (i, j, 0)), per_b(k), per_b(vt)]
    args = [q, k, vt]
    if n_ctx:
        in_specs += [per_b(k_c), per_b(v_ct)]
        args += [k_c, v_ct]
    return pl.pallas_call(
        functools.partial(_attd_kernel, n_ctx=n_ctx),
        grid=(b, t // tq), in_specs=in_specs,
        out_specs=pl.BlockSpec((1, tq, 512), lambda i, j: (i, j, 0)),
        out_shape=jax.ShapeDtypeStruct((b, t, 512), BF16),
        compiler_params=_cparams(2), name="attd_dec" if n_ctx else "attd_ctx",
    )(*args)


def _outproj1_kernel(oc_ref, zc_ref, od_ref, zd_ref, x_ref, mod_ref, gn_ref, w_ref, lnf_ref, y_ref):
    parts = []
    for j in range(GDN_HEADS):
        parts.append(_rms_rows(oc_ref[0, :, j * LANE:(j + 1) * LANE].astype(F32), gn_ref[...]).astype(BF16))
    gc = _silu_gate(jnp.concatenate(parts, axis=1), zc_ref[0])
    gd = _silu_gate(od_ref[0], zd_ref[0])
    y = _dot(gc, w_ref[0:512, :]) + _dot(gd, w_ref[512:1024, :])
    x2 = x_ref[0] + mod_ref[0, 2:3, :] * y
    y_ref[0] = _rms_rows(x2, lnf_ref[...])


def _outproj1(oc, zc, od, zd, x, mod, gn, w, lnf):
    b, t, d = x.shape
    tr = 1024 if t % 1024 == 0 else _row_tile(t)
    bm = mod.shape[0]
    rows = lambda c: pl.BlockSpec((1, tr, c), lambda i, j: (i, j, 0))
    full = lambda a: pl.BlockSpec(a.shape, lambda i, j: (0,) * a.ndim)
    return pl.pallas_call(
        _outproj1_kernel, grid=(b, t // tr),
        in_specs=[rows(512), rows(512), rows(512), rows(512), rows(d),
                  pl.BlockSpec((1, 3, d), (lambda i, j: (i, 0, 0)) if bm > 1 else (lambda i, j: (0, 0, 0))),
                  full(gn), full(w), full(lnf)],
        out_specs=rows(d), out_shape=jax.ShapeDtypeStruct((b, t, d), F32),
        compiler_params=_cparams(2), name="outproj1",
    )(oc, zc, od, zd, x, mod, gn, w, lnf)


def _rope_table(n_tok, rot_dim):
    quarter = rot_dim // 4
    inv = np.float32(ROPE_THETA) ** (-np.arange(quarter, dtype=np.float32) / np.float32(quarter))
    tt = np.arange(n_tok)
    pos = np.stack([tt // GRID_W, tt % GRID_W], axis=-1).astype(np.float32)
    ang = (pos[:, :, None] * inv).astype(np.float32)
    cos, sin = np.cos(ang), np.sin(ang)
    c = np.concatenate([cos, cos], axis=-1).reshape(n_tok, rot_dim)
    s = np.concatenate([-sin, sin], axis=-1).reshape(n_tok, rot_dim)
    return c.astype(np.float32), s.astype(np.float32)


def _place(tab, fill, off, width):
    out = np.full((tab.shape[0], width), fill, np.float32)
    out[:, off:off + tab.shape[1]] = tab
    return out


PACK_STEP = 256


def _pack_kernel(wt_ref, o_ref, *, segs):
    d = wt_ref.shape[1]
    lane = lax.broadcasted_iota(jnp.int32, (d, LANE), 1)
    off = 0
    for start, width, lane_off in segs:
        if width % LANE == 0:
            for c0 in range(0, width, PACK_STEP):
                cw = min(PACK_STEP, width - c0)
                o_ref[:, off + c0:off + c0 + cw] = wt_ref[start + c0:start + c0 + cw, :].T.astype(BF16)
            off += width
        else:
            blk = jnp.where(lane < width, wt_ref[start:start + LANE, :].T, 0.0)
            if lane_off:
                blk = pltpu.roll(blk, lane_off, 1)
            o_ref[:, off:off + LANE] = blk.astype(BF16)
            off += LANE


def _pack_cols(w, segs):
    d, n_in = w.shape
    n_out = sum(width if width % LANE == 0 else LANE for _, width, _ in segs)
    return pl.pallas_call(
        functools.partial(_pack_kernel, segs=segs), grid=(1,),
        in_specs=[pl.BlockSpec((n_in, d), lambda i: (0, 0))],
        out_specs=pl.BlockSpec((d, n_out), lambda i: (0, 0)),
        out_shape=jax.ShapeDtypeStruct((d, n_out), BF16),
        compiler_params=_cparams(1), name="pack_cols",
    )(w.T)


def _prep_l0(w_in0, w_uq, w_ukv):
    w = _pack_cols(w_in0, [(0, 640, 0), (640, MLA_ROPE, MLA_NOPE), (672, 1792, 0)])
    uq = w_uq.reshape(MLA_Q_LORA, MLA_HEADS, MLA_NOPE + MLA_ROPE)
    wuq = jnp.pad(uq, ((0, 0), (0, 0), (0, LANE - MLA_NOPE - MLA_ROPE))).reshape(MLA_Q_LORA, MLA_HEADS * LANE)
    ukv = w_ukv.reshape(MLA_KV_LORA, MLA_HEADS, MLA_NOPE + MLA_V)
    wuk = jnp.pad(ukv[:, :, :MLA_NOPE], ((0, 0), (0, 0), (0, LANE - MLA_NOPE))).reshape(MLA_KV_LORA, MLA_HEADS * LANE)
    wuvt = ukv[:, :, MLA_NOPE:].reshape(MLA_KV_LORA, MLA_HEADS * MLA_V).T
    return w, wuq.astype(BF16), wuk.astype(BF16), wuvt.astype(BF16)


def _prep_l1(w_in1):
    return _pack_cols(w_in1, [(0, 1536, 0), (1552, 2048, 0), (1536, 16, 0)])


def _chunk_rows(ab):
    b, t, c = ab.shape
    return jnp.swapaxes(ab.reshape(b, t // GDN_CHUNK, GDN_CHUNK, c), 2, 3)


def _trunk(x, mod, caches, p, tables, tq):
    dec = caches is not None
    t0m, t0s, t1 = tables if dec else (None, None, None)
    (qa, ckv, kr, za, qb, kb, vbt, zb, *ctx0) = _inproj0(
        x, mod[0], p["ln0"], p["w0"], p["qn"], p["wuq"], p["kvn"], (t0m + t0s) if dec else None)
    if dec:
        ckv_c, kr_c, kb_c, vb_ct, s0, kd_c, vd_ct = caches
    else:
        ckv_c = kr_c = kb_c = vb_ct = kd_c = vd_ct = None
        s0 = jnp.zeros((x.shape[0], 2, GDN_HEADS, GDN_DK, GDN_DV), F32)
    oa = _mla(qa, ckv, kr, ckv_c, kr_c, p["wuk"], p["wuvt"], tq)
    ob = _swa(p["sink"], qb, kb, vbt, kb_c, vb_ct, tq)
    (x1, qkv, zc, qd, kd, vdt, zd, ab, *ctx1) = _inproj1(
        oa, za, ob, zb, x, mod[0], p["wout0"], mod[1], p["ln1"], p["w1"], p["cw"], p["aqn"], p["akn"],
        t1 if dec else None)
    oc, sfin = _gdn(qkv, ab, _chunk_rows(ab), p["al"], p["dt"], p["alt"], p["dtt"], s0)
    od = _attd(qd, kd, vdt, kd_c, vd_ct, 2 * tq if x.shape[1] % (2 * tq) == 0 else tq)
    y = _outproj1(oc, zc, od, zd, x1, mod[1], p["gn"], p["wout1"], p["lnf"])
    return y, ctx0, sfin, ctx1


def kernel(x_prompt, x_sample, cache_l0_mla_ckv, cache_l0_mla_krope, cache_l0_swa_k, cache_l0_swa_v,
           state_l1_gdn, cache_l1_attn_k, cache_l1_attn_v, c, c_ctx,
           w_mod0, b_mod0, ln0, w_in0, mla_q_norm, w_uq, mla_kv_norm, w_ukv, swa_sink, w_out0,
           w_mod1, b_mod1, ln1, w_in1, gdn_conv, gdn_a_log, gdn_dt_bias, gdn_norm, att_q_norm, att_k_norm, w_out1,
           ln_f):
    d = x_prompt.shape[-1]
    bd, td = x_sample.shape[:2]
    bc, tc = x_prompt.shape[:2]
    past = cache_l0_mla_ckv.shape[1]
    row = lambda v: v.reshape(1, -1)
    w0, wuq, wuk, wuvt = _prep_l0(w_in0, w_uq, w_ukv)
    w1 = _prep_l1(w_in1)
    al8 = gdn_a_log.reshape(1, 2 * GDN_HEADS)
    dt8 = gdn_dt_bias.reshape(1, 2 * GDN_HEADS)
    al16 = jnp.pad(al8, ((0, 0), (0, 8)))
    dt16 = jnp.pad(dt8, ((0, 0), (0, 8)))
    p = dict(ln0=row(ln0), w0=w0, qn=row(mla_q_norm), wuq=wuq, kvn=row(mla_kv_norm), wuk=wuk, wuvt=wuvt,
             sink=swa_sink, wout0=w_out0.astype(BF16), ln1=row(ln1), w1=w1, aqn=row(att_q_norm),
             akn=row(att_k_norm), cw=gdn_conv, al=al16, dt=dt16, alt=al16.T, dtt=dt16.T, gn=row(gdn_norm),
             wout1=w_out1.astype(BF16), lnf=row(ln_f))
    n_rows = -(-(bd + 1) // 8) * 8
    c_rows = jnp.concatenate([c, c_ctx[None, :], jnp.zeros((n_rows - bd - 1, d), F32)], axis=0)
    mods = [_mod(c_rows, w_mod0, b_mod0), _mod(c_rows, w_mod1, b_mod1)]
    mod_dec = [m[:bd].reshape(bd, 3, d) for m in mods]
    mod_ctx = [m[bd:bd + 1].reshape(1, 3, d) for m in mods]
    cm, sm = _rope_table(td, MLA_ROPE)
    t0m = (jnp.asarray(_place(cm, 1.0, MLA_NOPE, LANE)), jnp.asarray(_place(sm, 0.0, MLA_NOPE, LANE)))
    cs, ss = _rope_table(td, SWA_DH)
    t0s = (jnp.asarray(np.tile(cs, (1, LANE // SWA_DH))), jnp.asarray(np.tile(ss, (1, LANE // SWA_DH))))
    t1 = tuple(jnp.asarray(a) for a in _rope_table(td, ATT_DH))
    caches = (cache_l0_mla_ckv,
              jnp.pad(cache_l0_mla_krope, ((0, 0), (0, 0), (MLA_NOPE, LANE - MLA_NOPE - MLA_ROPE))),
              cache_l0_swa_k.reshape(bd, past, SWA_KV_HEADS * SWA_DH),
              jnp.swapaxes(cache_l0_swa_v.reshape(bd, past, SWA_KV_HEADS * SWA_DH), 1, 2),
              state_l1_gdn,
              cache_l1_attn_k.reshape(bd, past, ATT_KV_HEADS * ATT_DH),
              jnp.swapaxes(cache_l1_attn_v.reshape(bd, past, ATT_KV_HEADS * ATT_DH), 1, 2))
    y_prompt, ctx0, sfin, ctx1 = _trunk(x_prompt, mod_ctx, None, p, None, tq=tc)
    y_sample, _, _, _ = _trunk(x_sample, mod_dec, caches, p, (t0m, t0s, t1), tq=256)
    ckv32, kr32, kb32, vb32 = ctx0
    kd32, vd32 = ctx1
    return (y_prompt, y_sample, ckv32, kr32,
            kb32.reshape(bc, tc, SWA_KV_HEADS, SWA_DH), vb32.reshape(bc, tc, SWA_KV_HEADS, SWA_DH),
            sfin, kd32.reshape(bc, tc, ATT_KV_HEADS, ATT_DH), vd32.reshape(bc, tc, ATT_KV_HEADS, ATT_DH))
```

```python
import functools
import math

import jax
import jax.numpy as jnp
import numpy as np
from jax import lax
from jax.experimental import pallas as pl
from jax.experimental.pallas import tpu as pltpu

F32 = jnp.float32
BF16 = jnp.bfloat16

GRID_W = 64
ROPE_THETA = 10000.0
EPS = 1e-6
WINDOW = 128
MLA_HEADS, MLA_NOPE, MLA_ROPE, MLA_V = 8, 64, 32, 64
MLA_Q_LORA, MLA_KV_LORA = 384, 256
SWA_HEADS, SWA_KV_HEADS, SWA_DH = 8, 2, 64
GDN_HEADS, GDN_DK, GDN_DV, GDN_CHUNK = 4, 128, 128, 64
GDN_LOCAL_CHUNKS = 4
ATT_HEADS, ATT_KV_HEADS, ATT_DH = 4, 2, 128
LANE = 128
LOG2E = math.log2(math.e)
NEG = -1e30
VMEM_LIMIT = 56 * 1024 * 1024


def _cparams(n_axes):
    return pltpu.CompilerParams(dimension_semantics=("arbitrary",) * n_axes, vmem_limit_bytes=VMEM_LIMIT)


def _dot(a, b):
    return jnp.dot(a, b, preferred_element_type=F32)


def _dot_nt(a, b):
    return lax.dot_general(a, b, (((1,), (1,)), ((), ())), preferred_element_type=F32)


def _dot_tn(a, b):
    return lax.dot_general(a, b, (((0,), (0,)), ((), ())), preferred_element_type=F32)


def _dot_exact(a, b):
    return jnp.dot(a, b, preferred_element_type=F32, precision=lax.Precision.HIGHEST)


def _silu(x):
    return x * jax.nn.sigmoid(x)


def _rms_rows(x, g):
    return x * lax.rsqrt(jnp.mean(x * x, axis=-1, keepdims=True) + EPS) * g


def _rope_block(x, cos, sin, half):
    lane = lax.broadcasted_iota(jnp.int32, x.shape, 1)
    first = (lane // half) % 2 == 0
    partner = jnp.where(first, pltpu.roll(x, LANE - half, 1), pltpu.roll(x, half, 1))
    return x * cos + partner * sin


def _mod_kernel(c_ref, w_ref, b_ref, o_ref):
    a = _silu(c_ref[...]).astype(BF16)
    o_ref[...] = _dot(a, w_ref[...].astype(BF16)) + b_ref[...]


def _mod(c_rows, w_mod, b_mod):
    r, d = c_rows.shape
    n = w_mod.shape[1]
    tn = 1024
    return pl.pallas_call(
        _mod_kernel,
        grid=(n // tn,),
        in_specs=[pl.BlockSpec((r, d), lambda j: (0, 0)),
                  pl.BlockSpec((d, tn), lambda j: (0, j)),
                  pl.BlockSpec((1, tn), lambda j: (0, j))],
        out_specs=pl.BlockSpec((r, tn), lambda j: (0, j)),
        out_shape=jax.ShapeDtypeStruct((r, n), F32),
        compiler_params=_cparams(1),
        name="mod",
    )(c_rows, w_mod, b_mod.reshape(1, n))


def _adaln(x, mod_ref, ln_ref):
    h = _rms_rows(x, ln_ref[...])
    return h * (1.0 + mod_ref[0, 1:2, :]) + mod_ref[0, 0:1, :]


L0_OFF = dict(cq=0, ckv=384, kr=640, za=768, qb=1280, kb=1792, vb=1920, zb=2048)


def _inproj0_kernel(*refs, rope):
    us_ref = refs[-1]
    g = pl.program_id(0)

    @pl.when(g == 0)
    def _():
        us_ref[...] = jnp.zeros_like(us_ref)

    @pl.when(g == pl.num_programs(0) - 1)
    def _():
        _inproj0_finish(refs, rope)

    @pl.when(g < pl.num_programs(0) - 1)
    def _():
        x_ref, mod_ref, ln_ref, w_ref = refs[:4]
        h = _adaln(x_ref[0], mod_ref, ln_ref).astype(BF16)
        _inproj0_finish(refs, rope)
        us_ref[...] = _dot(h, w_ref[...])


def _inproj0_finish(refs, rope):
    if rope:
        (x_ref, mod_ref, ln_ref, w_ref, qn_ref, wuq_ref, kvn_ref, cm_ref, sm_ref, cs_ref, ss_ref,
         qa_ref, ckv_ref, kr_ref, za_ref, qb_ref, kb_ref, vbt_ref, zb_ref, u) = refs
    else:
        (x_ref, mod_ref, ln_ref, w_ref, qn_ref, wuq_ref, kvn_ref,
         qa_ref, ckv_ref, kr_ref, za_ref, qb_ref, kb_ref, vbt_ref, zb_ref,
         ckv32_ref, kr32_ref, kb32_ref, vb32_ref, u) = refs
    o = L0_OFF
    cq = _rms_rows(u[:, o["cq"]:o["cq"] + 384], qn_ref[...]).astype(BF16)
    qa = _dot(cq, wuq_ref[...])
    ckv = _rms_rows(u[:, o["ckv"]:o["ckv"] + 256], kvn_ref[...])
    kr = u[:, o["kr"]:o["kr"] + 128]
    qb = u[:, o["qb"]:o["qb"] + 512]
    kb = u[:, o["kb"]:o["kb"] + 128]
    vb = u[:, o["vb"]:o["vb"] + 128]
    if not rope:
        ckv32_ref[0] = ckv
        kr32_ref[0] = kr[:, 64:96]
        kb32_ref[0] = kb
        vb32_ref[0] = vb
    qa_scale = (MLA_NOPE + MLA_ROPE) ** -0.5 * LOG2E
    qb_scale = SWA_DH ** -0.5 * LOG2E
    for j in range(MLA_HEADS):
        blk = qa[:, j * LANE:(j + 1) * LANE]
        if rope:
            blk = _rope_block(blk, cm_ref[...], sm_ref[...], MLA_ROPE // 4)
        qa_ref[0, :, j * LANE:(j + 1) * LANE] = (blk * qa_scale).astype(BF16)
    for j in range(SWA_HEADS * SWA_DH // LANE):
        blk = qb[:, j * LANE:(j + 1) * LANE]
        if rope:
            blk = _rope_block(blk, cs_ref[...], ss_ref[...], SWA_DH // 4)
        qb_ref[0, :, j * LANE:(j + 1) * LANE] = (blk * qb_scale).astype(BF16)
    if rope:
        kr = _rope_block(kr, cm_ref[...], sm_ref[...], MLA_ROPE // 4)
        kb = _rope_block(kb, cs_ref[...], ss_ref[...], SWA_DH // 4)
    ckv_ref[0] = ckv.astype(BF16)
    kr_ref[0] = kr.astype(BF16)
    kb_ref[0] = kb.astype(BF16)
    za_ref[0] = u[:, o["za"]:o["za"] + 512].astype(BF16)
    zb_ref[0] = u[:, o["zb"]:o["zb"] + 512].astype(BF16)
    vbt_ref[0] = vb.T.astype(BF16)


def _row_tile(t):
    return 512 if t % 512 == 0 else 256


def _inproj0(x, mod, ln, w, qn, wuq, kvn, tables):
    b, t, d = x.shape
    tr = _row_tile(t)
    nj = t // tr
    n_blocks = b * nj
    rope = tables is not None
    bm = mod.shape[0]
    cur = lambda g: jnp.minimum(g, n_blocks - 1)
    held = lambda g: jnp.maximum(g - 1, 0)
    full = lambda a: pl.BlockSpec(a.shape, lambda g: (0,) * a.ndim)
    rows = lambda c: pl.BlockSpec((1, tr, c), lambda g: (held(g) // nj, held(g) % nj, 0))
    in_specs = [pl.BlockSpec((1, tr, d), lambda g: (cur(g) // nj, cur(g) % nj, 0)),
                pl.BlockSpec((1, 3, d), (lambda g: (cur(g) // nj, 0, 0)) if bm > 1 else (lambda g: (0, 0, 0))),
                full(ln), full(w), full(qn), full(wuq), full(kvn)]
    args = [x, mod, ln, w, qn, wuq, kvn]
    if rope:
        for tab in tables:
            in_specs.append(pl.BlockSpec((tr, LANE), lambda g: (held(g) % nj, 0)))
            args.append(tab)
    out_shape = [jax.ShapeDtypeStruct((b, t, 1024), BF16), jax.ShapeDtypeStruct((b, t, 256), BF16),
                 jax.ShapeDtypeStruct((b, t, 128), BF16), jax.ShapeDtypeStruct((b, t, 512), BF16),
                 jax.ShapeDtypeStruct((b, t, 512), BF16), jax.ShapeDtypeStruct((b, t, 128), BF16),
                 jax.ShapeDtypeStruct((b, 128, t), BF16), jax.ShapeDtypeStruct((b, t, 512), BF16)]
    out_specs = [rows(1024), rows(256), rows(128), rows(512), rows(512), rows(128),
                 pl.BlockSpec((1, 128, tr), lambda g: (held(g) // nj, 0, held(g) % nj)), rows(512)]
    if not rope:
        out_shape += [jax.ShapeDtypeStruct((b, t, 256), F32), jax.ShapeDtypeStruct((b, t, 32), F32),
                      jax.ShapeDtypeStruct((b, t, 128), F32), jax.ShapeDtypeStruct((b, t, 128), F32)]
        out_specs += [rows(256), rows(32), rows(128), rows(128)]
    return pl.pallas_call(
        functools.partial(_inproj0_kernel, rope=rope),
        grid=(n_blocks + 1,), in_specs=in_specs, out_specs=out_specs, out_shape=out_shape,
        scratch_shapes=[pltpu.VMEM((tr, w.shape[1]), F32)],
        compiler_params=_cparams(1), name="inproj0_dec" if rope else "inproj0_ctx",
    )(*args)


KEY_CHUNK = 512
ATT_LOOKAHEAD = 4


SUM_ROWS = 16


def _attend_heads(qs, kv_of, chunks, dv, sinks=None):
    nh = len(qs)
    tq = qs[0].shape[0]
    m = [None] * nh
    acc = [None] * nh
    items = [(ci, i) for ci in range(len(chunks)) for i in range(nh)]
    loaded = {}

    def kv(ci, src):
        if (ci, src) not in loaded:
            vt = chunks[ci][1](src)
            ones = (lax.broadcasted_iota(jnp.int32, (SUM_ROWS, vt.shape[1]), 0) == 0).astype(BF16)
            loaded[(ci, src)] = (chunks[ci][0](src), jnp.concatenate([vt, ones], axis=0))
        return loaded[(ci, src)]

    if sinks is not None:
        m = [jnp.full((1, tq), sk, F32) for sk in sinks]
        unit = (lax.broadcasted_iota(jnp.int32, (dv + SUM_ROWS, tq), 0) == dv).astype(F32)
        acc = [unit for _ in sinks]

    scores = {}
    for t in range(len(items) + ATT_LOOKAHEAD):
        if t < len(items):
            ci, i = items[t]
            scores[t] = _dot_nt(kv(ci, kv_of[i])[0], qs[i])
        t0 = t - ATT_LOOKAHEAD
        if t0 < 0:
            continue
        ci, i = items[t0]
        mask = chunks[ci][2]
        si = scores.pop(t0)
        if mask is not None:
            si = jnp.where(mask, si, NEG)
        cm = si.max(axis=0, keepdims=True)
        alpha = None
        if m[i] is None:
            m_new = cm
        else:
            m_new = jnp.maximum(m[i], cm)
            alpha = jnp.exp2(m[i] - m_new)
        p = jnp.exp2(si - m_new)
        m[i] = m_new
        pv = _dot(kv(ci, kv_of[i])[1], p.astype(BF16))
        acc[i] = pv if acc[i] is None else acc[i] * alpha + pv
    return [acc[i][:dv] * (1.0 / acc[i][dv:dv + 1]) for i in range(nh)]


def _key_chunks(n):
    step = KEY_CHUNK if n % KEY_CHUNK == 0 else n
    return [(c0, step) for c0 in range(0, n, step)]


def _mla_kernel(*refs, n_new, n_ctx, hp):
    if n_ctx:
        q_ref, ckv_ref, kr_ref, ckvc_ref, krc_ref, wuk_ref, wuvt_ref, o_ref, k_s, vt_s = refs
    else:
        q_ref, ckv_ref, kr_ref, wuk_ref, wuvt_ref, o_ref, k_s, vt_s = refs
    qi, gi = pl.program_id(1), pl.program_id(2)

    @pl.when((qi == 0) & (gi == 0))
    def _():
        def expand(ckv, kr, r0, n):
            kn = _dot(ckv, wuk_ref[...])
            for j in range(MLA_HEADS):
                k_s[j, r0:r0 + n, :] = (kn[:, j * LANE:(j + 1) * LANE] + kr).astype(BF16)
            vt_s[:, r0:r0 + n] = _dot_nt(wuvt_ref[...], ckv).astype(BF16)

        blk = 512 if n_new % 512 == 0 else 256
        for r0 in range(0, n_new, blk):
            expand(ckv_ref[0, r0:r0 + blk, :], kr_ref[0, r0:r0 + blk, :].astype(F32), r0, blk)
        if n_ctx:
            expand(ckvc_ref[0].astype(BF16), krc_ref[0], n_new, n_ctx)

    qs = [q_ref[0, :, j * LANE:(j + 1) * LANE] for j in range(hp)]
    chunks = []
    for c0, cn in _key_chunks(n_new + n_ctx):
        chunks.append((
            lambda j, c0=c0, cn=cn: k_s[gi * hp + j, c0:c0 + cn, :],
            lambda j, c0=c0, cn=cn: vt_s[pl.ds(pl.multiple_of((gi * hp + j) * MLA_V, MLA_V), MLA_V), c0:c0 + cn],
            None))
    outs = _attend_heads(qs, list(range(hp)), chunks, MLA_V)
    o_ref[0] = jnp.concatenate(outs, axis=0).T.astype(BF16)


def _mla(q, ckv, kr, ckv_c, kr_c, wuk, wuvt, tq):
    b, t, _ = q.shape
    n_ctx = 0 if ckv_c is None else ckv_c.shape[1]
    hp = 8
    tk = t + n_ctx
    rows_q = pl.BlockSpec((1, tq, hp * LANE), lambda i, j, g: (i, j, g))
    per_b = lambda a: pl.BlockSpec((1,) + a.shape[1:], lambda i, j, g: (i, 0, 0))
    full = lambda a: pl.BlockSpec(a.shape, lambda i, j, g: (0, 0))
    in_specs = [rows_q, per_b(ckv), per_b(kr)]
    args = [q, ckv, kr]
    if n_ctx:
        in_specs += [per_b(ckv_c), per_b(kr_c)]
        args += [ckv_c, kr_c]
    in_specs += [full(wuk), full(wuvt)]
    args += [wuk, wuvt]
    return pl.pallas_call(
        functools.partial(_mla_kernel, n_new=t, n_ctx=n_ctx, hp=hp),
        grid=(b, t // tq, MLA_HEADS // hp), in_specs=in_specs,
        out_specs=pl.BlockSpec((1, tq, hp * MLA_V), lambda i, j, g: (i, j, g)),
        out_shape=jax.ShapeDtypeStruct((b, t, MLA_HEADS * MLA_V), BF16),
        scratch_shapes=[pltpu.VMEM((MLA_HEADS, tk, LANE), BF16), pltpu.VMEM((MLA_HEADS * MLA_V, tk), BF16)],
        compiler_params=_cparams(3), name="mla_dec" if n_ctx else "mla_ctx",
    )(*args)


def _swa_kernel(*refs, n_new, n_ctx, tq):
    if n_ctx:
        sink_ref, q_ref, k_ref, vt_ref, kc_ref, vct_ref, o_ref = refs
    else:
        sink_ref, q_ref, k_ref, vt_ref, o_ref = refs
    qi = pl.program_id(1)
    grp = SWA_HEADS // SWA_KV_HEADS
    if n_ctx:
        span = tq + 2 * WINDOW
        q0 = qi * tq
        start = pl.multiple_of(jnp.clip(q0 - WINDOW, 0, n_new - span), LANE)
        kpos = start + lax.broadcasted_iota(jnp.int32, (span, tq), 0)
        qpos = q0 + lax.broadcasted_iota(jnp.int32, (span, tq), 1)
        band = jnp.abs(kpos - qpos) <= WINDOW
    dh = SWA_DH
    chunks = []
    if n_ctx:
        for c0, cn in _key_chunks(span):
            chunks.append((
                lambda g, c0=c0, cn=cn: k_ref[0, pl.ds(start + c0, cn), g * dh:(g + 1) * dh],
                lambda g, c0=c0, cn=cn: vt_ref[0, g * dh:(g + 1) * dh, pl.ds(start + c0, cn)],
                band[c0:c0 + cn]))
        for c0, cn in _key_chunks(n_ctx):
            chunks.append((
                lambda g, c0=c0, cn=cn: kc_ref[0, c0:c0 + cn, g * dh:(g + 1) * dh].astype(BF16),
                lambda g, c0=c0, cn=cn: vct_ref[0, g * dh:(g + 1) * dh, c0:c0 + cn].astype(BF16),
                None))
    else:
        for c0, cn in _key_chunks(n_new):
            chunks.append((
                lambda g, c0=c0, cn=cn: k_ref[0, c0:c0 + cn, g * dh:(g + 1) * dh],
                lambda g, c0=c0, cn=cn: vt_ref[0, g * dh:(g + 1) * dh, c0:c0 + cn],
                None))
    qs = [q_ref[0, :, h * dh:(h + 1) * dh] for h in range(SWA_HEADS)]
    sinks = [sink_ref[h] * LOG2E for h in range(SWA_HEADS)]
    outs = _attend_heads(qs, [h // grp for h in range(SWA_HEADS)], chunks, SWA_DH, sinks)
    o_ref[0] = jnp.concatenate(outs, axis=0).T.astype(BF16)


def _swa(sink, q, k, vt, k_c, v_ct, tq):
    b, t, _ = q.shape
    n_ctx = 0 if k_c is None else k_c.shape[1]
    assert n_ctx == 0 or (t >= tq + 2 * WINDOW and tq % LANE == 0 and WINDOW % LANE == 0)
    per_b = lambda a: pl.BlockSpec((1,) + a.shape[1:], lambda i, j: (i, 0, 0))
    in_specs = [pl.BlockSpec(memory_space=pltpu.SMEM), pl.BlockSpec((1, tq, 512), lambda i, j: (i, j, 0)),
                per_b(k), per_b(vt)]
    args = [sink, q, k, vt]
    if n_ctx:
        in_specs += [per_b(k_c), per_b(v_ct)]
        args += [k_c, v_ct]
    return pl.pallas_call(
        functools.partial(_swa_kernel, n_new=t, n_ctx=n_ctx, tq=tq),
        grid=(b, t // tq), in_specs=in_specs,
        out_specs=pl.BlockSpec((1, tq, 512), lambda i, j: (i, j, 0)),
        out_shape=jax.ShapeDtypeStruct((b, t, 512), BF16),
        compiler_params=_cparams(2), name="swa_dec" if n_ctx else "swa_ctx",
    )(*args)


def _silu_gate(o, z):
    return o * (z / (1.0 + jnp.exp(-z)))


L1_OFF = dict(qkv=0, zc=1536, qd=2048, kd=2560, vd=2816, zd=3072, ab=3584)
L1_W = 3712


def _inproj1_kernel(*refs, rope, nj):
    if rope:
        (oa_ref, za_ref, ob_ref, zb_ref, x_ref, mod0_ref, wo_ref, mod_ref, ln_ref, w_ref, cw_ref,
         qn_ref, kn_ref, c_ref, s_ref,
         x1_ref, qkv_ref, zc_ref, qd_ref, kd_ref, vdt_ref, zd_ref, ab_ref, us_ref, pr_ref) = refs
    else:
        (oa_ref, za_ref, ob_ref, zb_ref, x_ref, mod0_ref, wo_ref, mod_ref, ln_ref, w_ref, cw_ref,
         qn_ref, kn_ref,
         x1_ref, qkv_ref, zc_ref, qd_ref, kd_ref, vdt_ref, zd_ref, ab_ref, kd32_ref, vd32_ref, us_ref, pr_ref) = refs
    g = pl.program_id(0)
    tr = us_ref.shape[0]
    nq = GDN_HEADS * GDN_DK
    n_qkv = 2 * nq + GDN_HEADS * GDN_DV
    o = L1_OFF

    @pl.when(g == 0)
    def _():
        us_ref[...] = jnp.zeros_like(us_ref)
        pr_ref[...] = jnp.zeros_like(pr_ref)

    def conv_held_block(next_row):
        jb = lax.rem(g + nj - 1, nj)
        prv = jnp.where(jb > 0, pr_ref[7:8, :], 0.0)
        nxt = jnp.where(jb < nj - 1, next_row, 0.0)
        row8 = lax.broadcasted_iota(jnp.int32, (8, LANE), 0)
        for j in range(n_qkv // LANE):
            cols = slice(j * LANE, (j + 1) * LANE)
            xj = us_ref[:, cols]
            xp = pltpu.roll(xj, 1, 0)
            xp = jnp.concatenate([jnp.where(row8 == 0, prv[:, cols], xp[0:8]), xp[8:]], axis=0)
            xn = pltpu.roll(xj, tr - 1, 0)
            xn = jnp.concatenate([xn[:tr - 8], jnp.where(row8 == 7, nxt[:, cols], xn[tr - 8:])], axis=0)
            cw = cw_ref[:, cols]
            y = _silu(xp * cw[0:1, :] + xj * cw[1:2, :] + xn * cw[2:3, :])
            if j < 2 * nq // LANE:
                y = y * lax.rsqrt(jnp.sum(y * y, axis=-1, keepdims=True) + EPS)
            if j < nq // LANE:
                y = y * GDN_DK ** -0.5
            qkv_ref[0, :, cols] = y.astype(BF16)
        pr_ref[...] = us_ref[tr - 8:tr, 0:n_qkv]
        zc_ref[0] = us_ref[:, o["zc"]:o["zc"] + 512].astype(BF16)
        zd_ref[0] = us_ref[:, o["zd"]:o["zd"] + 512].astype(BF16)
        ab_ref[0] = us_ref[:, o["ab"]:o["ab"] + 16]
        qd_scale = ATT_DH ** -0.5 * LOG2E
        for j in range(ATT_HEADS):
            blk = _rms_rows(us_ref[:, o["qd"] + j * LANE:o["qd"] + (j + 1) * LANE], qn_ref[...])
            if rope:
                blk = _rope_block(blk, c_ref[...], s_ref[...], ATT_DH // 4)
            qd_ref[0, :, j * LANE:(j + 1) * LANE] = (blk * qd_scale).astype(BF16)
        for j in range(ATT_KV_HEADS):
            blk = _rms_rows(us_ref[:, o["kd"] + j * LANE:o["kd"] + (j + 1) * LANE], kn_ref[...])
            if rope:
                blk = _rope_block(blk, c_ref[...], s_ref[...], ATT_DH // 4)
            else:
                kd32_ref[0, :, j * LANE:(j + 1) * LANE] = blk
            kd_ref[0, :, j * LANE:(j + 1) * LANE] = blk.astype(BF16)
        vd = us_ref[:, o["vd"]:o["vd"] + 256]
        if not rope:
            vd32_ref[0] = vd
        vdt_ref[0] = vd.T.astype(BF16)

    @pl.when(g == pl.num_programs(0) - 1)
    def _():
        conv_held_block(jnp.zeros((1, n_qkv), F32))

    @pl.when(g < pl.num_programs(0) - 1)
    def _():
        _inproj1_block(refs, rope, conv_held_block)


def _inproj1_block(refs, rope, conv_held_block):
    if rope:
        (oa_ref, za_ref, ob_ref, zb_ref, x_ref, mod0_ref, wo_ref, mod_ref, ln_ref, w_ref, cw_ref,
         qn_ref, kn_ref, c_ref, s_ref,
         x1_ref, qkv_ref, zc_ref, qd_ref, kd_ref, vdt_ref, zd_ref, ab_ref, us_ref, pr_ref) = refs
    else:
        (oa_ref, za_ref, ob_ref, zb_ref, x_ref, mod0_ref, wo_ref, mod_ref, ln_ref, w_ref, cw_ref,
         qn_ref, kn_ref,
         x1_ref, qkv_ref, zc_ref, qd_ref, kd_ref, vdt_ref, zd_ref, ab_ref, kd32_ref, vd32_ref, us_ref, pr_ref) = refs
    n_qkv = pr_ref.shape[1]
    y0 = (_dot(_silu_gate(oa_ref[0], za_ref[0]), wo_ref[0:512, :])
          + _dot(_silu_gate(ob_ref[0], zb_ref[0]), wo_ref[512:1024, :]))
    x1 = x_ref[0] + mod0_ref[0, 2:3, :] * y0
    x1_ref[0] = x1
    h = _adaln(x1, mod_ref, ln_ref).astype(BF16)
    conv_held_block(_dot(h[0:8], w_ref[:, 0:n_qkv])[0:1])
    us_ref[...] = _dot(h, w_ref[...])


def _inproj1(oa, za, ob, zb, x, mod0, wo, mod, ln, w, cw, qn, kn, tables):
    b, t, d = x.shape
    tr = _row_tile(t)
    nj = t // tr
    n_blocks = b * nj
    rope = tables is not None
    bm = mod.shape[0]
    n_qkv = cw.shape[1]
    cur = lambda g: jnp.minimum(g, n_blocks - 1)
    held = lambda g: jnp.maximum(g - 1, 0)
    full = lambda a: pl.BlockSpec(a.shape, lambda g: (0,) * a.ndim)
    rows = lambda c: pl.BlockSpec((1, tr, c), lambda g: (cur(g) // nj, cur(g) % nj, 0))
    mod_spec = pl.BlockSpec((1, 3, d), (lambda g: (cur(g) // nj, 0, 0)) if bm > 1 else (lambda g: (0, 0, 0)))
    widths = [512, 512, 512, 512, d]
    in_specs = ([rows(c) for c in widths]
                + [mod_spec, full(wo), mod_spec, full(ln), full(w), full(cw), full(qn), full(kn)])
    args = [oa, za, ob, zb, x, mod0, wo, mod, ln, w, cw, qn, kn]
    hrows = lambda c: pl.BlockSpec((1, tr, c), lambda g: (held(g) // nj, held(g) % nj, 0))
    if rope:
        for tab in tables:
            in_specs.append(pl.BlockSpec((tr, LANE), lambda g: (held(g) % nj, 0)))
            args.append(tab)
    out_shape = [jax.ShapeDtypeStruct((b, t, d), F32),
                 jax.ShapeDtypeStruct((b, t, n_qkv), BF16), jax.ShapeDtypeStruct((b, t, 512), BF16),
                 jax.ShapeDtypeStruct((b, t, 512), BF16), jax.ShapeDtypeStruct((b, t, 256), BF16),
                 jax.ShapeDtypeStruct((b, 256, t), BF16), jax.ShapeDtypeStruct((b, t, 512), BF16),
                 jax.ShapeDtypeStruct((b, t, 16), F32)]
    out_specs = [rows(d), hrows(n_qkv), hrows(512), hrows(512), hrows(256),
                 pl.BlockSpec((1, 256, tr), lambda g: (held(g) // nj, 0, held(g) % nj)), hrows(512), hrows(16)]
    if not rope:
        out_shape += [jax.ShapeDtypeStruct((b, t, 256), F32), jax.ShapeDtypeStruct((b, t, 256), F32)]
        out_specs += [hrows(256), hrows(256)]
    return pl.pallas_call(
        functools.partial(_inproj1_kernel, rope=rope, nj=nj),
        grid=(n_blocks + 1,), in_specs=in_specs, out_specs=out_specs, out_shape=out_shape,
        scratch_shapes=[pltpu.VMEM((tr, L1_W), F32), pltpu.VMEM((8, n_qkv), F32)],
        compiler_params=_cparams(1), name="inproj1_dec" if rope else "inproj1_ctx",
    )(*args)


def _gdn_local(blocks):
    c = GDN_CHUNK
    row = lax.broadcasted_iota(jnp.int32, (c, c), 0)
    col = lax.broadcasted_iota(jnp.int32, (c, c), 1)
    lane2 = lax.broadcasted_iota(jnp.int32, (c, 2 * c), 1)
    eye = (row == col).astype(F32)
    eye_t = jnp.concatenate([eye, jnp.zeros((c, c), F32)], axis=1).astype(BF16)
    chains = [ch for blk in blocks for ch in blk["dirs"]]
    for blk in blocks:
        for ch in blk["dirs"]:
            causal = (row <= col) if ch["upper"] else (row >= col)
            ch["strict"] = (row < col) if ch["upper"] else (row > col)
            ch["decay"] = jnp.exp(jnp.where(causal, ch["gc_col"] - ch["gc_row"], -jnp.inf))
            ch["kb"] = blk["k"] * ch["beta_col"]
            ch["egc"] = jnp.exp(ch["gc_col"])
    for blk in blocks:
        lhs = jnp.concatenate([ch["kb"] for ch in blk["dirs"]] + [blk["q"]], axis=0).astype(BF16)
        a = _dot_nt(lhs, blk["k"].astype(BF16))
        nd = len(blk["dirs"])
        for di, ch in enumerate(blk["dirs"]):
            x = jnp.where(ch["strict"], -(a[di * c:(di + 1) * c] * ch["decay"]), 0.0)
            ch["intra"] = (a[nd * c:] * ch["decay"]).astype(BF16)
            ch["w"] = jnp.concatenate([eye, x], axis=1)
    for _ in range(6):
        for ch in chains:
            w = ch["w"]
            wh = w.astype(BF16)
            lo = w - wh.astype(F32)
            php = jnp.where(lane2 < c, pltpu.roll(w, c, 1), lo).astype(BF16)
            ch["w"] = _dot(jnp.concatenate([wh, php], axis=1),
                           jnp.concatenate([eye_t, wh, lo.astype(BF16), wh], axis=0))
    for blk in blocks:
        for ch in blk["dirs"]:
            rhs = jnp.concatenate([blk["v"] * ch["beta_col"], ch["kb"] * ch["egc"]], axis=1).astype(BF16)
            sol = _dot(ch["w"][:, :c].astype(BF16), rhs)
            ch["u"], ch["wv"] = sol[:, :GDN_DV].astype(BF16), sol[:, GDN_DV:].astype(BF16)
            ch["qe"] = (blk["q"] * ch["egc"]).astype(BF16)
            ch["kd"] = (blk["k"] * jnp.exp(ch["glast"] - ch["gc_col"])).astype(BF16)
            ch["eg"] = jnp.exp(ch["glast"])


def _gdn_scan(chains):
    c = GDN_CHUNK
    for ch in chains:
        ch["sb"] = ch["s"].astype(BF16)
    for ch in chains:
        r = _dot(jnp.concatenate([ch["wv"], ch["qe"]], axis=0), ch["sb"])
        ch["vn"] = (ch["u"].astype(F32) - r[:c]).astype(BF16)
        ch["qs"] = r[c:]
    outs = []
    for ch in chains:
        o = ch["qs"] + _dot(ch["intra"], ch["vn"])
        s_new = ch["s"] * ch["eg"] + _dot_tn(ch["kd"], ch["vn"])
        outs.append((o, s_new))
    return outs


def _gdn_kernel(qkv_ref, ab_ref, abt_ref, al_ref, dt_ref, alt_ref, dtt_ref, s0_ref, o_ref, sf_ref,
                u_s, wv_s, qe_s, kd_s, in_s, eg_s, gcol_s, grow_s, beta_s, st_s, *, t):
    c = GDN_CHUNK
    n = t // c
    nh = GDN_HEADS
    ab = ab_ref[0]
    gact = -jnp.exp(al_ref[...]) * jax.nn.softplus(ab + dt_ref[...])
    lane16 = lax.broadcasted_iota(jnp.int32, ab.shape, 1)
    beta_s[...] = jnp.where(lane16 < 2 * nh, gact, jax.nn.sigmoid(ab))
    r64 = lax.broadcasted_iota(jnp.int32, (c, c), 0)
    c64 = lax.broadcasted_iota(jnp.int32, (c, c), 1)
    tril = (r64 >= c64).astype(F32)
    triu = (r64 <= c64).astype(F32)
    lane_c = lax.broadcasted_iota(jnp.int32, (c, 16), 1)
    sub_c = lax.broadcasted_iota(jnp.int32, (16, c), 0)

    def cum_chunk(i):
        r0 = pl.multiple_of(i * c, c)
        g = beta_s[pl.ds(r0, c), :]
        gcol_s[pl.ds(r0, c), :] = jnp.where(lane_c < nh, _dot_exact(tril, g), _dot_exact(triu, g))
        gt = -jnp.exp(alt_ref[...]) * jax.nn.softplus(abt_ref[0, i] + dtt_ref[...])
        grow_s[i] = jnp.where(sub_c < nh, _dot_exact(gt, triu), _dot_exact(gt, tril))

    ncs = GDN_LOCAL_CHUNKS
    rb = ncs * c
    for sub in range(ncs):
        cum_chunk(sub)

    def local_body(jb, carry):
        r0 = pl.multiple_of(jb * rb, rb)
        nxt_blk = jnp.minimum(jb + 1, n // ncs - 1)
        xs = [qkv_ref[0, pl.ds(r0, rb), j * LANE:(j + 1) * LANE].astype(F32) for j in range(3 * nh)]
        gcol = gcol_s[pl.ds(r0, rb), :]
        bet = beta_s[pl.ds(r0, rb), :]
        blocks = []
        for sub in range(ncs):
            ci = ncs * jb + sub
            rows = slice(sub * c, (sub + 1) * c)
            grow = grow_s[ci]
            for hh in range(nh):
                dirs = []
                for d in range(2):
                    ch = d * nh + hh
                    last = sub * c + (c - 1 if d == 0 else 0)
                    dirs.append(dict(gc_col=gcol[rows, ch:ch + 1], gc_row=grow[ch:ch + 1, :],
                                     beta_col=bet[rows, 2 * nh + ch:2 * nh + ch + 1],
                                     glast=gcol[last:last + 1, ch:ch + 1], upper=(d == 1), ch=ch, ci=ci,
                                     r0=r0 + sub * c))
                blocks.append(dict(q=xs[hh][rows], k=xs[nh + hh][rows], v=xs[2 * nh + hh][rows], dirs=dirs))
        _gdn_local(blocks)
        for blk in blocks:
            for chn in blk["dirs"]:
                ch, rr = chn["ch"], pl.ds(pl.multiple_of(chn["r0"], c), c)
                u_s[ch, rr, :] = chn["u"]
                wv_s[ch, rr, :] = chn["wv"]
                qe_s[ch, rr, :] = chn["qe"]
                kd_s[ch, rr, :] = chn["kd"]
                in_s[ch, chn["ci"]] = chn["intra"]
                eg_s[chn["ci"], ch:ch + 1, :] = jnp.broadcast_to(chn["eg"], (1, LANE))
        for sub in range(ncs):
            cum_chunk(nxt_blk * ncs + sub)
        return carry

    lax.fori_loop(0, n // ncs, local_body, 0)

    for d in range(2):
        for hh in range(nh):
            st_s[d * nh + hh] = s0_ref[0, d, hh]
    o_ref[...] = jnp.zeros_like(o_ref)

    def scan_body(i, carry):
        chains = []
        for d in range(2):
            ci = i if d == 0 else n - 1 - i
            rr = pl.ds(pl.multiple_of(ci * c, c), c)
            eg = eg_s[ci]
            for hh in range(nh):
                ch = d * nh + hh
                chains.append(dict(u=u_s[ch, rr, :], wv=wv_s[ch, rr, :], qe=qe_s[ch, rr, :], kd=kd_s[ch, rr, :],
                                   intra=in_s[ch, ci], eg=eg[ch:ch + 1, :], s=st_s[ch], rr=rr, hh=hh, ch=ch))
        for chn, (o, s_new) in zip(chains, _gdn_scan(chains)):
            st_s[chn["ch"]] = s_new
            cols = slice(chn["hh"] * LANE, (chn["hh"] + 1) * LANE)
            o_ref[0, chn["rr"], cols] = (o_ref[0, chn["rr"], cols].astype(F32) + o).astype(o_ref.dtype)
        return carry

    lax.fori_loop(0, n, scan_body, 0)
    for d in range(2):
        for hh in range(nh):
            sf_ref[0, d, hh] = st_s[d * nh + hh]


def _gdn(qkv, ab, abt, al, dt, alt, dtt, s0):
    b, t, _ = qkv.shape
    n = t // GDN_CHUNK
    assert t % (GDN_CHUNK * GDN_LOCAL_CHUNKS) == 0
    per_b = lambda a: pl.BlockSpec((1,) + a.shape[1:], lambda i: (i,) + (0,) * (a.ndim - 1))
    full = lambda a: pl.BlockSpec(a.shape, lambda i: (0,) * a.ndim)
    return pl.pallas_call(
        functools.partial(_gdn_kernel, t=t), grid=(b,),
        in_specs=[per_b(qkv), per_b(ab), per_b(abt), full(al), full(dt), full(alt), full(dtt), per_b(s0)],
        out_specs=[pl.BlockSpec((1, t, GDN_HEADS * GDN_DV), lambda i: (i, 0, 0)), per_b(s0)],
        out_shape=[jax.ShapeDtypeStruct((b, t, GDN_HEADS * GDN_DV), BF16), jax.ShapeDtypeStruct(s0.shape, F32)],
        scratch_shapes=[pltpu.VMEM((2 * GDN_HEADS, t, LANE), BF16), pltpu.VMEM((2 * GDN_HEADS, t, LANE), BF16),
                        pltpu.VMEM((2 * GDN_HEADS, t, LANE), BF16), pltpu.VMEM((2 * GDN_HEADS, t, LANE), BF16),
                        pltpu.VMEM((2 * GDN_HEADS, n, GDN_CHUNK, GDN_CHUNK), BF16),
                        pltpu.VMEM((n, 2 * GDN_HEADS, LANE), F32),
                        pltpu.VMEM((t, 16), F32), pltpu.VMEM((n, 16, GDN_CHUNK), F32), pltpu.VMEM((t, 16), F32),
                        pltpu.VMEM((2 * GDN_HEADS, GDN_DK, GDN_DV), F32)],
        compiler_params=_cparams(1), name="gdn",
    )(qkv, ab, abt, al, dt, alt, dtt, s0)


def _attd_kernel(*refs, n_ctx):
    if n_ctx:
        q_ref, k_ref, vt_ref, kc_ref, vct_ref, o_ref = refs
    else:
        q_ref, k_ref, vt_ref, o_ref = refs
    n_new = k_ref.shape[1]
    grp = ATT_HEADS // ATT_KV_HEADS
    chunks = []
    for c0, cn in _key_chunks(n_new):
        chunks.append((
            lambda g, c0=c0, cn=cn: k_ref[0, c0:c0 + cn, g * LANE:(g + 1) * LANE],
            lambda g, c0=c0, cn=cn: vt_ref[0, g * LANE:(g + 1) * LANE, c0:c0 + cn],
            None))
    if n_ctx:
        for c0, cn in _key_chunks(n_ctx):
            chunks.append((
                lambda g, c0=c0, cn=cn: kc_ref[0, c0:c0 + cn, g * LANE:(g + 1) * LANE].astype(BF16),
                lambda g, c0=c0, cn=cn: vct_ref[0, g * LANE:(g + 1) * LANE, c0:c0 + cn].astype(BF16),
                None))
    qs = [q_ref[0, :, h * LANE:(h + 1) * LANE] for h in range(ATT_HEADS)]
    outs = _attend_heads(qs, [h // grp for h in range(ATT_HEADS)], chunks, ATT_DH)
    o_ref[0] = jnp.concatenate(outs, axis=0).T.astype(BF16)


def _attd(q, k, vt, k_c, v_ct, tq):
    b, t, _ = q.shape
    n_ctx = 0 if k_c is None else k_c.shape[1]
    per_b = lambda a: pl.BlockSpec((1,) + a.shape[1:], lambda i, j: (i, 0, 0))
    in_specs = [pl.BlockSpec((1, tq, 512), lambda i, j: (i, j, 0)), per_b(k), per_b(vt)]
    args = [q, k, vt]
    if n_ctx:
        in_specs += [per_b(k_c), per_b(v_ct)]
        args += [k_c, v_ct]
    return pl.pallas_call(
        functools.partial(_attd_kernel, n_ctx=n_ctx),
        grid=(b, t // tq), in_specs=in_specs,
        out_specs=pl.BlockSpec((1, tq, 512), lambda i, j: (i, j, 0)),
        out_shape=jax.ShapeDtypeStruct((b, t, 512), BF16),
        compiler_params=_cparams(2), name="attd_dec" if n_ctx else "attd_ctx",
    )(*args)


def _outproj1_kernel(oc_ref, zc_ref, od_ref, zd_ref, x_ref, mod_ref, gn_ref, w_ref, lnf_ref, y_ref):
    parts = []
    for j in range(GDN_HEADS):
        parts.append(_rms_rows(oc_ref[0, :, j * LANE:(j + 1) * LANE].astype(F32), gn_ref[...]).astype(BF16))
    gc = _silu_gate(jnp.concatenate(parts, axis=1), zc_ref[0])
    gd = _silu_gate(od_ref[0], zd_ref[0])
    y = _dot(gc, w_ref[0:512, :]) + _dot(gd, w_ref[512:1024, :])
    x2 = x_ref[0] + mod_ref[0, 2:3, :] * y
    y_ref[0] = _rms_rows(x2, lnf_ref[...])


def _outproj1(oc, zc, od, zd, x, mod, gn, w, lnf):
    b, t, d = x.shape
    tr = 1024 if t % 1024 == 0 else _row_tile(t)
    bm = mod.shape[0]
    rows = lambda c: pl.BlockSpec((1, tr, c), lambda i, j: (i, j, 0))
    full = lambda a: pl.BlockSpec(a.shape, lambda i, j: (0,) * a.ndim)
    return pl.pallas_call(
        _outproj1_kernel, grid=(b, t // tr),
        in_specs=[rows(512), rows(512), rows(512), rows(512), rows(d),
                  pl.BlockSpec((1, 3, d), (lambda i, j: (i, 0, 0)) if bm > 1 else (lambda i, j: (0, 0, 0))),
                  full(gn), full(w), full(lnf)],
        out_specs=rows(d), out_shape=jax.ShapeDtypeStruct((b, t, d), F32),
        compiler_params=_cparams(2), name="outproj1",
    )(oc, zc, od, zd, x, mod, gn, w, lnf)


def _rope_table(n_tok, rot_dim):
    quarter = rot_dim // 4
    inv = np.float32(ROPE_THETA) ** (-np.arange(quarter, dtype=np.float32) / np.float32(quarter))
    tt = np.arange(n_tok)
    pos = np.stack([tt // GRID_W, tt % GRID_W], axis=-1).astype(np.float32)
    ang = (pos[:, :, None] * inv).astype(np.float32)
    cos, sin = np.cos(ang), np.sin(ang)
    c = np.concatenate([cos, cos], axis=-1).reshape(n_tok, rot_dim)
    s = np.concatenate([-sin, sin], axis=-1).reshape(n_tok, rot_dim)
    return c.astype(np.float32), s.astype(np.float32)


def _place(tab, fill, off, width):
    out = np.full((tab.shape[0], width), fill, np.float32)
    out[:, off:off + tab.shape[1]] = tab
    return out


PACK_STEP = 256


def _pack_kernel(wt_ref, o_ref, *, segs):
    d = wt_ref.shape[1]
    lane = lax.broadcasted_iota(jnp.int32, (d, LANE), 1)
    off = 0
    for start, width, lane_off in segs:
        if width % LANE == 0:
            for c0 in range(0, width, PACK_STEP):
                cw = min(PACK_STEP, width - c0)
                o_ref[:, off + c0:off + c0 + cw] = wt_ref[start + c0:start + c0 + cw, :].T.astype(BF16)
            off += width
        else:
            blk = jnp.where(lane < width, wt_ref[start:start + LANE, :].T, 0.0)
            if lane_off:
                blk = pltpu.roll(blk, lane_off, 1)
            o_ref[:, off:off + LANE] = blk.astype(BF16)
            off += LANE


def _pack_cols(w, segs):
    d, n_in = w.shape
    n_out = sum(width if width % LANE == 0 else LANE for _, width, _ in segs)
    return pl.pallas_call(
        functools.partial(_pack_kernel, segs=segs), grid=(1,),
        in_specs=[pl.BlockSpec((n_in, d), lambda i: (0, 0))],
        out_specs=pl.BlockSpec((d, n_out), lambda i: (0, 0)),
        out_shape=jax.ShapeDtypeStruct((d, n_out), BF16),
        compiler_params=_cparams(1), name="pack_cols",
    )(w.T)


def _prep_l0(w_in0, w_uq, w_ukv):
    w = _pack_cols(w_in0, [(0, 640, 0), (640, MLA_ROPE, MLA_NOPE), (672, 1792, 0)])
    uq = w_uq.reshape(MLA_Q_LORA, MLA_HEADS, MLA_NOPE + MLA_ROPE)
    wuq = jnp.pad(uq, ((0, 0), (0, 0), (0, LANE - MLA_NOPE - MLA_ROPE))).reshape(MLA_Q_LORA, MLA_HEADS * LANE)
    ukv = w_ukv.reshape(MLA_KV_LORA, MLA_HEADS, MLA_NOPE + MLA_V)
    wuk = jnp.pad(ukv[:, :, :MLA_NOPE], ((0, 0), (0, 0), (0, LANE - MLA_NOPE))).reshape(MLA_KV_LORA, MLA_HEADS * LANE)
    wuvt = ukv[:, :, MLA_NOPE:].reshape(MLA_KV_LORA, MLA_HEADS * MLA_V).T
    return w, wuq.astype(BF16), wuk.astype(BF16), wuvt.astype(BF16)


def _prep_l1(w_in1):
    return _pack_cols(w_in1, [(0, 1536, 0), (1552, 2048, 0), (1536, 16, 0)])


def _chunk_rows(ab):
    b, t, c = ab.shape
    return jnp.swapaxes(ab.reshape(b, t // GDN_CHUNK, GDN_CHUNK, c), 2, 3)


def _trunk(x, mod, caches, p, tables, tq):
    dec = caches is not None
    t0m, t0s, t1 = tables if dec else (None, None, None)
    (qa, ckv, kr, za, qb, kb, vbt, zb, *ctx0) = _inproj0(
        x, mod[0], p["ln0"], p["w0"], p["qn"], p["wuq"], p["kvn"], (t0m + t0s) if dec else None)
    if dec:
        ckv_c, kr_c, kb_c, vb_ct, s0, kd_c, vd_ct = caches
    else:
        ckv_c = kr_c = kb_c = vb_ct = kd_c = vd_ct = None
        s0 = jnp.zeros((x.shape[0], 2, GDN_HEADS, GDN_DK, GDN_DV), F32)
    oa = _mla(qa, ckv, kr, ckv_c, kr_c, p["wuk"], p["wuvt"], tq)
    ob = _swa(p["sink"], qb, kb, vbt, kb_c, vb_ct, tq)
    (x1, qkv, zc, qd, kd, vdt, zd, ab, *ctx1) = _inproj1(
        oa, za, ob, zb, x, mod[0], p["wout0"], mod[1], p["ln1"], p["w1"], p["cw"], p["aqn"], p["akn"],
        t1 if dec else None)
    oc, sfin = _gdn(qkv, ab, _chunk_rows(ab), p["al"], p["dt"], p["alt"], p["dtt"], s0)
    od = _attd(qd, kd, vdt, kd_c, vd_ct, tq)
    y = _outproj1(oc, zc, od, zd, x1, mod[1], p["gn"], p["wout1"], p["lnf"])
    return y, ctx0, sfin, ctx1


def kernel(x_prompt, x_sample, cache_l0_mla_ckv, cache_l0_mla_krope, cache_l0_swa_k, cache_l0_swa_v,
           state_l1_gdn, cache_l1_attn_k, cache_l1_attn_v, c, c_ctx,
           w_mod0, b_mod0, ln0, w_in0, mla_q_norm, w_uq, mla_kv_norm, w_ukv, swa_sink, w_out0,
           w_mod1, b_mod1, ln1, w_in1, gdn_conv, gdn_a_log, gdn_dt_bias, gdn_norm, att_q_norm, att_k_norm, w_out1,
           ln_f):
    d = x_prompt.shape[-1]
    bd, td = x_sample.shape[:2]
    bc, tc = x_prompt.shape[:2]
    past = cache_l0_mla_ckv.shape[1]
    row = lambda v: v.reshape(1, -1)
    w0, wuq, wuk, wuvt = _prep_l0(w_in0, w_uq, w_ukv)
    w1 = _prep_l1(w_in1)
    al8 = gdn_a_log.reshape(1, 2 * GDN_HEADS)
    dt8 = gdn_dt_bias.reshape(1, 2 * GDN_HEADS)
    al16 = jnp.pad(al8, ((0, 0), (0, 8)))
    dt16 = jnp.pad(dt8, ((0, 0), (0, 8)))
    p = dict(ln0=row(ln0), w0=w0, qn=row(mla_q_norm), wuq=wuq, kvn=row(mla_kv_norm), wuk=wuk, wuvt=wuvt,
             sink=swa_sink, wout0=w_out0.astype(BF16), ln1=row(ln1), w1=w1, aqn=row(att_q_norm),
             akn=row(att_k_norm), cw=gdn_conv, al=al16, dt=dt16, alt=al16.T, dtt=dt16.T, gn=row(gdn_norm),
             wout1=w_out1.astype(BF16), lnf=row(ln_f))
    n_rows = -(-(bd + 1) // 8) * 8
    c_rows = jnp.concatenate([c, c_ctx[None, :], jnp.zeros((n_rows - bd - 1, d), F32)], axis=0)
    mods = [_mod(c_rows, w_mod0, b_mod0), _mod(c_rows, w_mod1, b_mod1)]
    mod_dec = [m[:bd].reshape(bd, 3, d) for m in mods]
    mod_ctx = [m[bd:bd + 1].reshape(1, 3, d) for m in mods]
    cm, sm = _rope_table(td, MLA_ROPE)
    t0m = (jnp.asarray(_place(cm, 1.0, MLA_NOPE, LANE)), jnp.asarray(_place(sm, 0.0, MLA_NOPE, LANE)))
    cs, ss = _rope_table(td, SWA_DH)
    t0s = (jnp.asarray(np.tile(cs, (1, LANE // SWA_DH))), jnp.asarray(np.tile(ss, (1, LANE // SWA_DH))))
    t1 = tuple(jnp.asarray(a) for a in _rope_table(td, ATT_DH))
    caches = (cache_l0_mla_ckv,
              jnp.pad(cache_l0_mla_krope, ((0, 0), (0, 0), (MLA_NOPE, LANE - MLA_NOPE - MLA_ROPE))),
              cache_l0_swa_k.reshape(bd, past, SWA_KV_HEADS * SWA_DH),
              jnp.swapaxes(cache_l0_swa_v.reshape(bd, past, SWA_KV_HEADS * SWA_DH), 1, 2),
              state_l1_gdn,
              cache_l1_attn_k.reshape(bd, past, ATT_KV_HEADS * ATT_DH),
              jnp.swapaxes(cache_l1_attn_v.reshape(bd, past, ATT_KV_HEADS * ATT_DH), 1, 2))
    y_prompt, ctx0, sfin, ctx1 = _trunk(x_prompt, mod_ctx, None, p, None, tq=tc)
    y_sample, _, _, _ = _trunk(x_sample, mod_dec, caches, p, (t0m, t0s, t1), tq=256)
    ckv32, kr32, kb32, vb32 = ctx0
    kd32, vd32 = ctx1
    return (y_prompt, y_sample, ckv32, kr32,
            kb32.reshape(bc, tc, SWA_KV_HEADS, SWA_DH), vb32.reshape(bc, tc, SWA_KV_HEADS, SWA_DH),
            sfin, kd32.reshape(bc, tc, ATT_KV_HEADS, ATT_DH), vd32.reshape(bc, tc, ATT_KV_HEADS, ATT_DH))
```

```python
import functools
import math

import jax
import jax.numpy as jnp
import numpy as np
from jax import lax
from jax.experimental import pallas as pl
from jax.experimental.pallas import tpu as pltpu

F32 = jnp.float32
BF16 = jnp.bfloat16

GRID_W = 64
ROPE_THETA = 10000.0
EPS = 1e-6
WINDOW = 128
MLA_HEADS, MLA_NOPE, MLA_ROPE, MLA_V = 8, 64, 32, 64
MLA_Q_LORA, MLA_KV_LORA = 384, 256
SWA_HEADS, SWA_KV_HEADS, SWA_DH = 8, 2, 64
GDN_HEADS, GDN_DK, GDN_DV, GDN_CHUNK = 4, 128, 128, 64
GDN_LOCAL_CHUNKS = 4
ATT_HEADS, ATT_KV_HEADS, ATT_DH = 4, 2, 128
LANE = 128
LOG2E = math.log2(math.e)
NEG = -1e30
VMEM_LIMIT = 56 * 1024 * 1024


def _cparams(n_axes):
    return pltpu.CompilerParams(dimension_semantics=("arbitrary",) * n_axes, vmem_limit_bytes=VMEM_LIMIT)


def _dot(a, b):
    return jnp.dot(a, b, preferred_element_type=F32)


def _dot_nt(a, b):
    return lax.dot_general(a, b, (((1,), (1,)), ((), ())), preferred_element_type=F32)


def _dot_tn(a, b):
    return lax.dot_general(a, b, (((0,), (0,)), ((), ())), preferred_element_type=F32)


def _dot_exact(a, b):
    return jnp.dot(a, b, preferred_element_type=F32, precision=lax.Precision.HIGHEST)


def _silu(x):
    return x * jax.nn.sigmoid(x)


def _rms_rows(x, g):
    return x * lax.rsqrt(jnp.mean(x * x, axis=-1, keepdims=True) + EPS) * g


def _rope_block(x, cos, sin, half):
    lane = lax.broadcasted_iota(jnp.int32, x.shape, 1)
    first = (lane // half) % 2 == 0
    partner = jnp.where(first, pltpu.roll(x, LANE - half, 1), pltpu.roll(x, half, 1))
    return x * cos + partner * sin


def _mod_kernel(c_ref, w_ref, b_ref, o_ref):
    a = _silu(c_ref[...]).astype(BF16)
    o_ref[...] = _dot(a, w_ref[...].astype(BF16)) + b_ref[...]


def _mod(c_rows, w_mod, b_mod):
    r, d = c_rows.shape
    n = w_mod.shape[1]
    tn = 1024
    return pl.pallas_call(
        _mod_kernel,
        grid=(n // tn,),
        in_specs=[pl.BlockSpec((r, d), lambda j: (0, 0)),
                  pl.BlockSpec((d, tn), lambda j: (0, j)),
                  pl.BlockSpec((1, tn), lambda j: (0, j))],
        out_specs=pl.BlockSpec((r, tn), lambda j: (0, j)),
        out_shape=jax.ShapeDtypeStruct((r, n), F32),
        compiler_params=_cparams(1),
        name="mod",
    )(c_rows, w_mod, b_mod.reshape(1, n))


def _adaln(x, mod_ref, ln_ref):
    h = _rms_rows(x, ln_ref[...])
    return h * (1.0 + mod_ref[0, 1:2, :]) + mod_ref[0, 0:1, :]


L0_OFF = dict(cq=0, ckv=384, kr=640, za=768, qb=1280, kb=1792, vb=1920, zb=2048)


def _inproj0_kernel(*refs, rope):
    us_ref = refs[-1]
    g = pl.program_id(0)

    @pl.when(g == 0)
    def _():
        us_ref[...] = jnp.zeros_like(us_ref)

    @pl.when(g == pl.num_programs(0) - 1)
    def _():
        _inproj0_finish(refs, rope)

    @pl.when(g < pl.num_programs(0) - 1)
    def _():
        x_ref, mod_ref, ln_ref, w_ref = refs[:4]
        h = _adaln(x_ref[0], mod_ref, ln_ref).astype(BF16)
        _inproj0_finish(refs, rope)
        us_ref[...] = _dot(h, w_ref[...])


def _inproj0_finish(refs, rope):
    if rope:
        (x_ref, mod_ref, ln_ref, w_ref, qn_ref, wuq_ref, kvn_ref, cm_ref, sm_ref, cs_ref, ss_ref,
         qa_ref, ckv_ref, kr_ref, za_ref, qb_ref, kb_ref, vbt_ref, zb_ref, u) = refs
    else:
        (x_ref, mod_ref, ln_ref, w_ref, qn_ref, wuq_ref, kvn_ref,
         qa_ref, ckv_ref, kr_ref, za_ref, qb_ref, kb_ref, vbt_ref, zb_ref,
         ckv32_ref, kr32_ref, kb32_ref, vb32_ref, u) = refs
    o = L0_OFF
    cq = _rms_rows(u[:, o["cq"]:o["cq"] + 384], qn_ref[...]).astype(BF16)
    qa = _dot(cq, wuq_ref[...])
    ckv = _rms_rows(u[:, o["ckv"]:o["ckv"] + 256], kvn_ref[...])
    kr = u[:, o["kr"]:o["kr"] + 128]
    qb = u[:, o["qb"]:o["qb"] + 512]
    kb = u[:, o["kb"]:o["kb"] + 128]
    vb = u[:, o["vb"]:o["vb"] + 128]
    if not rope:
        ckv32_ref[0] = ckv
        kr32_ref[0] = kr[:, 64:96]
        kb32_ref[0] = kb
        vb32_ref[0] = vb
    qa_scale = (MLA_NOPE + MLA_ROPE) ** -0.5 * LOG2E
    qb_scale = SWA_DH ** -0.5 * LOG2E
    for j in range(MLA_HEADS):
        blk = qa[:, j * LANE:(j + 1) * LANE]
        if rope:
            blk = _rope_block(blk, cm_ref[...], sm_ref[...], MLA_ROPE // 4)
        qa_ref[0, :, j * LANE:(j + 1) * LANE] = (blk * qa_scale).astype(BF16)
    for j in range(SWA_HEADS * SWA_DH // LANE):
        blk = qb[:, j * LANE:(j + 1) * LANE]
        if rope:
            blk = _rope_block(blk, cs_ref[...], ss_ref[...], SWA_DH // 4)
        qb_ref[0, :, j * LANE:(j + 1) * LANE] = (blk * qb_scale).astype(BF16)
    if rope:
        kr = _rope_block(kr, cm_ref[...], sm_ref[...], MLA_ROPE // 4)
        kb = _rope_block(kb, cs_ref[...], ss_ref[...], SWA_DH // 4)
    ckv_ref[0] = ckv.astype(BF16)
    kr_ref[0] = kr.astype(BF16)
    kb_ref[0] = kb.astype(BF16)
    za_ref[0] = u[:, o["za"]:o["za"] + 512].astype(BF16)
    zb_ref[0] = u[:, o["zb"]:o["zb"] + 512].astype(BF16)
    vbt_ref[0] = vb.T.astype(BF16)


def _row_tile(t):
    return 512 if t % 512 == 0 else 256


def _inproj0(x, mod, ln, w, qn, wuq, kvn, tables):
    b, t, d = x.shape
    tr = _row_tile(t)
    nj = t // tr
    n_blocks = b * nj
    rope = tables is not None
    bm = mod.shape[0]
    cur = lambda g: jnp.minimum(g, n_blocks - 1)
    held = lambda g: jnp.maximum(g - 1, 0)
    full = lambda a: pl.BlockSpec(a.shape, lambda g: (0,) * a.ndim)
    rows = lambda c: pl.BlockSpec((1, tr, c), lambda g: (held(g) // nj, held(g) % nj, 0))
    in_specs = [pl.BlockSpec((1, tr, d), lambda g: (cur(g) // nj, cur(g) % nj, 0)),
                pl.BlockSpec((1, 3, d), (lambda g: (cur(g) // nj, 0, 0)) if bm > 1 else (lambda g: (0, 0, 0))),
                full(ln), full(w), full(qn), full(wuq), full(kvn)]
    args = [x, mod, ln, w, qn, wuq, kvn]
    if rope:
        for tab in tables:
            in_specs.append(pl.BlockSpec((tr, LANE), lambda g: (held(g) % nj, 0)))
            args.append(tab)
    out_shape = [jax.ShapeDtypeStruct((b, t, 1024), BF16), jax.ShapeDtypeStruct((b, t, 256), BF16),
                 jax.ShapeDtypeStruct((b, t, 128), BF16), jax.ShapeDtypeStruct((b, t, 512), BF16),
                 jax.ShapeDtypeStruct((b, t, 512), BF16), jax.ShapeDtypeStruct((b, t, 128), BF16),
                 jax.ShapeDtypeStruct((b, 128, t), BF16), jax.ShapeDtypeStruct((b, t, 512), BF16)]
    out_specs = [rows(1024), rows(256), rows(128), rows(512), rows(512), rows(128),
                 pl.BlockSpec((1, 128, tr), lambda g: (held(g) // nj, 0, held(g) % nj)), rows(512)]
    if not rope:
        out_shape += [jax.ShapeDtypeStruct((b, t, 256), F32), jax.ShapeDtypeStruct((b, t, 32), F32),
                      jax.ShapeDtypeStruct((b, t, 128), F32), jax.ShapeDtypeStruct((b, t, 128), F32)]
        out_specs += [rows(256), rows(32), rows(128), rows(128)]
    return pl.pallas_call(
        functools.partial(_inproj0_kernel, rope=rope),
        grid=(n_blocks + 1,), in_specs=in_specs, out_specs=out_specs, out_shape=out_shape,
        scratch_shapes=[pltpu.VMEM((tr, w.shape[1]), F32)],
        compiler_params=_cparams(1), name="inproj0_dec" if rope else "inproj0_ctx",
    )(*args)


KEY_CHUNK = 512
ATT_LOOKAHEAD = 4


SUM_ROWS = 16


def _attend_heads(qs, kv_of, chunks, dv, sinks=None):
    nh = len(qs)
    tq = qs[0].shape[0]
    m = [None] * nh
    acc = [None] * nh
    items = [(ci, i) for ci in range(len(chunks)) for i in range(nh)]
    loaded = {}

    def kv(ci, src):
        if (ci, src) not in loaded:
            vt = chunks[ci][1](src)
            ones = (lax.broadcasted_iota(jnp.int32, (SUM_ROWS, vt.shape[1]), 0) == 0).astype(BF16)
            loaded[(ci, src)] = (chunks[ci][0](src), jnp.concatenate([vt, ones], axis=0))
        return loaded[(ci, src)]

    if sinks is not None:
        m = [jnp.full((1, tq), sk, F32) for sk in sinks]
        unit = (lax.broadcasted_iota(jnp.int32, (dv + SUM_ROWS, tq), 0) == dv).astype(F32)
        acc = [unit for _ in sinks]

    scores = {}
    for t in range(len(items) + ATT_LOOKAHEAD):
        if t < len(items):
            ci, i = items[t]
            scores[t] = _dot_nt(kv(ci, kv_of[i])[0], qs[i])
        t0 = t - ATT_LOOKAHEAD
        if t0 < 0:
            continue
        ci, i = items[t0]
        mask = chunks[ci][2]
        si = scores.pop(t0)
        if mask is not None:
            si = jnp.where(mask, si, NEG)
        cm = si.max(axis=0, keepdims=True)
        alpha = None
        if m[i] is None:
            m_new = cm
        else:
            m_new = jnp.maximum(m[i], cm)
            alpha = jnp.exp2(m[i] - m_new)
        p = jnp.exp2(si - m_new)
        m[i] = m_new
        pv = _dot(kv(ci, kv_of[i])[1], p.astype(BF16))
        acc[i] = pv if acc[i] is None else acc[i] * alpha + pv
    return [acc[i][:dv] * (1.0 / acc[i][dv:dv + 1]) for i in range(nh)]


def _key_chunks(n):
    step = KEY_CHUNK if n % KEY_CHUNK == 0 else n
    return [(c0, step) for c0 in range(0, n, step)]


def _mla_kernel(*refs, n_new, n_ctx, hp):
    if n_ctx:
        q_ref, ckv_ref, kr_ref, ckvc_ref, krc_ref, wuk_ref, wuvt_ref, o_ref, k_s, vt_s = refs
    else:
        q_ref, ckv_ref, kr_ref, wuk_ref, wuvt_ref, o_ref, k_s, vt_s = refs
    qi, gi = pl.program_id(1), pl.program_id(2)

    @pl.when((qi == 0) & (gi == 0))
    def _():
        def expand(ckv, kr, r0, n):
            kn = _dot(ckv, wuk_ref[...])
            for j in range(MLA_HEADS):
                k_s[j, r0:r0 + n, :] = (kn[:, j * LANE:(j + 1) * LANE] + kr).astype(BF16)
            vt_s[:, r0:r0 + n] = _dot_nt(wuvt_ref[...], ckv).astype(BF16)

        blk = 512 if n_new % 512 == 0 else 256
        for r0 in range(0, n_new, blk):
            expand(ckv_ref[0, r0:r0 + blk, :], kr_ref[0, r0:r0 + blk, :].astype(F32), r0, blk)
        if n_ctx:
            expand(ckvc_ref[0].astype(BF16), krc_ref[0], n_new, n_ctx)

    qs = [q_ref[0, :, j * LANE:(j + 1) * LANE] for j in range(hp)]
    chunks = []
    for c0, cn in _key_chunks(n_new + n_ctx):
        chunks.append((
            lambda j, c0=c0, cn=cn: k_s[gi * hp + j, c0:c0 + cn, :],
            lambda j, c0=c0, cn=cn: vt_s[pl.ds(pl.multiple_of((gi * hp + j) * MLA_V, MLA_V), MLA_V), c0:c0 + cn],
            None))
    outs = _attend_heads(qs, list(range(hp)), chunks, MLA_V)
    o_ref[0] = jnp.concatenate(outs, axis=0).T.astype(BF16)


def _mla(q, ckv, kr, ckv_c, kr_c, wuk, wuvt, tq):
    b, t, _ = q.shape
    n_ctx = 0 if ckv_c is None else ckv_c.shape[1]
    hp = 8
    tk = t + n_ctx
    rows_q = pl.BlockSpec((1, tq, hp * LANE), lambda i, j, g: (i, j, g))
    per_b = lambda a: pl.BlockSpec((1,) + a.shape[1:], lambda i, j, g: (i, 0, 0))
    full = lambda a: pl.BlockSpec(a.shape, lambda i, j, g: (0, 0))
    in_specs = [rows_q, per_b(ckv), per_b(kr)]
    args = [q, ckv, kr]
    if n_ctx:
        in_specs += [per_b(ckv_c), per_b(kr_c)]
        args += [ckv_c, kr_c]
    in_specs += [full(wuk), full(wuvt)]
    args += [wuk, wuvt]
    return pl.pallas_call(
        functools.partial(_mla_kernel, n_new=t, n_ctx=n_ctx, hp=hp),
        grid=(b, t // tq, MLA_HEADS // hp), in_specs=in_specs,
        out_specs=pl.BlockSpec((1, tq, hp * MLA_V), lambda i, j, g: (i, j, g)),
        out_shape=jax.ShapeDtypeStruct((b, t, MLA_HEADS * MLA_V), BF16),
        scratch_shapes=[pltpu.VMEM((MLA_HEADS, tk, LANE), BF16), pltpu.VMEM((MLA_HEADS * MLA_V, tk), BF16)],
        compiler_params=_cparams(3), name="mla_dec" if n_ctx else "mla_ctx",
    )(*args)


def _swa_kernel(*refs, n_new, n_ctx, tq):
    if n_ctx:
        sink_ref, q_ref, k_ref, vt_ref, kc_ref, vct_ref, o_ref = refs
    else:
        sink_ref, q_ref, k_ref, vt_ref, o_ref = refs
    qi = pl.program_id(1)
    grp = SWA_HEADS // SWA_KV_HEADS
    if n_ctx:
        span = tq + 2 * WINDOW
        q0 = qi * tq
        start = pl.multiple_of(jnp.clip(q0 - WINDOW, 0, n_new - span), LANE)
        kpos = start + lax.broadcasted_iota(jnp.int32, (span, tq), 0)
        qpos = q0 + lax.broadcasted_iota(jnp.int32, (span, tq), 1)
        band = jnp.abs(kpos - qpos) <= WINDOW
    dh = SWA_DH
    chunks = []
    if n_ctx:
        for c0, cn in _key_chunks(span):
            chunks.append((
                lambda g, c0=c0, cn=cn: k_ref[0, pl.ds(start + c0, cn), g * dh:(g + 1) * dh],
                lambda g, c0=c0, cn=cn: vt_ref[0, g * dh:(g + 1) * dh, pl.ds(start + c0, cn)],
                band[c0:c0 + cn]))
        for c0, cn in _key_chunks(n_ctx):
            chunks.append((
                lambda g, c0=c0, cn=cn: kc_ref[0, c0:c0 + cn, g * dh:(g + 1) * dh].astype(BF16),
                lambda g, c0=c0, cn=cn: vct_ref[0, g * dh:(g + 1) * dh, c0:c0 + cn].astype(BF16),
                None))
    else:
        for c0, cn in _key_chunks(n_new):
            chunks.append((
                lambda g, c0=c0, cn=cn: k_ref[0, c0:c0 + cn, g * dh:(g + 1) * dh],
                lambda g, c0=c0, cn=cn: vt_ref[0, g * dh:(g + 1) * dh, c0:c0 + cn],
                None))
    qs = [q_ref[0, :, h * dh:(h + 1) * dh] for h in range(SWA_HEADS)]
    sinks = [sink_ref[h] * LOG2E for h in range(SWA_HEADS)]
    outs = _attend_heads(qs, [h // grp for h in range(SWA_HEADS)], chunks, SWA_DH, sinks)
    o_ref[0] = jnp.concatenate(outs, axis=0).T.astype(BF16)


def _swa(sink, q, k, vt, k_c, v_ct, tq):
    b, t, _ = q.shape
    n_ctx = 0 if k_c is None else k_c.shape[1]
    assert n_ctx == 0 or (t >= tq + 2 * WINDOW and tq % LANE == 0 and WINDOW % LANE == 0)
    per_b = lambda a: pl.BlockSpec((1,) + a.shape[1:], lambda i, j: (i, 0, 0))
    in_specs = [pl.BlockSpec(memory_space=pltpu.SMEM), pl.BlockSpec((1, tq, 512), lambda i, j: (i, j, 0)),
                per_b(k), per_b(vt)]
    args = [sink, q, k, vt]
    if n_ctx:
        in_specs += [per_b(k_c), per_b(v_ct)]
        args += [k_c, v_ct]
    return pl.pallas_call(
        functools.partial(_swa_kernel, n_new=t, n_ctx=n_ctx, tq=tq),
        grid=(b, t // tq), in_specs=in_specs,
        out_specs=pl.BlockSpec((1, tq, 512), lambda i, j: (i, j, 0)),
        out_shape=jax.ShapeDtypeStruct((b, t, 512), BF16),
        compiler_params=_cparams(2), name="swa_dec" if n_ctx else "swa_ctx",
    )(*args)


def _silu_gate(o, z):
    return o * (z / (1.0 + jnp.exp(-z)))


L1_OFF = dict(qkv=0, zc=1536, qd=2048, kd=2560, vd=2816, zd=3072, ab=3584)
L1_W = 3712


def _inproj1_kernel(*refs, rope, nj):
    if rope:
        (oa_ref, za_ref, ob_ref, zb_ref, x_ref, mod0_ref, wo_ref, mod_ref, ln_ref, w_ref, cw_ref,
         qn_ref, kn_ref, c_ref, s_ref,
         x1_ref, qkv_ref, zc_ref, qd_ref, kd_ref, vdt_ref, zd_ref, ab_ref, us_ref, pr_ref) = refs
    else:
        (oa_ref, za_ref, ob_ref, zb_ref, x_ref, mod0_ref, wo_ref, mod_ref, ln_ref, w_ref, cw_ref,
         qn_ref, kn_ref,
         x1_ref, qkv_ref, zc_ref, qd_ref, kd_ref, vdt_ref, zd_ref, ab_ref, kd32_ref, vd32_ref, us_ref, pr_ref) = refs
    g = pl.program_id(0)
    tr = us_ref.shape[0]
    nq = GDN_HEADS * GDN_DK
    n_qkv = 2 * nq + GDN_HEADS * GDN_DV
    o = L1_OFF

    @pl.when(g == 0)
    def _():
        us_ref[...] = jnp.zeros_like(us_ref)
        pr_ref[...] = jnp.zeros_like(pr_ref)

    def conv_held_block(next_row):
        jb = lax.rem(g + nj - 1, nj)
        prv = jnp.where(jb > 0, pr_ref[7:8, :], 0.0)
        nxt = jnp.where(jb < nj - 1, next_row, 0.0)
        row8 = lax.broadcasted_iota(jnp.int32, (8, LANE), 0)
        for j in range(n_qkv // LANE):
            cols = slice(j * LANE, (j + 1) * LANE)
            xj = us_ref[:, cols]
            xp = pltpu.roll(xj, 1, 0)
            xp = jnp.concatenate([jnp.where(row8 == 0, prv[:, cols], xp[0:8]), xp[8:]], axis=0)
            xn = pltpu.roll(xj, tr - 1, 0)
            xn = jnp.concatenate([xn[:tr - 8], jnp.where(row8 == 7, nxt[:, cols], xn[tr - 8:])], axis=0)
            cw = cw_ref[:, cols]
            y = _silu(xp * cw[0:1, :] + xj * cw[1:2, :] + xn * cw[2:3, :])
            if j < 2 * nq // LANE:
                y = y * lax.rsqrt(jnp.sum(y * y, axis=-1, keepdims=True) + EPS)
            if j < nq // LANE:
                y = y * GDN_DK ** -0.5
            qkv_ref[0, :, cols] = y.astype(BF16)
        pr_ref[...] = us_ref[tr - 8:tr, 0:n_qkv]
        zc_ref[0] = us_ref[:, o["zc"]:o["zc"] + 512].astype(BF16)
        zd_ref[0] = us_ref[:, o["zd"]:o["zd"] + 512].astype(BF16)
        ab_ref[0] = us_ref[:, o["ab"]:o["ab"] + 16]
        qd_scale = ATT_DH ** -0.5 * LOG2E
        for j in range(ATT_HEADS):
            blk = _rms_rows(us_ref[:, o["qd"] + j * LANE:o["qd"] + (j + 1) * LANE], qn_ref[...])
            if rope:
                blk = _rope_block(blk, c_ref[...], s_ref[...], ATT_DH // 4)
            qd_ref[0, :, j * LANE:(j + 1) * LANE] = (blk * qd_scale).astype(BF16)
        for j in range(ATT_KV_HEADS):
            blk = _rms_rows(us_ref[:, o["kd"] + j * LANE:o["kd"] + (j + 1) * LANE], kn_ref[...])
            if rope:
                blk = _rope_block(blk, c_ref[...], s_ref[...], ATT_DH // 4)
            else:
                kd32_ref[0, :, j * LANE:(j + 1) * LANE] = blk
            kd_ref[0, :, j * LANE:(j + 1) * LANE] = blk.astype(BF16)
        vd = us_ref[:, o["vd"]:o["vd"] + 256]
        if not rope:
            vd32_ref[0] = vd
        vdt_ref[0] = vd.T.astype(BF16)

    @pl.when(g == pl.num_programs(0) - 1)
    def _():
        conv_held_block(jnp.zeros((1, n_qkv), F32))

    @pl.when(g < pl.num_programs(0) - 1)
    def _():
        _inproj1_block(refs, rope, conv_held_block)


def _inproj1_block(refs, rope, conv_held_block):
    if rope:
        (oa_ref, za_ref, ob_ref, zb_ref, x_ref, mod0_ref, wo_ref, mod_ref, ln_ref, w_ref, cw_ref,
         qn_ref, kn_ref, c_ref, s_ref,
         x1_ref, qkv_ref, zc_ref, qd_ref, kd_ref, vdt_ref, zd_ref, ab_ref, us_ref, pr_ref) = refs
    else:
        (oa_ref, za_ref, ob_ref, zb_ref, x_ref, mod0_ref, wo_ref, mod_ref, ln_ref, w_ref, cw_ref,
         qn_ref, kn_ref,
         x1_ref, qkv_ref, zc_ref, qd_ref, kd_ref, vdt_ref, zd_ref, ab_ref, kd32_ref, vd32_ref, us_ref, pr_ref) = refs
    n_qkv = pr_ref.shape[1]
    y0 = (_dot(_silu_gate(oa_ref[0], za_ref[0]), wo_ref[0:512, :])
          + _dot(_silu_gate(ob_ref[0], zb_ref[0]), wo_ref[512:1024, :]))
    x1 = x_ref[0] + mod0_ref[0, 2:3, :] * y0
    x1_ref[0] = x1
    h = _adaln(x1, mod_ref, ln_ref).astype(BF16)
    conv_held_block(_dot(h[0:8], w_ref[:, 0:n_qkv])[0:1])
    us_ref[...] = _dot(h, w_ref[...])


def _inproj1(oa, za, ob, zb, x, mod0, wo, mod, ln, w, cw, qn, kn, tables):
    b, t, d = x.shape
    tr = _row_tile(t)
    nj = t // tr
    n_blocks = b * nj
    rope = tables is not None
    bm = mod.shape[0]
    n_qkv = cw.shape[1]
    cur = lambda g: jnp.minimum(g, n_blocks - 1)
    held = lambda g: jnp.maximum(g - 1, 0)
    full = lambda a: pl.BlockSpec(a.shape, lambda g: (0,) * a.ndim)
    rows = lambda c: pl.BlockSpec((1, tr, c), lambda g: (cur(g) // nj, cur(g) % nj, 0))
    mod_spec = pl.BlockSpec((1, 3, d), (lambda g: (cur(g) // nj, 0, 0)) if bm > 1 else (lambda g: (0, 0, 0)))
    widths = [512, 512, 512, 512, d]
    in_specs = ([rows(c) for c in widths]
                + [mod_spec, full(wo), mod_spec, full(ln), full(w), full(cw), full(qn), full(kn)])
    args = [oa, za, ob, zb, x, mod0, wo, mod, ln, w, cw, qn, kn]
    hrows = lambda c: pl.BlockSpec((1, tr, c), lambda g: (held(g) // nj, held(g) % nj, 0))
    if rope:
        for tab in tables:
            in_specs.append(pl.BlockSpec((tr, LANE), lambda g: (held(g) % nj, 0)))
            args.append(tab)
    out_shape = [jax.ShapeDtypeStruct((b, t, d), F32),
                 jax.ShapeDtypeStruct((b, t, n_qkv), BF16), jax.ShapeDtypeStruct((b, t, 512), BF16),
                 jax.ShapeDtypeStruct((b, t, 512), BF16), jax.ShapeDtypeStruct((b, t, 256), BF16),
                 jax.ShapeDtypeStruct((b, 256, t), BF16), jax.ShapeDtypeStruct((b, t, 512), BF16),
                 jax.ShapeDtypeStruct((b, t, 16), F32)]
    out_specs = [rows(d), hrows(n_qkv), hrows(512), hrows(512), hrows(256),
                 pl.BlockSpec((1, 256, tr), lambda g: (held(g) // nj, 0, held(g) % nj)), hrows(512), hrows(16)]
    if not rope:
        out_shape += [jax.ShapeDtypeStruct((b, t, 256), F32), jax.ShapeDtypeStruct((b, t, 256), F32)]
        out_specs += [hrows(256), hrows(256)]
    return pl.pallas_call(
        functools.partial(_inproj1_kernel, rope=rope, nj=nj),
        grid=(n_blocks + 1,), in_specs=in_specs, out_specs=out_specs, out_shape=out_shape,
        scratch_shapes=[pltpu.VMEM((tr, L1_W), F32), pltpu.VMEM((8, n_qkv), F32)],
        compiler_params=_cparams(1), name="inproj1_dec" if rope else "inproj1_ctx",
    )(*args)


def _gdn_local(blocks):
    c = GDN_CHUNK
    row = lax.broadcasted_iota(jnp.int32, (c, c), 0)
    col = lax.broadcasted_iota(jnp.int32, (c, c), 1)
    lane2 = lax.broadcasted_iota(jnp.int32, (c, 2 * c), 1)
    eye = (row == col).astype(F32)
    eye_t = jnp.concatenate([eye, jnp.zeros((c, c), F32)], axis=1).astype(BF16)
    chains = [ch for blk in blocks for ch in blk["dirs"]]
    for blk in blocks:
        for ch in blk["dirs"]:
            causal = (row <= col) if ch["upper"] else (row >= col)
            ch["strict"] = (row < col) if ch["upper"] else (row > col)
            ch["decay"] = jnp.exp(jnp.where(causal, ch["gc_col"] - ch["gc_row"], -jnp.inf))
            ch["kb"] = blk["k"] * ch["beta_col"]
            ch["egc"] = jnp.exp(ch["gc_col"])
    for blk in blocks:
        lhs = jnp.concatenate([ch["kb"] for ch in blk["dirs"]] + [blk["q"]], axis=0).astype(BF16)
        a = _dot_nt(lhs, blk["k"].astype(BF16))
        nd = len(blk["dirs"])
        for di, ch in enumerate(blk["dirs"]):
            x = jnp.where(ch["strict"], -(a[di * c:(di + 1) * c] * ch["decay"]), 0.0)
            ch["intra"] = (a[nd * c:] * ch["decay"]).astype(BF16)
            ch["w"] = jnp.concatenate([eye, x], axis=1)
    for _ in range(6):
        for ch in chains:
            w = ch["w"]
            wh = w.astype(BF16)
            lo = w - wh.astype(F32)
            php = jnp.where(lane2 < c, pltpu.roll(w, c, 1), lo).astype(BF16)
            ch["w"] = _dot(jnp.concatenate([wh, php], axis=1),
                           jnp.concatenate([eye_t, wh, lo.astype(BF16), wh], axis=0))
    for blk in blocks:
        for ch in blk["dirs"]:
            rhs = jnp.concatenate([blk["v"] * ch["beta_col"], ch["kb"] * ch["egc"]], axis=1).astype(BF16)
            sol = _dot(ch["w"][:, :c].astype(BF16), rhs)
            ch["u"], ch["wv"] = sol[:, :GDN_DV].astype(BF16), sol[:, GDN_DV:].astype(BF16)
            ch["qe"] = (blk["q"] * ch["egc"]).astype(BF16)
            ch["kd"] = (blk["k"] * jnp.exp(ch["glast"] - ch["gc_col"])).astype(BF16)
            ch["eg"] = jnp.exp(ch["glast"])


def _gdn_scan(chains):
    c = GDN_CHUNK
    for ch in chains:
        ch["sb"] = ch["s"].astype(BF16)
    for ch in chains:
        r = _dot(jnp.concatenate([ch["wv"], ch["qe"]], axis=0), ch["sb"])
        ch["vn"] = (ch["u"].astype(F32) - r[:c]).astype(BF16)
        ch["qs"] = r[c:]
    outs = []
    for ch in chains:
        o = ch["qs"] + _dot(ch["intra"], ch["vn"])
        s_new = ch["s"] * ch["eg"] + _dot_tn(ch["kd"], ch["vn"])
        outs.append((o, s_new))
    return outs


def _gdn_kernel(qkv_ref, ab_ref, abt_ref, al_ref, dt_ref, alt_ref, dtt_ref, s0_ref, o_ref, sf_ref,
                u_s, wv_s, qe_s, kd_s, in_s, eg_s, gcol_s, grow_s, beta_s, st_s, *, t):
    c = GDN_CHUNK
    n = t // c
    nh = GDN_HEADS
    ab = ab_ref[0]
    gact = -jnp.exp(al_ref[...]) * jax.nn.softplus(ab + dt_ref[...])
    lane16 = lax.broadcasted_iota(jnp.int32, ab.shape, 1)
    beta_s[...] = jnp.where(lane16 < 2 * nh, gact, jax.nn.sigmoid(ab))
    r64 = lax.broadcasted_iota(jnp.int32, (c, c), 0)
    c64 = lax.broadcasted_iota(jnp.int32, (c, c), 1)
    tril = (r64 >= c64).astype(F32)
    triu = (r64 <= c64).astype(F32)
    lane_c = lax.broadcasted_iota(jnp.int32, (c, 16), 1)
    sub_c = lax.broadcasted_iota(jnp.int32, (16, c), 0)

    def cum_chunk(i):
        r0 = pl.multiple_of(i * c, c)
        g = beta_s[pl.ds(r0, c), :]
        gcol_s[pl.ds(r0, c), :] = jnp.where(lane_c < nh, _dot_exact(tril, g), _dot_exact(triu, g))
        gt = -jnp.exp(alt_ref[...]) * jax.nn.softplus(abt_ref[0, i] + dtt_ref[...])
        grow_s[i] = jnp.where(sub_c < nh, _dot_exact(gt, triu), _dot_exact(gt, tril))

    ncs = GDN_LOCAL_CHUNKS
    rb = ncs * c
    for sub in range(ncs):
        cum_chunk(sub)

    def local_body(jb, carry):
        r0 = pl.multiple_of(jb * rb, rb)
        nxt_blk = jnp.minimum(jb + 1, n // ncs - 1)
        xs = [qkv_ref[0, pl.ds(r0, rb), j * LANE:(j + 1) * LANE].astype(F32) for j in range(3 * nh)]
        gcol = gcol_s[pl.ds(r0, rb), :]
        bet = beta_s[pl.ds(r0, rb), :]
        blocks = []
        for sub in range(ncs):
            ci = ncs * jb + sub
            rows = slice(sub * c, (sub + 1) * c)
            grow = grow_s[ci]
            for hh in range(nh):
                dirs = []
                for d in range(2):
                    ch = d * nh + hh
                    last = sub * c + (c - 1 if d == 0 else 0)
                    dirs.append(dict(gc_col=gcol[rows, ch:ch + 1], gc_row=grow[ch:ch + 1, :],
                                     beta_col=bet[rows, 2 * nh + ch:2 * nh + ch + 1],
                                     glast=gcol[last:last + 1, ch:ch + 1], upper=(d == 1), ch=ch, ci=ci,
                                     r0=r0 + sub * c))
                blocks.append(dict(q=xs[hh][rows], k=xs[nh + hh][rows], v=xs[2 * nh + hh][rows], dirs=dirs))
        _gdn_local(blocks)
        for blk in blocks:
            for chn in blk["dirs"]:
                ch, rr = chn["ch"], pl.ds(pl.multiple_of(chn["r0"], c), c)
                u_s[ch, rr, :] = chn["u"]
                wv_s[ch, rr, :] = chn["wv"]
                qe_s[ch, rr, :] = chn["qe"]
                kd_s[ch, rr, :] = chn["kd"]
                in_s[ch, chn["ci"]] = chn["intra"]
                eg_s[chn["ci"], ch:ch + 1, :] = jnp.broadcast_to(chn["eg"], (1, LANE))
        for sub in range(ncs):
            cum_chunk(nxt_blk * ncs + sub)
        return carry

    lax.fori_loop(0, n // ncs, local_body, 0)

    for d in range(2):
        for hh in range(nh):
            st_s[d * nh + hh] = s0_ref[0, d, hh]
    o_ref[...] = jnp.zeros_like(o_ref)

    def scan_body(i, carry):
        chains = []
        for d in range(2):
            ci = i if d == 0 else n - 1 - i
            rr = pl.ds(pl.multiple_of(ci * c, c), c)
            eg = eg_s[ci]
            for hh in range(nh):
                ch = d * nh + hh
                chains.append(dict(u=u_s[ch, rr, :], wv=wv_s[ch, rr, :], qe=qe_s[ch, rr, :], kd=kd_s[ch, rr, :],
                                   intra=in_s[ch, ci], eg=eg[ch:ch + 1, :], s=st_s[ch], rr=rr, hh=hh, ch=ch))
        for chn, (o, s_new) in zip(chains, _gdn_scan(chains)):
            st_s[chn["ch"]] = s_new
            cols = slice(chn["hh"] * LANE, (chn["hh"] + 1) * LANE)
            o_ref[0, chn["rr"], cols] = (o_ref[0, chn["rr"], cols].astype(F32) + o).astype(o_ref.dtype)
        return carry

    lax.fori_loop(0, n, scan_body, 0)
    for d in range(2):
        for hh in range(nh):
            sf_ref[0, d, hh] = st_s[d * nh + hh]


def _gdn(qkv, ab, abt, al, dt, alt, dtt, s0):
    b, t, _ = qkv.shape
    n = t // GDN_CHUNK
    assert t % (GDN_CHUNK * GDN_LOCAL_CHUNKS) == 0
    per_b = lambda a: pl.BlockSpec((1,) + a.shape[1:], lambda i: (i,) + (0,) * (a.ndim - 1))
    full = lambda a: pl.BlockSpec(a.shape, lambda i: (0,) * a.ndim)
    return pl.pallas_call(
        functools.partial(_gdn_kernel, t=t), grid=(b,),
        in_specs=[per_b(qkv), per_b(ab), per_b(abt), full(al), full(dt), full(alt), full(dtt), per_b(s0)],
        out_specs=[pl.BlockSpec((1, t, GDN_HEADS * GDN_DV), lambda i: (i, 0, 0)), per_b(s0)],
        out_shape=[jax.ShapeDtypeStruct((b, t, GDN_HEADS * GDN_DV), BF16), jax.ShapeDtypeStruct(s0.shape, F32)],
        scratch_shapes=[pltpu.VMEM((2 * GDN_HEADS, t, LANE), BF16), pltpu.VMEM((2 * GDN_HEADS, t, LANE), BF16),
                        pltpu.VMEM((2 * GDN_HEADS, t, LANE), BF16), pltpu.VMEM((2 * GDN_HEADS, t, LANE), BF16),
                        pltpu.VMEM((2 * GDN_HEADS, n, GDN_CHUNK, GDN_CHUNK), BF16),
                        pltpu.VMEM((n, 2 * GDN_HEADS, LANE), F32),
                        pltpu.VMEM((t, 16), F32), pltpu.VMEM((n, 16, GDN_CHUNK), F32), pltpu.VMEM((t, 16), F32),
                        pltpu.VMEM((2 * GDN_HEADS, GDN_DK, GDN_DV), F32)],
        compiler_params=_cparams(1), name="gdn",
    )(qkv, ab, abt, al, dt, alt, dtt, s0)


ATT_SUBQ = 256


def _attd_kernel(*refs, n_ctx):
    if n_ctx:
        q_ref, k_ref, vt_ref, kc_ref, vct_ref, o_ref = refs
    else:
        q_ref, k_ref, vt_ref, o_ref = refs
    n_new = k_ref.shape[1]
    grp = ATT_HEADS // ATT_KV_HEADS
    chunks = []
    for c0, cn in _key_chunks(n_new):
        chunks.append((
            lambda g, c0=c0, cn=cn: k_ref[0, c0:c0 + cn, g * LANE:(g + 1) * LANE],
            lambda g, c0=c0, cn=cn: vt_ref[0, g * LANE:(g + 1) * LANE, c0:c0 + cn],
            None))
    if n_ctx:
        for c0, cn in _key_chunks(n_ctx):
            chunks.append((
                lambda g, c0=c0, cn=cn: kc_ref[0, c0:c0 + cn, g * LANE:(g + 1) * LANE].astype(BF16),
                lambda g, c0=c0, cn=cn: vct_ref[0, g * LANE:(g + 1) * LANE, c0:c0 + cn].astype(BF16),
                None))
    tq = q_ref.shape[1]
    subs = [(s0, min(ATT_SUBQ, tq)) for s0 in range(0, tq, ATT_SUBQ)]
    qs = [q_ref[0, s0:s0 + sn, h * LANE:(h + 1) * LANE] for s0, sn in subs for h in range(ATT_HEADS)]
    outs = _attend_heads(qs, [h // grp for _ in subs for h in range(ATT_HEADS)], chunks, ATT_DH)
    for si, (s0, sn) in enumerate(subs):
        blk = jnp.concatenate(outs[si * ATT_HEADS:(si + 1) * ATT_HEADS], axis=0)
        o_ref[0, s0:s0 + sn, :] = blk.T.astype(BF16)


def _attd(q, k, vt, k_c, v_ct, tq):
    b, t, _ = q.shape
    n_ctx = 0 if k_c is None else k_c.shape[1]
    per_b = lambda a: pl.BlockSpec((1,) + a.shape[1:], lambda i, j: (i, 0, 0))
    in_specs = [pl.BlockSpec((1, tq, 512), lambda i, j: (i, j, 0)), per_b(k), per_b(vt)]
    args = [q, k, vt]
    if n_ctx:
        in_specs += [per_b(k_c), per_b(v_ct)]
        args += [k_c, v_ct]
    return pl.pallas_call(
        functools.partial(_attd_kernel, n_ctx=n_ctx),
        grid=(b, t // tq), in_specs=in_specs,
        out_specs=pl.BlockSpec((1, tq, 512), lambda i, j: (i, j, 0)),
        out_shape=jax.ShapeDtypeStruct((b, t, 512), BF16),
        compiler_params=_cparams(2), name="attd_dec" if n_ctx else "attd_ctx",
    )(*args)


def _outproj1_kernel(oc_ref, zc_ref, od_ref, zd_ref, x_ref, mod_ref, gn_ref, w_ref, lnf_ref, y_ref):
    parts = []
    for j in range(GDN_HEADS):
        parts.append(_rms_rows(oc_ref[0, :, j * LANE:(j + 1) * LANE].astype(F32), gn_ref[...]).astype(BF16))
    gc = _silu_gate(jnp.concatenate(parts, axis=1), zc_ref[0])
    gd = _silu_gate(od_ref[0], zd_ref[0])
    y = _dot(gc, w_ref[0:512, :]) + _dot(gd, w_ref[512:1024, :])
    x2 = x_ref[0] + mod_ref[0, 2:3, :] * y
    y_ref[0] = _rms_rows(x2, lnf_ref[...])


def _outproj1(oc, zc, od, zd, x, mod, gn, w, lnf):
    b, t, d = x.shape
    tr = 1024 if t % 1024 == 0 else _row_tile(t)
    bm = mod.shape[0]
    rows = lambda c: pl.BlockSpec((1, tr, c), lambda i, j: (i, j, 0))
    full = lambda a: pl.BlockSpec(a.shape, lambda i, j: (0,) * a.ndim)
    return pl.pallas_call(
        _outproj1_kernel, grid=(b, t // tr),
        in_specs=[rows(512), rows(512), rows(512), rows(512), rows(d),
                  pl.BlockSpec((1, 3, d), (lambda i, j: (i, 0, 0)) if bm > 1 else (lambda i, j: (0, 0, 0))),
                  full(gn), full(w), full(lnf)],
        out_specs=rows(d), out_shape=jax.ShapeDtypeStruct((b, t, d), F32),
        compiler_params=_cparams(2), name="outproj1",
    )(oc, zc, od, zd, x, mod, gn, w, lnf)


def _rope_table(n_tok, rot_dim):
    quarter = rot_dim // 4
    inv = np.float32(ROPE_THETA) ** (-np.arange(quarter, dtype=np.float32) / np.float32(quarter))
    tt = np.arange(n_tok)
    pos = np.stack([tt // GRID_W, tt % GRID_W], axis=-1).astype(np.float32)
    ang = (pos[:, :, None] * inv).astype(np.float32)
    cos, sin = np.cos(ang), np.sin(ang)
    c = np.concatenate([cos, cos], axis=-1).reshape(n_tok, rot_dim)
    s = np.concatenate([-sin, sin], axis=-1).reshape(n_tok, rot_dim)
    return c.astype(np.float32), s.astype(np.float32)


def _place(tab, fill, off, width):
    out = np.full((tab.shape[0], width), fill, np.float32)
    out[:, off:off + tab.shape[1]] = tab
    return out


PACK_STEP = 256


def _pack_kernel(wt_ref, o_ref, *, segs):
    d = wt_ref.shape[1]
    lane = lax.broadcasted_iota(jnp.int32, (d, LANE), 1)
    off = 0
    for start, width, lane_off in segs:
        if width % LANE == 0:
            for c0 in range(0, width, PACK_STEP):
                cw = min(PACK_STEP, width - c0)
                o_ref[:, off + c0:off + c0 + cw] = wt_ref[start + c0:start + c0 + cw, :].T.astype(BF16)
            off += width
        else:
            blk = jnp.where(lane < width, wt_ref[start:start + LANE, :].T, 0.0)
            if lane_off:
                blk = pltpu.roll(blk, lane_off, 1)
            o_ref[:, off:off + LANE] = blk.astype(BF16)
            off += LANE


def _pack_cols(w, segs):
    d, n_in = w.shape
    n_out = sum(width if width % LANE == 0 else LANE for _, width, _ in segs)
    return pl.pallas_call(
        functools.partial(_pack_kernel, segs=segs), grid=(1,),
        in_specs=[pl.BlockSpec((n_in, d), lambda i: (0, 0))],
        out_specs=pl.BlockSpec((d, n_out), lambda i: (0, 0)),
        out_shape=jax.ShapeDtypeStruct((d, n_out), BF16),
        compiler_params=_cparams(1), name="pack_cols",
    )(w.T)


def _prep_l0(w_in0, w_uq, w_ukv):
    w = _pack_cols(w_in0, [(0, 640, 0), (640, MLA_ROPE, MLA_NOPE), (672, 1792, 0)])
    uq = w_uq.reshape(MLA_Q_LORA, MLA_HEADS, MLA_NOPE + MLA_ROPE)
    wuq = jnp.pad(uq, ((0, 0), (0, 0), (0, LANE - MLA_NOPE - MLA_ROPE))).reshape(MLA_Q_LORA, MLA_HEADS * LANE)
    ukv = w_ukv.reshape(MLA_KV_LORA, MLA_HEADS, MLA_NOPE + MLA_V)
    wuk = jnp.pad(ukv[:, :, :MLA_NOPE], ((0, 0), (0, 0), (0, LANE - MLA_NOPE))).reshape(MLA_KV_LORA, MLA_HEADS * LANE)
    wuvt = ukv[:, :, MLA_NOPE:].reshape(MLA_KV_LORA, MLA_HEADS * MLA_V).T
    return w, wuq.astype(BF16), wuk.astype(BF16), wuvt.astype(BF16)


def _prep_l1(w_in1):
    return _pack_cols(w_in1, [(0, 1536, 0), (1552, 2048, 0), (1536, 16, 0)])


def _chunk_rows(ab):
    b, t, c = ab.shape
    return jnp.swapaxes(ab.reshape(b, t // GDN_CHUNK, GDN_CHUNK, c), 2, 3)


def _trunk(x, mod, caches, p, tables, tq):
    dec = caches is not None
    t0m, t0s, t1 = tables if dec else (None, None, None)
    (qa, ckv, kr, za, qb, kb, vbt, zb, *ctx0) = _inproj0(
        x, mod[0], p["ln0"], p["w0"], p["qn"], p["wuq"], p["kvn"], (t0m + t0s) if dec else None)
    if dec:
        ckv_c, kr_c, kb_c, vb_ct, s0, kd_c, vd_ct = caches
    else:
        ckv_c = kr_c = kb_c = vb_ct = kd_c = vd_ct = None
        s0 = jnp.zeros((x.shape[0], 2, GDN_HEADS, GDN_DK, GDN_DV), F32)
    oa = _mla(qa, ckv, kr, ckv_c, kr_c, p["wuk"], p["wuvt"], tq)
    ob = _swa(p["sink"], qb, kb, vbt, kb_c, vb_ct, tq)
    (x1, qkv, zc, qd, kd, vdt, zd, ab, *ctx1) = _inproj1(
        oa, za, ob, zb, x, mod[0], p["wout0"], mod[1], p["ln1"], p["w1"], p["cw"], p["aqn"], p["akn"],
        t1 if dec else None)
    oc, sfin = _gdn(qkv, ab, _chunk_rows(ab), p["al"], p["dt"], p["alt"], p["dtt"], s0)
    od = _attd(qd, kd, vdt, kd_c, vd_ct, 2 * tq if x.shape[1] % (2 * tq) == 0 else tq)
    y = _outproj1(oc, zc, od, zd, x1, mod[1], p["gn"], p["wout1"], p["lnf"])
    return y, ctx0, sfin, ctx1


def kernel(x_prompt, x_sample, cache_l0_mla_ckv, cache_l0_mla_krope, cache_l0_swa_k, cache_l0_swa_v,
           state_l1_gdn, cache_l1_attn_k, cache_l1_attn_v, c, c_ctx,
           w_mod0, b_mod0, ln0, w_in0, mla_q_norm, w_uq, mla_kv_norm, w_ukv, swa_sink, w_out0,
           w_mod1, b_mod1, ln1, w_in1, gdn_conv, gdn_a_log, gdn_dt_bias, gdn_norm, att_q_norm, att_k_norm, w_out1,
           ln_f):
    d = x_prompt.shape[-1]
    bd, td = x_sample.shape[:2]
    bc, tc = x_prompt.shape[:2]
    past = cache_l0_mla_ckv.shape[1]
    row = lambda v: v.reshape(1, -1)
    w0, wuq, wuk, wuvt = _prep_l0(w_in0, w_uq, w_ukv)
    w1 = _prep_l1(w_in1)
    al8 = gdn_a_log.reshape(1, 2 * GDN_HEADS)
    dt8 = gdn_dt_bias.reshape(1, 2 * GDN_HEADS)
    al16 = jnp.pad(al8, ((0, 0), (0, 8)))
    dt16 = jnp.pad(dt8, ((0, 0), (0, 8)))
    p = dict(ln0=row(ln0), w0=w0, qn=row(mla_q_norm), wuq=wuq, kvn=row(mla_kv_norm), wuk=wuk, wuvt=wuvt,
             sink=swa_sink, wout0=w_out0.astype(BF16), ln1=row(ln1), w1=w1, aqn=row(att_q_norm),
             akn=row(att_k_norm), cw=gdn_conv, al=al16, dt=dt16, alt=al16.T, dtt=dt16.T, gn=row(gdn_norm),
             wout1=w_out1.astype(BF16), lnf=row(ln_f))
    n_rows = -(-(bd + 1) // 8) * 8
    c_rows = jnp.concatenate([c, c_ctx[None, :], jnp.zeros((n_rows - bd - 1, d), F32)], axis=0)
    mods = [_mod(c_rows, w_mod0, b_mod0), _mod(c_rows, w_mod1, b_mod1)]
    mod_dec = [m[:bd].reshape(bd, 3, d) for m in mods]
    mod_ctx = [m[bd:bd + 1].reshape(1, 3, d) for m in mods]
    cm, sm = _rope_table(td, MLA_ROPE)
    t0m = (jnp.asarray(_place(cm, 1.0, MLA_NOPE, LANE)), jnp.asarray(_place(sm, 0.0, MLA_NOPE, LANE)))
    cs, ss = _rope_table(td, SWA_DH)
    t0s = (jnp.asarray(np.tile(cs, (1, LANE // SWA_DH))), jnp.asarray(np.tile(ss, (1, LANE // SWA_DH))))
    t1 = tuple(jnp.asarray(a) for a in _rope_table(td, ATT_DH))
    caches = (cache_l0_mla_ckv,
              jnp.pad(cache_l0_mla_krope, ((0, 0), (0, 0), (MLA_NOPE, LANE - MLA_NOPE - MLA_ROPE))),
              cache_l0_swa_k.reshape(bd, past, SWA_KV_HEADS * SWA_DH),
              jnp.swapaxes(cache_l0_swa_v.reshape(bd, past, SWA_KV_HEADS * SWA_DH), 1, 2),
              state_l1_gdn,
              cache_l1_attn_k.reshape(bd, past, ATT_KV_HEADS * ATT_DH),
              jnp.swapaxes(cache_l1_attn_v.reshape(bd, past, ATT_KV_HEADS * ATT_DH), 1, 2))
    y_prompt, ctx0, sfin, ctx1 = _trunk(x_prompt, mod_ctx, None, p, None, tq=tc)
    y_sample, _, _, _ = _trunk(x_sample, mod_dec, caches, p, (t0m, t0s, t1), tq=256)
    ckv32, kr32, kb32, vb32 = ctx0
    kd32, vd32 = ctx1
    return (y_prompt, y_sample, ckv32, kr32,
            kb32.reshape(bc, tc, SWA_KV_HEADS, SWA_DH), vb32.reshape(bc, tc, SWA_KV_HEADS, SWA_DH),
            sfin, kd32.reshape(bc, tc, ATT_KV_HEADS, ATT_DH), vd32.reshape(bc, tc, ATT_KV_HEADS, ATT_DH))
```

```python
import functools
import math

import jax
import jax.numpy as jnp
import numpy as np
from jax import lax
from jax.experimental import pallas as pl
from jax.experimental.pallas import tpu as pltpu

F32 = jnp.float32
BF16 = jnp.bfloat16

GRID_W = 64
ROPE_THETA = 10000.0
EPS = 1e-6
WINDOW = 128
MLA_HEADS, MLA_NOPE, MLA_ROPE, MLA_V = 8, 64, 32, 64
MLA_Q_LORA, MLA_KV_LORA = 384, 256
SWA_HEADS, SWA_KV_HEADS, SWA_DH = 8, 2, 64
GDN_HEADS, GDN_DK, GDN_DV, GDN_CHUNK = 4, 128, 128, 64
GDN_LOCAL_CHUNKS = 4
ATT_HEADS, ATT_KV_HEADS, ATT_DH = 4, 2, 128
LANE = 128
LOG2E = math.log2(math.e)
NEG = -1e30
VMEM_LIMIT = 56 * 1024 * 1024


def _cparams(n_axes):
    return pltpu.CompilerParams(dimension_semantics=("arbitrary",) * n_axes, vmem_limit_bytes=VMEM_LIMIT)


def _dot(a, b):
    return jnp.dot(a, b, preferred_element_type=F32)


def _dot_nt(a, b):
    return lax.dot_general(a, b, (((1,), (1,)), ((), ())), preferred_element_type=F32)


def _dot_tn(a, b):
    return lax.dot_general(a, b, (((0,), (0,)), ((), ())), preferred_element_type=F32)


def _dot_exact(a, b):
    return jnp.dot(a, b, preferred_element_type=F32, precision=lax.Precision.HIGHEST)


def _silu(x):
    return x * jax.nn.sigmoid(x)


def _rms_rows(x, g):
    return x * lax.rsqrt(jnp.mean(x * x, axis=-1, keepdims=True) + EPS) * g


def _rope_block(x, cos, sin, half):
    lane = lax.broadcasted_iota(jnp.int32, x.shape, 1)
    first = (lane // half) % 2 == 0
    partner = jnp.where(first, pltpu.roll(x, LANE - half, 1), pltpu.roll(x, half, 1))
    return x * cos + partner * sin


def _mod_kernel(c_ref, w_ref, b_ref, o_ref):
    a = _silu(c_ref[...]).astype(BF16)
    o_ref[...] = _dot(a, w_ref[...].astype(BF16)) + b_ref[...]


def _mod(c_rows, w_mod, b_mod):
    r, d = c_rows.shape
    n = w_mod.shape[1]
    tn = 1024
    return pl.pallas_call(
        _mod_kernel,
        grid=(n // tn,),
        in_specs=[pl.BlockSpec((r, d), lambda j: (0, 0)),
                  pl.BlockSpec((d, tn), lambda j: (0, j)),
                  pl.BlockSpec((1, tn), lambda j: (0, j))],
        out_specs=pl.BlockSpec((r, tn), lambda j: (0, j)),
        out_shape=jax.ShapeDtypeStruct((r, n), F32),
        compiler_params=_cparams(1),
        name="mod",
    )(c_rows, w_mod, b_mod.reshape(1, n))


def _adaln(x, mod_ref, ln_ref):
    h = _rms_rows(x, ln_ref[...])
    return h * (1.0 + mod_ref[0, 1:2, :]) + mod_ref[0, 0:1, :]


L0_OFF = dict(cq=0, ckv=384, kr=640, za=768, qb=1280, kb=1792, vb=1920, zb=2048)


def _inproj0_kernel(*refs, rope):
    us_ref = refs[-1]
    g = pl.program_id(0)

    @pl.when(g == 0)
    def _():
        us_ref[...] = jnp.zeros_like(us_ref)

    @pl.when(g == pl.num_programs(0) - 1)
    def _():
        _inproj0_finish(refs, rope)

    @pl.when(g < pl.num_programs(0) - 1)
    def _():
        x_ref, mod_ref, ln_ref, w_ref = refs[:4]
        h = _adaln(x_ref[0], mod_ref, ln_ref).astype(BF16)
        _inproj0_finish(refs, rope)
        us_ref[...] = _dot(h, w_ref[...])


def _inproj0_finish(refs, rope):
    if rope:
        (x_ref, mod_ref, ln_ref, w_ref, qn_ref, wuq_ref, kvn_ref, cm_ref, sm_ref, cs_ref, ss_ref,
         qa_ref, ckv_ref, kr_ref, za_ref, qb_ref, kb_ref, vbt_ref, zb_ref, u) = refs
    else:
        (x_ref, mod_ref, ln_ref, w_ref, qn_ref, wuq_ref, kvn_ref,
         qa_ref, ckv_ref, kr_ref, za_ref, qb_ref, kb_ref, vbt_ref, zb_ref,
         ckv32_ref, kr32_ref, kb32_ref, vb32_ref, u) = refs
    o = L0_OFF
    cq = _rms_rows(u[:, o["cq"]:o["cq"] + 384], qn_ref[...]).astype(BF16)
    qa = _dot(cq, wuq_ref[...])
    ckv = _rms_rows(u[:, o["ckv"]:o["ckv"] + 256], kvn_ref[...])
    kr = u[:, o["kr"]:o["kr"] + 128]
    qb = u[:, o["qb"]:o["qb"] + 512]
    kb = u[:, o["kb"]:o["kb"] + 128]
    vb = u[:, o["vb"]:o["vb"] + 128]
    if not rope:
        ckv32_ref[0] = ckv
        kr32_ref[0] = kr[:, 64:96]
        kb32_ref[0] = kb
        vb32_ref[0] = vb
    qa_scale = (MLA_NOPE + MLA_ROPE) ** -0.5 * LOG2E
    qb_scale = SWA_DH ** -0.5 * LOG2E
    for j in range(MLA_HEADS):
        blk = qa[:, j * LANE:(j + 1) * LANE]
        if rope:
            blk = _rope_block(blk, cm_ref[...], sm_ref[...], MLA_ROPE // 4)
        qa_ref[0, :, j * LANE:(j + 1) * LANE] = (blk * qa_scale).astype(BF16)
    for j in range(SWA_HEADS * SWA_DH // LANE):
        blk = qb[:, j * LANE:(j + 1) * LANE]
        if rope:
            blk = _rope_block(blk, cs_ref[...], ss_ref[...], SWA_DH // 4)
        qb_ref[0, :, j * LANE:(j + 1) * LANE] = (blk * qb_scale).astype(BF16)
    if rope:
        kr = _rope_block(kr, cm_ref[...], sm_ref[...], MLA_ROPE // 4)
        kb = _rope_block(kb, cs_ref[...], ss_ref[...], SWA_DH // 4)
    ckv_ref[0] = ckv.astype(BF16)
    kr_ref[0] = kr.astype(BF16)
    kb_ref[0] = kb.astype(BF16)
    za_ref[0] = u[:, o["za"]:o["za"] + 512].astype(BF16)
    zb_ref[0] = u[:, o["zb"]:o["zb"] + 512].astype(BF16)
    vbt_ref[0] = vb.T.astype(BF16)


def _row_tile(t):
    return 512 if t % 512 == 0 else 256


def _inproj0(x, mod, ln, w, qn, wuq, kvn, tables):
    b, t, d = x.shape
    tr = _row_tile(t)
    nj = t // tr
    n_blocks = b * nj
    rope = tables is not None
    bm = mod.shape[0]
    cur = lambda g: jnp.minimum(g, n_blocks - 1)
    held = lambda g: jnp.maximum(g - 1, 0)
    full = lambda a: pl.BlockSpec(a.shape, lambda g: (0,) * a.ndim)
    rows = lambda c: pl.BlockSpec((1, tr, c), lambda g: (held(g) // nj, held(g) % nj, 0))
    in_specs = [pl.BlockSpec((1, tr, d), lambda g: (cur(g) // nj, cur(g) % nj, 0)),
                pl.BlockSpec((1, 3, d), (lambda g: (cur(g) // nj, 0, 0)) if bm > 1 else (lambda g: (0, 0, 0))),
                full(ln), full(w), full(qn), full(wuq), full(kvn)]
    args = [x, mod, ln, w, qn, wuq, kvn]
    if rope:
        for tab in tables:
            in_specs.append(pl.BlockSpec((tr, LANE), lambda g: (held(g) % nj, 0)))
            args.append(tab)
    out_shape = [jax.ShapeDtypeStruct((b, t, 1024), BF16), jax.ShapeDtypeStruct((b, t, 256), BF16),
                 jax.ShapeDtypeStruct((b, t, 128), BF16), jax.ShapeDtypeStruct((b, t, 512), BF16),
                 jax.ShapeDtypeStruct((b, t, 512), BF16), jax.ShapeDtypeStruct((b, t, 128), BF16),
                 jax.ShapeDtypeStruct((b, 128, t), BF16), jax.ShapeDtypeStruct((b, t, 512), BF16)]
    out_specs = [rows(1024), rows(256), rows(128), rows(512), rows(512), rows(128),
                 pl.BlockSpec((1, 128, tr), lambda g: (held(g) // nj, 0, held(g) % nj)), rows(512)]
    if not rope:
        out_shape += [jax.ShapeDtypeStruct((b, t, 256), F32), jax.ShapeDtypeStruct((b, t, 32), F32),
                      jax.ShapeDtypeStruct((b, t, 128), F32), jax.ShapeDtypeStruct((b, t, 128), F32)]
        out_specs += [rows(256), rows(32), rows(128), rows(128)]
    return pl.pallas_call(
        functools.partial(_inproj0_kernel, rope=rope),
        grid=(n_blocks + 1,), in_specs=in_specs, out_specs=out_specs, out_shape=out_shape,
        scratch_shapes=[pltpu.VMEM((tr, w.shape[1]), F32)],
        compiler_params=_cparams(1), name="inproj0_dec" if rope else "inproj0_ctx",
    )(*args)


KEY_CHUNK = 512
ATT_LOOKAHEAD = 4


SUM_ROWS = 16


def _attend_heads(qs, kv_of, chunks, dv, sinks=None):
    nh = len(qs)
    tq = qs[0].shape[0]
    m = [None] * nh
    acc = [None] * nh
    items = [(ci, i) for ci in range(len(chunks)) for i in range(nh)]
    loaded = {}

    def kv(ci, src):
        if (ci, src) not in loaded:
            vt = chunks[ci][1](src)
            ones = (lax.broadcasted_iota(jnp.int32, (SUM_ROWS, vt.shape[1]), 0) == 0).astype(BF16)
            loaded[(ci, src)] = (chunks[ci][0](src), jnp.concatenate([vt, ones], axis=0))
        return loaded[(ci, src)]

    if sinks is not None:
        m = [jnp.full((1, tq), sk, F32) for sk in sinks]
        unit = (lax.broadcasted_iota(jnp.int32, (dv + SUM_ROWS, tq), 0) == dv).astype(F32)
        acc = [unit for _ in sinks]

    scores = {}
    for t in range(len(items) + ATT_LOOKAHEAD):
        if t < len(items):
            ci, i = items[t]
            scores[t] = _dot_nt(kv(ci, kv_of[i])[0], qs[i])
        t0 = t - ATT_LOOKAHEAD
        if t0 < 0:
            continue
        ci, i = items[t0]
        mask = chunks[ci][2]
        si = scores.pop(t0)
        if mask is not None:
            si = jnp.where(mask, si, NEG)
        cm = si.max(axis=0, keepdims=True)
        alpha = None
        if m[i] is None:
            m_new = cm
        else:
            m_new = jnp.maximum(m[i], cm)
            alpha = jnp.exp2(m[i] - m_new)
        p = jnp.exp2(si - m_new)
        m[i] = m_new
        pv = _dot(kv(ci, kv_of[i])[1], p.astype(BF16))
        acc[i] = pv if acc[i] is None else acc[i] * alpha + pv
    return [acc[i][:dv] * (1.0 / acc[i][dv:dv + 1]) for i in range(nh)]


def _key_chunks(n):
    step = KEY_CHUNK if n % KEY_CHUNK == 0 else n
    return [(c0, step) for c0 in range(0, n, step)]


def _mla_kernel(*refs, n_new, n_ctx, hp):
    if n_ctx:
        q_ref, ckv_ref, kr_ref, ckvc_ref, krc_ref, wuk_ref, wuvt_ref, o_ref, k_s, vt_s = refs
    else:
        q_ref, ckv_ref, kr_ref, wuk_ref, wuvt_ref, o_ref, k_s, vt_s = refs
    qi, gi = pl.program_id(1), pl.program_id(2)

    @pl.when((qi == 0) & (gi == 0))
    def _():
        def expand(ckv, kr, r0, n):
            kn = _dot(ckv, wuk_ref[...])
            for j in range(MLA_HEADS):
                k_s[j, r0:r0 + n, :] = (kn[:, j * LANE:(j + 1) * LANE] + kr).astype(BF16)
            vt_s[:, r0:r0 + n] = _dot_nt(wuvt_ref[...], ckv).astype(BF16)

        blk = 512 if n_new % 512 == 0 else 256
        for r0 in range(0, n_new, blk):
            expand(ckv_ref[0, r0:r0 + blk, :], kr_ref[0, r0:r0 + blk, :].astype(F32), r0, blk)
        if n_ctx:
            expand(ckvc_ref[0].astype(BF16), krc_ref[0], n_new, n_ctx)

    tq = q_ref.shape[1]
    subs = [(s0, min(ATT_SUBQ, tq)) for s0 in range(0, tq, ATT_SUBQ)]
    qs = [q_ref[0, s0:s0 + sn, j * LANE:(j + 1) * LANE] for s0, sn in subs for j in range(hp)]
    chunks = []
    for c0, cn in _key_chunks(n_new + n_ctx):
        chunks.append((
            lambda j, c0=c0, cn=cn: k_s[gi * hp + j, c0:c0 + cn, :],
            lambda j, c0=c0, cn=cn: vt_s[pl.ds(pl.multiple_of((gi * hp + j) * MLA_V, MLA_V), MLA_V), c0:c0 + cn],
            None))
    outs = _attend_heads(qs, [j for _ in subs for j in range(hp)], chunks, MLA_V)
    for si, (s0, sn) in enumerate(subs):
        blk = jnp.concatenate(outs[si * hp:(si + 1) * hp], axis=0)
        o_ref[0, s0:s0 + sn, :] = blk.T.astype(BF16)


def _mla(q, ckv, kr, ckv_c, kr_c, wuk, wuvt, tq):
    b, t, _ = q.shape
    n_ctx = 0 if ckv_c is None else ckv_c.shape[1]
    hp = 8
    tk = t + n_ctx
    rows_q = pl.BlockSpec((1, tq, hp * LANE), lambda i, j, g: (i, j, g))
    per_b = lambda a: pl.BlockSpec((1,) + a.shape[1:], lambda i, j, g: (i, 0, 0))
    full = lambda a: pl.BlockSpec(a.shape, lambda i, j, g: (0, 0))
    in_specs = [rows_q, per_b(ckv), per_b(kr)]
    args = [q, ckv, kr]
    if n_ctx:
        in_specs += [per_b(ckv_c), per_b(kr_c)]
        args += [ckv_c, kr_c]
    in_specs += [full(wuk), full(wuvt)]
    args += [wuk, wuvt]
    return pl.pallas_call(
        functools.partial(_mla_kernel, n_new=t, n_ctx=n_ctx, hp=hp),
        grid=(b, t // tq, MLA_HEADS // hp), in_specs=in_specs,
        out_specs=pl.BlockSpec((1, tq, hp * MLA_V), lambda i, j, g: (i, j, g)),
        out_shape=jax.ShapeDtypeStruct((b, t, MLA_HEADS * MLA_V), BF16),
        scratch_shapes=[pltpu.VMEM((MLA_HEADS, tk, LANE), BF16), pltpu.VMEM((MLA_HEADS * MLA_V, tk), BF16)],
        compiler_params=_cparams(3), name="mla_dec" if n_ctx else "mla_ctx",
    )(*args)


def _swa_kernel(*refs, n_new, n_ctx, tq):
    if n_ctx:
        sink_ref, q_ref, k_ref, vt_ref, kc_ref, vct_ref, o_ref = refs
    else:
        sink_ref, q_ref, k_ref, vt_ref, o_ref = refs
    qi = pl.program_id(1)
    grp = SWA_HEADS // SWA_KV_HEADS
    if n_ctx:
        span = tq + 2 * WINDOW
        q0 = qi * tq
        start = pl.multiple_of(jnp.clip(q0 - WINDOW, 0, n_new - span), LANE)
        kpos = start + lax.broadcasted_iota(jnp.int32, (span, tq), 0)
        qpos = q0 + lax.broadcasted_iota(jnp.int32, (span, tq), 1)
        band = jnp.abs(kpos - qpos) <= WINDOW
    dh = SWA_DH
    chunks = []
    if n_ctx:
        for c0, cn in _key_chunks(span):
            chunks.append((
                lambda g, c0=c0, cn=cn: k_ref[0, pl.ds(start + c0, cn), g * dh:(g + 1) * dh],
                lambda g, c0=c0, cn=cn: vt_ref[0, g * dh:(g + 1) * dh, pl.ds(start + c0, cn)],
                band[c0:c0 + cn]))
        for c0, cn in _key_chunks(n_ctx):
            chunks.append((
                lambda g, c0=c0, cn=cn: kc_ref[0, c0:c0 + cn, g * dh:(g + 1) * dh].astype(BF16),
                lambda g, c0=c0, cn=cn: vct_ref[0, g * dh:(g + 1) * dh, c0:c0 + cn].astype(BF16),
                None))
    else:
        for c0, cn in _key_chunks(n_new):
            chunks.append((
                lambda g, c0=c0, cn=cn: k_ref[0, c0:c0 + cn, g * dh:(g + 1) * dh],
                lambda g, c0=c0, cn=cn: vt_ref[0, g * dh:(g + 1) * dh, c0:c0 + cn],
                None))
    qs = [q_ref[0, :, h * dh:(h + 1) * dh] for h in range(SWA_HEADS)]
    sinks = [sink_ref[h] * LOG2E for h in range(SWA_HEADS)]
    outs = _attend_heads(qs, [h // grp for h in range(SWA_HEADS)], chunks, SWA_DH, sinks)
    o_ref[0] = jnp.concatenate(outs, axis=0).T.astype(BF16)


def _swa(sink, q, k, vt, k_c, v_ct, tq):
    b, t, _ = q.shape
    n_ctx = 0 if k_c is None else k_c.shape[1]
    assert n_ctx == 0 or (t >= tq + 2 * WINDOW and tq % LANE == 0 and WINDOW % LANE == 0)
    per_b = lambda a: pl.BlockSpec((1,) + a.shape[1:], lambda i, j: (i, 0, 0))
    in_specs = [pl.BlockSpec(memory_space=pltpu.SMEM), pl.BlockSpec((1, tq, 512), lambda i, j: (i, j, 0)),
                per_b(k), per_b(vt)]
    args = [sink, q, k, vt]
    if n_ctx:
        in_specs += [per_b(k_c), per_b(v_ct)]
        args += [k_c, v_ct]
    return pl.pallas_call(
        functools.partial(_swa_kernel, n_new=t, n_ctx=n_ctx, tq=tq),
        grid=(b, t // tq), in_specs=in_specs,
        out_specs=pl.BlockSpec((1, tq, 512), lambda i, j: (i, j, 0)),
        out_shape=jax.ShapeDtypeStruct((b, t, 512), BF16),
        compiler_params=_cparams(2), name="swa_dec" if n_ctx else "swa_ctx",
    )(*args)


def _silu_gate(o, z):
    return o * (z / (1.0 + jnp.exp(-z)))


L1_OFF = dict(qkv=0, zc=1536, qd=2048, kd=2560, vd=2816, zd=3072, ab=3584)
L1_W = 3712


def _inproj1_kernel(*refs, rope, nj):
    if rope:
        (oa_ref, za_ref, ob_ref, zb_ref, x_ref, mod0_ref, wo_ref, mod_ref, ln_ref, w_ref, cw_ref,
         qn_ref, kn_ref, c_ref, s_ref,
         x1_ref, qkv_ref, zc_ref, qd_ref, kd_ref, vdt_ref, zd_ref, ab_ref, us_ref, pr_ref) = refs
    else:
        (oa_ref, za_ref, ob_ref, zb_ref, x_ref, mod0_ref, wo_ref, mod_ref, ln_ref, w_ref, cw_ref,
         qn_ref, kn_ref,
         x1_ref, qkv_ref, zc_ref, qd_ref, kd_ref, vdt_ref, zd_ref, ab_ref, kd32_ref, vd32_ref, us_ref, pr_ref) = refs
    g = pl.program_id(0)
    tr = us_ref.shape[0]
    nq = GDN_HEADS * GDN_DK
    n_qkv = 2 * nq + GDN_HEADS * GDN_DV
    o = L1_OFF

    @pl.when(g == 0)
    def _():
        us_ref[...] = jnp.zeros_like(us_ref)
        pr_ref[...] = jnp.zeros_like(pr_ref)

    def conv_held_block(next_row):
        jb = lax.rem(g + nj - 1, nj)
        prv = jnp.where(jb > 0, pr_ref[7:8, :], 0.0)
        nxt = jnp.where(jb < nj - 1, next_row, 0.0)
        row8 = lax.broadcasted_iota(jnp.int32, (8, LANE), 0)
        for j in range(n_qkv // LANE):
            cols = slice(j * LANE, (j + 1) * LANE)
            xj = us_ref[:, cols]
            xp = pltpu.roll(xj, 1, 0)
            xp = jnp.concatenate([jnp.where(row8 == 0, prv[:, cols], xp[0:8]), xp[8:]], axis=0)
            xn = pltpu.roll(xj, tr - 1, 0)
            xn = jnp.concatenate([xn[:tr - 8], jnp.where(row8 == 7, nxt[:, cols], xn[tr - 8:])], axis=0)
            cw = cw_ref[:, cols]
            y = _silu(xp * cw[0:1, :] + xj * cw[1:2, :] + xn * cw[2:3, :])
            if j < 2 * nq // LANE:
                y = y * lax.rsqrt(jnp.sum(y * y, axis=-1, keepdims=True) + EPS)
            if j < nq // LANE:
                y = y * GDN_DK ** -0.5
            qkv_ref[0, :, cols] = y.astype(BF16)
        pr_ref[...] = us_ref[tr - 8:tr, 0:n_qkv]
        zc_ref[0] = us_ref[:, o["zc"]:o["zc"] + 512].astype(BF16)
        zd_ref[0] = us_ref[:, o["zd"]:o["zd"] + 512].astype(BF16)
        ab_ref[0] = us_ref[:, o["ab"]:o["ab"] + 16]
        qd_scale = ATT_DH ** -0.5 * LOG2E
        for j in range(ATT_HEADS):
            blk = _rms_rows(us_ref[:, o["qd"] + j * LANE:o["qd"] + (j + 1) * LANE], qn_ref[...])
            if rope:
                blk = _rope_block(blk, c_ref[...], s_ref[...], ATT_DH // 4)
            qd_ref[0, :, j * LANE:(j + 1) * LANE] = (blk * qd_scale).astype(BF16)
        for j in range(ATT_KV_HEADS):
            blk = _rms_rows(us_ref[:, o["kd"] + j * LANE:o["kd"] + (j + 1) * LANE], kn_ref[...])
            if rope:
                blk = _rope_block(blk, c_ref[...], s_ref[...], ATT_DH // 4)
            else:
                kd32_ref[0, :, j * LANE:(j + 1) * LANE] = blk
            kd_ref[0, :, j * LANE:(j + 1) * LANE] = blk.astype(BF16)
        vd = us_ref[:, o["vd"]:o["vd"] + 256]
        if not rope:
            vd32_ref[0] = vd
        vdt_ref[0] = vd.T.astype(BF16)

    @pl.when(g == pl.num_programs(0) - 1)
    def _():
        conv_held_block(jnp.zeros((1, n_qkv), F32))

    @pl.when(g < pl.num_programs(0) - 1)
    def _():
        _inproj1_block(refs, rope, conv_held_block)


def _inproj1_block(refs, rope, conv_held_block):
    if rope:
        (oa_ref, za_ref, ob_ref, zb_ref, x_ref, mod0_ref, wo_ref, mod_ref, ln_ref, w_ref, cw_ref,
         qn_ref, kn_ref, c_ref, s_ref,
         x1_ref, qkv_ref, zc_ref, qd_ref, kd_ref, vdt_ref, zd_ref, ab_ref, us_ref, pr_ref) = refs
    else:
        (oa_ref, za_ref, ob_ref, zb_ref, x_ref, mod0_ref, wo_ref, mod_ref, ln_ref, w_ref, cw_ref,
         qn_ref, kn_ref,
         x1_ref, qkv_ref, zc_ref, qd_ref, kd_ref, vdt_ref, zd_ref, ab_ref, kd32_ref, vd32_ref, us_ref, pr_ref) = refs
    n_qkv = pr_ref.shape[1]
    y0 = (_dot(_silu_gate(oa_ref[0], za_ref[0]), wo_ref[0:512, :])
          + _dot(_silu_gate(ob_ref[0], zb_ref[0]), wo_ref[512:1024, :]))
    x1 = x_ref[0] + mod0_ref[0, 2:3, :] * y0
    x1_ref[0] = x1
    h = _adaln(x1, mod_ref, ln_ref).astype(BF16)
    conv_held_block(_dot(h[0:8], w_ref[:, 0:n_qkv])[0:1])
    us_ref[...] = _dot(h, w_ref[...])


def _inproj1(oa, za, ob, zb, x, mod0, wo, mod, ln, w, cw, qn, kn, tables):
    b, t, d = x.shape
    tr = _row_tile(t)
    nj = t // tr
    n_blocks = b * nj
    rope = tables is not None
    bm = mod.shape[0]
    n_qkv = cw.shape[1]
    cur = lambda g: jnp.minimum(g, n_blocks - 1)
    held = lambda g: jnp.maximum(g - 1, 0)
    full = lambda a: pl.BlockSpec(a.shape, lambda g: (0,) * a.ndim)
    rows = lambda c: pl.BlockSpec((1, tr, c), lambda g: (cur(g) // nj, cur(g) % nj, 0))
    mod_spec = pl.BlockSpec((1, 3, d), (lambda g: (cur(g) // nj, 0, 0)) if bm > 1 else (lambda g: (0, 0, 0)))
    widths = [512, 512, 512, 512, d]
    in_specs = ([rows(c) for c in widths]
                + [mod_spec, full(wo), mod_spec, full(ln), full(w), full(cw), full(qn), full(kn)])
    args = [oa, za, ob, zb, x, mod0, wo, mod, ln, w, cw, qn, kn]
    hrows = lambda c: pl.BlockSpec((1, tr, c), lambda g: (held(g) // nj, held(g) % nj, 0))
    if rope:
        for tab in tables:
            in_specs.append(pl.BlockSpec((tr, LANE), lambda g: (held(g) % nj, 0)))
            args.append(tab)
    out_shape = [jax.ShapeDtypeStruct((b, t, d), F32),
                 jax.ShapeDtypeStruct((b, t, n_qkv), BF16), jax.ShapeDtypeStruct((b, t, 512), BF16),
                 jax.ShapeDtypeStruct((b, t, 512), BF16), jax.ShapeDtypeStruct((b, t, 256), BF16),
                 jax.ShapeDtypeStruct((b, 256, t), BF16), jax.ShapeDtypeStruct((b, t, 512), BF16),
                 jax.ShapeDtypeStruct((b, t, 16), F32)]
    out_specs = [rows(d), hrows(n_qkv), hrows(512), hrows(512), hrows(256),
                 pl.BlockSpec((1, 256, tr), lambda g: (held(g) // nj, 0, held(g) % nj)), hrows(512), hrows(16)]
    if not rope:
        out_shape += [jax.ShapeDtypeStruct((b, t, 256), F32), jax.ShapeDtypeStruct((b, t, 256), F32)]
        out_specs += [hrows(256), hrows(256)]
    return pl.pallas_call(
        functools.partial(_inproj1_kernel, rope=rope, nj=nj),
        grid=(n_blocks + 1,), in_specs=in_specs, out_specs=out_specs, out_shape=out_shape,
        scratch_shapes=[pltpu.VMEM((tr, L1_W), F32), pltpu.VMEM((8, n_qkv), F32)],
        compiler_params=_cparams(1), name="inproj1_dec" if rope else "inproj1_ctx",
    )(*args)


def _gdn_local(blocks):
    c = GDN_CHUNK
    row = lax.broadcasted_iota(jnp.int32, (c, c), 0)
    col = lax.broadcasted_iota(jnp.int32, (c, c), 1)
    lane2 = lax.broadcasted_iota(jnp.int32, (c, 2 * c), 1)
    eye = (row == col).astype(F32)
    eye_t = jnp.concatenate([eye, jnp.zeros((c, c), F32)], axis=1).astype(BF16)
    chains = [ch for blk in blocks for ch in blk["dirs"]]
    for blk in blocks:
        for ch in blk["dirs"]:
            causal = (row <= col) if ch["upper"] else (row >= col)
            ch["strict"] = (row < col) if ch["upper"] else (row > col)
            ch["decay"] = jnp.exp(jnp.where(causal, ch["gc_col"] - ch["gc_row"], -jnp.inf))
            ch["kb"] = blk["k"] * ch["beta_col"]
            ch["egc"] = jnp.exp(ch["gc_col"])
    for blk in blocks:
        lhs = jnp.concatenate([ch["kb"] for ch in blk["dirs"]] + [blk["q"]], axis=0).astype(BF16)
        a = _dot_nt(lhs, blk["k"].astype(BF16))
        nd = len(blk["dirs"])
        for di, ch in enumerate(blk["dirs"]):
            x = jnp.where(ch["strict"], -(a[di * c:(di + 1) * c] * ch["decay"]), 0.0)
            ch["intra"] = (a[nd * c:] * ch["decay"]).astype(BF16)
            ch["w"] = jnp.concatenate([eye, x], axis=1)
    for _ in range(6):
        for ch in chains:
            w = ch["w"]
            wh = w.astype(BF16)
            lo = w - wh.astype(F32)
            php = jnp.where(lane2 < c, pltpu.roll(w, c, 1), lo).astype(BF16)
            ch["w"] = _dot(jnp.concatenate([wh, php], axis=1),
                           jnp.concatenate([eye_t, wh, lo.astype(BF16), wh], axis=0))
    for blk in blocks:
        for ch in blk["dirs"]:
            rhs = jnp.concatenate([blk["v"] * ch["beta_col"], ch["kb"] * ch["egc"]], axis=1).astype(BF16)
            sol = _dot(ch["w"][:, :c].astype(BF16), rhs)
            ch["u"], ch["wv"] = sol[:, :GDN_DV].astype(BF16), sol[:, GDN_DV:].astype(BF16)
            ch["qe"] = (blk["q"] * ch["egc"]).astype(BF16)
            ch["kd"] = (blk["k"] * jnp.exp(ch["glast"] - ch["gc_col"])).astype(BF16)
            ch["eg"] = jnp.exp(ch["glast"])


def _gdn_scan(chains):
    c = GDN_CHUNK
    for ch in chains:
        ch["sb"] = ch["s"].astype(BF16)
    for ch in chains:
        r = _dot(jnp.concatenate([ch["wv"], ch["qe"]], axis=0), ch["sb"])
        ch["vn"] = (ch["u"].astype(F32) - r[:c]).astype(BF16)
        ch["qs"] = r[c:]
    outs = []
    for ch in chains:
        o = ch["qs"] + _dot(ch["intra"], ch["vn"])
        s_new = ch["s"] * ch["eg"] + _dot_tn(ch["kd"], ch["vn"])
        outs.append((o, s_new))
    return outs


def _gdn_kernel(qkv_ref, ab_ref, abt_ref, al_ref, dt_ref, alt_ref, dtt_ref, s0_ref, o_ref, sf_ref,
                u_s, wv_s, qe_s, kd_s, in_s, eg_s, gcol_s, grow_s, beta_s, st_s, *, t):
    c = GDN_CHUNK
    n = t // c
    nh = GDN_HEADS
    ab = ab_ref[0]
    gact = -jnp.exp(al_ref[...]) * jax.nn.softplus(ab + dt_ref[...])
    lane16 = lax.broadcasted_iota(jnp.int32, ab.shape, 1)
    beta_s[...] = jnp.where(lane16 < 2 * nh, gact, jax.nn.sigmoid(ab))
    r64 = lax.broadcasted_iota(jnp.int32, (c, c), 0)
    c64 = lax.broadcasted_iota(jnp.int32, (c, c), 1)
    tril = (r64 >= c64).astype(F32)
    triu = (r64 <= c64).astype(F32)
    lane_c = lax.broadcasted_iota(jnp.int32, (c, 16), 1)
    sub_c = lax.broadcasted_iota(jnp.int32, (16, c), 0)

    def cum_chunk(i):
        r0 = pl.multiple_of(i * c, c)
        g = beta_s[pl.ds(r0, c), :]
        gcol_s[pl.ds(r0, c), :] = jnp.where(lane_c < nh, _dot_exact(tril, g), _dot_exact(triu, g))
        gt = -jnp.exp(alt_ref[...]) * jax.nn.softplus(abt_ref[0, i] + dtt_ref[...])
        grow_s[i] = jnp.where(sub_c < nh, _dot_exact(gt, triu), _dot_exact(gt, tril))

    ncs = GDN_LOCAL_CHUNKS
    rb = ncs * c
    for sub in range(ncs):
        cum_chunk(sub)

    def local_body(jb, carry):
        r0 = pl.multiple_of(jb * rb, rb)
        nxt_blk = jnp.minimum(jb + 1, n // ncs - 1)
        xs = [qkv_ref[0, pl.ds(r0, rb), j * LANE:(j + 1) * LANE].astype(F32) for j in range(3 * nh)]
        gcol = gcol_s[pl.ds(r0, rb), :]
        bet = beta_s[pl.ds(r0, rb), :]
        blocks = []
        for sub in range(ncs):
            ci = ncs * jb + sub
            rows = slice(sub * c, (sub + 1) * c)
            grow = grow_s[ci]
            for hh in range(nh):
                dirs = []
                for d in range(2):
                    ch = d * nh + hh
                    last = sub * c + (c - 1 if d == 0 else 0)
                    dirs.append(dict(gc_col=gcol[rows, ch:ch + 1], gc_row=grow[ch:ch + 1, :],
                                     beta_col=bet[rows, 2 * nh + ch:2 * nh + ch + 1],
                                     glast=gcol[last:last + 1, ch:ch + 1], upper=(d == 1), ch=ch, ci=ci,
                                     r0=r0 + sub * c))
                blocks.append(dict(q=xs[hh][rows], k=xs[nh + hh][rows], v=xs[2 * nh + hh][rows], dirs=dirs))
        _gdn_local(blocks)
        for blk in blocks:
            for chn in blk["dirs"]:
                ch, rr = chn["ch"], pl.ds(pl.multiple_of(chn["r0"], c), c)
                u_s[ch, rr, :] = chn["u"]
                wv_s[ch, rr, :] = chn["wv"]
                qe_s[ch, rr, :] = chn["qe"]
                kd_s[ch, rr, :] = chn["kd"]
                in_s[ch, chn["ci"]] = chn["intra"]
                eg_s[chn["ci"], ch:ch + 1, :] = jnp.broadcast_to(chn["eg"], (1, LANE))
        for sub in range(ncs):
            cum_chunk(nxt_blk * ncs + sub)
        return carry

    lax.fori_loop(0, n // ncs, local_body, 0)

    for d in range(2):
        for hh in range(nh):
            st_s[d * nh + hh] = s0_ref[0, d, hh]
    o_ref[...] = jnp.zeros_like(o_ref)

    def scan_body(i, carry):
        chains = []
        for d in range(2):
            ci = i if d == 0 else n - 1 - i
            rr = pl.ds(pl.multiple_of(ci * c, c), c)
            eg = eg_s[ci]
            for hh in range(nh):
                ch = d * nh + hh
                chains.append(dict(u=u_s[ch, rr, :], wv=wv_s[ch, rr, :], qe=qe_s[ch, rr, :], kd=kd_s[ch, rr, :],
                                   intra=in_s[ch, ci], eg=eg[ch:ch + 1, :], s=st_s[ch], rr=rr, hh=hh, ch=ch))
        for chn, (o, s_new) in zip(chains, _gdn_scan(chains)):
            st_s[chn["ch"]] = s_new
            cols = slice(chn["hh"] * LANE, (chn["hh"] + 1) * LANE)
            o_ref[0, chn["rr"], cols] = (o_ref[0, chn["rr"], cols].astype(F32) + o).astype(o_ref.dtype)
        return carry

    lax.fori_loop(0, n, scan_body, 0)
    for d in range(2):
        for hh in range(nh):
            sf_ref[0, d, hh] = st_s[d * nh + hh]


def _gdn(qkv, ab, abt, al, dt, alt, dtt, s0):
    b, t, _ = qkv.shape
    n = t // GDN_CHUNK
    assert t % (GDN_CHUNK * GDN_LOCAL_CHUNKS) == 0
    per_b = lambda a: pl.BlockSpec((1,) + a.shape[1:], lambda i: (i,) + (0,) * (a.ndim - 1))
    full = lambda a: pl.BlockSpec(a.shape, lambda i: (0,) * a.ndim)
    return pl.pallas_call(
        functools.partial(_gdn_kernel, t=t), grid=(b,),
        in_specs=[per_b(qkv), per_b(ab), per_b(abt), full(al), full(dt), full(alt), full(dtt), per_b(s0)],
        out_specs=[pl.BlockSpec((1, t, GDN_HEADS * GDN_DV), lambda i: (i, 0, 0)), per_b(s0)],
        out_shape=[jax.ShapeDtypeStruct((b, t, GDN_HEADS * GDN_DV), BF16), jax.ShapeDtypeStruct(s0.shape, F32)],
        scratch_shapes=[pltpu.VMEM((2 * GDN_HEADS, t, LANE), BF16), pltpu.VMEM((2 * GDN_HEADS, t, LANE), BF16),
                        pltpu.VMEM((2 * GDN_HEADS, t, LANE), BF16), pltpu.VMEM((2 * GDN_HEADS, t, LANE), BF16),
                        pltpu.VMEM((2 * GDN_HEADS, n, GDN_CHUNK, GDN_CHUNK), BF16),
                        pltpu.VMEM((n, 2 * GDN_HEADS, LANE), F32),
                        pltpu.VMEM((t, 16), F32), pltpu.VMEM((n, 16, GDN_CHUNK), F32), pltpu.VMEM((t, 16), F32),
                        pltpu.VMEM((2 * GDN_HEADS, GDN_DK, GDN_DV), F32)],
        compiler_params=_cparams(1), name="gdn",
    )(qkv, ab, abt, al, dt, alt, dtt, s0)


ATT_SUBQ = 256


def _attd_kernel(*refs, n_ctx):
    if n_ctx:
        q_ref, k_ref, vt_ref, kc_ref, vct_ref, o_ref = refs
    else:
        q_ref, k_ref, vt_ref, o_ref = refs
    n_new = k_ref.shape[1]
    grp = ATT_HEADS // ATT_KV_HEADS
    chunks = []
    for c0, cn in _key_chunks(n_new):
        chunks.append((
            lambda g, c0=c0, cn=cn: k_ref[0, c0:c0 + cn, g * LANE:(g + 1) * LANE],
            lambda g, c0=c0, cn=cn: vt_ref[0, g * LANE:(g + 1) * LANE, c0:c0 + cn],
            None))
    if n_ctx:
        for c0, cn in _key_chunks(n_ctx):
            chunks.append((
                lambda g, c0=c0, cn=cn: kc_ref[0, c0:c0 + cn, g * LANE:(g + 1) * LANE].astype(BF16),
                lambda g, c0=c0, cn=cn: vct_ref[0, g * LANE:(g + 1) * LANE, c0:c0 + cn].astype(BF16),
                None))
    tq = q_ref.shape[1]
    subs = [(s0, min(ATT_SUBQ, tq)) for s0 in range(0, tq, ATT_SUBQ)]
    qs = [q_ref[0, s0:s0 + sn, h * LANE:(h + 1) * LANE] for s0, sn in subs for h in range(ATT_HEADS)]
    outs = _attend_heads(qs, [h // grp for _ in subs for h in range(ATT_HEADS)], chunks, ATT_DH)
    for si, (s0, sn) in enumerate(subs):
        blk = jnp.concatenate(outs[si * ATT_HEADS:(si + 1) * ATT_HEADS], axis=0)
        o_ref[0, s0:s0 + sn, :] = blk.T.astype(BF16)


def _attd(q, k, vt, k_c, v_ct, tq):
    b, t, _ = q.shape
    n_ctx = 0 if k_c is None else k_c.shape[1]
    per_b = lambda a: pl.BlockSpec((1,) + a.shape[1:], lambda i, j: (i, 0, 0))
    in_specs = [pl.BlockSpec((1, tq, 512), lambda i, j: (i, j, 0)), per_b(k), per_b(vt)]
    args = [q, k, vt]
    if n_ctx:
        in_specs += [per_b(k_c), per_b(v_ct)]
        args += [k_c, v_ct]
    return pl.pallas_call(
        functools.partial(_attd_kernel, n_ctx=n_ctx),
        grid=(b, t // tq), in_specs=in_specs,
        out_specs=pl.BlockSpec((1, tq, 512), lambda i, j: (i, j, 0)),
        out_shape=jax.ShapeDtypeStruct((b, t, 512), BF16),
        compiler_params=_cparams(2), name="attd_dec" if n_ctx else "attd_ctx",
    )(*args)


def _outproj1_kernel(oc_ref, zc_ref, od_ref, zd_ref, x_ref, mod_ref, gn_ref, w_ref, lnf_ref, y_ref):
    parts = []
    for j in range(GDN_HEADS):
        parts.append(_rms_rows(oc_ref[0, :, j * LANE:(j + 1) * LANE].astype(F32), gn_ref[...]).astype(BF16))
    gc = _silu_gate(jnp.concatenate(parts, axis=1), zc_ref[0])
    gd = _silu_gate(od_ref[0], zd_ref[0])
    y = _dot(gc, w_ref[0:512, :]) + _dot(gd, w_ref[512:1024, :])
    x2 = x_ref[0] + mod_ref[0, 2:3, :] * y
    y_ref[0] = _rms_rows(x2, lnf_ref[...])


def _outproj1(oc, zc, od, zd, x, mod, gn, w, lnf):
    b, t, d = x.shape
    tr = 1024 if t % 1024 == 0 else _row_tile(t)
    bm = mod.shape[0]
    rows = lambda c: pl.BlockSpec((1, tr, c), lambda i, j: (i, j, 0))
    full = lambda a: pl.BlockSpec(a.shape, lambda i, j: (0,) * a.ndim)
    return pl.pallas_call(
        _outproj1_kernel, grid=(b, t // tr),
        in_specs=[rows(512), rows(512), rows(512), rows(512), rows(d),
                  pl.BlockSpec((1, 3, d), (lambda i, j: (i, 0, 0)) if bm > 1 else (lambda i, j: (0, 0, 0))),
                  full(gn), full(w), full(lnf)],
        out_specs=rows(d), out_shape=jax.ShapeDtypeStruct((b, t, d), F32),
        compiler_params=_cparams(2), name="outproj1",
    )(oc, zc, od, zd, x, mod, gn, w, lnf)


def _rope_table(n_tok, rot_dim):
    quarter = rot_dim // 4
    inv = np.float32(ROPE_THETA) ** (-np.arange(quarter, dtype=np.float32) / np.float32(quarter))
    tt = np.arange(n_tok)
    pos = np.stack([tt // GRID_W, tt % GRID_W], axis=-1).astype(np.float32)
    ang = (pos[:, :, None] * inv).astype(np.float32)
    cos, sin = np.cos(ang), np.sin(ang)
    c = np.concatenate([cos, cos], axis=-1).reshape(n_tok, rot_dim)
    s = np.concatenate([-sin, sin], axis=-1).reshape(n_tok, rot_dim)
    return c.astype(np.float32), s.astype(np.float32)


def _place(tab, fill, off, width):
    out = np.full((tab.shape[0], width), fill, np.float32)
    out[:, off:off + tab.shape[1]] = tab
    return out


PACK_STEP = 256


def _pack_kernel(wt_ref, o_ref, *, segs):
    d = wt_ref.shape[1]
    lane = lax.broadcasted_iota(jnp.int32, (d, LANE), 1)
    off = 0
    for start, width, lane_off in segs:
        if width % LANE == 0:
            for c0 in range(0, width, PACK_STEP):
                cw = min(PACK_STEP, width - c0)
                o_ref[:, off + c0:off + c0 + cw] = wt_ref[start + c0:start + c0 + cw, :].T.astype(BF16)
            off += width
        else:
            blk = jnp.where(lane < width, wt_ref[start:start + LANE, :].T, 0.0)
            if lane_off:
                blk = pltpu.roll(blk, lane_off, 1)
            o_ref[:, off:off + LANE] = blk.astype(BF16)
            off += LANE


def _pack_cols(w, segs):
    d, n_in = w.shape
    n_out = sum(width if width % LANE == 0 else LANE for _, width, _ in segs)
    return pl.pallas_call(
        functools.partial(_pack_kernel, segs=segs), grid=(1,),
        in_specs=[pl.BlockSpec((n_in, d), lambda i: (0, 0))],
        out_specs=pl.BlockSpec((d, n_out), lambda i: (0, 0)),
        out_shape=jax.ShapeDtypeStruct((d, n_out), BF16),
        compiler_params=_cparams(1), name="pack_cols",
    )(w.T)


def _prep_l0(w_in0, w_uq, w_ukv):
    w = _pack_cols(w_in0, [(0, 640, 0), (640, MLA_ROPE, MLA_NOPE), (672, 1792, 0)])
    uq = w_uq.reshape(MLA_Q_LORA, MLA_HEADS, MLA_NOPE + MLA_ROPE)
    wuq = jnp.pad(uq, ((0, 0), (0, 0), (0, LANE - MLA_NOPE - MLA_ROPE))).reshape(MLA_Q_LORA, MLA_HEADS * LANE)
    ukv = w_ukv.reshape(MLA_KV_LORA, MLA_HEADS, MLA_NOPE + MLA_V)
    wuk = jnp.pad(ukv[:, :, :MLA_NOPE], ((0, 0), (0, 0), (0, LANE - MLA_NOPE))).reshape(MLA_KV_LORA, MLA_HEADS * LANE)
    wuvt = ukv[:, :, MLA_NOPE:].reshape(MLA_KV_LORA, MLA_HEADS * MLA_V).T
    return w, wuq.astype(BF16), wuk.astype(BF16), wuvt.astype(BF16)


def _prep_l1(w_in1):
    return _pack_cols(w_in1, [(0, 1536, 0), (1552, 2048, 0), (1536, 16, 0)])


def _chunk_rows(ab):
    b, t, c = ab.shape
    return jnp.swapaxes(ab.reshape(b, t // GDN_CHUNK, GDN_CHUNK, c), 2, 3)


def _trunk(x, mod, caches, p, tables, tq):
    dec = caches is not None
    t0m, t0s, t1 = tables if dec else (None, None, None)
    (qa, ckv, kr, za, qb, kb, vbt, zb, *ctx0) = _inproj0(
        x, mod[0], p["ln0"], p["w0"], p["qn"], p["wuq"], p["kvn"], (t0m + t0s) if dec else None)
    if dec:
        ckv_c, kr_c, kb_c, vb_ct, s0, kd_c, vd_ct = caches
    else:
        ckv_c = kr_c = kb_c = vb_ct = kd_c = vd_ct = None
        s0 = jnp.zeros((x.shape[0], 2, GDN_HEADS, GDN_DK, GDN_DV), F32)
    oa = _mla(qa, ckv, kr, ckv_c, kr_c, p["wuk"], p["wuvt"], 2 * tq if x.shape[1] % (2 * tq) == 0 else tq)
    ob = _swa(p["sink"], qb, kb, vbt, kb_c, vb_ct, tq)
    (x1, qkv, zc, qd, kd, vdt, zd, ab, *ctx1) = _inproj1(
        oa, za, ob, zb, x, mod[0], p["wout0"], mod[1], p["ln1"], p["w1"], p["cw"], p["aqn"], p["akn"],
        t1 if dec else None)
    oc, sfin = _gdn(qkv, ab, _chunk_rows(ab), p["al"], p["dt"], p["alt"], p["dtt"], s0)
    od = _attd(qd, kd, vdt, kd_c, vd_ct, 2 * tq if x.shape[1] % (2 * tq) == 0 else tq)
    y = _outproj1(oc, zc, od, zd, x1, mod[1], p["gn"], p["wout1"], p["lnf"])
    return y, ctx0, sfin, ctx1


def kernel(x_prompt, x_sample, cache_l0_mla_ckv, cache_l0_mla_krope, cache_l0_swa_k, cache_l0_swa_v,
           state_l1_gdn, cache_l1_attn_k, cache_l1_attn_v, c, c_ctx,
           w_mod0, b_mod0, ln0, w_in0, mla_q_norm, w_uq, mla_kv_norm, w_ukv, swa_sink, w_out0,
           w_mod1, b_mod1, ln1, w_in1, gdn_conv, gdn_a_log, gdn_dt_bias, gdn_norm, att_q_norm, att_k_norm, w_out1,
           ln_f):
    d = x_prompt.shape[-1]
    bd, td = x_sample.shape[:2]
    bc, tc = x_prompt.shape[:2]
    past = cache_l0_mla_ckv.shape[1]
    row = lambda v: v.reshape(1, -1)
    w0, wuq, wuk, wuvt = _prep_l0(w_in0, w_uq, w_ukv)
    w1 = _prep_l1(w_in1)
    al8 = gdn_a_log.reshape(1, 2 * GDN_HEADS)
    dt8 = gdn_dt_bias.reshape(1, 2 * GDN_HEADS)
    al16 = jnp.pad(al8, ((0, 0), (0, 8)))
    dt16 = jnp.pad(dt8, ((0, 0), (0, 8)))
    p = dict(ln0=row(ln0), w0=w0, qn=row(mla_q_norm), wuq=wuq, kvn=row(mla_kv_norm), wuk=wuk, wuvt=wuvt,
             sink=swa_sink, wout0=w_out0.astype(BF16), ln1=row(ln1), w1=w1, aqn=row(att_q_norm),
             akn=row(att_k_norm), cw=gdn_conv, al=al16, dt=dt16, alt=al16.T, dtt=dt16.T, gn=row(gdn_norm),
             wout1=w_out1.astype(BF16), lnf=row(ln_f))
    n_rows = -(-(bd + 1) // 8) * 8
    c_rows = jnp.concatenate([c, c_ctx[None, :], jnp.zeros((n_rows - bd - 1, d), F32)], axis=0)
    mods = [_mod(c_rows, w_mod0, b_mod0), _mod(c_rows, w_mod1, b_mod1)]
    mod_dec = [m[:bd].reshape(bd, 3, d) for m in mods]
    mod_ctx = [m[bd:bd + 1].reshape(1, 3, d) for m in mods]
    cm, sm = _rope_table(td, MLA_ROPE)
    t0m = (jnp.asarray(_place(cm, 1.0, MLA_NOPE, LANE)), jnp.asarray(_place(sm, 0.0, MLA_NOPE, LANE)))
    cs, ss = _rope_table(td, SWA_DH)
    t0s = (jnp.asarray(np.tile(cs, (1, LANE // SWA_DH))), jnp.asarray(np.tile(ss, (1, LANE // SWA_DH))))
    t1 = tuple(jnp.asarray(a) for a in _rope_table(td, ATT_DH))
    caches = (cache_l0_mla_ckv,
              jnp.pad(cache_l0_mla_krope, ((0, 0), (0, 0), (MLA_NOPE, LANE - MLA_NOPE - MLA_ROPE))),
              cache_l0_swa_k.reshape(bd, past, SWA_KV_HEADS * SWA_DH),
              jnp.swapaxes(cache_l0_swa_v.reshape(bd, past, SWA_KV_HEADS * SWA_DH), 1, 2),
              state_l1_gdn,
              cache_l1_attn_k.reshape(bd, past, ATT_KV_HEADS * ATT_DH),
              jnp.swapaxes(cache_l1_attn_v.reshape(bd, past, ATT_KV_HEADS * ATT_DH), 1, 2))
    y_prompt, ctx0, sfin, ctx1 = _trunk(x_prompt, mod_ctx, None, p, None, tq=tc)
    y_sample, _, _, _ = _trunk(x_sample, mod_dec, caches, p, (t0m, t0s, t1), tq=256)
    ckv32, kr32, kb32, vb32 = ctx0
    kd32, vd32 = ctx1
    return (y_prompt, y_sample, ckv32, kr32,
            kb32.reshape(bc, tc, SWA_KV_HEADS, SWA_DH), vb32.reshape(bc, tc, SWA_KV_HEADS, SWA_DH),
            sfin, kd32.reshape(bc, tc, ATT_KV_HEADS, ATT_DH), vd32.reshape(bc, tc, ATT_KV_HEADS, ATT_DH))
```

```python
import functools
import math

import jax
import jax.numpy as jnp
import numpy as np
from jax import lax
from jax.experimental import pallas as pl
from jax.experimental.pallas import tpu as pltpu

F32 = jnp.float32
BF16 = jnp.bfloat16

GRID_W = 64
ROPE_THETA = 10000.0
EPS = 1e-6
WINDOW = 128
MLA_HEADS, MLA_NOPE, MLA_ROPE, MLA_V = 8, 64, 32, 64
MLA_Q_LORA, MLA_KV_LORA = 384, 256
SWA_HEADS, SWA_KV_HEADS, SWA_DH = 8, 2, 64
GDN_HEADS, GDN_DK, GDN_DV, GDN_CHUNK = 4, 128, 128, 64
GDN_LOCAL_CHUNKS = 4
ATT_HEADS, ATT_KV_HEADS, ATT_DH = 4, 2, 128
LANE = 128
LOG2E = math.log2(math.e)
NEG = -1e30
VMEM_LIMIT = 56 * 1024 * 1024


def _cparams(n_axes):
    return pltpu.CompilerParams(dimension_semantics=("arbitrary",) * n_axes, vmem_limit_bytes=VMEM_LIMIT)


def _dot(a, b):
    return jnp.dot(a, b, preferred_element_type=F32)


def _dot_nt(a, b):
    return lax.dot_general(a, b, (((1,), (1,)), ((), ())), preferred_element_type=F32)


def _dot_tn(a, b):
    return lax.dot_general(a, b, (((0,), (0,)), ((), ())), preferred_element_type=F32)


def _dot_exact(a, b):
    return jnp.dot(a, b, preferred_element_type=F32, precision=lax.Precision.HIGHEST)


def _silu(x):
    return x * jax.nn.sigmoid(x)


def _rms_rows(x, g):
    return x * lax.rsqrt(jnp.mean(x * x, axis=-1, keepdims=True) + EPS) * g


def _rope_block(x, cos, sin, half):
    lane = lax.broadcasted_iota(jnp.int32, x.shape, 1)
    first = (lane // half) % 2 == 0
    partner = jnp.where(first, pltpu.roll(x, LANE - half, 1), pltpu.roll(x, half, 1))
    return x * cos + partner * sin


def _mod_kernel(c_ref, w_ref, b_ref, o_ref):
    a = _silu(c_ref[...]).astype(BF16)
    o_ref[...] = _dot(a, w_ref[...].astype(BF16)) + b_ref[...]


def _mod(c_rows, w_mod, b_mod):
    r, d = c_rows.shape
    n = w_mod.shape[1]
    tn = 1024
    return pl.pallas_call(
        _mod_kernel,
        grid=(n // tn,),
        in_specs=[pl.BlockSpec((r, d), lambda j: (0, 0)),
                  pl.BlockSpec((d, tn), lambda j: (0, j)),
                  pl.BlockSpec((1, tn), lambda j: (0, j))],
        out_specs=pl.BlockSpec((r, tn), lambda j: (0, j)),
        out_shape=jax.ShapeDtypeStruct((r, n), F32),
        compiler_params=_cparams(1),
        name="mod",
    )(c_rows, w_mod, b_mod.reshape(1, n))


def _adaln(x, mod_ref, ln_ref):
    h = _rms_rows(x, ln_ref[...])
    return h * (1.0 + mod_ref[0, 1:2, :]) + mod_ref[0, 0:1, :]


L0_OFF = dict(cq=0, ckv=384, kr=640, za=768, qb=1280, kb=1792, vb=1920, zb=2048)


def _inproj0_kernel(*refs, rope):
    us_ref = refs[-1]
    g = pl.program_id(0)

    @pl.when(g == 0)
    def _():
        us_ref[...] = jnp.zeros_like(us_ref)

    @pl.when(g == pl.num_programs(0) - 1)
    def _():
        _inproj0_finish(refs, rope)

    @pl.when(g < pl.num_programs(0) - 1)
    def _():
        x_ref, mod_ref, ln_ref, w_ref = refs[:4]
        h = _adaln(x_ref[0], mod_ref, ln_ref).astype(BF16)
        _inproj0_finish(refs, rope)
        us_ref[...] = _dot(h, w_ref[...])


def _inproj0_finish(refs, rope):
    if rope:
        (x_ref, mod_ref, ln_ref, w_ref, qn_ref, wuq_ref, kvn_ref, cm_ref, sm_ref, cs_ref, ss_ref,
         qa_ref, ckv_ref, kr_ref, za_ref, qb_ref, kb_ref, vbt_ref, zb_ref, u) = refs
    else:
        (x_ref, mod_ref, ln_ref, w_ref, qn_ref, wuq_ref, kvn_ref,
         qa_ref, ckv_ref, kr_ref, za_ref, qb_ref, kb_ref, vbt_ref, zb_ref,
         ckv32_ref, kr32_ref, kb32_ref, vb32_ref, u) = refs
    o = L0_OFF
    cq = _rms_rows(u[:, o["cq"]:o["cq"] + 384], qn_ref[...]).astype(BF16)
    qa = _dot(cq, wuq_ref[...])
    ckv = _rms_rows(u[:, o["ckv"]:o["ckv"] + 256], kvn_ref[...])
    kr = u[:, o["kr"]:o["kr"] + 128]
    qb = u[:, o["qb"]:o["qb"] + 512]
    kb = u[:, o["kb"]:o["kb"] + 128]
    vb = u[:, o["vb"]:o["vb"] + 128]
    if not rope:
        ckv32_ref[0] = ckv
        kr32_ref[0] = kr[:, 64:96]
        kb32_ref[0] = kb
        vb32_ref[0] = vb
    qa_scale = (MLA_NOPE + MLA_ROPE) ** -0.5 * LOG2E
    qb_scale = SWA_DH ** -0.5 * LOG2E
    for j in range(MLA_HEADS):
        blk = qa[:, j * LANE:(j + 1) * LANE]
        if rope:
            blk = _rope_block(blk, cm_ref[...], sm_ref[...], MLA_ROPE // 4)
        qa_ref[0, :, j * LANE:(j + 1) * LANE] = (blk * qa_scale).astype(BF16)
    for j in range(SWA_HEADS * SWA_DH // LANE):
        blk = qb[:, j * LANE:(j + 1) * LANE]
        if rope:
            blk = _rope_block(blk, cs_ref[...], ss_ref[...], SWA_DH // 4)
        qb_ref[0, :, j * LANE:(j + 1) * LANE] = (blk * qb_scale).astype(BF16)
    if rope:
        kr = _rope_block(kr, cm_ref[...], sm_ref[...], MLA_ROPE // 4)
        kb = _rope_block(kb, cs_ref[...], ss_ref[...], SWA_DH // 4)
    ckv_ref[0] = ckv.astype(BF16)
    kr_ref[0] = kr.astype(BF16)
    kb_ref[0] = kb.astype(BF16)
    za_ref[0] = u[:, o["za"]:o["za"] + 512].astype(BF16)
    zb_ref[0] = u[:, o["zb"]:o["zb"] + 512].astype(BF16)
    vbt_ref[0] = vb.T.astype(BF16)


def _row_tile(t):
    return 512 if t % 512 == 0 else 256


def _inproj0(x, mod, ln, w, qn, wuq, kvn, tables):
    b, t, d = x.shape
    tr = _row_tile(t)
    nj = t // tr
    n_blocks = b * nj
    rope = tables is not None
    bm = mod.shape[0]
    cur = lambda g: jnp.minimum(g, n_blocks - 1)
    held = lambda g: jnp.maximum(g - 1, 0)
    full = lambda a: pl.BlockSpec(a.shape, lambda g: (0,) * a.ndim)
    rows = lambda c: pl.BlockSpec((1, tr, c), lambda g: (held(g) // nj, held(g) % nj, 0))
    in_specs = [pl.BlockSpec((1, tr, d), lambda g: (cur(g) // nj, cur(g) % nj, 0)),
                pl.BlockSpec((1, 3, d), (lambda g: (cur(g) // nj, 0, 0)) if bm > 1 else (lambda g: (0, 0, 0))),
                full(ln), full(w), full(qn), full(wuq), full(kvn)]
    args = [x, mod, ln, w, qn, wuq, kvn]
    if rope:
        for tab in tables:
            in_specs.append(pl.BlockSpec((tr, LANE), lambda g: (held(g) % nj, 0)))
            args.append(tab)
    out_shape = [jax.ShapeDtypeStruct((b, t, 1024), BF16), jax.ShapeDtypeStruct((b, t, 256), BF16),
                 jax.ShapeDtypeStruct((b, t, 128), BF16), jax.ShapeDtypeStruct((b, t, 512), BF16),
                 jax.ShapeDtypeStruct((b, t, 512), BF16), jax.ShapeDtypeStruct((b, t, 128), BF16),
                 jax.ShapeDtypeStruct((b, 128, t), BF16), jax.ShapeDtypeStruct((b, t, 512), BF16)]
    out_specs = [rows(1024), rows(256), rows(128), rows(512), rows(512), rows(128),
                 pl.BlockSpec((1, 128, tr), lambda g: (held(g) // nj, 0, held(g) % nj)), rows(512)]
    if not rope:
        out_shape += [jax.ShapeDtypeStruct((b, t, 256), F32), jax.ShapeDtypeStruct((b, t, 32), F32),
                      jax.ShapeDtypeStruct((b, t, 128), F32), jax.ShapeDtypeStruct((b, t, 128), F32)]
        out_specs += [rows(256), rows(32), rows(128), rows(128)]
    return pl.pallas_call(
        functools.partial(_inproj0_kernel, rope=rope),
        grid=(n_blocks + 1,), in_specs=in_specs, out_specs=out_specs, out_shape=out_shape,
        scratch_shapes=[pltpu.VMEM((tr, w.shape[1]), F32)],
        compiler_params=_cparams(1), name="inproj0_dec" if rope else "inproj0_ctx",
    )(*args)


KEY_CHUNK = 512
ATT_LOOKAHEAD = 4


SUM_ROWS = 16


def _attend_heads(qs, kv_of, chunks, dv, sinks=None):
    nh = len(qs)
    tq = qs[0].shape[0]
    m = [None] * nh
    acc = [None] * nh
    items = [(ci, i) for ci in range(len(chunks)) for i in range(nh)]
    loaded = {}

    def kv(ci, src):
        if (ci, src) not in loaded:
            vt = chunks[ci][1](src)
            ones = (lax.broadcasted_iota(jnp.int32, (SUM_ROWS, vt.shape[1]), 0) == 0).astype(BF16)
            loaded[(ci, src)] = (chunks[ci][0](src), jnp.concatenate([vt, ones], axis=0))
        return loaded[(ci, src)]

    if sinks is not None:
        m = [jnp.full((1, tq), sk, F32) for sk in sinks]
        unit = (lax.broadcasted_iota(jnp.int32, (dv + SUM_ROWS, tq), 0) == dv).astype(F32)
        acc = [unit for _ in sinks]

    scores = {}
    for t in range(len(items) + ATT_LOOKAHEAD):
        if t < len(items):
            ci, i = items[t]
            scores[t] = _dot_nt(kv(ci, kv_of[i])[0], qs[i])
        t0 = t - ATT_LOOKAHEAD
        if t0 < 0:
            continue
        ci, i = items[t0]
        mask = chunks[ci][2]
        si = scores.pop(t0)
        if mask is not None:
            si = jnp.where(mask, si, NEG)
        cm = si.max(axis=0, keepdims=True)
        alpha = None
        if m[i] is None:
            m_new = cm
        else:
            m_new = jnp.maximum(m[i], cm)
            alpha = jnp.exp2(m[i] - m_new)
        p = jnp.exp2(si - m_new)
        m[i] = m_new
        pv = _dot(kv(ci, kv_of[i])[1], p.astype(BF16))
        acc[i] = pv if acc[i] is None else acc[i] * alpha + pv
    return [acc[i][:dv] * (1.0 / acc[i][dv:dv + 1]) for i in range(nh)]


def _key_chunks(n):
    step = KEY_CHUNK if n % KEY_CHUNK == 0 else n
    return [(c0, step) for c0 in range(0, n, step)]


def _mla_kernel(*refs, n_new, n_ctx, hp):
    if n_ctx:
        q_ref, ckv_ref, kr_ref, ckvc_ref, krc_ref, wuk_ref, wuvt_ref, o_ref, k_s, vt_s = refs
    else:
        q_ref, ckv_ref, kr_ref, wuk_ref, wuvt_ref, o_ref, k_s, vt_s = refs
    qi, gi = pl.program_id(1), pl.program_id(2)

    @pl.when((qi == 0) & (gi == 0))
    def _():
        def expand(ckv, kr, r0, n):
            kn = _dot(ckv, wuk_ref[...])
            for j in range(MLA_HEADS):
                k_s[j, r0:r0 + n, :] = (kn[:, j * LANE:(j + 1) * LANE] + kr).astype(BF16)
            vt_s[:, r0:r0 + n] = _dot_nt(wuvt_ref[...], ckv).astype(BF16)

        blk = 512 if n_new % 512 == 0 else 256
        for r0 in range(0, n_new, blk):
            expand(ckv_ref[0, r0:r0 + blk, :], kr_ref[0, r0:r0 + blk, :].astype(F32), r0, blk)
        if n_ctx:
            expand(ckvc_ref[0].astype(BF16), krc_ref[0], n_new, n_ctx)

    tq = q_ref.shape[1]
    subs = [(s0, min(ATT_SUBQ, tq)) for s0 in range(0, tq, ATT_SUBQ)]
    qs = [q_ref[0, s0:s0 + sn, j * LANE:(j + 1) * LANE] for s0, sn in subs for j in range(hp)]
    chunks = []
    for c0, cn in _key_chunks(n_new + n_ctx):
        chunks.append((
            lambda j, c0=c0, cn=cn: k_s[gi * hp + j, c0:c0 + cn, :],
            lambda j, c0=c0, cn=cn: vt_s[pl.ds(pl.multiple_of((gi * hp + j) * MLA_V, MLA_V), MLA_V), c0:c0 + cn],
            None))
    outs = _attend_heads(qs, [j for _ in subs for j in range(hp)], chunks, MLA_V)
    for si, (s0, sn) in enumerate(subs):
        blk = jnp.concatenate(outs[si * hp:(si + 1) * hp], axis=0)
        o_ref[0, s0:s0 + sn, :] = blk.T.astype(BF16)


def _mla(q, ckv, kr, ckv_c, kr_c, wuk, wuvt, tq):
    b, t, _ = q.shape
    n_ctx = 0 if ckv_c is None else ckv_c.shape[1]
    hp = 8
    tk = t + n_ctx
    rows_q = pl.BlockSpec((1, tq, hp * LANE), lambda i, j, g: (i, j, g))
    per_b = lambda a: pl.BlockSpec((1,) + a.shape[1:], lambda i, j, g: (i, 0, 0))
    full = lambda a: pl.BlockSpec(a.shape, lambda i, j, g: (0, 0))
    in_specs = [rows_q, per_b(ckv), per_b(kr)]
    args = [q, ckv, kr]
    if n_ctx:
        in_specs += [per_b(ckv_c), per_b(kr_c)]
        args += [ckv_c, kr_c]
    in_specs += [full(wuk), full(wuvt)]
    args += [wuk, wuvt]
    return pl.pallas_call(
        functools.partial(_mla_kernel, n_new=t, n_ctx=n_ctx, hp=hp),
        grid=(b, t // tq, MLA_HEADS // hp), in_specs=in_specs,
        out_specs=pl.BlockSpec((1, tq, hp * MLA_V), lambda i, j, g: (i, j, g)),
        out_shape=jax.ShapeDtypeStruct((b, t, MLA_HEADS * MLA_V), BF16),
        scratch_shapes=[pltpu.VMEM((MLA_HEADS, tk, LANE), BF16), pltpu.VMEM((MLA_HEADS * MLA_V, tk), BF16)],
        compiler_params=_cparams(3), name="mla_dec" if n_ctx else "mla_ctx",
    )(*args)


def _swa_kernel(*refs, n_new, n_ctx, tq):
    if n_ctx:
        sink_ref, q_ref, k_ref, vt_ref, kc_ref, vct_ref, o_ref = refs
    else:
        sink_ref, q_ref, k_ref, vt_ref, o_ref = refs
    qi = pl.program_id(1)
    grp = SWA_HEADS // SWA_KV_HEADS
    if n_ctx:
        span = tq + 2 * WINDOW
        q0 = qi * tq
        start = pl.multiple_of(jnp.clip(q0 - WINDOW, 0, n_new - span), LANE)
        kpos = start + lax.broadcasted_iota(jnp.int32, (span, tq), 0)
        qpos = q0 + lax.broadcasted_iota(jnp.int32, (span, tq), 1)
        band = jnp.abs(kpos - qpos) <= WINDOW
    dh = SWA_DH
    chunks = []
    if n_ctx:
        for c0, cn in _key_chunks(span):
            chunks.append((
                lambda g, c0=c0, cn=cn: k_ref[0, pl.ds(start + c0, cn), g * dh:(g + 1) * dh],
                lambda g, c0=c0, cn=cn: vt_ref[0, g * dh:(g + 1) * dh, pl.ds(start + c0, cn)],
                band[c0:c0 + cn]))
        for c0, cn in _key_chunks(n_ctx):
            chunks.append((
                lambda g, c0=c0, cn=cn: kc_ref[0, c0:c0 + cn, g * dh:(g + 1) * dh].astype(BF16),
                lambda g, c0=c0, cn=cn: vct_ref[0, g * dh:(g + 1) * dh, c0:c0 + cn].astype(BF16),
                None))
    else:
        for c0, cn in _key_chunks(n_new):
            chunks.append((
                lambda g, c0=c0, cn=cn: k_ref[0, c0:c0 + cn, g * dh:(g + 1) * dh],
                lambda g, c0=c0, cn=cn: vt_ref[0, g * dh:(g + 1) * dh, c0:c0 + cn],
                None))
    qs = [q_ref[0, :, h * dh:(h + 1) * dh] for h in range(SWA_HEADS)]
    sinks = [sink_ref[h] * LOG2E for h in range(SWA_HEADS)]
    outs = _attend_heads(qs, [h // grp for h in range(SWA_HEADS)], chunks, SWA_DH, sinks)
    o_ref[0] = jnp.concatenate(outs, axis=0).T.astype(BF16)


def _swa(sink, q, k, vt, k_c, v_ct, tq):
    b, t, _ = q.shape
    n_ctx = 0 if k_c is None else k_c.shape[1]
    assert n_ctx == 0 or (t >= tq + 2 * WINDOW and tq % LANE == 0 and WINDOW % LANE == 0)
    per_b = lambda a: pl.BlockSpec((1,) + a.shape[1:], lambda i, j: (i, 0, 0))
    in_specs = [pl.BlockSpec(memory_space=pltpu.SMEM), pl.BlockSpec((1, tq, 512), lambda i, j: (i, j, 0)),
                per_b(k), per_b(vt)]
    args = [sink, q, k, vt]
    if n_ctx:
        in_specs += [per_b(k_c), per_b(v_ct)]
        args += [k_c, v_ct]
    return pl.pallas_call(
        functools.partial(_swa_kernel, n_new=t, n_ctx=n_ctx, tq=tq),
        grid=(b, t // tq), in_specs=in_specs,
        out_specs=pl.BlockSpec((1, tq, 512), lambda i, j: (i, j, 0)),
        out_shape=jax.ShapeDtypeStruct((b, t, 512), BF16),
        compiler_params=_cparams(2), name="swa_dec" if n_ctx else "swa_ctx",
    )(*args)


def _silu_gate(o, z):
    return o * (z / (1.0 + jnp.exp(-z)))


L1_OFF = dict(qkv=0, zc=1536, qd=2048, kd=2560, vd=2816, zd=3072, ab=3584)
L1_W = 3712


def _inproj1_kernel(*refs, rope, nj):
    if rope:
        (oa_ref, za_ref, ob_ref, zb_ref, x_ref, mod0_ref, wo_ref, mod_ref, ln_ref, w_ref, cw_ref,
         qn_ref, kn_ref, c_ref, s_ref,
         x1_ref, qkv_ref, zc_ref, qd_ref, kd_ref, vdt_ref, zd_ref, ab_ref, us_ref, pr_ref) = refs
    else:
        (oa_ref, za_ref, ob_ref, zb_ref, x_ref, mod0_ref, wo_ref, mod_ref, ln_ref, w_ref, cw_ref,
         qn_ref, kn_ref,
         x1_ref, qkv_ref, zc_ref, qd_ref, kd_ref, vdt_ref, zd_ref, ab_ref, kd32_ref, vd32_ref, us_ref, pr_ref) = refs
    g = pl.program_id(0)
    tr = us_ref.shape[0]
    nq = GDN_HEADS * GDN_DK
    n_qkv = 2 * nq + GDN_HEADS * GDN_DV
    o = L1_OFF

    @pl.when(g == 0)
    def _():
        us_ref[...] = jnp.zeros_like(us_ref)
        pr_ref[...] = jnp.zeros_like(pr_ref)

    def conv_held_block(next_row):
        jb = lax.rem(g + nj - 1, nj)
        prv = jnp.where(jb > 0, pr_ref[7:8, :], 0.0)
        nxt = jnp.where(jb < nj - 1, next_row, 0.0)
        row8 = lax.broadcasted_iota(jnp.int32, (8, LANE), 0)
        for j in range(n_qkv // LANE):
            cols = slice(j * LANE, (j + 1) * LANE)
            xj = us_ref[:, cols]
            xp = pltpu.roll(xj, 1, 0)
            xp = jnp.concatenate([jnp.where(row8 == 0, prv[:, cols], xp[0:8]), xp[8:]], axis=0)
            xn = pltpu.roll(xj, tr - 1, 0)
            xn = jnp.concatenate([xn[:tr - 8], jnp.where(row8 == 7, nxt[:, cols], xn[tr - 8:])], axis=0)
            cw = cw_ref[:, cols]
            y = _silu(xp * cw[0:1, :] + xj * cw[1:2, :] + xn * cw[2:3, :])
            if j < 2 * nq // LANE:
                y = y * lax.rsqrt(jnp.sum(y * y, axis=-1, keepdims=True) + EPS)
            if j < nq // LANE:
                y = y * GDN_DK ** -0.5
            qkv_ref[0, :, cols] = y.astype(BF16)
        pr_ref[...] = us_ref[tr - 8:tr, 0:n_qkv]
        zc_ref[0] = us_ref[:, o["zc"]:o["zc"] + 512].astype(BF16)
        zd_ref[0] = us_ref[:, o["zd"]:o["zd"] + 512].astype(BF16)
        ab_ref[0] = us_ref[:, o["ab"]:o["ab"] + 16]
        qd_scale = ATT_DH ** -0.5 * LOG2E
        for j in range(ATT_HEADS):
            blk = _rms_rows(us_ref[:, o["qd"] + j * LANE:o["qd"] + (j + 1) * LANE], qn_ref[...])
            if rope:
                blk = _rope_block(blk, c_ref[...], s_ref[...], ATT_DH // 4)
            qd_ref[0, :, j * LANE:(j + 1) * LANE] = (blk * qd_scale).astype(BF16)
        for j in range(ATT_KV_HEADS):
            blk = _rms_rows(us_ref[:, o["kd"] + j * LANE:o["kd"] + (j + 1) * LANE], kn_ref[...])
            if rope:
                blk = _rope_block(blk, c_ref[...], s_ref[...], ATT_DH // 4)
            else:
                kd32_ref[0, :, j * LANE:(j + 1) * LANE] = blk
            kd_ref[0, :, j * LANE:(j + 1) * LANE] = blk.astype(BF16)
        vd = us_ref[:, o["vd"]:o["vd"] + 256]
        if not rope:
            vd32_ref[0] = vd
        vdt_ref[0] = vd.T.astype(BF16)

    @pl.when(g == pl.num_programs(0) - 1)
    def _():
        conv_held_block(jnp.zeros((1, n_qkv), F32))

    @pl.when(g < pl.num_programs(0) - 1)
    def _():
        _inproj1_block(refs, rope, conv_held_block)


def _inproj1_block(refs, rope, conv_held_block):
    if rope:
        (oa_ref, za_ref, ob_ref, zb_ref, x_ref, mod0_ref, wo_ref, mod_ref, ln_ref, w_ref, cw_ref,
         qn_ref, kn_ref, c_ref, s_ref,
         x1_ref, qkv_ref, zc_ref, qd_ref, kd_ref, vdt_ref, zd_ref, ab_ref, us_ref, pr_ref) = refs
    else:
        (oa_ref, za_ref, ob_ref, zb_ref, x_ref, mod0_ref, wo_ref, mod_ref, ln_ref, w_ref, cw_ref,
         qn_ref, kn_ref,
         x1_ref, qkv_ref, zc_ref, qd_ref, kd_ref, vdt_ref, zd_ref, ab_ref, kd32_ref, vd32_ref, us_ref, pr_ref) = refs
    n_qkv = pr_ref.shape[1]
    y0 = (_dot(_silu_gate(oa_ref[0], za_ref[0]), wo_ref[0:512, :])
          + _dot(_silu_gate(ob_ref[0], zb_ref[0]), wo_ref[512:1024, :]))
    x1 = x_ref[0] + mod0_ref[0, 2:3, :] * y0
    x1_ref[0] = x1
    h = _adaln(x1, mod_ref, ln_ref).astype(BF16)
    conv_held_block(_dot(h[0:8], w_ref[:, 0:n_qkv])[0:1])
    us_ref[...] = _dot(h, w_ref[...])


def _inproj1(oa, za, ob, zb, x, mod0, wo, mod, ln, w, cw, qn, kn, tables):
    b, t, d = x.shape
    tr = _row_tile(t)
    nj = t // tr
    n_blocks = b * nj
    rope = tables is not None
    bm = mod.shape[0]
    n_qkv = cw.shape[1]
    cur = lambda g: jnp.minimum(g, n_blocks - 1)
    held = lambda g: jnp.maximum(g - 1, 0)
    full = lambda a: pl.BlockSpec(a.shape, lambda g: (0,) * a.ndim)
    rows = lambda c: pl.BlockSpec((1, tr, c), lambda g: (cur(g) // nj, cur(g) % nj, 0))
    mod_spec = pl.BlockSpec((1, 3, d), (lambda g: (cur(g) // nj, 0, 0)) if bm > 1 else (lambda g: (0, 0, 0)))
    widths = [512, 512, 512, 512, d]
    in_specs = ([rows(c) for c in widths]
                + [mod_spec, full(wo), mod_spec, full(ln), full(w), full(cw), full(qn), full(kn)])
    args = [oa, za, ob, zb, x, mod0, wo, mod, ln, w, cw, qn, kn]
    hrows = lambda c: pl.BlockSpec((1, tr, c), lambda g: (held(g) // nj, held(g) % nj, 0))
    if rope:
        for tab in tables:
            in_specs.append(pl.BlockSpec((tr, LANE), lambda g: (held(g) % nj, 0)))
            args.append(tab)
    out_shape = [jax.ShapeDtypeStruct((b, t, d), F32),
                 jax.ShapeDtypeStruct((b, t, n_qkv), BF16), jax.ShapeDtypeStruct((b, t, 512), BF16),
                 jax.ShapeDtypeStruct((b, t, 512), BF16), jax.ShapeDtypeStruct((b, t, 256), BF16),
                 jax.ShapeDtypeStruct((b, 256, t), BF16), jax.ShapeDtypeStruct((b, t, 512), BF16),
                 jax.ShapeDtypeStruct((b, t, 16), F32)]
    out_specs = [rows(d), hrows(n_qkv), hrows(512), hrows(512), hrows(256),
                 pl.BlockSpec((1, 256, tr), lambda g: (held(g) // nj, 0, held(g) % nj)), hrows(512), hrows(16)]
    if not rope:
        out_shape += [jax.ShapeDtypeStruct((b, t, 256), F32), jax.ShapeDtypeStruct((b, t, 256), F32)]
        out_specs += [hrows(256), hrows(256)]
    return pl.pallas_call(
        functools.partial(_inproj1_kernel, rope=rope, nj=nj),
        grid=(n_blocks + 1,), in_specs=in_specs, out_specs=out_specs, out_shape=out_shape,
        scratch_shapes=[pltpu.VMEM((tr, L1_W), F32), pltpu.VMEM((8, n_qkv), F32)],
        compiler_params=_cparams(1), name="inproj1_dec" if rope else "inproj1_ctx",
    )(*args)


def _gdn_local(blocks):
    c = GDN_CHUNK
    row = lax.broadcasted_iota(jnp.int32, (c, c), 0)
    col = lax.broadcasted_iota(jnp.int32, (c, c), 1)
    lane2 = lax.broadcasted_iota(jnp.int32, (c, 2 * c), 1)
    eye = (row == col).astype(F32)
    eye_t = jnp.concatenate([eye, jnp.zeros((c, c), F32)], axis=1).astype(BF16)
    chains = [ch for blk in blocks for ch in blk["dirs"]]
    for blk in blocks:
        for ch in blk["dirs"]:
            causal = (row <= col) if ch["upper"] else (row >= col)
            ch["strict"] = (row < col) if ch["upper"] else (row > col)
            ch["decay"] = jnp.exp(jnp.where(causal, ch["gc_col"] - ch["gc_row"], -jnp.inf))
            ch["kb"] = blk["k"] * ch["beta_col"]
            ch["egc"] = jnp.exp(ch["gc_col"])
    for blk in blocks:
        lhs = jnp.concatenate([ch["kb"] for ch in blk["dirs"]] + [blk["q"]], axis=0).astype(BF16)
        a = _dot_nt(lhs, blk["k"].astype(BF16))
        nd = len(blk["dirs"])
        for di, ch in enumerate(blk["dirs"]):
            x = jnp.where(ch["strict"], -(a[di * c:(di + 1) * c] * ch["decay"]), 0.0)
            ch["intra"] = (a[nd * c:] * ch["decay"]).astype(BF16)
            ch["w"] = jnp.concatenate([eye, x], axis=1)
    for _ in range(6):
        for ch in chains:
            w = ch["w"]
            wh = w.astype(BF16)
            lo = w - wh.astype(F32)
            php = jnp.where(lane2 < c, pltpu.roll(w, c, 1), lo).astype(BF16)
            ch["w"] = _dot(jnp.concatenate([wh, php], axis=1),
                           jnp.concatenate([eye_t, wh, lo.astype(BF16), wh], axis=0))
    for blk in blocks:
        for ch in blk["dirs"]:
            rhs = jnp.concatenate([blk["v"] * ch["beta_col"], ch["kb"] * ch["egc"]], axis=1).astype(BF16)
            sol = _dot(ch["w"][:, :c].astype(BF16), rhs)
            ch["u"], ch["wv"] = sol[:, :GDN_DV].astype(BF16), sol[:, GDN_DV:].astype(BF16)
            ch["qe"] = (blk["q"] * ch["egc"]).astype(BF16)
            ch["kd"] = (blk["k"] * jnp.exp(ch["glast"] - ch["gc_col"])).astype(BF16)
            ch["eg"] = jnp.exp(ch["glast"])


def _gdn_scan(chains):
    c = GDN_CHUNK
    for ch in chains:
        ch["sb"] = ch["s"].astype(BF16)
    for ch in chains:
        r = _dot(jnp.concatenate([ch["wv"], ch["qe"]], axis=0), ch["sb"])
        ch["vn"] = (ch["u"].astype(F32) - r[:c]).astype(BF16)
        ch["qs"] = r[c:]
    outs = []
    for ch in chains:
        o = ch["qs"] + _dot(ch["intra"], ch["vn"])
        s_new = ch["s"] * ch["eg"] + _dot_tn(ch["kd"], ch["vn"])
        outs.append((o, s_new))
    return outs


def _gdn_kernel(qkv_ref, ab_ref, abt_ref, al_ref, dt_ref, alt_ref, dtt_ref, s0_ref, o_ref, sf_ref,
                u_s, wv_s, qe_s, kd_s, in_s, eg_s, gcol_s, grow_s, beta_s, st_s, *, t):
    c = GDN_CHUNK
    n = t // c
    nh = GDN_HEADS
    ab = ab_ref[0]
    gact = -jnp.exp(al_ref[...]) * jax.nn.softplus(ab + dt_ref[...])
    lane16 = lax.broadcasted_iota(jnp.int32, ab.shape, 1)
    beta_s[...] = jnp.where(lane16 < 2 * nh, gact, jax.nn.sigmoid(ab))
    r64 = lax.broadcasted_iota(jnp.int32, (c, c), 0)
    c64 = lax.broadcasted_iota(jnp.int32, (c, c), 1)
    tril = (r64 >= c64).astype(F32)
    triu = (r64 <= c64).astype(F32)
    lane_c = lax.broadcasted_iota(jnp.int32, (c, 16), 1)
    sub_c = lax.broadcasted_iota(jnp.int32, (16, c), 0)

    def cum_chunk(i):
        r0 = pl.multiple_of(i * c, c)
        g = beta_s[pl.ds(r0, c), :]
        gcol_s[pl.ds(r0, c), :] = jnp.where(lane_c < nh, _dot_exact(tril, g), _dot_exact(triu, g))
        gt = -jnp.exp(alt_ref[...]) * jax.nn.softplus(abt_ref[0, i] + dtt_ref[...])
        grow_s[i] = jnp.where(sub_c < nh, _dot_exact(gt, triu), _dot_exact(gt, tril))

    ncs = GDN_LOCAL_CHUNKS
    rb = ncs * c
    for sub in range(ncs):
        cum_chunk(sub)

    def local_body(jb, carry):
        r0 = pl.multiple_of(jb * rb, rb)
        nxt_blk = jnp.minimum(jb + 1, n // ncs - 1)
        xs = [qkv_ref[0, pl.ds(r0, rb), j * LANE:(j + 1) * LANE].astype(F32) for j in range(3 * nh)]
        gcol = gcol_s[pl.ds(r0, rb), :]
        bet = beta_s[pl.ds(r0, rb), :]
        blocks = []
        for sub in range(ncs):
            ci = ncs * jb + sub
            rows = slice(sub * c, (sub + 1) * c)
            grow = grow_s[ci]
            for hh in range(nh):
                dirs = []
                for d in range(2):
                    ch = d * nh + hh
                    last = sub * c + (c - 1 if d == 0 else 0)
                    dirs.append(dict(gc_col=gcol[rows, ch:ch + 1], gc_row=grow[ch:ch + 1, :],
                                     beta_col=bet[rows, 2 * nh + ch:2 * nh + ch + 1],
                                     glast=gcol[last:last + 1, ch:ch + 1], upper=(d == 1), ch=ch, ci=ci,
                                     r0=r0 + sub * c))
                blocks.append(dict(q=xs[hh][rows], k=xs[nh + hh][rows], v=xs[2 * nh + hh][rows], dirs=dirs))
        _gdn_local(blocks)
        for blk in blocks:
            for chn in blk["dirs"]:
                ch, rr = chn["ch"], pl.ds(pl.multiple_of(chn["r0"], c), c)
                u_s[ch, rr, :] = chn["u"]
                wv_s[ch, rr, :] = chn["wv"]
                qe_s[ch, rr, :] = chn["qe"]
                kd_s[ch, rr, :] = chn["kd"]
                in_s[ch, chn["ci"]] = chn["intra"]
                eg_s[chn["ci"], ch:ch + 1, :] = jnp.broadcast_to(chn["eg"], (1, LANE))
        for sub in range(ncs):
            cum_chunk(nxt_blk * ncs + sub)
        return carry

    lax.fori_loop(0, n // ncs, local_body, 0)

    for d in range(2):
        for hh in range(nh):
            st_s[d * nh + hh] = s0_ref[0, d, hh]
    o_ref[...] = jnp.zeros_like(o_ref)

    def scan_body(i, carry):
        chains = []
        for d in range(2):
            ci = i if d == 0 else n - 1 - i
            rr = pl.ds(pl.multiple_of(ci * c, c), c)
            eg = eg_s[ci]
            for hh in range(nh):
                ch = d * nh + hh
                chains.append(dict(u=u_s[ch, rr, :], wv=wv_s[ch, rr, :], qe=qe_s[ch, rr, :], kd=kd_s[ch, rr, :],
                                   intra=in_s[ch, ci], eg=eg[ch:ch + 1, :], s=st_s[ch], rr=rr, hh=hh, ch=ch))
        for chn, (o, s_new) in zip(chains, _gdn_scan(chains)):
            st_s[chn["ch"]] = s_new
            cols = slice(chn["hh"] * LANE, (chn["hh"] + 1) * LANE)
            o_ref[0, chn["rr"], cols] = (o_ref[0, chn["rr"], cols].astype(F32) + o).astype(o_ref.dtype)
        return carry

    lax.fori_loop(0, n, scan_body, 0)
    for d in range(2):
        for hh in range(nh):
            sf_ref[0, d, hh] = st_s[d * nh + hh]


def _gdn(qkv, ab, abt, al, dt, alt, dtt, s0):
    b, t, _ = qkv.shape
    n = t // GDN_CHUNK
    assert t % (GDN_CHUNK * GDN_LOCAL_CHUNKS) == 0
    per_b = lambda a: pl.BlockSpec((1,) + a.shape[1:], lambda i: (i,) + (0,) * (a.ndim - 1))
    full = lambda a: pl.BlockSpec(a.shape, lambda i: (0,) * a.ndim)
    return pl.pallas_call(
        functools.partial(_gdn_kernel, t=t), grid=(b,),
        in_specs=[per_b(qkv), per_b(ab), per_b(abt), full(al), full(dt), full(alt), full(dtt), per_b(s0)],
        out_specs=[pl.BlockSpec((1, t, GDN_HEADS * GDN_DV), lambda i: (i, 0, 0)), per_b(s0)],
        out_shape=[jax.ShapeDtypeStruct((b, t, GDN_HEADS * GDN_DV), BF16), jax.ShapeDtypeStruct(s0.shape, F32)],
        scratch_shapes=[pltpu.VMEM((2 * GDN_HEADS, t, LANE), BF16), pltpu.VMEM((2 * GDN_HEADS, t, LANE), BF16),
                        pltpu.VMEM((2 * GDN_HEADS, t, LANE), BF16), pltpu.VMEM((2 * GDN_HEADS, t, LANE), BF16),
                        pltpu.VMEM((2 * GDN_HEADS, n, GDN_CHUNK, GDN_CHUNK), BF16),
                        pltpu.VMEM((n, 2 * GDN_HEADS, LANE), F32),
                        pltpu.VMEM((t, 16), F32), pltpu.VMEM((n, 16, GDN_CHUNK), F32), pltpu.VMEM((t, 16), F32),
                        pltpu.VMEM((2 * GDN_HEADS, GDN_DK, GDN_DV), F32)],
        compiler_params=_cparams(1), name="gdn",
    )(qkv, ab, abt, al, dt, alt, dtt, s0)


ATT_SUBQ = 256


def _attd_kernel(*refs, n_ctx):
    if n_ctx:
        q_ref, k_ref, vt_ref, kc_ref, vct_ref, o_ref = refs
    else:
        q_ref, k_ref, vt_ref, o_ref = refs
    n_new = k_ref.shape[1]
    grp = ATT_HEADS // ATT_KV_HEADS
    chunks = []
    for c0, cn in _key_chunks(n_new):
        chunks.append((
            lambda g, c0=c0, cn=cn: k_ref[0, c0:c0 + cn, g * LANE:(g + 1) * LANE],
            lambda g, c0=c0, cn=cn: vt_ref[0, g * LANE:(g + 1) * LANE, c0:c0 + cn],
            None))
    if n_ctx:
        for c0, cn in _key_chunks(n_ctx):
            chunks.append((
                lambda g, c0=c0, cn=cn: kc_ref[0, c0:c0 + cn, g * LANE:(g + 1) * LANE].astype(BF16),
                lambda g, c0=c0, cn=cn: vct_ref[0, g * LANE:(g + 1) * LANE, c0:c0 + cn].astype(BF16),
                None))
    tq = q_ref.shape[1]
    subs = [(s0, min(ATT_SUBQ, tq)) for s0 in range(0, tq, ATT_SUBQ)]
    qs = [q_ref[0, s0:s0 + sn, h * LANE:(h + 1) * LANE] for s0, sn in subs for h in range(ATT_HEADS)]
    outs = _attend_heads(qs, [h // grp for _ in subs for h in range(ATT_HEADS)], chunks, ATT_DH)
    for si, (s0, sn) in enumerate(subs):
        blk = jnp.concatenate(outs[si * ATT_HEADS:(si + 1) * ATT_HEADS], axis=0)
        o_ref[0, s0:s0 + sn, :] = blk.T.astype(BF16)


def _attd(q, k, vt, k_c, v_ct, tq):
    b, t, _ = q.shape
    n_ctx = 0 if k_c is None else k_c.shape[1]
    per_b = lambda a: pl.BlockSpec((1,) + a.shape[1:], lambda i, j: (i, 0, 0))
    in_specs = [pl.BlockSpec((1, tq, 512), lambda i, j: (i, j, 0)), per_b(k), per_b(vt)]
    args = [q, k, vt]
    if n_ctx:
        in_specs += [per_b(k_c), per_b(v_ct)]
        args += [k_c, v_ct]
    return pl.pallas_call(
        functools.partial(_attd_kernel, n_ctx=n_ctx),
        grid=(b, t // tq), in_specs=in_specs,
        out_specs=pl.BlockSpec((1, tq, 512), lambda i, j: (i, j, 0)),
        out_shape=jax.ShapeDtypeStruct((b, t, 512), BF16),
        compiler_params=_cparams(2), name="attd_dec" if n_ctx else "attd_ctx",
    )(*args)


def _outproj1_kernel(oc_ref, zc_ref, od_ref, zd_ref, x_ref, mod_ref, gn_ref, w_ref, lnf_ref, y_ref):
    parts = []
    for j in range(GDN_HEADS):
        parts.append(_rms_rows(oc_ref[0, :, j * LANE:(j + 1) * LANE].astype(F32), gn_ref[...]).astype(BF16))
    gc = _silu_gate(jnp.concatenate(parts, axis=1), zc_ref[0])
    gd = _silu_gate(od_ref[0], zd_ref[0])
    y = _dot(gc, w_ref[0:512, :]) + _dot(gd, w_ref[512:1024, :])
    x2 = x_ref[0] + mod_ref[0, 2:3, :] * y
    y_ref[0] = _rms_rows(x2, lnf_ref[...])


def _outproj1(oc, zc, od, zd, x, mod, gn, w, lnf):
    b, t, d = x.shape
    tr = 1024 if t % 1024 == 0 else _row_tile(t)
    bm = mod.shape[0]
    rows = lambda c: pl.BlockSpec((1, tr, c), lambda i, j: (i, j, 0))
    full = lambda a: pl.BlockSpec(a.shape, lambda i, j: (0,) * a.ndim)
    return pl.pallas_call(
        _outproj1_kernel, grid=(b, t // tr),
        in_specs=[rows(512), rows(512), rows(512), rows(512), rows(d),
                  pl.BlockSpec((1, 3, d), (lambda i, j: (i, 0, 0)) if bm > 1 else (lambda i, j: (0, 0, 0))),
                  full(gn), full(w), full(lnf)],
        out_specs=rows(d), out_shape=jax.ShapeDtypeStruct((b, t, d), F32),
        compiler_params=_cparams(2), name="outproj1",
    )(oc, zc, od, zd, x, mod, gn, w, lnf)


def _rope_table(n_tok, rot_dim):
    quarter = rot_dim // 4
    inv = np.float32(ROPE_THETA) ** (-np.arange(quarter, dtype=np.float32) / np.float32(quarter))
    tt = np.arange(n_tok)
    pos = np.stack([tt // GRID_W, tt % GRID_W], axis=-1).astype(np.float32)
    ang = (pos[:, :, None] * inv).astype(np.float32)
    cos, sin = np.cos(ang), np.sin(ang)
    c = np.concatenate([cos, cos], axis=-1).reshape(n_tok, rot_dim)
    s = np.concatenate([-sin, sin], axis=-1).reshape(n_tok, rot_dim)
    return c.astype(np.float32), s.astype(np.float32)


def _place(tab, fill, off, width):
    out = np.full((tab.shape[0], width), fill, np.float32)
    out[:, off:off + tab.shape[1]] = tab
    return out


PACK_STEP = 256


def _pack_kernel(wt_ref, o_ref, *, segs):
    d = wt_ref.shape[1]
    lane = lax.broadcasted_iota(jnp.int32, (d, LANE), 1)
    off = 0
    for start, width, lane_off in segs:
        if width % LANE == 0:
            for c0 in range(0, width, PACK_STEP):
                cw = min(PACK_STEP, width - c0)
                o_ref[:, off + c0:off + c0 + cw] = wt_ref[start + c0:start + c0 + cw, :].T.astype(BF16)
            off += width
        else:
            blk = jnp.where(lane < width, wt_ref[start:start + LANE, :].T, 0.0)
            if lane_off:
                blk = pltpu.roll(blk, lane_off, 1)
            o_ref[:, off:off + LANE] = blk.astype(BF16)
            off += LANE


def _pack_cols(w, segs):
    d, n_in = w.shape
    n_out = sum(width if width % LANE == 0 else LANE for _, width, _ in segs)
    return pl.pallas_call(
        functools.partial(_pack_kernel, segs=segs), grid=(1,),
        in_specs=[pl.BlockSpec((n_in, d), lambda i: (0, 0))],
        out_specs=pl.BlockSpec((d, n_out), lambda i: (0, 0)),
        out_shape=jax.ShapeDtypeStruct((d, n_out), BF16),
        compiler_params=_cparams(1), name="pack_cols",
    )(w.T)


def _prep_l0(w_in0, w_uq, w_ukv):
    w = _pack_cols(w_in0, [(0, 640, 0), (640, MLA_ROPE, MLA_NOPE), (672, 1792, 0)])
    uq = w_uq.reshape(MLA_Q_LORA, MLA_HEADS, MLA_NOPE + MLA_ROPE)
    wuq = jnp.pad(uq, ((0, 0), (0, 0), (0, LANE - MLA_NOPE - MLA_ROPE))).reshape(MLA_Q_LORA, MLA_HEADS * LANE)
    ukv = w_ukv.reshape(MLA_KV_LORA, MLA_HEADS, MLA_NOPE + MLA_V)
    wuk = jnp.pad(ukv[:, :, :MLA_NOPE], ((0, 0), (0, 0), (0, LANE - MLA_NOPE))).reshape(MLA_KV_LORA, MLA_HEADS * LANE)
    wuvt = ukv[:, :, MLA_NOPE:].reshape(MLA_KV_LORA, MLA_HEADS * MLA_V).T
    return w, wuq.astype(BF16), wuk.astype(BF16), wuvt.astype(BF16)


def _prep_l1(w_in1):
    return _pack_cols(w_in1, [(0, 1536, 0), (1552, 2048, 0), (1536, 16, 0)])


def _chunk_rows(ab):
    b, t, c = ab.shape
    return jnp.swapaxes(ab.reshape(b, t // GDN_CHUNK, GDN_CHUNK, c), 2, 3)


def _trunk(x, mod, caches, p, tables, tq):
    dec = caches is not None
    t0m, t0s, t1 = tables if dec else (None, None, None)
    (qa, ckv, kr, za, qb, kb, vbt, zb, *ctx0) = _inproj0(
        x, mod[0], p["ln0"], p["w0"], p["qn"], p["wuq"], p["kvn"], (t0m + t0s) if dec else None)
    if dec:
        ckv_c, kr_c, kb_c, vb_ct, s0, kd_c, vd_ct = caches
    else:
        ckv_c = kr_c = kb_c = vb_ct = kd_c = vd_ct = None
        s0 = jnp.zeros((x.shape[0], 2, GDN_HEADS, GDN_DK, GDN_DV), F32)
    oa = _mla(qa, ckv, kr, ckv_c, kr_c, p["wuk"], p["wuvt"], 2 * tq if x.shape[1] % (2 * tq) == 0 else tq)
    ob = _swa(p["sink"], qb, kb, vbt, kb_c, vb_ct, tq)
    (x1, qkv, zc, qd, kd, vdt, zd, ab, *ctx1) = _inproj1(
        oa, za, ob, zb, x, mod[0], p["wout0"], mod[1], p["ln1"], p["w1"], p["cw"], p["aqn"], p["akn"],
        t1 if dec else None)
    oc, sfin = _gdn(qkv, ab, _chunk_rows(ab), p["al"], p["dt"], p["alt"], p["dtt"], s0)
    od = _attd(qd, kd, vdt, kd_c, vd_ct, 4 * tq if x.shape[1] % (4 * tq) == 0 else tq)
    y = _outproj1(oc, zc, od, zd, x1, mod[1], p["gn"], p["wout1"], p["lnf"])
    return y, ctx0, sfin, ctx1


def kernel(x_prompt, x_sample, cache_l0_mla_ckv, cache_l0_mla_krope, cache_l0_swa_k, cache_l0_swa_v,
           state_l1_gdn, cache_l1_attn_k, cache_l1_attn_v, c, c_ctx,
           w_mod0, b_mod0, ln0, w_in0, mla_q_norm, w_uq, mla_kv_norm, w_ukv, swa_sink, w_out0,
           w_mod1, b_mod1, ln1, w_in1, gdn_conv, gdn_a_log, gdn_dt_bias, gdn_norm, att_q_norm, att_k_norm, w_out1,
           ln_f):
    d = x_prompt.shape[-1]
    bd, td = x_sample.shape[:2]
    bc, tc = x_prompt.shape[:2]
    past = cache_l0_mla_ckv.shape[1]
    row = lambda v: v.reshape(1, -1)
    w0, wuq, wuk, wuvt = _prep_l0(w_in0, w_uq, w_ukv)
    w1 = _prep_l1(w_in1)
    al8 = gdn_a_log.reshape(1, 2 * GDN_HEADS)
    dt8 = gdn_dt_bias.reshape(1, 2 * GDN_HEADS)
    al16 = jnp.pad(al8, ((0, 0), (0, 8)))
    dt16 = jnp.pad(dt8, ((0, 0), (0, 8)))
    p = dict(ln0=row(ln0), w0=w0, qn=row(mla_q_norm), wuq=wuq, kvn=row(mla_kv_norm), wuk=wuk, wuvt=wuvt,
             sink=swa_sink, wout0=w_out0.astype(BF16), ln1=row(ln1), w1=w1, aqn=row(att_q_norm),
             akn=row(att_k_norm), cw=gdn_conv, al=al16, dt=dt16, alt=al16.T, dtt=dt16.T, gn=row(gdn_norm),
             wout1=w_out1.astype(BF16), lnf=row(ln_f))
    n_rows = -(-(bd + 1) // 8) * 8
    c_rows = jnp.concatenate([c, c_ctx[None, :], jnp.zeros((n_rows - bd - 1, d), F32)], axis=0)
    mods = [_mod(c_rows, w_mod0, b_mod0), _mod(c_rows, w_mod1, b_mod1)]
    mod_dec = [m[:bd].reshape(bd, 3, d) for m in mods]
    mod_ctx = [m[bd:bd + 1].reshape(1, 3, d) for m in mods]
    cm, sm = _rope_table(td, MLA_ROPE)
    t0m = (jnp.asarray(_place(cm, 1.0, MLA_NOPE, LANE)), jnp.asarray(_place(sm, 0.0, MLA_NOPE, LANE)))
    cs, ss = _rope_table(td, SWA_DH)
    t0s = (jnp.asarray(np.tile(cs, (1, LANE // SWA_DH))), jnp.asarray(np.tile(ss, (1, LANE // SWA_DH))))
    t1 = tuple(jnp.asarray(a) for a in _rope_table(td, ATT_DH))
    caches = (cache_l0_mla_ckv,
              jnp.pad(cache_l0_mla_krope, ((0, 0), (0, 0), (MLA_NOPE, LANE - MLA_NOPE - MLA_ROPE))),
              cache_l0_swa_k.reshape(bd, past, SWA_KV_HEADS * SWA_DH),
              jnp.swapaxes(cache_l0_swa_v.reshape(bd, past, SWA_KV_HEADS * SWA_DH), 1, 2),
              state_l1_gdn,
              cache_l1_attn_k.reshape(bd, past, ATT_KV_HEADS * ATT_DH),
              jnp.swapaxes(cache_l1_attn_v.reshape(bd, past, ATT_KV_HEADS * ATT_DH), 1, 2))
    y_prompt, ctx0, sfin, ctx1 = _trunk(x_prompt, mod_ctx, None, p, None, tq=tc)
    y_sample, _, _, _ = _trunk(x_sample, mod_dec, caches, p, (t0m, t0s, t1), tq=256)
    ckv32, kr32, kb32, vb32 = ctx0
    kd32, vd32 = ctx1
    return (y_prompt, y_sample, ckv32, kr32,
            kb32.reshape(bc, tc, SWA_KV_HEADS, SWA_DH), vb32.reshape(bc, tc, SWA_KV_HEADS, SWA_DH),
            sfin, kd32.reshape(bc, tc, ATT_KV_HEADS, ATT_DH), vd32.reshape(bc, tc, ATT_KV_HEADS, ATT_DH))
```
